```python
import math
import functools
import jax
import jax.numpy as jnp
from jax import lax
import numpy as np

D_MODEL = 1024
BATCH = 32
SEQ = 256
DEPTH = 4
DEC_BATCH = 8
DEC_SEQ = 4096
PAST_LEN = 512

GRID_W = 64
CHUNK = 128
S5_WIDTH = 512
S5_GROUP = 16
S5_GROUPS = S5_WIDTH // S5_GROUP
S5_STATE = 64
S5_DT_MIN = 0.001
S5_DT_MAX = 0.1
RET_WIDTH = 512
RET_HEADS = 4
RET_HEAD_DIM = RET_WIDTH // RET_HEADS
ROPE_BASE = 10000.0
ML_WIDTH = 512
ML_HEADS = 4
ML_HEAD_DIM = ML_WIDTH // ML_HEADS
N_BRANCH = 3
D_FF = 2816
N_EXPERTS = 8
TOP_K = 2
MOE_D_FF = 2816
N_DENSE = (DEPTH + 1) // 2
N_MOE = DEPTH // 2
ALPHA = (2.0 * DEPTH) ** 0.25
BETA = (8.0 * DEPTH) ** -0.25
LN_EPS = 1e-5
GN_EPS = 1e-5
SPLIT_SIZES = (S5_WIDTH, RET_WIDTH, RET_WIDTH, RET_WIDTH, RET_WIDTH,
               ML_WIDTH, ML_WIDTH, ML_WIDTH, ML_WIDTH, 4 * ML_HEADS, N_BRANCH * D_MODEL)
SPLIT_IDX = tuple(int(v) for v in np.cumsum(SPLIT_SIZES)[:-1])
IN_COLS = sum(SPLIT_SIZES)
ML_GATE_OFF = SPLIT_IDX[8]

kernel_name = 'hybrid_s5_retention_mlstm_diffusion_step'


def layer_norm(x, g, b):
    xf = x.astype(jnp.float32)
    mu = jnp.mean(xf, -1, keepdims=True)
    var = jnp.mean(jnp.square(xf - mu), -1, keepdims=True)
    return ((xf - mu) * lax.rsqrt(var + LN_EPS)).astype(x.dtype) * g + b


def head_group_norm(x, g, b):
    mu = jnp.mean(x, -1, keepdims=True)
    var = jnp.mean(jnp.square(x - mu), -1, keepdims=True)
    y = ((x - mu) * lax.rsqrt(var + GN_EPS)).reshape(x.shape[:2] + (-1,))
    return y * g.astype(jnp.float32) + b.astype(jnp.float32)


def grid_rotary(n_tok):
    rows = n_tok // GRID_W
    r = jnp.repeat(jnp.arange(rows, dtype=jnp.float32), GRID_W)
    col = jnp.tile(jnp.arange(GRID_W, dtype=jnp.float32), rows)
    n_freq = RET_HEAD_DIM // 4
    inv = ROPE_BASE ** (-jnp.arange(n_freq, dtype=jnp.float32) / n_freq)
    ang = jnp.concatenate([r[:, None] * inv, col[:, None] * inv], -1)
    return jnp.cos(ang)[None, :, None, :], jnp.sin(ang)[None, :, None, :]


def apply_rotary(x, cos, sin):
    half = x.shape[-1] // 2
    x1, x2 = x[..., :half], x[..., half:]
    return jnp.concatenate([x1 * cos - x2 * sin, x1 * sin + x2 * cos], -1)


def dir_seq(t, d):
    return t if d == 0 else jnp.flip(t, axis=1)


def to_chunks(t):
    b, n, h, d = t.shape
    return t.reshape(b, n // CHUNK, CHUNK, h, d).transpose(1, 0, 3, 2, 4)


def from_chunks(t):
    nc, b, h, L, d = t.shape
    return t.transpose(1, 0, 3, 2, 4).reshape(b, nc * L, h, d)


def s5_discretise(lam_re, lam_im, log_step, b_re, b_im):
    f32 = jnp.float32
    lam = lax.complex(lam_re.astype(f32), lam_im.astype(f32))
    lam_dt = lam * jnp.exp(log_step.astype(f32))[:, None]
    lam_bar = jnp.exp(lam_dt)
    b_mat = lax.complex(b_re.astype(f32), b_im.astype(f32))
    b_bar = ((lam_bar - 1.0) / lam)[..., None] * b_mat
    return lam_dt, lam_bar, b_bar


def s5_combine(e1, e2):
    a1, b1 = e1
    a2, b2 = e2
    return a1 * a2, a2 * b1 + b2


def s5_scan(u, lam_dt, lam_bar, b_bar, c_mat, x0):
    bsz, n = u.shape[:2]
    nc = n // CHUNK
    uc = jnp.moveaxis(u.reshape(bsz, nc, CHUNK, S5_GROUPS, S5_GROUP), 1, 0)
    powers = jnp.exp(lam_dt[None] * jnp.arange(1, CHUNK + 1, dtype=jnp.float32)[:, None, None])
    a = jnp.broadcast_to(lam_bar, (bsz, CHUNK, S5_GROUPS, S5_STATE))

    def step(x_prev, u_j):
        bu = jnp.einsum('gpc,blgc->blgp', b_bar, u_j.astype(jnp.complex64))
        _, xs = lax.associative_scan(s5_combine, (a, bu), axis=1)
        xs = xs + powers[None] * x_prev[:, None]
        y = jnp.einsum('gcp,blgp->blgc', c_mat, xs).real
        return xs[:, -1], y

    x_last, ys = lax.scan(step, x0, uc)
    return jnp.moveaxis(ys, 0, 1).reshape(bsz, n, S5_GROUPS, S5_GROUP), x_last


def retention_scan(q, k, v, log_gamma, s0):
    pos = jnp.arange(CHUNK, dtype=jnp.float32)
    lag = pos[:, None] - pos[None, :]
    decay = jnp.where(lag >= 0, jnp.exp(log_gamma[:, None, None] * jnp.maximum(lag, 0.0)), 0.0)
    q_decay = jnp.exp(log_gamma[:, None] * (pos + 1.0))[..., None]
    k_decay = jnp.exp(log_gamma[:, None] * (CHUNK - 1.0 - pos))[..., None]
    chunk_decay = jnp.exp(log_gamma * CHUNK)[:, None, None]

    def step(s, inp):
        qj, kj, vj = inp
        att = jnp.einsum('bhld,bhmd->bhlm', qj, kj) * decay
        o = jnp.einsum('bhlm,bhme->bhle', att, vj) + jnp.einsum('bhld,bhde->bhle', qj, s) * q_decay
        s_new = s * chunk_decay + jnp.einsum('bhmd,bhme->bhde', kj * k_decay, vj)
        return s_new, o

    s_last, outs = lax.scan(step, s0, (to_chunks(q), to_chunks(k), to_chunks(v)))
    return from_chunks(outs), s_last


def mlstm_scan(q, k, v, log_i, log_f, c0, n0, m0):
    causal = jnp.tril(jnp.ones((CHUNK, CHUNK), dtype=bool))

    def step(carry, inp):
        c, nv, m = carry
        qj, kj, vj, ij, lfj = inp
        bc = jnp.cumsum(lfj, -1)
        log_d = jnp.where(causal, bc[..., :, None] - bc[..., None, :] + ij[..., None, :], -jnp.inf)
        log_prev = bc + m[..., None]
        m_t = jnp.maximum(log_prev, jnp.max(log_d, -1))
        w = jnp.einsum('bhtd,bhsd->bhts', qj, kj) * jnp.exp(log_d - m_t[..., None])
        w_prev = jnp.exp(log_prev - m_t)
        num = jnp.einsum('bhts,bhse->bhte', w, vj) + w_prev[..., None] * jnp.einsum('bhed,bhtd->bhte', c, qj)
        den = jnp.sum(w, -1) + w_prev * jnp.einsum('bhd,bhtd->bht', nv, qj)
        h_t = num / jnp.maximum(jnp.abs(den), jnp.exp(-m_t))[..., None]
        b_last = bc[..., -1]
        log_k = b_last[..., None] - bc + ij
        m_new = jnp.maximum(b_last + m, jnp.max(log_k, -1))
        kw = jnp.exp(log_k - m_new[..., None])
        carry_decay = jnp.exp(b_last + m - m_new)
        c_new = carry_decay[..., None, None] * c + jnp.einsum('bhse,bhsd->bhed', vj * kw[..., None], kj)
        n_new = carry_decay[..., None] * nv + jnp.einsum('bhs,bhsd->bhd', kw, kj)
        return (c_new, n_new, m_new), h_t

    gates = (to_chunks(log_i[..., None])[..., 0], to_chunks(log_f[..., None])[..., 0])
    (c_l, n_l, m_l), hs = lax.scan(step, (c0, n0, m0), (to_chunks(q), to_chunks(k), to_chunks(v)) + gates)
    return from_chunks(hs), (c_l, n_l, m_l)


def token_mixer(h, lp, st, rot):
    f32 = jnp.float32
    bsz, n, _ = h.shape
    proj = h @ lp['w_in'] + lp['b_in']
    (u, rq, rk, rv, rg, mq, mk, mv, mo, mgate, merge) = jnp.split(proj, SPLIT_IDX, axis=-1)

    u32 = u.astype(f32)
    ug = u32.reshape(bsz, n, S5_GROUPS, S5_GROUP)
    s5_y = lp['s5_d'].astype(f32) * u32
    s5_fin = []
    for d in range(2):
        lam_dt, lam_bar, b_bar = s5_discretise(lp['s5_lam_re'][d], lp['s5_lam_im'][d], lp['s5_log_step'][d],
                                               lp['s5_b_re'][d], lp['s5_b_im'][d])
        c_mat = lax.complex(lp['s5_c_re'][d].astype(f32), lp['s5_c_im'][d].astype(f32))
        x0 = lax.complex(st['s5_re'][:, d].astype(f32), st['s5_im'][:, d].astype(f32))
        yd, xl = s5_scan(dir_seq(ug, d), lam_dt, lam_bar, b_bar, c_mat, x0)
        s5_y = s5_y + dir_seq(yd, d).reshape(bsz, n, S5_WIDTH)
        s5_fin.append(xl)
    z = jax.nn.gelu(s5_y).astype(h.dtype)
    ya = z * jax.nn.sigmoid(z @ lp['s5_glu_w'] + lp['s5_glu_b'])

    q = rq.astype(f32).reshape(bsz, n, RET_HEADS, RET_HEAD_DIM)
    k = rk.astype(f32).reshape(bsz, n, RET_HEADS, RET_HEAD_DIM) * RET_HEAD_DIM ** -0.5
    v = rv.astype(f32).reshape(bsz, n, RET_HEADS, RET_HEAD_DIM)
    if rot is not None:
        q = apply_rotary(q, rot[0], rot[1])
        k = apply_rotary(k, rot[0], rot[1])
    log_gamma = -jnp.exp(lp['ret_decay'].astype(f32))
    ret_o = jnp.zeros_like(v)
    ret_fin = []
    for d in range(2):
        od, sl = retention_scan(dir_seq(q, d), dir_seq(k, d), dir_seq(v, d), log_gamma[d],
                                st['ret'][:, d].astype(f32))
        ret_o = ret_o + dir_seq(od, d)
        ret_fin.append(sl)
    yb = (jax.nn.silu(rg.astype(f32)) * head_group_norm(ret_o, lp['ret_gn_g'], lp['ret_gn_b'])).astype(h.dtype)

    q = mq.astype(f32).reshape(bsz, n, ML_HEADS, ML_HEAD_DIM)
    k = mk.astype(f32).reshape(bsz, n, ML_HEADS, ML_HEAD_DIM) * ML_HEAD_DIM ** -0.5
    v = mv.astype(f32).reshape(bsz, n, ML_HEADS, ML_HEAD_DIM)
    g_if = mgate.astype(f32).reshape(bsz, n, 2, 2, ML_HEADS)
    ml_h = jnp.zeros_like(v)
    ml_fin = []
    for d in range(2):
        hd, fin = mlstm_scan(dir_seq(q, d), dir_seq(k, d), dir_seq(v, d),
                             dir_seq(g_if[:, :, d, 0], d), jax.nn.log_sigmoid(dir_seq(g_if[:, :, d, 1], d)),
                             st['ml_c'][:, d].astype(f32), st['ml_n'][:, d].astype(f32),
                             st['ml_m'][:, d].astype(f32))
        ml_h = ml_h + dir_seq(hd, d)
        ml_fin.append(fin)
    o_gate = jax.nn.sigmoid(mo.astype(f32)).reshape(bsz, n, ML_HEADS, ML_HEAD_DIM)
    yc = head_group_norm(o_gate * ml_h, lp['ml_gn_g'], lp['ml_gn_b']).astype(h.dtype)

    gates = jax.nn.sigmoid(merge.reshape(bsz, n, N_BRANCH, D_MODEL))
    merged = (gates[:, :, 0] * (ya @ lp['w_a']) + gates[:, :, 1] * (yb @ lp['w_b'])
              + gates[:, :, 2] * (yc @ lp['w_c']))
    out = merged @ lp['w_o']
    new_st = dict(
        s5_re=jnp.stack([xl.real for xl in s5_fin], 1),
        s5_im=jnp.stack([xl.imag for xl in s5_fin], 1),
        ret=jnp.stack(ret_fin, 1),
        ml_c=jnp.stack([f[0] for f in ml_fin], 1),
        ml_n=jnp.stack([f[1] for f in ml_fin], 1),
        ml_m=jnp.stack([f[2] for f in ml_fin], 1))
    return out, new_st


def dense_swiglu(h, w1, w3, w2):
    return (jax.nn.silu(h @ w1) * (h @ w3)) @ w2


def moe_swiglu(h, router_w, router_b, w1, w3, w2):
    bsz, n, d = h.shape
    t = h.reshape(-1, d)
    logits = (t @ router_w + router_b).astype(jnp.float32)
    top_v, top_i = lax.top_k(logits, TOP_K)
    wts = jax.nn.softmax(top_v, axis=-1)
    gates = jnp.sum(jax.nn.one_hot(top_i, N_EXPERTS, dtype=jnp.float32) * wts[..., None], axis=1)
    out = jnp.zeros_like(t)
    for e in range(N_EXPERTS):
        he = (jax.nn.silu(t @ w1[e]) * (t @ w3[e])) @ w2[e]
        out = out + gates[:, e:e + 1].astype(t.dtype) * he
    return out.reshape(bsz, n, d)


def trunk_layer(x, cond, lp, st, rot, ffn):
    mod = (jax.nn.silu(cond) @ lp['ada_w'] + lp['ada_b'])[:, None, :]
    sh1, sc1, g1, sh2, sc2, g2 = jnp.split(mod, 6, axis=-1)
    mix, new_st = token_mixer(x * (1 + sc1) + sh1, lp, st, rot)
    x = layer_norm(ALPHA * x + g1 * mix, lp['ln1_g'], lp['ln1_b'])
    x = layer_norm(ALPHA * x + g2 * ffn(x * (1 + sc2) + sh2), lp['ln2_g'], lp['ln2_b'])
    return x, new_st


def setup_inputs(seed: int = 0) -> dict:
    key = jax.random.key(seed)
    ks = iter(jax.random.split(key, 64))
    f32 = jnp.float32

    def nrm(shape, scale):
        return scale * jax.random.normal(next(ks), shape, f32)

    x_prompt = nrm((BATCH, SEQ, D_MODEL), 1.0)
    x_sample = nrm((DEC_BATCH, DEC_SEQ, D_MODEL), 1.0)
    cache_s5_re = nrm((DEC_BATCH, DEPTH, 2, S5_GROUPS, S5_STATE), 0.1)
    cache_s5_im = nrm((DEC_BATCH, DEPTH, 2, S5_GROUPS, S5_STATE), 0.1)
    cache_ret = nrm((DEC_BATCH, DEPTH, 2, RET_HEADS, RET_HEAD_DIM, RET_HEAD_DIM), 0.5)
    cache_ml_c = nrm((DEC_BATCH, DEPTH, 2, ML_HEADS, ML_HEAD_DIM, ML_HEAD_DIM), 0.3)
    cache_ml_n = nrm((DEC_BATCH, DEPTH, 2, ML_HEADS, ML_HEAD_DIM), 0.3)
    cache_ml_m = 2.0 + nrm((DEC_BATCH, DEPTH, 2, ML_HEADS), 0.5)
    c = nrm((DEC_BATCH, D_MODEL), 1.0)
    c_ctx = nrm((D_MODEL,), 1.0)
    ada_w = nrm((DEPTH, D_MODEL, 6 * D_MODEL), 0.5 * D_MODEL ** -0.5)
    ada_b = nrm((DEPTH, 6 * D_MODEL), 0.02)
    w_in = nrm((DEPTH, D_MODEL, IN_COLS), D_MODEL ** -0.5)
    b_in = nrm((DEPTH, IN_COLS), 0.01)
    i_bias = nrm((DEPTH, 2, 1, ML_HEADS), 0.1)
    f_bias = jnp.linspace(3.0, 6.0, ML_HEADS, dtype=f32) + nrm((DEPTH, 2, 1, ML_HEADS), 0.01)
    gate_bias = jnp.concatenate([i_bias, f_bias], axis=2).reshape(DEPTH, 4 * ML_HEADS)
    b_in = b_in.at[:, ML_GATE_OFF:ML_GATE_OFF + 4 * ML_HEADS].set(gate_bias)
    s5_shape = (DEPTH, 2, S5_GROUPS, S5_STATE)
    s5_lam_re = -0.5 + nrm(s5_shape, 0.01)
    s5_lam_im = jnp.pi * jnp.arange(S5_STATE, dtype=f32) + nrm(s5_shape, 0.01)
    s5_log_step = jax.random.uniform(next(ks), (DEPTH, 2, S5_GROUPS), f32,
                                     math.log(S5_DT_MIN), math.log(S5_DT_MAX))
    s5_b_re = nrm((DEPTH, 2, S5_GROUPS, S5_STATE, S5_GROUP), (2 * S5_GROUP) ** -0.5)
    s5_b_im = nrm((DEPTH, 2, S5_GROUPS, S5_STATE, S5_GROUP), (2 * S5_GROUP) ** -0.5)
    s5_c_re = nrm((DEPTH, 2, S5_GROUPS, S5_GROUP, S5_STATE), (2 * S5_STATE) ** -0.5)
    s5_c_im = nrm((DEPTH, 2, S5_GROUPS, S5_GROUP, S5_STATE), (2 * S5_STATE) ** -0.5)
    s5_d = nrm((DEPTH, S5_WIDTH), 1.0)
    s5_glu_w = nrm((DEPTH, S5_WIDTH, S5_WIDTH), S5_WIDTH ** -0.5)
    s5_glu_b = nrm((DEPTH, S5_WIDTH), 0.01)
    ret_base = jnp.log(-jnp.log1p(-(2.0 ** (-5.0 - jnp.arange(RET_HEADS, dtype=f32)))))
    ret_decay = ret_base + nrm((DEPTH, 2, RET_HEADS), 0.01)
    ret_gn_g = 1.0 + nrm((DEPTH, RET_WIDTH), 0.01)
    ret_gn_b = nrm((DEPTH, RET_WIDTH), 0.01)
    ml_gn_g = 1.0 + nrm((DEPTH, ML_WIDTH), 0.01)
    ml_gn_b = nrm((DEPTH, ML_WIDTH), 0.01)
    w_a = nrm((DEPTH, S5_WIDTH, D_MODEL), BETA * S5_WIDTH ** -0.5)
    w_b = nrm((DEPTH, RET_WIDTH, D_MODEL), BETA * RET_WIDTH ** -0.5)
    w_c = nrm((DEPTH, ML_WIDTH, D_MODEL), BETA * ML_WIDTH ** -0.5)
    w_o = nrm((DEPTH, D_MODEL, D_MODEL), BETA * D_MODEL ** -0.5)
    ln1_g = 1.0 + nrm((DEPTH, D_MODEL), 0.01)
    ln1_b = nrm((DEPTH, D_MODEL), 0.01)
    ln2_g = 1.0 + nrm((DEPTH, D_MODEL), 0.01)
    ln2_b = nrm((DEPTH, D_MODEL), 0.01)
    ffn_w1 = nrm((N_DENSE, D_MODEL, D_FF), D_MODEL ** -0.5)
    ffn_w3 = nrm((N_DENSE, D_MODEL, D_FF), D_MODEL ** -0.5)
    ffn_w2 = nrm((N_DENSE, D_FF, D_MODEL), BETA * D_FF ** -0.5)
    moe_router = nrm((N_MOE, D_MODEL, N_EXPERTS), D_MODEL ** -0.5)
    moe_router_b = nrm((N_MOE, N_EXPERTS), 0.01)
    moe_w1 = nrm((N_MOE, N_EXPERTS, D_MODEL, MOE_D_FF), D_MODEL ** -0.5)
    moe_w3 = nrm((N_MOE, N_EXPERTS, D_MODEL, MOE_D_FF), D_MODEL ** -0.5)
    moe_w2 = nrm((N_MOE, N_EXPERTS, MOE_D_FF, D_MODEL), BETA * MOE_D_FF ** -0.5)
    return {
        'x_prompt': x_prompt, 'x_sample': x_sample,
        'cache_s5_re': cache_s5_re, 'cache_s5_im': cache_s5_im, 'cache_ret': cache_ret,
        'cache_ml_c': cache_ml_c, 'cache_ml_n': cache_ml_n, 'cache_ml_m': cache_ml_m,
        'c': c, 'c_ctx': c_ctx, 'ada_w': ada_w, 'ada_b': ada_b, 'w_in': w_in, 'b_in': b_in,
        's5_lam_re': s5_lam_re, 's5_lam_im': s5_lam_im, 's5_log_step': s5_log_step,
        's5_b_re': s5_b_re, 's5_b_im': s5_b_im, 's5_c_re': s5_c_re, 's5_c_im': s5_c_im,
        's5_d': s5_d, 's5_glu_w': s5_glu_w, 's5_glu_b': s5_glu_b,
        'ret_decay': ret_decay, 'ret_gn_g': ret_gn_g, 'ret_gn_b': ret_gn_b,
        'ml_gn_g': ml_gn_g, 'ml_gn_b': ml_gn_b,
        'w_a': w_a, 'w_b': w_b, 'w_c': w_c, 'w_o': w_o,
        'ln1_g': ln1_g, 'ln1_b': ln1_b, 'ln2_g': ln2_g, 'ln2_b': ln2_b,
        'ffn_w1': ffn_w1, 'ffn_w3': ffn_w3, 'ffn_w2': ffn_w2,
        'moe_router': moe_router, 'moe_router_b': moe_router_b,
        'moe_w1': moe_w1, 'moe_w3': moe_w3, 'moe_w2': moe_w2,
    }


def reference(x_prompt, x_sample, cache_s5_re, cache_s5_im, cache_ret, cache_ml_c, cache_ml_n, cache_ml_m,
              c, c_ctx, ada_w, ada_b, w_in, b_in, s5_lam_re, s5_lam_im, s5_log_step,
              s5_b_re, s5_b_im, s5_c_re, s5_c_im, s5_d, s5_glu_w, s5_glu_b,
              ret_decay, ret_gn_g, ret_gn_b, ml_gn_g, ml_gn_b, w_a, w_b, w_c, w_o,
              ln1_g, ln1_b, ln2_g, ln2_b, ffn_w1, ffn_w3, ffn_w2,
              moe_router, moe_router_b, moe_w1, moe_w3, moe_w2):
    f32 = jnp.float32
    bp = x_prompt.shape[0]
    zero_st = dict(
        s5_re=jnp.zeros((bp, 2, S5_GROUPS, S5_STATE), f32),
        s5_im=jnp.zeros((bp, 2, S5_GROUPS, S5_STATE), f32),
        ret=jnp.zeros((bp, 2, RET_HEADS, RET_HEAD_DIM, RET_HEAD_DIM), f32),
        ml_c=jnp.zeros((bp, 2, ML_HEADS, ML_HEAD_DIM, ML_HEAD_DIM), f32),
        ml_n=jnp.zeros((bp, 2, ML_HEADS, ML_HEAD_DIM), f32),
        ml_m=jnp.zeros((bp, 2, ML_HEADS), f32))
    names = ('s5_re', 's5_im', 'ret', 'ml_c', 'ml_n', 'ml_m')
    caches = (cache_s5_re, cache_s5_im, cache_ret, cache_ml_c, cache_ml_n, cache_ml_m)
    rot = grid_rotary(x_sample.shape[1])
    ctx_cond = c_ctx[None, :]
    y_p, y_s = x_prompt, x_sample
    ctx_states = {k: [] for k in names}
    for l in range(DEPTH):
        lp = dict(ada_w=ada_w[l], ada_b=ada_b[l], w_in=w_in[l], b_in=b_in[l],
                  s5_lam_re=s5_lam_re[l], s5_lam_im=s5_lam_im[l], s5_log_step=s5_log_step[l],
                  s5_b_re=s5_b_re[l], s5_b_im=s5_b_im[l], s5_c_re=s5_c_re[l], s5_c_im=s5_c_im[l],
                  s5_d=s5_d[l], s5_glu_w=s5_glu_w[l], s5_glu_b=s5_glu_b[l],
                  ret_decay=ret_decay[l], ret_gn_g=ret_gn_g[l], ret_gn_b=ret_gn_b[l],
                  ml_gn_g=ml_gn_g[l], ml_gn_b=ml_gn_b[l],
                  w_a=w_a[l], w_b=w_b[l], w_c=w_c[l], w_o=w_o[l],
                  ln1_g=ln1_g[l], ln1_b=ln1_b[l], ln2_g=ln2_g[l], ln2_b=ln2_b[l])
        j = l // 2
        if l % 2 == 0:
            ffn = functools.partial(dense_swiglu, w1=ffn_w1[j], w3=ffn_w3[j], w2=ffn_w2[j])
        else:
            ffn = functools.partial(moe_swiglu, router_w=moe_router[j], router_b=moe_router_b[j],
                                    w1=moe_w1[j], w3=moe_w3[j], w2=moe_w2[j])
        y_p, st_p = trunk_layer(y_p, ctx_cond, lp, zero_st, None, ffn)
        for k in names:
            ctx_states[k].append(st_p[k])
        lat_st = {k: v[:, l] for k, v in zip(names, caches)}
        y_s, _ = trunk_layer(y_s, c, lp, lat_st, rot, ffn)
    state_s5_re = jnp.stack(ctx_states['s5_re'], axis=1)
    state_s5_im = jnp.stack(ctx_states['s5_im'], axis=1)
    state_ret = jnp.stack(ctx_states['ret'], axis=1)
    state_ml_c = jnp.stack(ctx_states['ml_c'], axis=1)
    state_ml_n = jnp.stack(ctx_states['ml_n'], axis=1)
    state_ml_m = jnp.stack(ctx_states['ml_m'], axis=1)
    return (y_p, y_s, state_s5_re, state_s5_im, state_ret, state_ml_c, state_ml_n, state_ml_m)
```

```python
import functools
import math

import jax
import jax.numpy as jnp
import numpy as np
from jax import lax
from jax.experimental import pallas as pl
from jax.experimental.pallas import tpu as pltpu

F32 = jnp.float32
BF16 = jnp.bfloat16

D_MODEL = 1024
DEPTH = 4
GRID_W = 64
CHUNK = 128
S5_WIDTH = 512
S5_GROUP = 16
S5_GROUPS = 32
S5_STATE = 64
HEADS = 4
HEAD_DIM = 128
WIDTH = 512
ROPE_BASE = 10000.0
D_FF = 2816
N_EXPERTS = 8
ALPHA = (2.0 * DEPTH) ** 0.25
LN_EPS = 1e-5
GN_EPS = 1e-5
N_COND = 16
SUB = 16
LANE = 128
S5_Q = S5_WIDTH // LANE
S5_FLAT = SUB * LANE
S5_ST = 2 * 2 * 8 * S5_STATE
NEG_BIG = -1e30


def _dot(a, b):
    return jnp.dot(a, b, preferred_element_type=F32)


def _dot_hi(a, b):
    return jnp.dot(a, b, preferred_element_type=F32, precision=lax.Precision.HIGHEST)


def _dot_nt(a, b):
    return lax.dot_general(a, b, (((1,), (1,)), ((), ())), preferred_element_type=F32)


def _dot_tn(a, b):
    return lax.dot_general(a, b, (((0,), (0,)), ((), ())), preferred_element_type=F32)


def _params(sem, vmem_mb):
    return pltpu.CompilerParams(dimension_semantics=sem, vmem_limit_bytes=vmem_mb << 20)


def _cond_row(tile, tm, n_ctx_tok, lat_len):
    start = tile * tm
    return jnp.where(start < n_ctx_tok, 0, 1 + (start - n_ctx_tok) // lat_len)


def _mod_kernel(c_ref, w_ref, b_ref, o_ref):
    o_ref[...] = _dot_hi(jax.nn.silu(c_ref[...]), w_ref[...]) + b_ref[...]


def _modulation(cond, ada_w, ada_b):
    tn = 1536
    n = ada_w.shape[-1]
    return pl.pallas_call(
        _mod_kernel,
        grid=(DEPTH, n // tn),
        in_specs=[pl.BlockSpec((N_COND, D_MODEL), lambda l, j: (0, 0)),
                  pl.BlockSpec((None, D_MODEL, tn), lambda l, j: (l, 0, j)),
                  pl.BlockSpec((None, 1, tn), lambda l, j: (l, 0, j))],
        out_specs=pl.BlockSpec((None, N_COND, tn), lambda l, j: (l, 0, j)),
        out_shape=jax.ShapeDtypeStruct((DEPTH, N_COND, n), F32),
        compiler_params=_params(("parallel", "parallel"), 40),
        name="modulation",
    )(cond, ada_w, ada_b.reshape(DEPTH, 1, n))


def _inproj_kernel(x_ref, mod_ref, w_ref, b_ref, wg_ref, bg_ref, o_ref, u4_ref, g_ref, h_scr, u_scr):
    j = pl.program_id(1)
    tm = x_ref.shape[0]

    @pl.when(j == 0)
    def _():
        h = (x_ref[...] * (1.0 + mod_ref[1:2, :]) + mod_ref[0:1, :]).astype(BF16)
        h_scr[...] = h
        g_ref[...] = _dot(h, wg_ref[...]) + bg_ref[...]
        u = _dot(h, w_ref[...]) + b_ref[...]
        for q in range(S5_Q):
            u_scr[q] = u[:, q * LANE:(q + 1) * LANE]
            for t in range(SUB):
                u4_ref[q, :, t * LANE:(t + 1) * LANE] = u_scr[q, pl.ds(t, tm // SUB, stride=SUB), :].astype(BF16)

    @pl.when(j > 0)
    def _():
        o_ref[...] = (_dot(h_scr[...], w_ref[...]) + b_ref[...]).astype(o_ref.dtype)


def _inproj(x, mod_l, w, b, wg, bg, cfg):
    t_tok = x.shape[0]
    tm, tn = cfg["tm_in"], 512
    nj = w.shape[1] // tn
    cond = functools.partial(_cond_row, tm=tm, n_ctx_tok=cfg["n_ctx_tok"], lat_len=cfg["lat_len"])
    return pl.pallas_call(
        _inproj_kernel,
        grid=(t_tok // tm, nj),
        in_specs=[pl.BlockSpec((tm, D_MODEL), lambda i, j: (i, 0)),
                  pl.BlockSpec((None, 6, D_MODEL), lambda i, j: (cond(i), 0, 0)),
                  pl.BlockSpec((D_MODEL, tn), lambda i, j: (0, j)),
                  pl.BlockSpec((1, tn), lambda i, j: (0, j)),
                  pl.BlockSpec((D_MODEL, LANE), lambda i, j: (0, 0)),
                  pl.BlockSpec((1, LANE), lambda i, j: (0, 0))],
        out_specs=[pl.BlockSpec((tm, tn), lambda i, j: (i, jnp.maximum(j - 1, 0))),
                   pl.BlockSpec((S5_Q, tm // SUB, S5_FLAT), lambda i, j: (0, i, 0)),
                   pl.BlockSpec((tm, LANE), lambda i, j: (i, 0))],
        out_shape=[jax.ShapeDtypeStruct((t_tok, (nj - 1) * tn), cfg["proj_dtype"]),
                   jax.ShapeDtypeStruct((S5_Q, t_tok // SUB, S5_FLAT), BF16),
                   jax.ShapeDtypeStruct((t_tok, LANE), F32)],
        scratch_shapes=[pltpu.VMEM((tm, D_MODEL), BF16), pltpu.VMEM((S5_Q, tm, LANE), F32)],
        compiler_params=_params(("parallel", "arbitrary"), 48),
        name="inproj",
    )(x, mod_l, w, b, wg, bg)


def _s5_matrices(lam_re, lam_im, log_step, b_re, b_im, c_re, c_im):
    lam = lax.complex(lam_re.astype(F32), lam_im.astype(F32))
    lam_dt = lam * jnp.exp(log_step.astype(F32))[..., None]
    lam_bar = jnp.exp(lam_dt)
    bbar = ((lam_bar - 1.0) / lam)[..., None] * lax.complex(b_re.astype(F32), b_im.astype(F32))
    cmat = lax.complex(c_re.astype(F32), c_im.astype(F32))
    ks = jnp.arange(SUB + 1, dtype=F32)
    pw = jnp.exp(lam_dt[None] * ks[:, None, None, None])
    kern = jnp.einsum('dgcp,tdgp,dgpe->dgtce', cmat, pw[:SUB], bbar).real
    t_in = jnp.arange(SUB)[:, None]
    t_out = jnp.arange(SUB)[None, :]
    lag_f = t_out - t_in
    kf = jnp.where((lag_f >= 0)[None, :, :, None, None], kern[0][:, jnp.clip(lag_f, 0, SUB - 1)], 0.0)
    kb = jnp.where((lag_f <= 0)[None, :, :, None, None], kern[1][:, jnp.clip(-lag_f, 0, SUB - 1)], 0.0)
    ktot = (kf + kb).reshape(S5_Q, 8, SUB, SUB, S5_GROUP, S5_GROUP)
    eye = jnp.eye(8, dtype=F32)
    m_intra = jnp.einsum('qgtsoi,gh->qtgisho', ktot, eye).reshape(S5_Q, S5_FLAT, S5_FLAT)
    pw_in = jnp.stack([pw[:SUB][::-1, 0], pw[:SUB][:, 1]], 0)
    wb = pw_in[..., None] * bbar[:, None]
    wb = jnp.stack([wb.real, wb.imag], 1).reshape(2, 2, SUB, S5_Q, 8, S5_STATE, S5_GROUP)
    w_in = jnp.einsum('drtqgpc,gh->qtgcdrhp', wb, eye).reshape(S5_Q, S5_FLAT, S5_ST)
    pw_out = jnp.stack([pw[1:, 0], pw[1:][::-1, 1]], 0)
    ce = cmat[:, None] * pw_out[:, :, :, None, :]
    ce = jnp.stack([ce.real, -ce.imag], 1).reshape(2, 2, SUB, S5_Q, 8, S5_GROUP, S5_STATE)
    w_out = jnp.einsum('drtqgcp,gh->qdrgpthc', ce, eye).reshape(S5_Q, S5_ST, S5_FLAT)
    a = pw[SUB]
    a = jnp.stack([a.real, a.imag], 1).reshape(2, 2, S5_Q, 1, 8 * S5_STATE).transpose(2, 0, 1, 3, 4)
    return w_in.astype(BF16), m_intra.astype(BF16), w_out.astype(BF16), a


def _s5a_kernel(u_ref, w_ref, o_ref):
    res = _dot(u_ref[...], w_ref[...])
    for d in range(2):
        for r in range(2):
            k = 2 * d + r
            o_ref[d, r] = res[:, k * 512:(k + 1) * 512]


def _s5_state_in(u4, w_in, cfg):
    rows = u4.shape[1]
    rt = cfg["s5_rows"]
    return pl.pallas_call(
        _s5a_kernel,
        grid=(S5_Q, rows // rt),
        in_specs=[pl.BlockSpec((None, rt, S5_FLAT), lambda q, i: (q, i, 0)),
                  pl.BlockSpec((None, S5_FLAT, S5_ST), lambda q, i: (q, 0, 0))],
        out_specs=pl.BlockSpec((None, 2, 2, rt, 512), lambda q, i: (q, 0, 0, i, 0)),
        out_shape=jax.ShapeDtypeStruct((S5_Q, 2, 2, rows, 512), F32),
        compiler_params=_params(("parallel", "parallel"), 48),
        name="s5_state_in",
    )(u4, w_in)


def _s5b_kernel(loc_ref, a_ref, x0_ref, xp_ref, fin_ref, *, n_ctx_seq, ctx_sub, n_lat_seq, lat_sub):
    d = pl.program_id(1)
    ar = jnp.broadcast_to(a_ref[0], (8, LANE))
    ai = jnp.broadcast_to(a_ref[1], (8, LANE))

    def run(base, nsub, xr0, xi0):
        def body(jj, carry):
            xr, xi = carry
            j = jnp.where(d == 0, jj, nsub - 1 - jj)
            idx = pl.ds(base + j, 8, stride=nsub)
            xp_ref[0, idx, :] = xr
            xp_ref[1, idx, :] = xi
            lr = loc_ref[0, idx, :]
            li = loc_ref[1, idx, :]
            return ar * xr - ai * xi + lr, ar * xi + ai * xr + li
        return lax.fori_loop(0, nsub, body, (xr0, xi0))

    zero = jnp.zeros((8, LANE), F32)
    for bg in range(n_ctx_seq // 8):
        xr, xi = run(bg * 8 * ctx_sub, ctx_sub, zero, zero)
        fin_ref[0, bg * 8:(bg + 1) * 8, :] = xr
        fin_ref[1, bg * 8:(bg + 1) * 8, :] = xi
    for bg in range(n_lat_seq // 8):
        run(n_ctx_seq * ctx_sub + bg * 8 * lat_sub, lat_sub, x0_ref[0, bg * 8:(bg + 1) * 8, :],
            x0_ref[1, bg * 8:(bg + 1) * 8, :])


def _s5_scan(loc, a, x0, cfg):
    rows = loc.shape[3]
    n_ctx_seq, n_lat_seq = cfg["n_ctx_seq"], cfg["n_lat_seq"]
    kern = functools.partial(_s5b_kernel, n_ctx_seq=n_ctx_seq, ctx_sub=cfg["ctx_len"] // SUB,
                             n_lat_seq=n_lat_seq, lat_sub=cfg["lat_len"] // SUB)
    nlb = 512 // LANE
    return pl.pallas_call(
        kern,
        grid=(S5_Q, 2, nlb),
        in_specs=[pl.BlockSpec((None, None, 2, rows, LANE), lambda q, d, b: (q, d, 0, 0, b)),
                  pl.BlockSpec((None, None, 2, 1, LANE), lambda q, d, b: (q, d, 0, 0, b)),
                  pl.BlockSpec((None, None, 2, n_lat_seq, LANE), lambda q, d, b: (q, d, 0, 0, b))],
        out_specs=[pl.BlockSpec((None, None, 2, rows, LANE), lambda q, d, b: (q, d, 0, 0, b)),
                   pl.BlockSpec((None, None, 2, n_ctx_seq, LANE), lambda q, d, b: (q, d, 0, 0, b))],
        out_shape=[jax.ShapeDtypeStruct(loc.shape, F32),
                   jax.ShapeDtypeStruct((S5_Q, 2, 2, n_ctx_seq, 512), F32)],
        compiler_params=_params(("parallel", "parallel", "parallel"), 48),
        name="s5_scan",
    )(loc, a, x0)


def _s5c_kernel(u_ref, xp_ref, m_ref, wo_ref, d_ref, z_ref, z_scr):
    rt = u_ref.shape[0]
    u = u_ref[...]
    xcat = jnp.concatenate([xp_ref[0, 0], xp_ref[0, 1], xp_ref[1, 0], xp_ref[1, 1]], axis=1).astype(BF16)
    y = _dot(u, m_ref[...]) + _dot(xcat, wo_ref[...])
    z = jax.nn.gelu(d_ref[...] * u.astype(F32) + y)
    for t in range(SUB):
        z_scr[pl.ds(t, rt, stride=SUB), :] = z[:, t * LANE:(t + 1) * LANE]
    z_ref[...] = z_scr[...].astype(z_ref.dtype)


def _s5_output(u4, xprev, m_intra, w_out, d4, cfg):
    rows = u4.shape[1]
    rt = cfg["s5_rows"]
    return pl.pallas_call(
        _s5c_kernel,
        grid=(S5_Q, rows // rt),
        in_specs=[pl.BlockSpec((None, rt, S5_FLAT), lambda q, i: (q, i, 0)),
                  pl.BlockSpec((None, 2, 2, rt, 512), lambda q, i: (q, 0, 0, i, 0)),
                  pl.BlockSpec((None, S5_FLAT, S5_FLAT), lambda q, i: (q, 0, 0)),
                  pl.BlockSpec((None, S5_ST, S5_FLAT), lambda q, i: (q, 0, 0)),
                  pl.BlockSpec((None, 1, S5_FLAT), lambda q, i: (q, 0, 0))],
        out_specs=pl.BlockSpec((rt * SUB, LANE), lambda q, i: (i, q)),
        out_shape=jax.ShapeDtypeStruct((rows * SUB, S5_WIDTH), BF16),
        scratch_shapes=[pltpu.VMEM((rt * SUB, LANE), F32)],
        compiler_params=_params(("parallel", "parallel"), 56),
        name="s5_output",
    )(u4, xprev, m_intra, w_out, d4)


def _group_norm(o, g, b):
    mu = jnp.mean(o, axis=-1, keepdims=True)
    var = jnp.mean(jnp.square(o - mu), axis=-1, keepdims=True)
    return (o - mu) * lax.rsqrt(var + GN_EPS) * g + b


def _ret_kernel(dec_ref, q_ref, k_ref, v_ref, g_ref, cos_ref, sin_ref, s0_ref, gg_ref, gb_ref,
                y_ref, sfin_ref, sb_scr, *, nc, use_rot, has_init, want_final):
    h = pl.program_id(1)
    row = lax.broadcasted_iota(jnp.int32, (CHUNK, CHUNK), 0).astype(F32)
    col = lax.broadcasted_iota(jnp.int32, (CHUNK, CHUNK), 1).astype(F32)
    lg_f = -jnp.exp(jnp.full((CHUNK, CHUNK), dec_ref[0, h], F32))
    lg_b = -jnp.exp(jnp.full((CHUNK, CHUNK), dec_ref[1, h], F32))
    lag = row - col
    dmat = (jnp.where(lag >= 0, jnp.exp(lg_f * jnp.maximum(lag, 0.0)), 0.0)
            + jnp.where(lag <= 0, jnp.exp(lg_b * jnp.maximum(-lag, 0.0)), 0.0))
    qd_f = jnp.exp(lg_f * (row + 1.0))
    qd_b = jnp.exp(lg_b * (CHUNK - row))
    kd_f = jnp.exp(lg_f * (CHUNK - 1.0 - row)) * HEAD_DIM ** -0.5
    kd_b = jnp.exp(lg_b * row) * HEAD_DIM ** -0.5
    cd_f = jnp.exp(lg_f * CHUNK)
    cd_b = jnp.exp(lg_b * CHUNK)

    def rot(x, sl):
        if not use_rot:
            return x
        return x * cos_ref[sl, :] + pltpu.roll(x, HEAD_DIM // 2, 1) * sin_ref[sl, :]

    def load(j):
        sl = pl.ds(pl.multiple_of(j * CHUNK, CHUNK), CHUNK)
        return sl, rot(q_ref[sl, :].astype(F32), sl), rot(k_ref[sl, :].astype(F32), sl), v_ref[sl, :].astype(BF16)

    zero = jnp.zeros((CHUNK, CHUNK), F32)

    def bwd(jj, s_b):
        j = nc - 1 - jj
        _, _, k, v = load(j)
        sb_scr[j] = s_b
        return s_b * cd_b + _dot_tn((k * kd_b).astype(BF16), v)

    s_b = lax.fori_loop(0, nc, bwd, s0_ref[1] if has_init else zero)

    def fwd(j, s_f):
        sl, q, k, v = load(j)
        att = _dot_nt(q.astype(BF16), (k * HEAD_DIM ** -0.5).astype(BF16)) * dmat
        o = (_dot(att.astype(BF16), v) + _dot((q * qd_f).astype(BF16), s_f.astype(BF16))
             + _dot((q * qd_b).astype(BF16), sb_scr[j].astype(BF16)))
        y = jax.nn.silu(g_ref[sl, :].astype(F32)) * _group_norm(o, gg_ref[...], gb_ref[...])
        y_ref[sl, :] = y.astype(y_ref.dtype)
        return s_f * cd_f + _dot_tn((k * kd_f).astype(BF16), v)

    s_f = lax.fori_loop(0, nc, fwd, s0_ref[0] if has_init else zero)
    if want_final:
        sfin_ref[0] = s_f
        sfin_ref[1] = s_b


def _retention(proj, dec, cos_t, sin_t, s0, gn_g, gn_b, *, n_seq, seq_len, row0, use_rot, has_init, want_final):
    nc = seq_len // CHUNK
    blk0 = row0 // seq_len
    kern = functools.partial(_ret_kernel, nc=nc, use_rot=use_rot, has_init=has_init, want_final=want_final)
    t_tok = proj.shape[0]

    def tok(cb):
        return pl.BlockSpec((seq_len, HEAD_DIM), lambda s, h, cb=cb: (blk0 + s, cb + h))

    rot_spec = pl.BlockSpec((seq_len, HEAD_DIM), lambda s, h: (0, 0))
    st_spec = pl.BlockSpec((None, 2, None, HEAD_DIM, HEAD_DIM), lambda s, h: (s, 0, h, 0, 0))
    gn_spec = pl.BlockSpec((1, HEAD_DIM), lambda s, h: (0, h))
    y, sfin = pl.pallas_call(
        kern,
        grid=(n_seq, HEADS),
        in_specs=[pl.BlockSpec(memory_space=pltpu.SMEM), tok(0), tok(4), tok(8), tok(12),
                  rot_spec, rot_spec, st_spec, gn_spec, gn_spec],
        out_specs=[pl.BlockSpec((seq_len, HEAD_DIM), lambda s, h: (s, h)), st_spec],
        out_shape=[jax.ShapeDtypeStruct((n_seq * seq_len, WIDTH), BF16),
                   jax.ShapeDtypeStruct((n_seq, 2, HEADS, HEAD_DIM, HEAD_DIM), F32)],
        scratch_shapes=[pltpu.VMEM((nc, HEAD_DIM, HEAD_DIM), F32)],
        compiler_params=_params(("parallel", "parallel"), 48),
        name="retention",
    )(dec, proj, proj, proj, proj, cos_t, sin_t, s0, gn_g, gn_b)
    return y, sfin


def _mlstm_kernel(q_ref, k_ref, v_ref, o_ref, gt_ref, c0_ref, n0_ref, m0_ref, gg_ref, gb_ref,
                  y_ref, cfin_ref, nfin_ref, mfin_ref, h_scr, *, nc, has_init, want_final):
    h = pl.program_id(1)
    row = lax.broadcasted_iota(jnp.int32, (CHUNK, CHUNK), 0)
    col = lax.broadcasted_iota(jnp.int32, (CHUNK, CHUNK), 1)
    tri = (row >= col).astype(F32)
    lane = lax.broadcasted_iota(jnp.int32, (CHUNK, LANE), 1)

    def pick_col(x, c):
        return jnp.sum(jnp.where(lane == c, x, 0.0), axis=1, keepdims=True)

    def pick_row(xt, c):
        return jnp.sum(jnp.where(row == c, xt, 0.0), axis=0, keepdims=True)

    def chunk(j, carry, d):
        cmat, nvec, m = carry
        sl = pl.ds(pl.multiple_of(j * CHUNK, CHUNK), CHUNK)
        q = q_ref[sl, :].astype(BF16)
        kf = k_ref[sl, :].astype(F32) * HEAD_DIM ** -0.5
        k = kf.astype(BF16)
        v = v_ref[sl, :].astype(F32)
        g = gt_ref[sl, :]
        lf = jnp.minimum(g, 0.0) - jnp.log1p(jnp.exp(-jnp.abs(g)))
        cs = _dot_hi(tri, lf)
        if d == 0:
            bc = cs
            rest = cs[CHUNK - 1:CHUNK, :] - cs
            causal = row >= col
        else:
            bc = cs[CHUNK - 1:CHUNK, :] - cs + lf
            rest = cs - lf
            causal = row <= col
        ci = d * 8 + h
        cf = d * 8 + 4 + h
        bc_col = pick_col(bc, cf)
        i_col = pick_col(g, ci)
        rest_col = pick_col(rest, cf)
        bc_row = pick_row(bc.T, cf)
        i_row = pick_row(g.T, ci)
        rest_row = pick_row(rest.T, cf)
        b_last = bc_col[CHUNK - 1:CHUNK, :] if d == 0 else bc_col[0:1, :]
        log_d = jnp.where(causal, bc_col - bc_row + i_row, -jnp.inf)
        log_prev = bc_col + m
        m_t = jnp.maximum(log_prev, jnp.max(log_d, axis=1, keepdims=True))
        w = _dot_nt(q, k) * jnp.exp(log_d - m_t)
        w_prev = jnp.exp(log_prev - m_t)
        qf = q.astype(F32)
        num = _dot(w.astype(BF16), v.astype(BF16)) + w_prev * _dot_nt(q, cmat.astype(BF16))
        den = jnp.sum(w, axis=1, keepdims=True) + w_prev * jnp.sum(qf * nvec, axis=1, keepdims=True)
        h_t = num / jnp.maximum(jnp.abs(den), jnp.exp(-m_t))
        log_k_row = rest_row + i_row
        log_k_col = rest_col + i_col
        m_new = jnp.maximum(b_last + m, jnp.max(log_k_row, axis=1, keepdims=True))
        kw = jnp.exp(log_k_col - m_new)
        decay = jnp.exp(b_last + m - m_new)
        c_new = decay * cmat + _dot_tn((v * kw).astype(BF16), k)
        n_new = decay * nvec + jnp.sum(kw * kf, axis=0, keepdims=True)
        return sl, h_t, (c_new, n_new, m_new)

    def init(d):
        if has_init:
            return c0_ref[d], n0_ref[d], m0_ref[d]
        return jnp.zeros((HEAD_DIM, HEAD_DIM), F32), jnp.zeros((1, HEAD_DIM), F32), jnp.zeros((1, HEAD_DIM), F32)

    def fwd(j, carry):
        sl, h_t, carry = chunk(j, carry, 0)
        h_scr[sl, :] = h_t
        return carry

    fin_f = lax.fori_loop(0, nc, fwd, init(0))

    def bwd(jj, carry):
        sl, h_t, carry = chunk(nc - 1 - jj, carry, 1)
        y = jax.nn.sigmoid(o_ref[sl, :].astype(F32)) * (h_scr[sl, :] + h_t)
        y_ref[sl, :] = _group_norm(y, gg_ref[...], gb_ref[...]).astype(y_ref.dtype)
        return carry

    fin_b = lax.fori_loop(0, nc, bwd, init(1))
    if want_final:
        for d, (c_f, n_f, m_f) in enumerate((fin_f, fin_b)):
            cfin_ref[d] = c_f
            nfin_ref[d] = n_f
            mfin_ref[d] = m_f


def _mlstm(proj, gates, c0, n0, m0, gn_g, gn_b, *, n_seq, seq_len, row0, has_init, want_final):
    nc = seq_len // CHUNK
    blk0 = row0 // seq_len
    kern = functools.partial(_mlstm_kernel, nc=nc, has_init=has_init, want_final=want_final)
    t_tok = proj.shape[0]

    def tok(cb):
        return pl.BlockSpec((seq_len, HEAD_DIM), lambda s, h, cb=cb: (blk0 + s, cb + h))

    c_spec = pl.BlockSpec((None, 2, None, HEAD_DIM, HEAD_DIM), lambda s, h: (s, 0, h, 0, 0))
    v_spec = pl.BlockSpec((None, 2, None, 1, HEAD_DIM), lambda s, h: (s, 0, h, 0, 0))
    gn_spec = pl.BlockSpec((1, HEAD_DIM), lambda s, h: (0, h))
    return pl.pallas_call(
        kern,
        grid=(n_seq, HEADS),
        in_specs=[tok(16), tok(20), tok(24), tok(28),
                  pl.BlockSpec((seq_len, LANE), lambda s, h: (blk0 + s, 0)),
                  c_spec, v_spec, v_spec, gn_spec, gn_spec],
        out_specs=[pl.BlockSpec((seq_len, HEAD_DIM), lambda s, h: (s, h)), c_spec, v_spec, v_spec],
        out_shape=[jax.ShapeDtypeStruct((n_seq * seq_len, WIDTH), BF16),
                   jax.ShapeDtypeStruct((n_seq, 2, HEADS, HEAD_DIM, HEAD_DIM), F32),
                   jax.ShapeDtypeStruct((n_seq, 2, HEADS, 1, HEAD_DIM), F32),
                   jax.ShapeDtypeStruct((n_seq, 2, HEADS, 1, HEAD_DIM), F32)],
        scratch_shapes=[pltpu.VMEM((seq_len, HEAD_DIM), F32)],
        compiler_params=_params(("parallel", "parallel"), 48),
        name="mlstm",
    )(proj, proj, proj, proj, gates, c0, n0, m0, gn_g, gn_b)


def _layer_norm(x, g, b):
    mu = jnp.mean(x, axis=-1, keepdims=True)
    var = jnp.mean(jnp.square(x - mu), axis=-1, keepdims=True)
    return (x - mu) * lax.rsqrt(var + LN_EPS) * g + b


def _merge_kernel(x_ref, mod_ref, z_ref, yb_ref, yc_ref, wm_ref, bm_ref, wglu_ref, bglu_ref,
                  wa_ref, wb_ref, wc_ref, wo_ref, lg_ref, lb_ref, o_ref):
    x = x_ref[...]
    h = (x * (1.0 + mod_ref[1:2, :]) + mod_ref[0:1, :]).astype(BF16)
    z = z_ref[...]
    ya = (z.astype(F32) * jax.nn.sigmoid(_dot(z, wglu_ref[...]) + bglu_ref[...])).astype(BF16)
    merged = None
    for j, (y, w_ref) in enumerate(((ya, wa_ref), (yb_ref[...], wb_ref), (yc_ref[...], wc_ref))):
        gate = jax.nn.sigmoid(_dot(h, wm_ref[:, j * D_MODEL:(j + 1) * D_MODEL]) + bm_ref[:, j * D_MODEL:(j + 1) * D_MODEL])
        term = gate * _dot(y, w_ref[...])
        merged = term if merged is None else merged + term
    mix = _dot(merged.astype(BF16), wo_ref[...])
    o_ref[...] = _layer_norm(ALPHA * x + mod_ref[2:3, :] * mix, lg_ref[...], lb_ref[...])


def _merge(x, mod_l, z, yb, yc, wm, bm, wglu, bglu, wa, wb, wc, wo, lg, lb, cfg):
    t_tok = x.shape[0]
    tm = cfg["tm_merge"]
    cond = functools.partial(_cond_row, tm=tm, n_ctx_tok=cfg["n_ctx_tok"], lat_len=cfg["lat_len"])

    def full(shape):
        return pl.BlockSpec(shape, lambda i: (0,) * len(shape))

    def tok(w):
        return pl.BlockSpec((tm, w), lambda i: (i, 0))

    return pl.pallas_call(
        _merge_kernel,
        grid=(t_tok // tm,),
        in_specs=[tok(D_MODEL), pl.BlockSpec((None, 6, D_MODEL), lambda i: (cond(i), 0, 0)),
                  tok(WIDTH), tok(WIDTH), tok(WIDTH),
                  full((D_MODEL, 3 * D_MODEL)), full((1, 3 * D_MODEL)), full((WIDTH, WIDTH)), full((1, WIDTH)),
                  full((WIDTH, D_MODEL)), full((WIDTH, D_MODEL)), full((WIDTH, D_MODEL)),
                  full((D_MODEL, D_MODEL)), full((1, D_MODEL)), full((1, D_MODEL))],
        out_specs=tok(D_MODEL),
        out_shape=jax.ShapeDtypeStruct((t_tok, D_MODEL), F32),
        compiler_params=_params(("parallel",), 56),
        name="merge",
    )(x, mod_l, z, yb, yc, wm, bm, wglu, bglu, wa, wb, wc, wo, lg, lb)


def _ffn_kernel(x_ref, mod_ref, w1_ref, w3_ref, w2_ref, lg_ref, lb_ref, o_ref, h_scr, acc_scr):
    f = pl.program_id(1)

    @pl.when(f == 0)
    def _():
        h_scr[...] = (x_ref[...] * (1.0 + mod_ref[4:5, :]) + mod_ref[3:4, :]).astype(BF16)
        acc_scr[...] = jnp.zeros_like(acc_scr)

    h = h_scr[...]
    act = (jax.nn.silu(_dot(h, w1_ref[...])) * _dot(h, w3_ref[...])).astype(BF16)
    acc_scr[...] += _dot(act, w2_ref[...])

    @pl.when(f == pl.num_programs(1) - 1)
    def _():
        o_ref[...] = _layer_norm(ALPHA * x_ref[...] + mod_ref[5:6, :] * acc_scr[...], lg_ref[...], lb_ref[...])


def _dense_ffn(x, mod_l, w1, w3, w2, lg, lb, cfg):
    t_tok = x.shape[0]
    tm, tf = cfg["tm_ffn"], cfg["tf"]
    cond = functools.partial(_cond_row, tm=tm, n_ctx_tok=cfg["n_ctx_tok"], lat_len=cfg["lat_len"])
    return pl.pallas_call(
        _ffn_kernel,
        grid=(t_tok // tm, D_FF // tf),
        in_specs=[pl.BlockSpec((tm, D_MODEL), lambda i, f: (i, 0)),
                  pl.BlockSpec((None, 6, D_MODEL), lambda i, f: (cond(i), 0, 0)),
                  pl.BlockSpec((D_MODEL, tf), lambda i, f: (0, f)),
                  pl.BlockSpec((D_MODEL, tf), lambda i, f: (0, f)),
                  pl.BlockSpec((tf, D_MODEL), lambda i, f: (f, 0)),
                  pl.BlockSpec((1, D_MODEL), lambda i, f: (0, 0)),
                  pl.BlockSpec((1, D_MODEL), lambda i, f: (0, 0))],
        out_specs=pl.BlockSpec((tm, D_MODEL), lambda i, f: (i, 0)),
        out_shape=jax.ShapeDtypeStruct((t_tok, D_MODEL), F32),
        scratch_shapes=[pltpu.VMEM((tm, D_MODEL), BF16), pltpu.VMEM((tm, D_MODEL), F32)],
        compiler_params=_params(("parallel", "arbitrary"), 56),
        name="dense_ffn",
    )(x, mod_l, w1, w3, w2, lg, lb)


def _router_kernel(x_ref, mod_ref, w_ref, b_ref, g_ref):
    h = x_ref[...] * (1.0 + mod_ref[4:5, :]) + mod_ref[3:4, :]
    lane = lax.broadcasted_iota(jnp.int32, g_ref.shape, 1)
    logits = jnp.where(lane < N_EXPERTS, _dot_hi(h, w_ref[...]) + b_ref[...], NEG_BIG)
    m1 = jnp.max(logits, axis=1, keepdims=True)
    i1 = jnp.min(jnp.where(logits == m1, lane, LANE), axis=1, keepdims=True)
    rest = jnp.where(lane == i1, NEG_BIG, logits)
    m2 = jnp.max(rest, axis=1, keepdims=True)
    i2 = jnp.min(jnp.where(rest == m2, lane, LANE), axis=1, keepdims=True)
    e2 = jnp.exp(m2 - m1)
    den = 1.0 + e2
    g_ref[...] = jnp.where(lane == i1, 1.0 / den, 0.0) + jnp.where(lane == i2, e2 / den, 0.0)


def _router(x, mod_l, w, b, cfg):
    t_tok = x.shape[0]
    tm = cfg["tm_ffn"]
    cond = functools.partial(_cond_row, tm=tm, n_ctx_tok=cfg["n_ctx_tok"], lat_len=cfg["lat_len"])
    return pl.pallas_call(
        _router_kernel,
        grid=(t_tok // tm,),
        in_specs=[pl.BlockSpec((tm, D_MODEL), lambda i: (i, 0)),
                  pl.BlockSpec((None, 6, D_MODEL), lambda i: (cond(i), 0, 0)),
                  pl.BlockSpec((D_MODEL, LANE), lambda i: (0, 0)),
                  pl.BlockSpec((1, LANE), lambda i: (0, 0))],
        out_specs=pl.BlockSpec((tm, LANE), lambda i: (i, 0)),
        out_shape=jax.ShapeDtypeStruct((t_tok, LANE), F32),
        compiler_params=_params(("parallel",), 40),
        name="router",
    )(x, mod_l, w, b)


def _moe_kernel(x_ref, mod_ref, gate_ref, w1_ref, w3_ref, w2_ref, lg_ref, lb_ref, o_ref, h_scr, acc_scr):
    e = pl.program_id(1)
    f = pl.program_id(2)

    @pl.when((e == 0) & (f == 0))
    def _():
        h_scr[...] = (x_ref[...] * (1.0 + mod_ref[4:5, :]) + mod_ref[3:4, :]).astype(BF16)
        acc_scr[...] = jnp.zeros_like(acc_scr)

    lane = lax.broadcasted_iota(jnp.int32, gate_ref.shape, 1)
    gate = jnp.sum(jnp.where(lane == e, gate_ref[...], 0.0), axis=1, keepdims=True)
    h = h_scr[...]
    act = (jax.nn.silu(_dot(h, w1_ref[...])) * _dot(h, w3_ref[...])).astype(BF16)
    acc_scr[...] += gate * _dot(act, w2_ref[...])

    @pl.when((e == pl.num_programs(1) - 1) & (f == pl.num_programs(2) - 1))
    def _():
        o_ref[...] = _layer_norm(ALPHA * x_ref[...] + mod_ref[5:6, :] * acc_scr[...], lg_ref[...], lb_ref[...])


def _moe_ffn(x, mod_l, gates, w1, w3, w2, lg, lb, cfg):
    t_tok = x.shape[0]
    tm, tf = cfg["tm_ffn"], cfg["tf"]
    cond = functools.partial(_cond_row, tm=tm, n_ctx_tok=cfg["n_ctx_tok"], lat_len=cfg["lat_len"])
    return pl.pallas_call(
        _moe_kernel,
        grid=(t_tok // tm, N_EXPERTS, D_FF // tf),
        in_specs=[pl.BlockSpec((tm, D_MODEL), lambda i, e, f: (i, 0)),
                  pl.BlockSpec((None, 6, D_MODEL), lambda i, e, f: (cond(i), 0, 0)),
                  pl.BlockSpec((tm, LANE), lambda i, e, f: (i, 0)),
                  pl.BlockSpec((None, D_MODEL, tf), lambda i, e, f: (e, 0, f)),
                  pl.BlockSpec((None, D_MODEL, tf), lambda i, e, f: (e, 0, f)),
                  pl.BlockSpec((None, tf, D_MODEL), lambda i, e, f: (e, f, 0)),
                  pl.BlockSpec((1, D_MODEL), lambda i, e, f: (0, 0)),
                  pl.BlockSpec((1, D_MODEL), lambda i, e, f: (0, 0))],
        out_specs=pl.BlockSpec((tm, D_MODEL), lambda i, e, f: (i, 0)),
        out_shape=jax.ShapeDtypeStruct((t_tok, D_MODEL), F32),
        scratch_shapes=[pltpu.VMEM((tm, D_MODEL), BF16), pltpu.VMEM((tm, D_MODEL), F32)],
        compiler_params=_params(("parallel", "arbitrary", "arbitrary"), 56),
        name="moe_ffn",
    )(x, mod_l, gates, w1, w3, w2, lg, lb)


def _rotary_tables(n_tok):
    rows = n_tok // GRID_W
    r = jnp.repeat(jnp.arange(rows, dtype=F32), GRID_W)
    col = jnp.tile(jnp.arange(GRID_W, dtype=F32), rows)
    n_freq = HEAD_DIM // 4
    inv = ROPE_BASE ** (-jnp.arange(n_freq, dtype=F32) / n_freq)
    ang = jnp.concatenate([r[:, None] * inv, col[:, None] * inv], -1)
    cos, sin = jnp.cos(ang), jnp.sin(ang)
    return jnp.concatenate([cos, cos], -1), jnp.concatenate([-sin, sin], -1)


def _row2(v):
    return v.reshape(1, -1).astype(F32)


def kernel(x_prompt, x_sample, cache_s5_re, cache_s5_im, cache_ret, cache_ml_c, cache_ml_n, cache_ml_m, c, c_ctx, ada_w, ada_b, w_in, b_in, s5_lam_re, s5_lam_im, s5_log_step, s5_b_re, s5_b_im, s5_c_re, s5_c_im, s5_d, s5_glu_w, s5_glu_b, ret_decay, ret_gn_g, ret_gn_b, ml_gn_g, ml_gn_b, w_a, w_b, w_c, w_o, ln1_g, ln1_b, ln2_g, ln2_b, ffn_w1, ffn_w3, ffn_w2, moe_router, moe_router_b, moe_w1, moe_w3, moe_w2):
    n_ctx_seq, ctx_len, _ = x_prompt.shape
    n_lat_seq, lat_len, _ = x_sample.shape
    n_ctx_tok = n_ctx_seq * ctx_len
    n_lat_tok = n_lat_seq * lat_len
    t_tok = n_ctx_tok + n_lat_tok
    assert n_lat_seq + 1 <= N_COND and n_ctx_seq % 8 == 0 and n_lat_seq % 8 == 0
    assert ctx_len % CHUNK == 0 and lat_len % CHUNK == 0 and n_ctx_tok % lat_len == 0
    tm = min(1024, lat_len)
    cfg = dict(n_ctx_seq=n_ctx_seq, ctx_len=ctx_len, n_lat_seq=n_lat_seq, lat_len=lat_len, n_ctx_tok=n_ctx_tok,
               tm_in=tm, tm_merge=min(256, lat_len), tm_ffn=min(512, lat_len), tf=1408,
               s5_rows=min(256, t_tok // SUB // 8), proj_dtype=F32)

    x = jnp.concatenate([x_prompt.reshape(n_ctx_tok, D_MODEL), x_sample.reshape(n_lat_tok, D_MODEL)], 0)
    cond = jnp.zeros((N_COND, D_MODEL), F32).at[0].set(c_ctx).at[1:1 + n_lat_seq].set(c)
    mod = _modulation(cond, ada_w, ada_b).reshape(DEPTH, N_COND, 6, D_MODEL)
    cos_t, sin_t = _rotary_tables(lat_len)

    n_main = S5_WIDTH + 8 * WIDTH
    gate_off = n_main
    merge_off = gate_off + 4 * HEADS
    s5_mats = jax.vmap(_s5_matrices)(s5_lam_re, s5_lam_im, s5_log_step, s5_b_re, s5_b_im, s5_c_re, s5_c_im)

    st_s5, st_ret, st_c, st_n, st_m = [], [], [], [], []
    zero_ret = jnp.zeros((n_ctx_seq, 2, HEADS, HEAD_DIM, HEAD_DIM), F32)
    zero_vec = jnp.zeros((n_ctx_seq, 2, HEADS, 1, HEAD_DIM), F32)
    for l in range(DEPTH):
        mod_l = mod[l]
        w_main = w_in[l][:, :n_main].astype(BF16)
        b_main = _row2(b_in[l][:n_main])
        w_gate = jnp.zeros((D_MODEL, LANE), F32).at[:, :4 * HEADS].set(w_in[l][:, gate_off:merge_off]).astype(BF16)
        b_gate = jnp.zeros((1, LANE), F32).at[0, :4 * HEADS].set(b_in[l][gate_off:merge_off])
        proj, u4, gates = _inproj(x, mod_l, w_main, b_main, w_gate, b_gate, cfg)

        s5_win, s5_mi, s5_wout, s5_a = (m[l] for m in s5_mats)
        loc = _s5_state_in(u4, s5_win, cfg)
        x0 = jnp.stack([cache_s5_re[:, l], cache_s5_im[:, l]], 0)
        x0 = x0.reshape(2, n_lat_seq, 2, S5_Q, 8 * S5_STATE).transpose(3, 2, 0, 1, 4).astype(F32)
        xprev, s5_fin = _s5_scan(loc, s5_a, x0, cfg)
        d4 = jnp.tile(s5_d[l].astype(F32).reshape(S5_Q, 1, LANE), (1, SUB, 1)).reshape(S5_Q, 1, S5_FLAT)
        z = _s5_output(u4, xprev, s5_mi, s5_wout, d4, cfg)
        st_s5.append(s5_fin)

        gg, gb = _row2(ret_gn_g[l]), _row2(ret_gn_b[l])
        dec = ret_decay[l].astype(F32)
        yb_c, ret_fin = _retention(proj, dec, cos_t, sin_t, zero_ret, gg, gb, n_seq=n_ctx_seq, seq_len=ctx_len,
                                   row0=0, use_rot=False, has_init=False, want_final=True)
        yb_s, _ = _retention(proj, dec, cos_t, sin_t, cache_ret[:, l].astype(F32), gg, gb, n_seq=n_lat_seq,
                             seq_len=lat_len, row0=n_ctx_tok, use_rot=True, has_init=True, want_final=False)
        yb = jnp.concatenate([yb_c, yb_s], 0)
        st_ret.append(ret_fin)

        gg, gb = _row2(ml_gn_g[l]), _row2(ml_gn_b[l])
        yc_c, c_fin, n_fin, m_fin = _mlstm(proj, gates, zero_ret, zero_vec, zero_vec, gg, gb, n_seq=n_ctx_seq,
                                           seq_len=ctx_len, row0=0, has_init=False, want_final=True)
        n0 = cache_ml_n[:, l].astype(F32)[:, :, :, None, :]
        m0 = jnp.broadcast_to(cache_ml_m[:, l].astype(F32)[:, :, :, None, None], n0.shape)
        yc_s = _mlstm(proj, gates, cache_ml_c[:, l].astype(F32), n0, m0, gg, gb, n_seq=n_lat_seq,
                      seq_len=lat_len, row0=n_ctx_tok, has_init=True, want_final=False)[0]
        yc = jnp.concatenate([yc_c, yc_s], 0)
        st_c.append(c_fin)
        st_n.append(n_fin[:, :, :, 0, :])
        st_m.append(m_fin[:, :, :, 0, 0])

        x = _merge(x, mod_l, z, yb, yc, w_in[l][:, merge_off:].astype(BF16), _row2(b_in[l][merge_off:]),
                   s5_glu_w[l].astype(BF16), _row2(s5_glu_b[l]), w_a[l].astype(BF16), w_b[l].astype(BF16),
                   w_c[l].astype(BF16), w_o[l].astype(BF16), _row2(ln1_g[l]), _row2(ln1_b[l]), cfg)

        j = l // 2
        if l % 2 == 0:
            x = _dense_ffn(x, mod_l, ffn_w1[j].astype(BF16), ffn_w3[j].astype(BF16), ffn_w2[j].astype(BF16),
                           _row2(ln2_g[l]), _row2(ln2_b[l]), cfg)
        else:
            rw = jnp.zeros((D_MODEL, LANE), F32).at[:, :N_EXPERTS].set(moe_router[j])
            rb = jnp.zeros((1, LANE), F32).at[0, :N_EXPERTS].set(moe_router_b[j])
            mg = _router(x, mod_l, rw, rb, cfg)
            x = _moe_ffn(x, mod_l, mg, moe_w1[j].astype(BF16), moe_w3[j].astype(BF16), moe_w2[j].astype(BF16),
                         _row2(ln2_g[l]), _row2(ln2_b[l]), cfg)

    y_p = x[:n_ctx_tok].reshape(n_ctx_seq, ctx_len, D_MODEL)
    y_s = x[n_ctx_tok:].reshape(n_lat_seq, lat_len, D_MODEL)
    s5 = jnp.stack(st_s5, 0)
    s5 = s5.reshape(DEPTH, S5_Q, 2, 2, n_ctx_seq, 8, S5_STATE).transpose(3, 4, 0, 2, 1, 5, 6)
    s5 = s5.reshape(2, n_ctx_seq, DEPTH, 2, S5_GROUPS, S5_STATE)
    return (y_p, y_s, s5[0], s5[1], jnp.stack(st_ret, 1), jnp.stack(st_c, 1), jnp.stack(st_n, 1),
            jnp.stack(st_m, 1))
```

```python
import functools
import math

import jax
import jax.numpy as jnp
import numpy as np
from jax import lax
from jax.experimental import pallas as pl
from jax.experimental.pallas import tpu as pltpu

F32 = jnp.float32
BF16 = jnp.bfloat16

D_MODEL = 1024
DEPTH = 4
GRID_W = 64
CHUNK = 128
S5_WIDTH = 512
S5_GROUP = 16
S5_GROUPS = 32
S5_STATE = 64
HEADS = 4
HEAD_DIM = 128
WIDTH = 512
ROPE_BASE = 10000.0
D_FF = 2816
N_EXPERTS = 8
ALPHA = (2.0 * DEPTH) ** 0.25
LN_EPS = 1e-5
GN_EPS = 1e-5
N_COND = 16
SUB = 16
LANE = 128
S5_Q = S5_WIDTH // LANE
S5_FLAT = SUB * LANE
S5_ST = 2 * 2 * 8 * S5_STATE
NEG_BIG = -1e30
MOE_BLK = 128


def _dot(a, b):
    return jnp.dot(a, b, preferred_element_type=F32)


def _dot_hi(a, b):
    return jnp.dot(a, b, preferred_element_type=F32, precision=lax.Precision.HIGHEST)


def _dot_nt(a, b):
    return lax.dot_general(a, b, (((1,), (1,)), ((), ())), preferred_element_type=F32)


def _dot_tn(a, b):
    return lax.dot_general(a, b, (((0,), (0,)), ((), ())), preferred_element_type=F32)


def _params(sem, vmem_mb):
    return pltpu.CompilerParams(dimension_semantics=sem, vmem_limit_bytes=vmem_mb << 20)


def _cond_row(tile, tm, n_ctx_tok, lat_len):
    start = tile * tm
    return jnp.where(start < n_ctx_tok, 0, 1 + (start - n_ctx_tok) // lat_len)


def _mod_kernel(c_ref, w_ref, b_ref, o_ref):
    o_ref[...] = _dot_hi(jax.nn.silu(c_ref[...]), w_ref[...]) + b_ref[...]


def _modulation(cond, ada_w, ada_b):
    tn = 1536
    n = ada_w.shape[-1]
    return pl.pallas_call(
        _mod_kernel,
        grid=(DEPTH, n // tn),
        in_specs=[pl.BlockSpec((N_COND, D_MODEL), lambda l, j: (0, 0)),
                  pl.BlockSpec((None, D_MODEL, tn), lambda l, j: (l, 0, j)),
                  pl.BlockSpec((None, 1, tn), lambda l, j: (l, 0, j))],
        out_specs=pl.BlockSpec((None, N_COND, tn), lambda l, j: (l, 0, j)),
        out_shape=jax.ShapeDtypeStruct((DEPTH, N_COND, n), F32),
        compiler_params=_params(("parallel", "parallel"), 40),
        name="modulation",
    )(cond, ada_w, ada_b.reshape(DEPTH, 1, n))


def _inproj_kernel(x_ref, mod_ref, w_ref, b_ref, wg_ref, bg_ref, o_ref, u4_ref, g_ref, h_scr, u_scr):
    j = pl.program_id(1)
    tm = x_ref.shape[0]

    @pl.when(j == 0)
    def _():
        h = (x_ref[...] * (1.0 + mod_ref[1:2, :]) + mod_ref[0:1, :]).astype(BF16)
        h_scr[...] = h
        g_ref[...] = _dot(h, wg_ref[...]) + bg_ref[...]
        u = _dot(h, w_ref[...]) + b_ref[...]
        for q in range(S5_Q):
            u_scr[q] = u[:, q * LANE:(q + 1) * LANE]
            for t in range(SUB):
                u4_ref[q, :, t * LANE:(t + 1) * LANE] = u_scr[q, pl.ds(t, tm // SUB, stride=SUB), :].astype(BF16)

    @pl.when(j > 0)
    def _():
        o_ref[...] = (_dot(h_scr[...], w_ref[...]) + b_ref[...]).astype(o_ref.dtype)


def _inproj(x, mod_l, w, b, wg, bg, cfg):
    t_tok = x.shape[0]
    tm, tn = cfg["tm_in"], 512
    nj = w.shape[1] // tn
    cond = functools.partial(_cond_row, tm=tm, n_ctx_tok=cfg["n_ctx_tok"], lat_len=cfg["lat_len"])
    return pl.pallas_call(
        _inproj_kernel,
        grid=(t_tok // tm, nj),
        in_specs=[pl.BlockSpec((tm, D_MODEL), lambda i, j: (i, 0)),
                  pl.BlockSpec((None, 6, D_MODEL), lambda i, j: (cond(i), 0, 0)),
                  pl.BlockSpec((D_MODEL, tn), lambda i, j: (0, j)),
                  pl.BlockSpec((1, tn), lambda i, j: (0, j)),
                  pl.BlockSpec((D_MODEL, LANE), lambda i, j: (0, 0)),
                  pl.BlockSpec((1, LANE), lambda i, j: (0, 0))],
        out_specs=[pl.BlockSpec((tm, tn), lambda i, j: (i, jnp.maximum(j - 1, 0))),
                   pl.BlockSpec((S5_Q, tm // SUB, S5_FLAT), lambda i, j: (0, i, 0)),
                   pl.BlockSpec((tm, LANE), lambda i, j: (i, 0))],
        out_shape=[jax.ShapeDtypeStruct((t_tok, (nj - 1) * tn), cfg["proj_dtype"]),
                   jax.ShapeDtypeStruct((S5_Q, t_tok // SUB, S5_FLAT), BF16),
                   jax.ShapeDtypeStruct((t_tok, LANE), F32)],
        scratch_shapes=[pltpu.VMEM((tm, D_MODEL), BF16), pltpu.VMEM((S5_Q, tm, LANE), F32)],
        compiler_params=_params(("parallel", "arbitrary"), 48),
        name="inproj",
    )(x, mod_l, w, b, wg, bg)


def _s5_matrices(lam_re, lam_im, log_step, b_re, b_im, c_re, c_im):
    lam = lax.complex(lam_re.astype(F32), lam_im.astype(F32))
    lam_dt = lam * jnp.exp(log_step.astype(F32))[..., None]
    lam_bar = jnp.exp(lam_dt)
    bbar = ((lam_bar - 1.0) / lam)[..., None] * lax.complex(b_re.astype(F32), b_im.astype(F32))
    cmat = lax.complex(c_re.astype(F32), c_im.astype(F32))
    ks = jnp.arange(SUB + 1, dtype=F32)
    pw = jnp.exp(lam_dt[None] * ks[:, None, None, None])
    kern = jnp.einsum('dgcp,tdgp,dgpe->dgtce', cmat, pw[:SUB], bbar).real
    t_in = jnp.arange(SUB)[:, None]
    t_out = jnp.arange(SUB)[None, :]
    lag_f = t_out - t_in
    kf = jnp.where((lag_f >= 0)[None, :, :, None, None], kern[0][:, jnp.clip(lag_f, 0, SUB - 1)], 0.0)
    kb = jnp.where((lag_f <= 0)[None, :, :, None, None], kern[1][:, jnp.clip(-lag_f, 0, SUB - 1)], 0.0)
    ktot = (kf + kb).reshape(S5_Q, 8, SUB, SUB, S5_GROUP, S5_GROUP)
    eye = jnp.eye(8, dtype=F32)
    m_intra = jnp.einsum('qgtsoi,gh->qtgisho', ktot, eye).reshape(S5_Q, S5_FLAT, S5_FLAT)
    pw_in = jnp.stack([pw[:SUB][::-1, 0], pw[:SUB][:, 1]], 0)
    wb = pw_in[..., None] * bbar[:, None]
    wb = jnp.stack([wb.real, wb.imag], 1).reshape(2, 2, SUB, S5_Q, 8, S5_STATE, S5_GROUP)
    w_in = jnp.einsum('drtqgpc,gh->qtgcdrhp', wb, eye).reshape(S5_Q, S5_FLAT, S5_ST)
    pw_out = jnp.stack([pw[1:, 0], pw[1:][::-1, 1]], 0)
    ce = cmat[:, None] * pw_out[:, :, :, None, :]
    ce = jnp.stack([ce.real, -ce.imag], 1).reshape(2, 2, SUB, S5_Q, 8, S5_GROUP, S5_STATE)
    w_out = jnp.einsum('drtqgcp,gh->qdrgpthc', ce, eye).reshape(S5_Q, S5_ST, S5_FLAT)
    a = pw[SUB]
    a = jnp.stack([a.real, a.imag], 1).reshape(2, 2, S5_Q, 1, 8 * S5_STATE).transpose(2, 0, 1, 3, 4)
    return w_in.astype(BF16), m_intra.astype(BF16), w_out.astype(BF16), a


def _s5a_kernel(u_ref, w_ref, o_ref):
    res = _dot(u_ref[...], w_ref[...])
    for d in range(2):
        for r in range(2):
            k = 2 * d + r
            o_ref[d, r] = res[:, k * 512:(k + 1) * 512]


def _s5_state_in(u4, w_in, cfg):
    rows = u4.shape[1]
    rt = cfg["s5_rows"]
    return pl.pallas_call(
        _s5a_kernel,
        grid=(S5_Q, rows // rt),
        in_specs=[pl.BlockSpec((None, rt, S5_FLAT), lambda q, i: (q, i, 0)),
                  pl.BlockSpec((None, S5_FLAT, S5_ST), lambda q, i: (q, 0, 0))],
        out_specs=pl.BlockSpec((None, 2, 2, rt, 512), lambda q, i: (q, 0, 0, i, 0)),
        out_shape=jax.ShapeDtypeStruct((S5_Q, 2, 2, rows, 512), F32),
        compiler_params=_params(("parallel", "parallel"), 48),
        name="s5_state_in",
    )(u4, w_in)


def _s5b_kernel(loc_ref, a_ref, x0_ref, xp_ref, fin_ref, *, n_ctx_seq, ctx_sub, n_lat_seq, lat_sub):
    d = pl.program_id(1)
    ar = jnp.broadcast_to(a_ref[0], (8, LANE))
    ai = jnp.broadcast_to(a_ref[1], (8, LANE))

    def run(base, nsub, xr0, xi0):
        def body(jj, carry):
            xr, xi = carry
            j = jnp.where(d == 0, jj, nsub - 1 - jj)
            idx = pl.ds(base + j, 8, stride=nsub)
            xp_ref[0, idx, :] = xr
            xp_ref[1, idx, :] = xi
            lr = loc_ref[0, idx, :]
            li = loc_ref[1, idx, :]
            return ar * xr - ai * xi + lr, ar * xi + ai * xr + li
        return lax.fori_loop(0, nsub, body, (xr0, xi0))

    zero = jnp.zeros((8, LANE), F32)
    for bg in range(n_ctx_seq // 8):
        xr, xi = run(bg * 8 * ctx_sub, ctx_sub, zero, zero)
        fin_ref[0, bg * 8:(bg + 1) * 8, :] = xr
        fin_ref[1, bg * 8:(bg + 1) * 8, :] = xi
    for bg in range(n_lat_seq // 8):
        run(n_ctx_seq * ctx_sub + bg * 8 * lat_sub, lat_sub, x0_ref[0, bg * 8:(bg + 1) * 8, :],
            x0_ref[1, bg * 8:(bg + 1) * 8, :])


def _s5_scan(loc, a, x0, cfg):
    rows = loc.shape[3]
    n_ctx_seq, n_lat_seq = cfg["n_ctx_seq"], cfg["n_lat_seq"]
    kern = functools.partial(_s5b_kernel, n_ctx_seq=n_ctx_seq, ctx_sub=cfg["ctx_len"] // SUB,
                             n_lat_seq=n_lat_seq, lat_sub=cfg["lat_len"] // SUB)
    nlb = 512 // LANE
    return pl.pallas_call(
        kern,
        grid=(S5_Q, 2, nlb),
        in_specs=[pl.BlockSpec((None, None, 2, rows, LANE), lambda q, d, b: (q, d, 0, 0, b)),
                  pl.BlockSpec((None, None, 2, 1, LANE), lambda q, d, b: (q, d, 0, 0, b)),
                  pl.BlockSpec((None, None, 2, n_lat_seq, LANE), lambda q, d, b: (q, d, 0, 0, b))],
        out_specs=[pl.BlockSpec((None, None, 2, rows, LANE), lambda q, d, b: (q, d, 0, 0, b)),
                   pl.BlockSpec((None, None, 2, n_ctx_seq, LANE), lambda q, d, b: (q, d, 0, 0, b))],
        out_shape=[jax.ShapeDtypeStruct(loc.shape, F32),
                   jax.ShapeDtypeStruct((S5_Q, 2, 2, n_ctx_seq, 512), F32)],
        compiler_params=_params(("parallel", "parallel", "parallel"), 48),
        name="s5_scan",
    )(loc, a, x0)


def _s5c_kernel(u_ref, xp_ref, m_ref, wo_ref, d_ref, z_ref, z_scr):
    rt = u_ref.shape[0]
    u = u_ref[...]
    xcat = jnp.concatenate([xp_ref[0, 0], xp_ref[0, 1], xp_ref[1, 0], xp_ref[1, 1]], axis=1).astype(BF16)
    y = _dot(u, m_ref[...]) + _dot(xcat, wo_ref[...])
    z = jax.nn.gelu(d_ref[...] * u.astype(F32) + y)
    for t in range(SUB):
        z_scr[pl.ds(t, rt, stride=SUB), :] = z[:, t * LANE:(t + 1) * LANE]
    z_ref[...] = z_scr[...].astype(z_ref.dtype)


def _s5_output(u4, xprev, m_intra, w_out, d4, cfg):
    rows = u4.shape[1]
    rt = cfg["s5_rows"]
    return pl.pallas_call(
        _s5c_kernel,
        grid=(S5_Q, rows // rt),
        in_specs=[pl.BlockSpec((None, rt, S5_FLAT), lambda q, i: (q, i, 0)),
                  pl.BlockSpec((None, 2, 2, rt, 512), lambda q, i: (q, 0, 0, i, 0)),
                  pl.BlockSpec((None, S5_FLAT, S5_FLAT), lambda q, i: (q, 0, 0)),
                  pl.BlockSpec((None, S5_ST, S5_FLAT), lambda q, i: (q, 0, 0)),
                  pl.BlockSpec((None, 1, S5_FLAT), lambda q, i: (q, 0, 0))],
        out_specs=pl.BlockSpec((rt * SUB, LANE), lambda q, i: (i, q)),
        out_shape=jax.ShapeDtypeStruct((rows * SUB, S5_WIDTH), BF16),
        scratch_shapes=[pltpu.VMEM((rt * SUB, LANE), F32)],
        compiler_params=_params(("parallel", "parallel"), 56),
        name="s5_output",
    )(u4, xprev, m_intra, w_out, d4)


def _group_norm(o, g, b):
    mu = jnp.mean(o, axis=-1, keepdims=True)
    var = jnp.mean(jnp.square(o - mu), axis=-1, keepdims=True)
    return (o - mu) * lax.rsqrt(var + GN_EPS) * g + b


def _ret_kernel(dec_ref, q_ref, k_ref, v_ref, g_ref, cos_ref, sin_ref, s0_ref, gg_ref, gb_ref,
                y_ref, sfin_ref, sb_scr, *, nc, use_rot, has_init, want_final):
    h = pl.program_id(1)
    row = lax.broadcasted_iota(jnp.int32, (CHUNK, CHUNK), 0).astype(F32)
    col = lax.broadcasted_iota(jnp.int32, (CHUNK, CHUNK), 1).astype(F32)
    lg_f = -jnp.exp(jnp.full((CHUNK, CHUNK), dec_ref[0, h], F32))
    lg_b = -jnp.exp(jnp.full((CHUNK, CHUNK), dec_ref[1, h], F32))
    lag = row - col
    dmat = (jnp.where(lag >= 0, jnp.exp(lg_f * jnp.maximum(lag, 0.0)), 0.0)
            + jnp.where(lag <= 0, jnp.exp(lg_b * jnp.maximum(-lag, 0.0)), 0.0))
    qd_f = jnp.exp(lg_f * (row + 1.0))
    qd_b = jnp.exp(lg_b * (CHUNK - row))
    kd_f = jnp.exp(lg_f * (CHUNK - 1.0 - row)) * HEAD_DIM ** -0.5
    kd_b = jnp.exp(lg_b * row) * HEAD_DIM ** -0.5
    cd_f = jnp.exp(lg_f * CHUNK)
    cd_b = jnp.exp(lg_b * CHUNK)

    def rot(x, sl):
        if not use_rot:
            return x
        return x * cos_ref[sl, :] + pltpu.roll(x, HEAD_DIM // 2, 1) * sin_ref[sl, :]

    def load(j):
        sl = pl.ds(pl.multiple_of(j * CHUNK, CHUNK), CHUNK)
        return sl, rot(q_ref[sl, :].astype(F32), sl), rot(k_ref[sl, :].astype(F32), sl), v_ref[sl, :].astype(BF16)

    zero = jnp.zeros((CHUNK, CHUNK), F32)

    def bwd(jj, s_b):
        j = nc - 1 - jj
        _, _, k, v = load(j)
        sb_scr[j] = s_b
        return s_b * cd_b + _dot_tn((k * kd_b).astype(BF16), v)

    s_b = lax.fori_loop(0, nc, bwd, s0_ref[1] if has_init else zero)

    def fwd(j, s_f):
        sl, q, k, v = load(j)
        att = _dot_nt(q.astype(BF16), (k * HEAD_DIM ** -0.5).astype(BF16)) * dmat
        o = (_dot(att.astype(BF16), v) + _dot((q * qd_f).astype(BF16), s_f.astype(BF16))
             + _dot((q * qd_b).astype(BF16), sb_scr[j].astype(BF16)))
        y = jax.nn.silu(g_ref[sl, :].astype(F32)) * _group_norm(o, gg_ref[...], gb_ref[...])
        y_ref[sl, :] = y.astype(y_ref.dtype)
        return s_f * cd_f + _dot_tn((k * kd_f).astype(BF16), v)

    s_f = lax.fori_loop(0, nc, fwd, s0_ref[0] if has_init else zero)
    if want_final:
        sfin_ref[0] = s_f
        sfin_ref[1] = s_b


def _retention(proj, dec, cos_t, sin_t, s0, gn_g, gn_b, *, n_seq, seq_len, row0, use_rot, has_init, want_final):
    nc = seq_len // CHUNK
    blk0 = row0 // seq_len
    kern = functools.partial(_ret_kernel, nc=nc, use_rot=use_rot, has_init=has_init, want_final=want_final)
    t_tok = proj.shape[0]

    def tok(cb):
        return pl.BlockSpec((seq_len, HEAD_DIM), lambda s, h, cb=cb: (blk0 + s, cb + h))

    rot_spec = pl.BlockSpec((seq_len, HEAD_DIM), lambda s, h: (0, 0))
    st_spec = pl.BlockSpec((None, 2, None, HEAD_DIM, HEAD_DIM), lambda s, h: (s, 0, h, 0, 0))
    gn_spec = pl.BlockSpec((1, HEAD_DIM), lambda s, h: (0, h))
    y, sfin = pl.pallas_call(
        kern,
        grid=(n_seq, HEADS),
        in_specs=[pl.BlockSpec(memory_space=pltpu.SMEM), tok(0), tok(4), tok(8), tok(12),
                  rot_spec, rot_spec, st_spec, gn_spec, gn_spec],
        out_specs=[pl.BlockSpec((seq_len, HEAD_DIM), lambda s, h: (s, h)), st_spec],
        out_shape=[jax.ShapeDtypeStruct((n_seq * seq_len, WIDTH), BF16),
                   jax.ShapeDtypeStruct((n_seq, 2, HEADS, HEAD_DIM, HEAD_DIM), F32)],
        scratch_shapes=[pltpu.VMEM((nc, HEAD_DIM, HEAD_DIM), F32)],
        compiler_params=_params(("parallel", "parallel"), 48),
        name="retention",
    )(dec, proj, proj, proj, proj, cos_t, sin_t, s0, gn_g, gn_b)
    return y, sfin


def _mlstm_kernel(q_ref, k_ref, v_ref, o_ref, gt_ref, c0_ref, n0_ref, m0_ref, gg_ref, gb_ref,
                  y_ref, cfin_ref, nfin_ref, mfin_ref, h_scr, *, nc, has_init, want_final):
    h = pl.program_id(1)
    row = lax.broadcasted_iota(jnp.int32, (CHUNK, CHUNK), 0)
    col = lax.broadcasted_iota(jnp.int32, (CHUNK, CHUNK), 1)
    tri = (row >= col).astype(F32)
    lane = lax.broadcasted_iota(jnp.int32, (CHUNK, LANE), 1)

    def pick_col(x, c):
        return jnp.sum(jnp.where(lane == c, x, 0.0), axis=1, keepdims=True)

    def pick_row(xt, c):
        return jnp.sum(jnp.where(row == c, xt, 0.0), axis=0, keepdims=True)

    def chunk(j, carry, d):
        cmat, nvec, m = carry
        sl = pl.ds(pl.multiple_of(j * CHUNK, CHUNK), CHUNK)
        q = q_ref[sl, :].astype(BF16)
        kf = k_ref[sl, :].astype(F32) * HEAD_DIM ** -0.5
        k = kf.astype(BF16)
        v = v_ref[sl, :].astype(F32)
        g = gt_ref[sl, :]
        lf = jnp.minimum(g, 0.0) - jnp.log1p(jnp.exp(-jnp.abs(g)))
        cs = _dot_hi(tri, lf)
        if d == 0:
            bc = cs
            rest = cs[CHUNK - 1:CHUNK, :] - cs
            causal = row >= col
        else:
            bc = cs[CHUNK - 1:CHUNK, :] - cs + lf
            rest = cs - lf
            causal = row <= col
        ci = d * 8 + h
        cf = d * 8 + 4 + h
        bc_col = pick_col(bc, cf)
        i_col = pick_col(g, ci)
        rest_col = pick_col(rest, cf)
        bc_row = pick_row(bc.T, cf)
        i_row = pick_row(g.T, ci)
        rest_row = pick_row(rest.T, cf)
        b_last = bc_col[CHUNK - 1:CHUNK, :] if d == 0 else bc_col[0:1, :]
        log_d = jnp.where(causal, bc_col - bc_row + i_row, -jnp.inf)
        log_prev = bc_col + m
        m_t = jnp.maximum(log_prev, jnp.max(log_d, axis=1, keepdims=True))
        w = _dot_nt(q, k) * jnp.exp(log_d - m_t)
        w_prev = jnp.exp(log_prev - m_t)
        qf = q.astype(F32)
        num = _dot(w.astype(BF16), v.astype(BF16)) + w_prev * _dot_nt(q, cmat.astype(BF16))
        den = jnp.sum(w, axis=1, keepdims=True) + w_prev * jnp.sum(qf * nvec, axis=1, keepdims=True)
        h_t = num / jnp.maximum(jnp.abs(den), jnp.exp(-m_t))
        log_k_row = rest_row + i_row
        log_k_col = rest_col + i_col
        m_new = jnp.maximum(b_last + m, jnp.max(log_k_row, axis=1, keepdims=True))
        kw = jnp.exp(log_k_col - m_new)
        decay = jnp.exp(b_last + m - m_new)
        c_new = decay * cmat + _dot_tn((v * kw).astype(BF16), k)
        n_new = decay * nvec + jnp.sum(kw * kf, axis=0, keepdims=True)
        return sl, h_t, (c_new, n_new, m_new)

    def init(d):
        if has_init:
            return c0_ref[d], n0_ref[d], m0_ref[d]
        return jnp.zeros((HEAD_DIM, HEAD_DIM), F32), jnp.zeros((1, HEAD_DIM), F32), jnp.zeros((1, HEAD_DIM), F32)

    def fwd(j, carry):
        sl, h_t, carry = chunk(j, carry, 0)
        h_scr[sl, :] = h_t
        return carry

    fin_f = lax.fori_loop(0, nc, fwd, init(0))

    def bwd(jj, carry):
        sl, h_t, carry = chunk(nc - 1 - jj, carry, 1)
        y = jax.nn.sigmoid(o_ref[sl, :].astype(F32)) * (h_scr[sl, :] + h_t)
        y_ref[sl, :] = _group_norm(y, gg_ref[...], gb_ref[...]).astype(y_ref.dtype)
        return carry

    fin_b = lax.fori_loop(0, nc, bwd, init(1))
    if want_final:
        for d, (c_f, n_f, m_f) in enumerate((fin_f, fin_b)):
            cfin_ref[d] = c_f
            nfin_ref[d] = n_f
            mfin_ref[d] = m_f


def _mlstm(proj, gates, c0, n0, m0, gn_g, gn_b, *, n_seq, seq_len, row0, has_init, want_final):
    nc = seq_len // CHUNK
    blk0 = row0 // seq_len
    kern = functools.partial(_mlstm_kernel, nc=nc, has_init=has_init, want_final=want_final)
    t_tok = proj.shape[0]

    def tok(cb):
        return pl.BlockSpec((seq_len, HEAD_DIM), lambda s, h, cb=cb: (blk0 + s, cb + h))

    c_spec = pl.BlockSpec((None, 2, None, HEAD_DIM, HEAD_DIM), lambda s, h: (s, 0, h, 0, 0))
    v_spec = pl.BlockSpec((None, 2, None, 1, HEAD_DIM), lambda s, h: (s, 0, h, 0, 0))
    gn_spec = pl.BlockSpec((1, HEAD_DIM), lambda s, h: (0, h))
    return pl.pallas_call(
        kern,
        grid=(n_seq, HEADS),
        in_specs=[tok(16), tok(20), tok(24), tok(28),
                  pl.BlockSpec((seq_len, LANE), lambda s, h: (blk0 + s, 0)),
                  c_spec, v_spec, v_spec, gn_spec, gn_spec],
        out_specs=[pl.BlockSpec((seq_len, HEAD_DIM), lambda s, h: (s, h)), c_spec, v_spec, v_spec],
        out_shape=[jax.ShapeDtypeStruct((n_seq * seq_len, WIDTH), BF16),
                   jax.ShapeDtypeStruct((n_seq, 2, HEADS, HEAD_DIM, HEAD_DIM), F32),
                   jax.ShapeDtypeStruct((n_seq, 2, HEADS, 1, HEAD_DIM), F32),
                   jax.ShapeDtypeStruct((n_seq, 2, HEADS, 1, HEAD_DIM), F32)],
        scratch_shapes=[pltpu.VMEM((seq_len, HEAD_DIM), F32)],
        compiler_params=_params(("parallel", "parallel"), 48),
        name="mlstm",
    )(proj, proj, proj, proj, gates, c0, n0, m0, gn_g, gn_b)


def _layer_norm(x, g, b):
    mu = jnp.mean(x, axis=-1, keepdims=True)
    var = jnp.mean(jnp.square(x - mu), axis=-1, keepdims=True)
    return (x - mu) * lax.rsqrt(var + LN_EPS) * g + b


def _merge_kernel(x_ref, mod_ref, z_ref, yb_ref, yc_ref, wm_ref, bm_ref, wglu_ref, bglu_ref,
                  wa_ref, wb_ref, wc_ref, wo_ref, lg_ref, lb_ref, o_ref):
    x = x_ref[...]
    h = (x * (1.0 + mod_ref[1:2, :]) + mod_ref[0:1, :]).astype(BF16)
    z = z_ref[...]
    ya = (z.astype(F32) * jax.nn.sigmoid(_dot(z, wglu_ref[...]) + bglu_ref[...])).astype(BF16)
    merged = None
    for j, (y, w_ref) in enumerate(((ya, wa_ref), (yb_ref[...], wb_ref), (yc_ref[...], wc_ref))):
        gate = jax.nn.sigmoid(_dot(h, wm_ref[:, j * D_MODEL:(j + 1) * D_MODEL]) + bm_ref[:, j * D_MODEL:(j + 1) * D_MODEL])
        term = gate * _dot(y, w_ref[...])
        merged = term if merged is None else merged + term
    mix = _dot(merged.astype(BF16), wo_ref[...])
    o_ref[...] = _layer_norm(ALPHA * x + mod_ref[2:3, :] * mix, lg_ref[...], lb_ref[...])


def _merge(x, mod_l, z, yb, yc, wm, bm, wglu, bglu, wa, wb, wc, wo, lg, lb, cfg):
    t_tok = x.shape[0]
    tm = cfg["tm_merge"]
    cond = functools.partial(_cond_row, tm=tm, n_ctx_tok=cfg["n_ctx_tok"], lat_len=cfg["lat_len"])

    def full(shape):
        return pl.BlockSpec(shape, lambda i: (0,) * len(shape))

    def tok(w):
        return pl.BlockSpec((tm, w), lambda i: (i, 0))

    return pl.pallas_call(
        _merge_kernel,
        grid=(t_tok // tm,),
        in_specs=[tok(D_MODEL), pl.BlockSpec((None, 6, D_MODEL), lambda i: (cond(i), 0, 0)),
                  tok(WIDTH), tok(WIDTH), tok(WIDTH),
                  full((D_MODEL, 3 * D_MODEL)), full((1, 3 * D_MODEL)), full((WIDTH, WIDTH)), full((1, WIDTH)),
                  full((WIDTH, D_MODEL)), full((WIDTH, D_MODEL)), full((WIDTH, D_MODEL)),
                  full((D_MODEL, D_MODEL)), full((1, D_MODEL)), full((1, D_MODEL))],
        out_specs=tok(D_MODEL),
        out_shape=jax.ShapeDtypeStruct((t_tok, D_MODEL), F32),
        compiler_params=_params(("parallel",), 56),
        name="merge",
    )(x, mod_l, z, yb, yc, wm, bm, wglu, bglu, wa, wb, wc, wo, lg, lb)


def _ffn_kernel(x_ref, mod_ref, w1_ref, w3_ref, w2_ref, lg_ref, lb_ref, o_ref, h_scr, acc_scr):
    f = pl.program_id(1)

    @pl.when(f == 0)
    def _():
        h_scr[...] = (x_ref[...] * (1.0 + mod_ref[4:5, :]) + mod_ref[3:4, :]).astype(BF16)
        acc_scr[...] = jnp.zeros_like(acc_scr)

    h = h_scr[...]
    act = (jax.nn.silu(_dot(h, w1_ref[...])) * _dot(h, w3_ref[...])).astype(BF16)
    acc_scr[...] += _dot(act, w2_ref[...])

    @pl.when(f == pl.num_programs(1) - 1)
    def _():
        o_ref[...] = _layer_norm(ALPHA * x_ref[...] + mod_ref[5:6, :] * acc_scr[...], lg_ref[...], lb_ref[...])


def _dense_ffn(x, mod_l, w1, w3, w2, lg, lb, cfg):
    t_tok = x.shape[0]
    tm, tf = cfg["tm_ffn"], cfg["tf"]
    cond = functools.partial(_cond_row, tm=tm, n_ctx_tok=cfg["n_ctx_tok"], lat_len=cfg["lat_len"])
    return pl.pallas_call(
        _ffn_kernel,
        grid=(t_tok // tm, D_FF // tf),
        in_specs=[pl.BlockSpec((tm, D_MODEL), lambda i, f: (i, 0)),
                  pl.BlockSpec((None, 6, D_MODEL), lambda i, f: (cond(i), 0, 0)),
                  pl.BlockSpec((D_MODEL, tf), lambda i, f: (0, f)),
                  pl.BlockSpec((D_MODEL, tf), lambda i, f: (0, f)),
                  pl.BlockSpec((tf, D_MODEL), lambda i, f: (f, 0)),
                  pl.BlockSpec((1, D_MODEL), lambda i, f: (0, 0)),
                  pl.BlockSpec((1, D_MODEL), lambda i, f: (0, 0))],
        out_specs=pl.BlockSpec((tm, D_MODEL), lambda i, f: (i, 0)),
        out_shape=jax.ShapeDtypeStruct((t_tok, D_MODEL), F32),
        scratch_shapes=[pltpu.VMEM((tm, D_MODEL), BF16), pltpu.VMEM((tm, D_MODEL), F32)],
        compiler_params=_params(("parallel", "arbitrary"), 56),
        name="dense_ffn",
    )(x, mod_l, w1, w3, w2, lg, lb)


def _router_kernel(x_ref, mod_ref, w_ref, b_ref, g_ref):
    h = x_ref[...] * (1.0 + mod_ref[4:5, :]) + mod_ref[3:4, :]
    lane = lax.broadcasted_iota(jnp.int32, g_ref.shape, 1)
    logits = jnp.where(lane < N_EXPERTS, _dot_hi(h, w_ref[...]) + b_ref[...], NEG_BIG)
    m1 = jnp.max(logits, axis=1, keepdims=True)
    i1 = jnp.min(jnp.where(logits == m1, lane, LANE), axis=1, keepdims=True)
    rest = jnp.where(lane == i1, NEG_BIG, logits)
    m2 = jnp.max(rest, axis=1, keepdims=True)
    i2 = jnp.min(jnp.where(rest == m2, lane, LANE), axis=1, keepdims=True)
    e2 = jnp.exp(m2 - m1)
    den = 1.0 + e2
    g_ref[...] = jnp.where(lane == i1, 1.0 / den, 0.0) + jnp.where(lane == i2, e2 / den, 0.0)


def _router(x, mod_l, w, b, cfg):
    t_tok = x.shape[0]
    tm = cfg["tm_ffn"]
    cond = functools.partial(_cond_row, tm=tm, n_ctx_tok=cfg["n_ctx_tok"], lat_len=cfg["lat_len"])
    return pl.pallas_call(
        _router_kernel,
        grid=(t_tok // tm,),
        in_specs=[pl.BlockSpec((tm, D_MODEL), lambda i: (i, 0)),
                  pl.BlockSpec((None, 6, D_MODEL), lambda i: (cond(i), 0, 0)),
                  pl.BlockSpec((D_MODEL, LANE), lambda i: (0, 0)),
                  pl.BlockSpec((1, LANE), lambda i: (0, 0))],
        out_specs=pl.BlockSpec((tm, LANE), lambda i: (i, 0)),
        out_shape=jax.ShapeDtypeStruct((t_tok, LANE), F32),
        compiler_params=_params(("parallel",), 40),
        name="router",
    )(x, mod_l, w, b)


def _moe_kernel(x_ref, mod_ref, gate_ref, w1_ref, w3_ref, w2_ref, lg_ref, lb_ref, o_ref,
                h_scr, acc_scr, hc_scr, ob_scr, sp_scr, gt_scr, cnt_smem):
    e = pl.program_id(1)
    f = pl.program_id(2)
    last_f = pl.num_programs(2) - 1
    tm = x_ref.shape[0]

    @pl.when((e == 0) & (f == 0))
    def _():
        h_scr[...] = (x_ref[...] * (1.0 + mod_ref[4:5, :]) + mod_ref[3:4, :]).astype(BF16)
        acc_scr[...] = jnp.zeros_like(acc_scr)
        g = gate_ref[...]
        sel = g > 0.0
        ones = jnp.where(sel, 1.0, 0.0)
        r = lax.broadcasted_iota(jnp.int32, (tm, tm), 0)
        c = lax.broadcasted_iota(jnp.int32, (tm, tm), 1)
        before = jnp.where(r > c, 1.0, 0.0).astype(BF16)
        pos = _dot(before, ones.astype(BF16))
        spt = jnp.where(sel, pos, -1.0).T
        gt = g.T
        cnt = jnp.sum(ones, axis=0, keepdims=True)
        for ee in range(N_EXPERTS):
            sp_scr[ee] = spt[ee:ee + 1, :]
            gt_scr[ee] = gt[ee:ee + 1, :]
            cnt_smem[ee] = cnt[0, ee].astype(jnp.int32)

    nb = (cnt_smem[e] + (MOE_BLK - 1)) // MOE_BLK
    row = lax.broadcasted_iota(jnp.int32, (MOE_BLK, tm), 0)

    def onehot(b):
        return sp_scr[e] == (row + b * MOE_BLK).astype(F32)

    @pl.when(f == 0)
    def _():
        def gather(b, carry):
            p = jnp.where(onehot(b), 1.0, 0.0).astype(BF16)
            hc_scr[b] = _dot(p, h_scr[...]).astype(BF16)
            ob_scr[b] = jnp.zeros((MOE_BLK, D_MODEL), F32)
            return carry
        lax.fori_loop(0, nb, gather, 0)

    def ffn(b, carry):
        hc = hc_scr[b]
        act = (jax.nn.silu(_dot(hc, w1_ref[...])) * _dot(hc, w3_ref[...])).astype(BF16)
        ob_scr[b] += _dot(act, w2_ref[...])
        return carry
    lax.fori_loop(0, nb, ffn, 0)

    @pl.when(f == last_f)
    def _():
        def scatter(b, carry):
            m = onehot(b)
            gc = jnp.sum(jnp.where(m, gt_scr[e], 0.0), axis=1, keepdims=True)
            og = (ob_scr[b] * gc).astype(BF16)
            acc_scr[...] += _dot_tn(jnp.where(m, 1.0, 0.0).astype(BF16), og)
            return carry
        lax.fori_loop(0, nb, scatter, 0)

    @pl.when((e == pl.num_programs(1) - 1) & (f == last_f))
    def _():
        o_ref[...] = _layer_norm(ALPHA * x_ref[...] + mod_ref[5:6, :] * acc_scr[...], lg_ref[...], lb_ref[...])


def _moe_ffn(x, mod_l, gates, w1, w3, w2, lg, lb, cfg):
    t_tok = x.shape[0]
    tm, tf = cfg["tm_moe"], cfg["tf"]
    cond = functools.partial(_cond_row, tm=tm, n_ctx_tok=cfg["n_ctx_tok"], lat_len=cfg["lat_len"])
    return pl.pallas_call(
        _moe_kernel,
        grid=(t_tok // tm, N_EXPERTS, D_FF // tf),
        in_specs=[pl.BlockSpec((tm, D_MODEL), lambda i, e, f: (i, 0)),
                  pl.BlockSpec((None, 6, D_MODEL), lambda i, e, f: (cond(i), 0, 0)),
                  pl.BlockSpec((tm, LANE), lambda i, e, f: (i, 0)),
                  pl.BlockSpec((None, D_MODEL, tf), lambda i, e, f: (e, 0, f)),
                  pl.BlockSpec((None, D_MODEL, tf), lambda i, e, f: (e, 0, f)),
                  pl.BlockSpec((None, tf, D_MODEL), lambda i, e, f: (e, f, 0)),
                  pl.BlockSpec((1, D_MODEL), lambda i, e, f: (0, 0)),
                  pl.BlockSpec((1, D_MODEL), lambda i, e, f: (0, 0))],
        out_specs=pl.BlockSpec((tm, D_MODEL), lambda i, e, f: (i, 0)),
        out_shape=jax.ShapeDtypeStruct((t_tok, D_MODEL), F32),
        scratch_shapes=[pltpu.VMEM((tm, D_MODEL), BF16), pltpu.VMEM((tm, D_MODEL), F32),
                        pltpu.VMEM((tm // MOE_BLK, MOE_BLK, D_MODEL), BF16),
                        pltpu.VMEM((tm // MOE_BLK, MOE_BLK, D_MODEL), F32),
                        pltpu.VMEM((N_EXPERTS, 1, tm), F32), pltpu.VMEM((N_EXPERTS, 1, tm), F32),
                        pltpu.SMEM((N_EXPERTS,), jnp.int32)],
        compiler_params=_params(("parallel", "arbitrary", "arbitrary"), 56),
        name="moe_ffn",
    )(x, mod_l, gates, w1, w3, w2, lg, lb)


def _rotary_tables(n_tok):
    rows = n_tok // GRID_W
    r = jnp.repeat(jnp.arange(rows, dtype=F32), GRID_W)
    col = jnp.tile(jnp.arange(GRID_W, dtype=F32), rows)
    n_freq = HEAD_DIM // 4
    inv = ROPE_BASE ** (-jnp.arange(n_freq, dtype=F32) / n_freq)
    ang = jnp.concatenate([r[:, None] * inv, col[:, None] * inv], -1)
    cos, sin = jnp.cos(ang), jnp.sin(ang)
    return jnp.concatenate([cos, cos], -1), jnp.concatenate([-sin, sin], -1)


def _row2(v):
    return v.reshape(1, -1).astype(F32)


def kernel(x_prompt, x_sample, cache_s5_re, cache_s5_im, cache_ret, cache_ml_c, cache_ml_n, cache_ml_m, c, c_ctx, ada_w, ada_b, w_in, b_in, s5_lam_re, s5_lam_im, s5_log_step, s5_b_re, s5_b_im, s5_c_re, s5_c_im, s5_d, s5_glu_w, s5_glu_b, ret_decay, ret_gn_g, ret_gn_b, ml_gn_g, ml_gn_b, w_a, w_b, w_c, w_o, ln1_g, ln1_b, ln2_g, ln2_b, ffn_w1, ffn_w3, ffn_w2, moe_router, moe_router_b, moe_w1, moe_w3, moe_w2):
    n_ctx_seq, ctx_len, _ = x_prompt.shape
    n_lat_seq, lat_len, _ = x_sample.shape
    n_ctx_tok = n_ctx_seq * ctx_len
    n_lat_tok = n_lat_seq * lat_len
    t_tok = n_ctx_tok + n_lat_tok
    assert n_lat_seq + 1 <= N_COND and n_ctx_seq % 8 == 0 and n_lat_seq % 8 == 0
    assert ctx_len % CHUNK == 0 and lat_len % CHUNK == 0 and n_ctx_tok % lat_len == 0
    tm = min(1024, lat_len)
    cfg = dict(n_ctx_seq=n_ctx_seq, ctx_len=ctx_len, n_lat_seq=n_lat_seq, lat_len=lat_len, n_ctx_tok=n_ctx_tok,
               tm_in=tm, tm_merge=min(256, lat_len), tm_ffn=min(512, lat_len), tm_moe=min(1024, lat_len), tf=1408,
               s5_rows=min(256, t_tok // SUB // 8), proj_dtype=F32)

    x = jnp.concatenate([x_prompt.reshape(n_ctx_tok, D_MODEL), x_sample.reshape(n_lat_tok, D_MODEL)], 0)
    cond = jnp.zeros((N_COND, D_MODEL), F32).at[0].set(c_ctx).at[1:1 + n_lat_seq].set(c)
    mod = _modulation(cond, ada_w, ada_b).reshape(DEPTH, N_COND, 6, D_MODEL)
    cos_t, sin_t = _rotary_tables(lat_len)

    n_main = S5_WIDTH + 8 * WIDTH
    gate_off = n_main
    merge_off = gate_off + 4 * HEADS
    s5_mats = jax.vmap(_s5_matrices)(s5_lam_re, s5_lam_im, s5_log_step, s5_b_re, s5_b_im, s5_c_re, s5_c_im)

    st_s5, st_ret, st_c, st_n, st_m = [], [], [], [], []
    zero_ret = jnp.zeros((n_ctx_seq, 2, HEADS, HEAD_DIM, HEAD_DIM), F32)
    zero_vec = jnp.zeros((n_ctx_seq, 2, HEADS, 1, HEAD_DIM), F32)
    for l in range(DEPTH):
        mod_l = mod[l]
        w_main = w_in[l][:, :n_main].astype(BF16)
        b_main = _row2(b_in[l][:n_main])
        w_gate = jnp.zeros((D_MODEL, LANE), F32).at[:, :4 * HEADS].set(w_in[l][:, gate_off:merge_off]).astype(BF16)
        b_gate = jnp.zeros((1, LANE), F32).at[0, :4 * HEADS].set(b_in[l][gate_off:merge_off])
        proj, u4, gates = _inproj(x, mod_l, w_main, b_main, w_gate, b_gate, cfg)

        s5_win, s5_mi, s5_wout, s5_a = (m[l] for m in s5_mats)
        loc = _s5_state_in(u4, s5_win, cfg)
        x0 = jnp.stack([cache_s5_re[:, l], cache_s5_im[:, l]], 0)
        x0 = x0.reshape(2, n_lat_seq, 2, S5_Q, 8 * S5_STATE).transpose(3, 2, 0, 1, 4).astype(F32)
        xprev, s5_fin = _s5_scan(loc, s5_a, x0, cfg)
        d4 = jnp.tile(s5_d[l].astype(F32).reshape(S5_Q, 1, LANE), (1, SUB, 1)).reshape(S5_Q, 1, S5_FLAT)
        z = _s5_output(u4, xprev, s5_mi, s5_wout, d4, cfg)
        st_s5.append(s5_fin)

        gg, gb = _row2(ret_gn_g[l]), _row2(ret_gn_b[l])
        dec = ret_decay[l].astype(F32)
        yb_c, ret_fin = _retention(proj, dec, cos_t, sin_t, zero_ret, gg, gb, n_seq=n_ctx_seq, seq_len=ctx_len,
                                   row0=0, use_rot=False, has_init=False, want_final=True)
        yb_s, _ = _retention(proj, dec, cos_t, sin_t, cache_ret[:, l].astype(F32), gg, gb, n_seq=n_lat_seq,
                             seq_len=lat_len, row0=n_ctx_tok, use_rot=True, has_init=True, want_final=False)
        yb = jnp.concatenate([yb_c, yb_s], 0)
        st_ret.append(ret_fin)

        gg, gb = _row2(ml_gn_g[l]), _row2(ml_gn_b[l])
        yc_c, c_fin, n_fin, m_fin = _mlstm(proj, gates, zero_ret, zero_vec, zero_vec, gg, gb, n_seq=n_ctx_seq,
                                           seq_len=ctx_len, row0=0, has_init=False, want_final=True)
        n0 = cache_ml_n[:, l].astype(F32)[:, :, :, None, :]
        m0 = jnp.broadcast_to(cache_ml_m[:, l].astype(F32)[:, :, :, None, None], n0.shape)
        yc_s = _mlstm(proj, gates, cache_ml_c[:, l].astype(F32), n0, m0, gg, gb, n_seq=n_lat_seq,
                      seq_len=lat_len, row0=n_ctx_tok, has_init=True, want_final=False)[0]
        yc = jnp.concatenate([yc_c, yc_s], 0)
        st_c.append(c_fin)
        st_n.append(n_fin[:, :, :, 0, :])
        st_m.append(m_fin[:, :, :, 0, 0])

        x = _merge(x, mod_l, z, yb, yc, w_in[l][:, merge_off:].astype(BF16), _row2(b_in[l][merge_off:]),
                   s5_glu_w[l].astype(BF16), _row2(s5_glu_b[l]), w_a[l].astype(BF16), w_b[l].astype(BF16),
                   w_c[l].astype(BF16), w_o[l].astype(BF16), _row2(ln1_g[l]), _row2(ln1_b[l]), cfg)

        j = l // 2
        if l % 2 == 0:
            x = _dense_ffn(x, mod_l, ffn_w1[j].astype(BF16), ffn_w3[j].astype(BF16), ffn_w2[j].astype(BF16),
                           _row2(ln2_g[l]), _row2(ln2_b[l]), cfg)
        else:
            rw = jnp.zeros((D_MODEL, LANE), F32).at[:, :N_EXPERTS].set(moe_router[j])
            rb = jnp.zeros((1, LANE), F32).at[0, :N_EXPERTS].set(moe_router_b[j])
            mg = _router(x, mod_l, rw, rb, cfg)
            x = _moe_ffn(x, mod_l, mg, moe_w1[j].astype(BF16), moe_w3[j].astype(BF16), moe_w2[j].astype(BF16),
                         _row2(ln2_g[l]), _row2(ln2_b[l]), cfg)

    y_p = x[:n_ctx_tok].reshape(n_ctx_seq, ctx_len, D_MODEL)
    y_s = x[n_ctx_tok:].reshape(n_lat_seq, lat_len, D_MODEL)
    s5 = jnp.stack(st_s5, 0)
    s5 = s5.reshape(DEPTH, S5_Q, 2, 2, n_ctx_seq, 8, S5_STATE).transpose(3, 4, 0, 2, 1, 5, 6)
    s5 = s5.reshape(2, n_ctx_seq, DEPTH, 2, S5_GROUPS, S5_STATE)
    return (y_p, y_s, s5[0], s5[1], jnp.stack(st_ret, 1), jnp.stack(st_c, 1), jnp.stack(st_n, 1),
            jnp.stack(st_m, 1))
```

```python
import functools

import jax
import jax.numpy as jnp
from jax import lax
from jax.experimental import pallas as pl
from jax.experimental.pallas import tpu as pltpu

F32 = jnp.float32
BF16 = jnp.bfloat16

D_MODEL = 1024
DEPTH = 4
GRID_W = 64
CHUNK = 128
S5_WIDTH = 512
S5_GROUP = 16
S5_GROUPS = 32
S5_STATE = 64
HEADS = 4
HEAD_DIM = 128
WIDTH = 512
ROPE_BASE = 10000.0
D_FF = 2816
N_EXPERTS = 8
ALPHA = (2.0 * DEPTH) ** 0.25
LN_EPS = 1e-5
GN_EPS = 1e-5
N_COND = 16
SUB = 16
N_LAG = 2 * SUB - 1
LANE = 128
S5_Q = S5_WIDTH // LANE
S5_QG = LANE // S5_GROUP
S5_FLAT = SUB * LANE
S5_HALF = S5_QG * S5_STATE
S5_ST = 4 * S5_HALF
NEG_BIG = -1e30
MOE_BLK = 128
GATE_BC = 16
GATE_REST = 32


def _dot(a, b):
    return jnp.dot(a, b, preferred_element_type=F32)


def _dot_hi(a, b):
    return jnp.dot(a, b, preferred_element_type=F32, precision=lax.Precision.HIGHEST)


def _dot_nt(a, b):
    return lax.dot_general(a, b, (((1,), (1,)), ((), ())), preferred_element_type=F32)


def _dot_tn(a, b):
    return lax.dot_general(a, b, (((0,), (0,)), ((), ())), preferred_element_type=F32)


def _params(sem, vmem_mb):
    return pltpu.CompilerParams(dimension_semantics=sem, vmem_limit_bytes=vmem_mb << 20)


def _cond_row(tile, tm, n_ctx_tok, lat_len):
    start = tile * tm
    return jnp.where(start < n_ctx_tok, 0, 1 + (start - n_ctx_tok) // lat_len)


def _iota(shape, axis):
    return lax.broadcasted_iota(jnp.int32, shape, axis)


def _mod_kernel(c_ref, w_ref, b_ref, o_ref):
    o_ref[...] = _dot_hi(jax.nn.silu(c_ref[...]), w_ref[...]) + b_ref[...]


def _modulation(cond, ada_w, ada_b):
    tn = 1536
    n = ada_w.shape[-1]
    return pl.pallas_call(
        _mod_kernel,
        grid=(DEPTH, n // tn),
        in_specs=[pl.BlockSpec((N_COND, D_MODEL), lambda l, j: (0, 0)),
                  pl.BlockSpec((None, D_MODEL, tn), lambda l, j: (l, 0, j)),
                  pl.BlockSpec((None, 1, tn), lambda l, j: (l, 0, j))],
        out_specs=pl.BlockSpec((None, N_COND, tn), lambda l, j: (l, 0, j)),
        out_shape=jax.ShapeDtypeStruct((DEPTH, N_COND, n), F32),
        compiler_params=_params(("parallel", "parallel"), 40),
        name="modulation",
    )(cond, ada_w, ada_b.reshape(DEPTH, 1, n))


def _inproj_kernel(x_ref, mod_ref, w_ref, b_ref, wg_ref, bg_ref, o_ref, u4_ref, g_ref, h_scr, u_scr):
    j = pl.program_id(1)
    tm = x_ref.shape[0]

    @pl.when(j == 0)
    def _():
        h = (x_ref[...] * (1.0 + mod_ref[1:2, :]) + mod_ref[0:1, :]).astype(BF16)
        h_scr[...] = h
        g_ref[...] = _dot(h, wg_ref[...]) + bg_ref[...]
        u = _dot(h, w_ref[...]) + b_ref[...]
        for q in range(S5_Q):
            u_scr[q] = u[:, q * LANE:(q + 1) * LANE]
            for t in range(SUB):
                u4_ref[q, :, t * LANE:(t + 1) * LANE] = u_scr[q, pl.ds(t, tm // SUB, stride=SUB), :].astype(BF16)

    @pl.when(j > 0)
    def _():
        o_ref[...] = (_dot(h_scr[...], w_ref[...]) + b_ref[...]).astype(o_ref.dtype)


def _inproj(x, mod_l, w, b, wg, bg, cfg):
    t_tok = x.shape[0]
    tm, tn = cfg["tm_in"], 512
    nj = w.shape[1] // tn
    cond = functools.partial(_cond_row, tm=tm, n_ctx_tok=cfg["n_ctx_tok"], lat_len=cfg["lat_len"])
    return pl.pallas_call(
        _inproj_kernel,
        grid=(t_tok // tm, nj),
        in_specs=[pl.BlockSpec((tm, D_MODEL), lambda i, j: (i, 0)),
                  pl.BlockSpec((None, 6, D_MODEL), lambda i, j: (cond(i), 0, 0)),
                  pl.BlockSpec((D_MODEL, tn), lambda i, j: (0, j)),
                  pl.BlockSpec((1, tn), lambda i, j: (0, j)),
                  pl.BlockSpec((D_MODEL, LANE), lambda i, j: (0, 0)),
                  pl.BlockSpec((1, LANE), lambda i, j: (0, 0))],
        out_specs=[pl.BlockSpec((tm, tn), lambda i, j: (i, jnp.maximum(j - 1, 0))),
                   pl.BlockSpec((S5_Q, tm // SUB, S5_FLAT), lambda i, j: (0, i, 0)),
                   pl.BlockSpec((tm, LANE), lambda i, j: (i, 0))],
        out_shape=[jax.ShapeDtypeStruct((t_tok, (nj - 1) * tn), BF16),
                   jax.ShapeDtypeStruct((S5_Q, t_tok // SUB, S5_FLAT), BF16),
                   jax.ShapeDtypeStruct((t_tok, LANE), F32)],
        scratch_shapes=[pltpu.VMEM((tm, D_MODEL), BF16), pltpu.VMEM((S5_Q, tm, LANE), F32)],
        compiler_params=_params(("parallel", "arbitrary"), 48),
        name="inproj",
    )(x, mod_l, w, b, wg, bg)


def _s5_factors(lam_re, lam_im, log_step, b_re, b_im, c_re, c_im):
    lam = lax.complex(lam_re.astype(F32), lam_im.astype(F32))
    lam_dt = lam * jnp.exp(log_step.astype(F32))[..., None]
    lam_bar = jnp.exp(lam_dt)
    bbar = ((lam_bar - 1.0) / lam)[..., None] * lax.complex(b_re.astype(F32), b_im.astype(F32))
    cmat = lax.complex(c_re.astype(F32), c_im.astype(F32))
    ks = jnp.arange(SUB + 1, dtype=F32)
    pw = jnp.exp(lam_dt[None] * ks[:, None, None, None])
    kern = jnp.einsum('dgcp,tdgp,dgpe->dgtce', cmat, pw[:SUB], bbar).real
    pad = jnp.zeros_like(kern[0][:, :SUB - 1])
    ktab = jnp.concatenate([pad, kern[0]], 1) + jnp.concatenate([kern[1][:, ::-1], pad], 1)
    k_lag = ktab.reshape(S5_Q, S5_QG, N_LAG, S5_GROUP, S5_GROUP).transpose(0, 2, 3, 1, 4)
    k_lag = k_lag.reshape(S5_Q, N_LAG, S5_GROUP, LANE)
    pw_in = jnp.stack([pw[:SUB][::-1, 0], pw[:SUB][:, 1]], 0)
    wb = pw_in[..., None] * bbar[:, None]
    wb = jnp.stack([wb.real, wb.imag], 1).reshape(2, 2, SUB, S5_Q, S5_QG, S5_STATE, S5_GROUP)
    a_in = wb.transpose(3, 2, 0, 1, 5, 4, 6).reshape(S5_Q, SUB, 4, S5_STATE, LANE)
    pw_out = jnp.stack([pw[1:, 0], pw[1:][::-1, 1]], 0)
    ce = cmat[:, None] * pw_out[:, :, :, None, :]
    ce = jnp.stack([ce.real, -ce.imag], 1).reshape(2, 2, SUB, S5_Q, S5_QG, S5_GROUP, S5_STATE)
    b_out = ce.transpose(3, 0, 1, 2, 5, 4, 6).reshape(S5_Q, 4, SUB, S5_GROUP, S5_HALF)
    a = pw[SUB]
    a = jnp.stack([a.real, a.imag], 1).reshape(2, 2, S5_Q, 1, S5_HALF).transpose(2, 0, 1, 3, 4)
    return a_in.astype(BF16), k_lag.astype(BF16), b_out.astype(BF16), a


def _expand(src_t, n_rep, row_shift, col_shift):
    k, r = src_t.shape
    rep = jnp.where(_iota((k, n_rep * k), 0) == (_iota((k, n_rep * k), 1) & (k - 1)), 1.0, 0.0).astype(BF16)
    same = (_iota((r, n_rep * k), 0) >> row_shift) == (_iota((r, n_rep * k), 1) >> col_shift)
    return jnp.where(same, _dot_tn(src_t, rep), 0.0).astype(BF16)


def _s5a_kernel(u_ref, a_ref, o_ref, w_scr):
    @pl.when(pl.program_id(1) == 0)
    def _():
        for t in range(SUB):
            for k in range(4):
                w_scr[t * LANE:(t + 1) * LANE, k * S5_HALF:(k + 1) * S5_HALF] = _expand(a_ref[t, k], S5_QG, 4, 6)

    res = _dot(u_ref[...], w_scr[...])
    for d in range(2):
        for r in range(2):
            k = 2 * d + r
            o_ref[d, r] = res[:, k * S5_HALF:(k + 1) * S5_HALF]


def _s5_state_in(u4, a_in, cfg):
    rows = u4.shape[1]
    rt = cfg["s5_rows"]
    return pl.pallas_call(
        _s5a_kernel,
        grid=(S5_Q, rows // rt),
        in_specs=[pl.BlockSpec((None, rt, S5_FLAT), lambda q, i: (q, i, 0)),
                  pl.BlockSpec((None, SUB, 4, S5_STATE, LANE), lambda q, i: (q, 0, 0, 0, 0))],
        out_specs=pl.BlockSpec((None, 2, 2, rt, S5_HALF), lambda q, i: (q, 0, 0, i, 0)),
        out_shape=jax.ShapeDtypeStruct((S5_Q, 2, 2, rows, S5_HALF), F32),
        scratch_shapes=[pltpu.VMEM((S5_FLAT, S5_ST), BF16)],
        compiler_params=_params(("parallel", "arbitrary"), 48),
        name="s5_state_in",
    )(u4, a_in)


def _s5b_kernel(loc_ref, a_ref, x0_ref, xp_ref, fin_ref, *, n_ctx_seq, ctx_sub, n_lat_seq, lat_sub):
    d = pl.program_id(1)
    ar = jnp.broadcast_to(a_ref[0], (8, LANE))
    ai = jnp.broadcast_to(a_ref[1], (8, LANE))

    def run(base, nsub, xr0, xi0):
        def body(jj, carry):
            xr, xi = carry
            j = jnp.where(d == 0, jj, nsub - 1 - jj)
            idx = pl.ds(base + j, 8, stride=nsub)
            xp_ref[0, idx, :] = xr
            xp_ref[1, idx, :] = xi
            lr = loc_ref[0, idx, :]
            li = loc_ref[1, idx, :]
            return ar * xr - ai * xi + lr, ar * xi + ai * xr + li
        return lax.fori_loop(0, nsub, body, (xr0, xi0))

    zero = jnp.zeros((8, LANE), F32)
    for bg in range(n_ctx_seq // 8):
        xr, xi = run(bg * 8 * ctx_sub, ctx_sub, zero, zero)
        fin_ref[0, bg * 8:(bg + 1) * 8, :] = xr
        fin_ref[1, bg * 8:(bg + 1) * 8, :] = xi
    for bg in range(n_lat_seq // 8):
        run(n_ctx_seq * ctx_sub + bg * 8 * lat_sub, lat_sub, x0_ref[0, bg * 8:(bg + 1) * 8, :],
            x0_ref[1, bg * 8:(bg + 1) * 8, :])


def _s5_scan(loc, a, x0, cfg):
    rows = loc.shape[3]
    n_ctx_seq, n_lat_seq = cfg["n_ctx_seq"], cfg["n_lat_seq"]
    kern = functools.partial(_s5b_kernel, n_ctx_seq=n_ctx_seq, ctx_sub=cfg["ctx_len"] // SUB,
                             n_lat_seq=n_lat_seq, lat_sub=cfg["lat_len"] // SUB)
    nlb = S5_HALF // LANE
    return pl.pallas_call(
        kern,
        grid=(S5_Q, 2, nlb),
        in_specs=[pl.BlockSpec((None, None, 2, rows, LANE), lambda q, d, b: (q, d, 0, 0, b)),
                  pl.BlockSpec((None, None, 2, 1, LANE), lambda q, d, b: (q, d, 0, 0, b)),
                  pl.BlockSpec((None, None, 2, n_lat_seq, LANE), lambda q, d, b: (q, d, 0, 0, b))],
        out_specs=[pl.BlockSpec((None, None, 2, rows, LANE), lambda q, d, b: (q, d, 0, 0, b)),
                   pl.BlockSpec((None, None, 2, n_ctx_seq, LANE), lambda q, d, b: (q, d, 0, 0, b))],
        out_shape=[jax.ShapeDtypeStruct(loc.shape, F32),
                   jax.ShapeDtypeStruct((S5_Q, 2, 2, n_ctx_seq, S5_HALF), F32)],
        compiler_params=_params(("parallel", "parallel", "parallel"), 48),
        name="s5_scan",
    )(loc, a, x0)


def _s5c_kernel(u_ref, xp_ref, k_ref, b_ref, d_ref, z_ref, m_scr, wo_scr, bd_scr, z_scr):
    rt = u_ref.shape[0]

    @pl.when(pl.program_id(1) == 0)
    def _():
        for l in range(N_LAG):
            bd_scr[l] = _expand(k_ref[l], S5_QG, 4, 4)
        for t in range(SUB):
            for s in range(SUB):
                m_scr[t * LANE:(t + 1) * LANE, s * LANE:(s + 1) * LANE] = bd_scr[s - t + SUB - 1]
        for k in range(4):
            for t in range(SUB):
                wo_scr[k * S5_HALF:(k + 1) * S5_HALF, t * LANE:(t + 1) * LANE] = _expand(b_ref[k, t], S5_QG, 6, 4)

    u = u_ref[...]
    xcat = jnp.concatenate([xp_ref[0, 0], xp_ref[0, 1], xp_ref[1, 0], xp_ref[1, 1]], axis=1).astype(BF16)
    y = _dot(u, m_scr[...]) + _dot(xcat, wo_scr[...])
    z = jax.nn.gelu(d_ref[...] * u.astype(F32) + y)
    for t in range(SUB):
        z_scr[pl.ds(t, rt, stride=SUB), :] = z[:, t * LANE:(t + 1) * LANE]
    z_ref[...] = z_scr[...].astype(z_ref.dtype)


def _s5_output(u4, xprev, k_lag, b_out, d4, cfg):
    rows = u4.shape[1]
    rt = cfg["s5_rows"]
    return pl.pallas_call(
        _s5c_kernel,
        grid=(S5_Q, rows // rt),
        in_specs=[pl.BlockSpec((None, rt, S5_FLAT), lambda q, i: (q, i, 0)),
                  pl.BlockSpec((None, 2, 2, rt, S5_HALF), lambda q, i: (q, 0, 0, i, 0)),
                  pl.BlockSpec((None, N_LAG, S5_GROUP, LANE), lambda q, i: (q, 0, 0, 0)),
                  pl.BlockSpec((None, 4, SUB, S5_GROUP, S5_HALF), lambda q, i: (q, 0, 0, 0, 0)),
                  pl.BlockSpec((None, 1, S5_FLAT), lambda q, i: (q, 0, 0))],
        out_specs=pl.BlockSpec((rt * SUB, LANE), lambda q, i: (i, q)),
        out_shape=jax.ShapeDtypeStruct((rows * SUB, S5_WIDTH), BF16),
        scratch_shapes=[pltpu.VMEM((S5_FLAT, S5_FLAT), BF16), pltpu.VMEM((S5_ST, S5_FLAT), BF16),
                        pltpu.VMEM((N_LAG, LANE, LANE), BF16), pltpu.VMEM((rt * SUB, LANE), F32)],
        compiler_params=_params(("parallel", "arbitrary"), 56),
        name="s5_output",
    )(u4, xprev, k_lag, b_out, d4)


def _group_norm(o, g, b):
    mu = jnp.mean(o, axis=-1, keepdims=True)
    var = jnp.mean(jnp.square(o - mu), axis=-1, keepdims=True)
    return (o - mu) * lax.rsqrt(var + GN_EPS) * g + b


def _ret_kernel(dec_ref, q_ref, k_ref, v_ref, g_ref, cos_ref, sin_ref, s0_ref, gg_ref, gb_ref, y_ref, *rest,
                nc, use_rot, has_init, want_final):
    if want_final:
        sfin_ref, sf_scr, sb_scr = rest
    else:
        sf_scr, sb_scr = rest
    h = pl.program_id(1)
    row = _iota((CHUNK, CHUNK), 0).astype(F32)
    col = _iota((CHUNK, CHUNK), 1).astype(F32)
    lg_f = -jnp.exp(jnp.full((CHUNK, CHUNK), dec_ref[0, h], F32))
    lg_b = -jnp.exp(jnp.full((CHUNK, CHUNK), dec_ref[1, h], F32))
    lag = row - col
    dmat = (jnp.where(lag >= 0, jnp.exp(lg_f * jnp.maximum(lag, 0.0)), 0.0)
            + jnp.where(lag <= 0, jnp.exp(lg_b * jnp.maximum(-lag, 0.0)), 0.0))
    scale = HEAD_DIM ** -0.5
    qd_f = jnp.exp(lg_f * (row + 1.0))
    qd_b = jnp.exp(lg_b * (CHUNK - row))
    kd_f = jnp.exp(lg_f * (CHUNK - 1.0 - row)) * scale
    kd_b = jnp.exp(lg_b * row) * scale
    cd_f = jnp.exp(lg_f * CHUNK)
    cd_b = jnp.exp(lg_b * CHUNK)

    def chunk(j):
        return pl.ds(pl.multiple_of(j * CHUNK, CHUNK), CHUNK)

    def rot(ref, sl):
        x = ref[sl, :].astype(F32)
        if not use_rot:
            return x
        return x * cos_ref[sl, :] + pltpu.roll(x, HEAD_DIM // 2, 1) * sin_ref[sl, :]

    zero = jnp.zeros((CHUNK, CHUNK), F32)

    def states(jj, carry):
        s_f, s_b = carry
        jf, jb = jj, nc - 1 - jj
        sf_scr[jf] = s_f.astype(BF16)
        sb_scr[jb] = s_b.astype(BF16)
        slf, slb = chunk(jf), chunk(jb)
        s_f = s_f * cd_f + _dot_tn((rot(k_ref, slf) * kd_f).astype(BF16), v_ref[slf, :])
        s_b = s_b * cd_b + _dot_tn((rot(k_ref, slb) * kd_b).astype(BF16), v_ref[slb, :])
        return s_f, s_b

    init = (s0_ref[0], s0_ref[1]) if has_init else (zero, zero)
    s_f, s_b = lax.fori_loop(0, nc, states, init, unroll=2)
    if want_final:
        sfin_ref[0] = s_f
        sfin_ref[1] = s_b

    def outputs(j, carry):
        sl = chunk(j)
        q = rot(q_ref, sl)
        k = rot(k_ref, sl)
        att = _dot_nt(q.astype(BF16), (k * scale).astype(BF16)) * dmat
        lhs = jnp.concatenate([att.astype(BF16), (q * qd_f).astype(BF16), (q * qd_b).astype(BF16)], axis=1)
        rhs = jnp.concatenate([v_ref[sl, :], sf_scr[j], sb_scr[j]], axis=0)
        o = _dot(lhs, rhs)
        y = jax.nn.silu(g_ref[sl, :].astype(F32)) * _group_norm(o, gg_ref[...], gb_ref[...])
        y_ref[sl, :] = y.astype(y_ref.dtype)
        return carry

    lax.fori_loop(0, nc, outputs, 0, unroll=min(4, nc))


def _retention(proj, dec, cos_t, sin_t, s0, gn_g, gn_b, *, n_seq, seq_len, row0, use_rot, has_init, want_final):
    nc = seq_len // CHUNK
    blk0 = row0 // seq_len
    kern = functools.partial(_ret_kernel, nc=nc, use_rot=use_rot, has_init=has_init, want_final=want_final)

    def tok(cb):
        return pl.BlockSpec((seq_len, HEAD_DIM), lambda s, h, cb=cb: (blk0 + s, cb + h))

    rot_spec = pl.BlockSpec((seq_len, HEAD_DIM), lambda s, h: (0, 0))
    st_spec = pl.BlockSpec((None, 2, None, HEAD_DIM, HEAD_DIM), lambda s, h: (s, 0, h, 0, 0))
    gn_spec = pl.BlockSpec((1, HEAD_DIM), lambda s, h: (0, h))
    out_specs = [pl.BlockSpec((seq_len, HEAD_DIM), lambda s, h: (s, h))]
    out_shape = [jax.ShapeDtypeStruct((n_seq * seq_len, WIDTH), BF16)]
    if want_final:
        out_specs.append(st_spec)
        out_shape.append(jax.ShapeDtypeStruct((n_seq, 2, HEADS, HEAD_DIM, HEAD_DIM), F32))
    return pl.pallas_call(
        kern,
        grid=(n_seq, HEADS),
        in_specs=[pl.BlockSpec(memory_space=pltpu.SMEM), tok(0), tok(4), tok(8), tok(12),
                  rot_spec, rot_spec, st_spec, gn_spec, gn_spec],
        out_specs=out_specs,
        out_shape=out_shape,
        scratch_shapes=[pltpu.VMEM((nc, HEAD_DIM, HEAD_DIM), BF16), pltpu.VMEM((nc, HEAD_DIM, HEAD_DIM), BF16)],
        compiler_params=_params(("parallel", "parallel"), 48),
        name="retention",
    )(dec, proj, proj, proj, proj, cos_t, sin_t, s0, gn_g, gn_b)


def _gate_prep_kernel(g_ref, col_ref, row_ref):
    lane = _iota((CHUNK, LANE), 1)
    tri = jnp.where(_iota((CHUNK, CHUNK), 0) >= _iota((CHUNK, CHUNK), 1), 1.0, 0.0)
    for c in range(g_ref.shape[0] // CHUNK):
        sl = slice(c * CHUNK, (c + 1) * CHUNK)
        g = g_ref[sl, :]
        lf = jnp.where(lane < 4 * HEADS, jnp.minimum(g, 0.0) - jnp.log1p(jnp.exp(-jnp.abs(g))), 0.0)
        cs = _dot_hi(tri, lf)
        tot = cs[CHUNK - 1:CHUNK, :]
        bc = jnp.where(lane < 2 * HEADS, cs, tot - cs + lf)
        rest = jnp.where(lane < 2 * HEADS, tot - cs, cs - lf)
        pack = g + pltpu.roll(bc, GATE_BC, 1) + pltpu.roll(rest, GATE_REST, 1)
        col_ref[sl, :] = pack
        row_ref[sl, :] = pack.T


def _gate_prep(gates, cfg):
    t_tok = gates.shape[0]
    tm = cfg["tm_in"]
    spec = pl.BlockSpec((tm, LANE), lambda i: (i, 0))
    return pl.pallas_call(
        _gate_prep_kernel,
        grid=(t_tok // tm,),
        in_specs=[spec],
        out_specs=[spec, spec],
        out_shape=[jax.ShapeDtypeStruct((t_tok, LANE), F32)] * 2,
        compiler_params=_params(("parallel",), 32),
        name="gate_prep",
    )(gates)


def _mlstm_kernel(q_ref, k_ref, v_ref, o_ref, col_ref, row_ref, c0_ref, n0_ref, m0_ref, gg_ref, gb_ref, y_ref, *rest,
                  nc, has_init, want_final):
    if want_final:
        cfin_ref, nfin_ref, mfin_ref, ct_scr, n_scr, m_scr = rest
    else:
        ct_scr, n_scr, m_scr = rest
    row = _iota((CHUNK, CHUNK), 0)
    col = _iota((CHUNK, CHUNK), 1)
    scale = HEAD_DIM ** -0.5

    def chunk(j):
        return pl.ds(pl.multiple_of(j * CHUNK, CHUNK), CHUNK)

    def gate_idx(d, h):
        return d * 2 * HEADS + h, GATE_BC + d * 2 * HEADS + HEADS + h, GATE_REST + d * 2 * HEADS + HEADS + h

    def state_step(sl, d, h, carry):
        ct, nvec, m = carry
        ii, bi, ri = gate_idx(d, h)
        cp = col_ref[sl, :]
        rp = row_ref[sl, :]
        bc_row = rp[bi:bi + 1, :]
        b_last = bc_row[:, CHUNK - 1:CHUNK] if d == 0 else bc_row[:, 0:1]
        log_k_row = rp[ri:ri + 1, :] + rp[ii:ii + 1, :]
        log_k_col = cp[:, ri:ri + 1] + cp[:, ii:ii + 1]
        m_new = jnp.maximum(b_last + m, jnp.max(log_k_row, axis=1, keepdims=True))
        kw = jnp.exp(log_k_col - m_new)
        decay = jnp.exp(b_last + m - m_new)
        kf = k_ref[sl, :].astype(F32) * scale
        ct_new = decay * ct + _dot_tn(kf.astype(BF16), (v_ref[sl, :].astype(F32) * kw).astype(BF16))
        n_new = decay * nvec + jnp.sum(kw * kf, axis=0, keepdims=True)
        return ct_new, n_new, m_new

    def head_body(h):
        def init(d):
            if has_init:
                return c0_ref[d].T, n0_ref[d], m0_ref[d]
            return (jnp.zeros((HEAD_DIM, HEAD_DIM), F32), jnp.zeros((1, HEAD_DIM), F32),
                    jnp.zeros((1, HEAD_DIM), F32))

        def states(jj, carry):
            out = []
            for d, j in ((0, jj), (1, nc - 1 - jj)):
                ct, nvec, m = carry[d]
                ct_scr[d, j] = ct.astype(BF16)
                n_scr[d, j] = nvec
                m_scr[d, j] = m
                out.append(state_step(chunk(j), d, h, carry[d]))
            return tuple(out)

        fin = lax.fori_loop(0, nc, states, (init(0), init(1)), unroll=2)
        if want_final:
            for d in range(2):
                cfin_ref[d] = fin[d][0].T
                nfin_ref[d] = fin[d][1]
                mfin_ref[d] = fin[d][2]

        def outputs(j, carry):
            sl = chunk(j)
            q = q_ref[sl, :]
            qf = q.astype(F32)
            k = (k_ref[sl, :].astype(F32) * scale).astype(BF16)
            v = v_ref[sl, :]
            s = _dot_nt(q, k)
            cp = col_ref[sl, :]
            rp = row_ref[sl, :]
            lhs, rhs = [], []
            for d in range(2):
                ii, bi, _ = gate_idx(d, h)
                causal = (row >= col) if d == 0 else (row <= col)
                bc_col = cp[:, bi:bi + 1]
                log_d = jnp.where(causal, bc_col - rp[bi:bi + 1, :] + rp[ii:ii + 1, :], -jnp.inf)
                log_prev = bc_col + m_scr[d, j]
                m_t = jnp.maximum(log_prev, jnp.max(log_d, axis=1, keepdims=True))
                w = s * jnp.exp(log_d - m_t)
                w_prev = jnp.exp(log_prev - m_t)
                den = jnp.sum(w, axis=1, keepdims=True) + w_prev * jnp.sum(qf * n_scr[d, j], axis=1, keepdims=True)
                inv = 1.0 / jnp.maximum(jnp.abs(den), jnp.exp(-m_t))
                lhs += [(w * inv).astype(BF16), (qf * (w_prev * inv)).astype(BF16)]
                rhs += [v, ct_scr[d, j]]
            h_t = _dot(jnp.concatenate(lhs, axis=1), jnp.concatenate(rhs, axis=0))
            y = jax.nn.sigmoid(o_ref[sl, :].astype(F32)) * h_t
            y_ref[sl, :] = _group_norm(y, gg_ref[...], gb_ref[...]).astype(y_ref.dtype)
            return carry

        lax.fori_loop(0, nc, outputs, 0, unroll=2)

    hh = pl.program_id(1)
    for h in range(HEADS):
        pl.when(hh == h)(functools.partial(head_body, h))


def _mlstm(proj, gcol, grow, c0, n0, m0, gn_g, gn_b, *, n_seq, seq_len, row0, has_init, want_final):
    nc = seq_len // CHUNK
    blk0 = row0 // seq_len
    kern = functools.partial(_mlstm_kernel, nc=nc, has_init=has_init, want_final=want_final)

    def tok(cb):
        return pl.BlockSpec((seq_len, HEAD_DIM), lambda s, h, cb=cb: (blk0 + s, cb + h))

    gate_spec = pl.BlockSpec((seq_len, LANE), lambda s, h: (blk0 + s, 0))
    c_spec = pl.BlockSpec((None, 2, None, HEAD_DIM, HEAD_DIM), lambda s, h: (s, 0, h, 0, 0))
    v_spec = pl.BlockSpec((None, 2, None, 1, HEAD_DIM), lambda s, h: (s, 0, h, 0, 0))
    gn_spec = pl.BlockSpec((1, HEAD_DIM), lambda s, h: (0, h))
    out_specs = [pl.BlockSpec((seq_len, HEAD_DIM), lambda s, h: (s, h))]
    out_shape = [jax.ShapeDtypeStruct((n_seq * seq_len, WIDTH), BF16)]
    if want_final:
        out_specs += [c_spec, v_spec, v_spec]
        out_shape += [jax.ShapeDtypeStruct((n_seq, 2, HEADS, HEAD_DIM, HEAD_DIM), F32),
                      jax.ShapeDtypeStruct((n_seq, 2, HEADS, 1, HEAD_DIM), F32),
                      jax.ShapeDtypeStruct((n_seq, 2, HEADS, 1, HEAD_DIM), F32)]
    return pl.pallas_call(
        kern,
        grid=(n_seq, HEADS),
        in_specs=[tok(16), tok(20), tok(24), tok(28), gate_spec, gate_spec,
                  c_spec, v_spec, v_spec, gn_spec, gn_spec],
        out_specs=out_specs,
        out_shape=out_shape,
        scratch_shapes=[pltpu.VMEM((2, nc, HEAD_DIM, HEAD_DIM), BF16), pltpu.VMEM((2, nc, 1, HEAD_DIM), F32),
                        pltpu.VMEM((2, nc, 1, HEAD_DIM), F32)],
        compiler_params=_params(("parallel", "parallel"), 48),
        name="mlstm",
    )(proj, proj, proj, proj, gcol, grow, c0, n0, m0, gn_g, gn_b)


def _layer_norm(x, g, b):
    mu = jnp.mean(x, axis=-1, keepdims=True)
    var = jnp.mean(jnp.square(x - mu), axis=-1, keepdims=True)
    return (x - mu) * lax.rsqrt(var + LN_EPS) * g + b


def _merge_kernel(x_ref, mod_ref, z_ref, ybc_ref, ybs_ref, ycc_ref, ycs_ref, wm_ref, bm_ref, wglu_ref, bglu_ref,
                  wa_ref, wb_ref, wc_ref, wo_ref, lg_ref, lb_ref, o_ref, *, n_ctx_tiles):
    is_ctx = pl.program_id(0) < n_ctx_tiles
    x = x_ref[...]
    h = (x * (1.0 + mod_ref[1:2, :]) + mod_ref[0:1, :]).astype(BF16)
    z = z_ref[...]
    ya = (z.astype(F32) * jax.nn.sigmoid(_dot(z, wglu_ref[...]) + bglu_ref[...])).astype(BF16)
    yb = jnp.where(is_ctx, ybc_ref[...], ybs_ref[...])
    yc = jnp.where(is_ctx, ycc_ref[...], ycs_ref[...])
    merged = None
    for j, (y, w_ref) in enumerate(((ya, wa_ref), (yb, wb_ref), (yc, wc_ref))):
        gate = jax.nn.sigmoid(_dot(h, wm_ref[:, j * D_MODEL:(j + 1) * D_MODEL]) + bm_ref[:, j * D_MODEL:(j + 1) * D_MODEL])
        term = gate * _dot(y, w_ref[...])
        merged = term if merged is None else merged + term
    mix = _dot(merged.astype(BF16), wo_ref[...])
    o_ref[...] = _layer_norm(ALPHA * x + mod_ref[2:3, :] * mix, lg_ref[...], lb_ref[...])


def _merge(x, mod_l, z, yb_c, yb_s, yc_c, yc_s, wm, bm, wglu, bglu, wa, wb, wc, wo, lg, lb, cfg):
    t_tok = x.shape[0]
    tm = cfg["tm_merge"]
    n_ctx_tiles = cfg["n_ctx_tok"] // tm
    cond = functools.partial(_cond_row, tm=tm, n_ctx_tok=cfg["n_ctx_tok"], lat_len=cfg["lat_len"])

    def full(shape):
        return pl.BlockSpec(shape, lambda i: (0,) * len(shape))

    def tok(w):
        return pl.BlockSpec((tm, w), lambda i: (i, 0))

    ctx_spec = pl.BlockSpec((tm, WIDTH), lambda i: (jnp.minimum(i, n_ctx_tiles - 1), 0))
    lat_spec = pl.BlockSpec((tm, WIDTH), lambda i: (jnp.maximum(i - n_ctx_tiles, 0), 0))
    return pl.pallas_call(
        functools.partial(_merge_kernel, n_ctx_tiles=n_ctx_tiles),
        grid=(t_tok // tm,),
        in_specs=[tok(D_MODEL), pl.BlockSpec((None, 6, D_MODEL), lambda i: (cond(i), 0, 0)),
                  tok(WIDTH), ctx_spec, lat_spec, ctx_spec, lat_spec,
                  full((D_MODEL, 3 * D_MODEL)), full((1, 3 * D_MODEL)), full((WIDTH, WIDTH)), full((1, WIDTH)),
                  full((WIDTH, D_MODEL)), full((WIDTH, D_MODEL)), full((WIDTH, D_MODEL)),
                  full((D_MODEL, D_MODEL)), full((1, D_MODEL)), full((1, D_MODEL))],
        out_specs=tok(D_MODEL),
        out_shape=jax.ShapeDtypeStruct((t_tok, D_MODEL), F32),
        compiler_params=_params(("parallel",), 56),
        name="merge",
    )(x, mod_l, z, yb_c, yb_s, yc_c, yc_s, wm, bm, wglu, bglu, wa, wb, wc, wo, lg, lb)


def _ffn_kernel(x_ref, mod_ref, w1_ref, w3_ref, w2_ref, lg_ref, lb_ref, o_ref, h_scr, acc_scr):
    f = pl.program_id(1)

    @pl.when(f == 0)
    def _():
        h_scr[...] = (x_ref[...] * (1.0 + mod_ref[4:5, :]) + mod_ref[3:4, :]).astype(BF16)
        acc_scr[...] = jnp.zeros_like(acc_scr)

    h = h_scr[...]
    act = (jax.nn.silu(_dot(h, w1_ref[...])) * _dot(h, w3_ref[...])).astype(BF16)
    acc_scr[...] += _dot(act, w2_ref[...])

    @pl.when(f == pl.num_programs(1) - 1)
    def _():
        o_ref[...] = _layer_norm(ALPHA * x_ref[...] + mod_ref[5:6, :] * acc_scr[...], lg_ref[...], lb_ref[...])


def _dense_ffn(x, mod_l, w1, w3, w2, lg, lb, cfg):
    t_tok = x.shape[0]
    tm, tf = cfg["tm_ffn"], cfg["tf"]
    cond = functools.partial(_cond_row, tm=tm, n_ctx_tok=cfg["n_ctx_tok"], lat_len=cfg["lat_len"])
    return pl.pallas_call(
        _ffn_kernel,
        grid=(t_tok // tm, D_FF // tf),
        in_specs=[pl.BlockSpec((tm, D_MODEL), lambda i, f: (i, 0)),
                  pl.BlockSpec((None, 6, D_MODEL), lambda i, f: (cond(i), 0, 0)),
                  pl.BlockSpec((D_MODEL, tf), lambda i, f: (0, f)),
                  pl.BlockSpec((D_MODEL, tf), lambda i, f: (0, f)),
                  pl.BlockSpec((tf, D_MODEL), lambda i, f: (f, 0)),
                  pl.BlockSpec((1, D_MODEL), lambda i, f: (0, 0)),
                  pl.BlockSpec((1, D_MODEL), lambda i, f: (0, 0))],
        out_specs=pl.BlockSpec((tm, D_MODEL), lambda i, f: (i, 0)),
        out_shape=jax.ShapeDtypeStruct((t_tok, D_MODEL), F32),
        scratch_shapes=[pltpu.VMEM((tm, D_MODEL), BF16), pltpu.VMEM((tm, D_MODEL), F32)],
        compiler_params=_params(("parallel", "arbitrary"), 56),
        name="dense_ffn",
    )(x, mod_l, w1, w3, w2, lg, lb)


def _router_kernel(x_ref, mod_ref, w_ref, b_ref, g_ref):
    h = x_ref[...] * (1.0 + mod_ref[4:5, :]) + mod_ref[3:4, :]
    lane = _iota(g_ref.shape, 1)
    logits = jnp.where(lane < N_EXPERTS, _dot_hi(h, w_ref[...]) + b_ref[...], NEG_BIG)
    m1 = jnp.max(logits, axis=1, keepdims=True)
    i1 = jnp.min(jnp.where(logits == m1, lane, LANE), axis=1, keepdims=True)
    rest = jnp.where(lane == i1, NEG_BIG, logits)
    m2 = jnp.max(rest, axis=1, keepdims=True)
    i2 = jnp.min(jnp.where(rest == m2, lane, LANE), axis=1, keepdims=True)
    e2 = jnp.exp(m2 - m1)
    den = 1.0 + e2
    g_ref[...] = jnp.where(lane == i1, 1.0 / den, 0.0) + jnp.where(lane == i2, e2 / den, 0.0)


def _router(x, mod_l, w, b, cfg):
    t_tok = x.shape[0]
    tm = cfg["tm_ffn"]
    cond = functools.partial(_cond_row, tm=tm, n_ctx_tok=cfg["n_ctx_tok"], lat_len=cfg["lat_len"])
    return pl.pallas_call(
        _router_kernel,
        grid=(t_tok // tm,),
        in_specs=[pl.BlockSpec((tm, D_MODEL), lambda i: (i, 0)),
                  pl.BlockSpec((None, 6, D_MODEL), lambda i: (cond(i), 0, 0)),
                  pl.BlockSpec((D_MODEL, LANE), lambda i: (0, 0)),
                  pl.BlockSpec((1, LANE), lambda i: (0, 0))],
        out_specs=pl.BlockSpec((tm, LANE), lambda i: (i, 0)),
        out_shape=jax.ShapeDtypeStruct((t_tok, LANE), F32),
        compiler_params=_params(("parallel",), 40),
        name="router",
    )(x, mod_l, w, b)


def _moe_kernel(x_ref, mod_ref, gate_ref, w1_ref, w3_ref, w2_ref, lg_ref, lb_ref, o_ref,
                h_scr, acc_scr, hc_scr, ob_scr, sp_scr, gt_scr, cnt_smem):
    e = pl.program_id(1)
    f = pl.program_id(2)
    last_f = pl.num_programs(2) - 1
    tm = x_ref.shape[0]

    @pl.when((e == 0) & (f == 0))
    def _():
        h_scr[...] = (x_ref[...] * (1.0 + mod_ref[4:5, :]) + mod_ref[3:4, :]).astype(BF16)
        acc_scr[...] = jnp.zeros_like(acc_scr)
        g = gate_ref[...]
        sel = g > 0.0
        ones = jnp.where(sel, 1.0, 0.0)
        before = jnp.where(_iota((tm, tm), 0) > _iota((tm, tm), 1), 1.0, 0.0).astype(BF16)
        pos = _dot(before, ones.astype(BF16))
        spt = jnp.where(sel, pos, -1.0).T
        gt = g.T
        cnt = jnp.sum(ones, axis=0, keepdims=True)
        for ee in range(N_EXPERTS):
            sp_scr[ee] = spt[ee:ee + 1, :]
            gt_scr[ee] = gt[ee:ee + 1, :]
            cnt_smem[ee] = cnt[0, ee].astype(jnp.int32)

    nb = (cnt_smem[e] + (MOE_BLK - 1)) // MOE_BLK
    row = _iota((MOE_BLK, tm), 0)

    def onehot(b):
        return sp_scr[e] == (row + b * MOE_BLK).astype(F32)

    @pl.when(f == 0)
    def _():
        def gather(b, carry):
            p = jnp.where(onehot(b), 1.0, 0.0).astype(BF16)
            hc_scr[b] = _dot(p, h_scr[...]).astype(BF16)
            ob_scr[b] = jnp.zeros((MOE_BLK, D_MODEL), F32)
            return carry
        lax.fori_loop(0, nb, gather, 0)

    def ffn(b, carry):
        hc = hc_scr[b]
        act = (jax.nn.silu(_dot(hc, w1_ref[...])) * _dot(hc, w3_ref[...])).astype(BF16)
        ob_scr[b] += _dot(act, w2_ref[...])
        return carry
    lax.fori_loop(0, nb, ffn, 0)

    @pl.when(f == last_f)
    def _():
        def scatter(b, carry):
            m = onehot(b)
            gc = jnp.sum(jnp.where(m, gt_scr[e], 0.0), axis=1, keepdims=True)
            og = (ob_scr[b] * gc).astype(BF16)
            acc_scr[...] += _dot_tn(jnp.where(m, 1.0, 0.0).astype(BF16), og)
            return carry
        lax.fori_loop(0, nb, scatter, 0)

    @pl.when((e == pl.num_programs(1) - 1) & (f == last_f))
    def _():
        o_ref[...] = _layer_norm(ALPHA * x_ref[...] + mod_ref[5:6, :] * acc_scr[...], lg_ref[...], lb_ref[...])


def _moe_ffn(x, mod_l, gates, w1, w3, w2, lg, lb, cfg):
    t_tok = x.shape[0]
    tm, tf = cfg["tm_moe"], cfg["tf"]
    cond = functools.partial(_cond_row, tm=tm, n_ctx_tok=cfg["n_ctx_tok"], lat_len=cfg["lat_len"])
    return pl.pallas_call(
        _moe_kernel,
        grid=(t_tok // tm, N_EXPERTS, D_FF // tf),
        in_specs=[pl.BlockSpec((tm, D_MODEL), lambda i, e, f: (i, 0)),
                  pl.BlockSpec((None, 6, D_MODEL), lambda i, e, f: (cond(i), 0, 0)),
                  pl.BlockSpec((tm, LANE), lambda i, e, f: (i, 0)),
                  pl.BlockSpec((None, D_MODEL, tf), lambda i, e, f: (e, 0, f)),
                  pl.BlockSpec((None, D_MODEL, tf), lambda i, e, f: (e, 0, f)),
                  pl.BlockSpec((None, tf, D_MODEL), lambda i, e, f: (e, f, 0)),
                  pl.BlockSpec((1, D_MODEL), lambda i, e, f: (0, 0)),
                  pl.BlockSpec((1, D_MODEL), lambda i, e, f: (0, 0))],
        out_specs=pl.BlockSpec((tm, D_MODEL), lambda i, e, f: (i, 0)),
        out_shape=jax.ShapeDtypeStruct((t_tok, D_MODEL), F32),
        scratch_shapes=[pltpu.VMEM((tm, D_MODEL), BF16), pltpu.VMEM((tm, D_MODEL), F32),
                        pltpu.VMEM((tm // MOE_BLK, MOE_BLK, D_MODEL), BF16),
                        pltpu.VMEM((tm // MOE_BLK, MOE_BLK, D_MODEL), F32),
                        pltpu.VMEM((N_EXPERTS, 1, tm), F32), pltpu.VMEM((N_EXPERTS, 1, tm), F32),
                        pltpu.SMEM((N_EXPERTS,), jnp.int32)],
        compiler_params=_params(("parallel", "arbitrary", "arbitrary"), 56),
        name="moe_ffn",
    )(x, mod_l, gates, w1, w3, w2, lg, lb)


def _rotary_tables(n_tok):
    rows = n_tok // GRID_W
    r = jnp.repeat(jnp.arange(rows, dtype=F32), GRID_W)
    col = jnp.tile(jnp.arange(GRID_W, dtype=F32), rows)
    n_freq = HEAD_DIM // 4
    inv = ROPE_BASE ** (-jnp.arange(n_freq, dtype=F32) / n_freq)
    ang = jnp.concatenate([r[:, None] * inv, col[:, None] * inv], -1)
    cos, sin = jnp.cos(ang), jnp.sin(ang)
    return jnp.concatenate([cos, cos], -1), jnp.concatenate([-sin, sin], -1)


def _row2(v):
    return v.reshape(1, -1).astype(F32)


def kernel(x_prompt, x_sample, cache_s5_re, cache_s5_im, cache_ret, cache_ml_c, cache_ml_n, cache_ml_m, c, c_ctx, ada_w, ada_b, w_in, b_in, s5_lam_re, s5_lam_im, s5_log_step, s5_b_re, s5_b_im, s5_c_re, s5_c_im, s5_d, s5_glu_w, s5_glu_b, ret_decay, ret_gn_g, ret_gn_b, ml_gn_g, ml_gn_b, w_a, w_b, w_c, w_o, ln1_g, ln1_b, ln2_g, ln2_b, ffn_w1, ffn_w3, ffn_w2, moe_router, moe_router_b, moe_w1, moe_w3, moe_w2):
    n_ctx_seq, ctx_len, _ = x_prompt.shape
    n_lat_seq, lat_len, _ = x_sample.shape
    n_ctx_tok = n_ctx_seq * ctx_len
    n_lat_tok = n_lat_seq * lat_len
    t_tok = n_ctx_tok + n_lat_tok
    assert n_lat_seq + 1 <= N_COND and n_ctx_seq % 8 == 0 and n_lat_seq % 8 == 0
    assert ctx_len % CHUNK == 0 and lat_len % CHUNK == 0 and n_ctx_tok % lat_len == 0
    tm = min(1024, lat_len)
    cfg = dict(n_ctx_seq=n_ctx_seq, ctx_len=ctx_len, n_lat_seq=n_lat_seq, lat_len=lat_len, n_ctx_tok=n_ctx_tok,
               tm_in=tm, tm_merge=min(256, lat_len), tm_ffn=min(512, lat_len), tm_moe=min(1024, lat_len), tf=1408,
               s5_rows=min(256, t_tok // SUB // 8))

    x = jnp.concatenate([x_prompt.reshape(n_ctx_tok, D_MODEL), x_sample.reshape(n_lat_tok, D_MODEL)], 0)
    cond = jnp.zeros((N_COND, D_MODEL), F32).at[0].set(c_ctx).at[1:1 + n_lat_seq].set(c)
    mod = _modulation(cond, ada_w, ada_b).reshape(DEPTH, N_COND, 6, D_MODEL)
    cos_t, sin_t = _rotary_tables(lat_len)

    n_main = S5_WIDTH + 8 * WIDTH
    gate_off = n_main
    merge_off = gate_off + 4 * HEADS
    s5_fac = jax.vmap(_s5_factors)(s5_lam_re, s5_lam_im, s5_log_step, s5_b_re, s5_b_im, s5_c_re, s5_c_im)

    st_s5, st_ret, st_c, st_n, st_m = [], [], [], [], []
    zero_ret = jnp.zeros((n_ctx_seq, 2, HEADS, HEAD_DIM, HEAD_DIM), F32)
    zero_vec = jnp.zeros((n_ctx_seq, 2, HEADS, 1, HEAD_DIM), F32)
    for l in range(DEPTH):
        mod_l = mod[l]
        w_main = w_in[l][:, :n_main].astype(BF16)
        b_main = _row2(b_in[l][:n_main])
        w_gate = jnp.zeros((D_MODEL, LANE), F32).at[:, :4 * HEADS].set(w_in[l][:, gate_off:merge_off]).astype(BF16)
        b_gate = jnp.zeros((1, LANE), F32).at[0, :4 * HEADS].set(b_in[l][gate_off:merge_off])
        proj, u4, gates = _inproj(x, mod_l, w_main, b_main, w_gate, b_gate, cfg)

        s5_ain, s5_klag, s5_bout, s5_a = (m[l] for m in s5_fac)
        loc = _s5_state_in(u4, s5_ain, cfg)
        x0 = jnp.stack([cache_s5_re[:, l], cache_s5_im[:, l]], 0)
        x0 = x0.reshape(2, n_lat_seq, 2, S5_Q, S5_HALF).transpose(3, 2, 0, 1, 4).astype(F32)
        xprev, s5_fin = _s5_scan(loc, s5_a, x0, cfg)
        d4 = jnp.tile(s5_d[l].astype(F32).reshape(S5_Q, 1, LANE), (1, SUB, 1)).reshape(S5_Q, 1, S5_FLAT)
        z = _s5_output(u4, xprev, s5_klag, s5_bout, d4, cfg)
        st_s5.append(s5_fin)

        gg, gb = _row2(ret_gn_g[l]), _row2(ret_gn_b[l])
        dec = ret_decay[l].astype(F32)
        yb_c, ret_fin = _retention(proj, dec, cos_t, sin_t, zero_ret, gg, gb, n_seq=n_ctx_seq, seq_len=ctx_len,
                                   row0=0, use_rot=False, has_init=False, want_final=True)
        yb_s, = _retention(proj, dec, cos_t, sin_t, cache_ret[:, l].astype(F32), gg, gb, n_seq=n_lat_seq,
                           seq_len=lat_len, row0=n_ctx_tok, use_rot=True, has_init=True, want_final=False)
        st_ret.append(ret_fin)

        gg, gb = _row2(ml_gn_g[l]), _row2(ml_gn_b[l])
        gcol, grow = _gate_prep(gates, cfg)
        yc_c, c_fin, n_fin, m_fin = _mlstm(proj, gcol, grow, zero_ret, zero_vec, zero_vec, gg, gb, n_seq=n_ctx_seq,
                                           seq_len=ctx_len, row0=0, has_init=False, want_final=True)
        n0 = cache_ml_n[:, l].astype(F32)[:, :, :, None, :]
        m0 = jnp.broadcast_to(cache_ml_m[:, l].astype(F32)[:, :, :, None, None], n0.shape)
        yc_s, = _mlstm(proj, gcol, grow, cache_ml_c[:, l].astype(F32), n0, m0, gg, gb, n_seq=n_lat_seq,
                       seq_len=lat_len, row0=n_ctx_tok, has_init=True, want_final=False)
        st_c.append(c_fin)
        st_n.append(n_fin[:, :, :, 0, :])
        st_m.append(m_fin[:, :, :, 0, 0])

        x = _merge(x, mod_l, z, yb_c, yb_s, yc_c, yc_s, w_in[l][:, merge_off:].astype(BF16),
                   _row2(b_in[l][merge_off:]), s5_glu_w[l].astype(BF16), _row2(s5_glu_b[l]), w_a[l].astype(BF16),
                   w_b[l].astype(BF16), w_c[l].astype(BF16), w_o[l].astype(BF16), _row2(ln1_g[l]), _row2(ln1_b[l]), cfg)

        j = l // 2
        if l % 2 == 0:
            x = _dense_ffn(x, mod_l, ffn_w1[j].astype(BF16), ffn_w3[j].astype(BF16), ffn_w2[j].astype(BF16),
                           _row2(ln2_g[l]), _row2(ln2_b[l]), cfg)
        else:
            rw = jnp.zeros((D_MODEL, LANE), F32).at[:, :N_EXPERTS].set(moe_router[j])
            rb = jnp.zeros((1, LANE), F32).at[0, :N_EXPERTS].set(moe_router_b[j])
            mg = _router(x, mod_l, rw, rb, cfg)
            x = _moe_ffn(x, mod_l, mg, moe_w1[j].astype(BF16), moe_w3[j].astype(BF16), moe_w2[j].astype(BF16),
                         _row2(ln2_g[l]), _row2(ln2_b[l]), cfg)

    y_p = x[:n_ctx_tok].reshape(n_ctx_seq, ctx_len, D_MODEL)
    y_s = x[n_ctx_tok:].reshape(n_lat_seq, lat_len, D_MODEL)
    s5 = jnp.stack(st_s5, 0)
    s5 = s5.reshape(DEPTH, S5_Q, 2, 2, n_ctx_seq, S5_QG, S5_STATE).transpose(3, 4, 0, 2, 1, 5, 6)
    s5 = s5.reshape(2, n_ctx_seq, DEPTH, 2, S5_GROUPS, S5_STATE)
    return (y_p, y_s, s5[0], s5[1], jnp.stack(st_ret, 1), jnp.stack(st_c, 1), jnp.stack(st_n, 1),
            jnp.stack(st_m, 1))
```

```python
import functools

import jax
import jax.numpy as jnp
from jax import lax
from jax.experimental import pallas as pl
from jax.experimental.pallas import tpu as pltpu

F32 = jnp.float32
BF16 = jnp.bfloat16

D_MODEL = 1024
DEPTH = 4
GRID_W = 64
CHUNK = 128
S5_WIDTH = 512
S5_GROUP = 16
S5_GROUPS = 32
S5_STATE = 64
HEADS = 4
HEAD_DIM = 128
WIDTH = 512
ROPE_BASE = 10000.0
D_FF = 2816
N_EXPERTS = 8
ALPHA = (2.0 * DEPTH) ** 0.25
LN_EPS = 1e-5
GN_EPS = 1e-5
N_COND = 16
SUB = 16
N_LAG = 2 * SUB - 1
LANE = 128
S5_Q = S5_WIDTH // LANE
S5_QG = LANE // S5_GROUP
S5_FLAT = SUB * LANE
S5_HALF = S5_QG * S5_STATE
S5_ST = 4 * S5_HALF
NEG_BIG = -1e30
MOE_BLK = 128
GATE_BC = 16
GATE_REST = 32


def _dot(a, b):
    return jnp.dot(a, b, preferred_element_type=F32)


def _dot_hi(a, b):
    return jnp.dot(a, b, preferred_element_type=F32, precision=lax.Precision.HIGHEST)


def _dot_nt(a, b):
    return lax.dot_general(a, b, (((1,), (1,)), ((), ())), preferred_element_type=F32)


def _dot_tn(a, b):
    return lax.dot_general(a, b, (((0,), (0,)), ((), ())), preferred_element_type=F32)


def _params(sem, vmem_mb):
    return pltpu.CompilerParams(dimension_semantics=sem, vmem_limit_bytes=vmem_mb << 20)


def _cond_row(tile, tm, n_ctx_tok, lat_len):
    start = tile * tm
    return jnp.where(start < n_ctx_tok, 0, 1 + (start - n_ctx_tok) // lat_len)


def _iota(shape, axis):
    return lax.broadcasted_iota(jnp.int32, shape, axis)


def _mod_kernel(c_ref, w_ref, b_ref, o_ref):
    o_ref[...] = _dot_hi(jax.nn.silu(c_ref[...]), w_ref[...]) + b_ref[...]


def _modulation(cond, ada_w, ada_b):
    tn = 1536
    n = ada_w.shape[-1]
    return pl.pallas_call(
        _mod_kernel,
        grid=(DEPTH, n // tn),
        in_specs=[pl.BlockSpec((N_COND, D_MODEL), lambda l, j: (0, 0)),
                  pl.BlockSpec((None, D_MODEL, tn), lambda l, j: (l, 0, j)),
                  pl.BlockSpec((None, 1, tn), lambda l, j: (l, 0, j))],
        out_specs=pl.BlockSpec((None, N_COND, tn), lambda l, j: (l, 0, j)),
        out_shape=jax.ShapeDtypeStruct((DEPTH, N_COND, n), F32),
        compiler_params=_params(("parallel", "parallel"), 40),
        name="modulation",
    )(cond, ada_w, ada_b.reshape(DEPTH, 1, n))


def _inproj_kernel(x_ref, mod_ref, w_ref, b_ref, wg_ref, bg_ref, o_ref, u4_ref, g_ref, h_scr, u_scr):
    j = pl.program_id(1)
    tm = x_ref.shape[0]

    @pl.when(j == 0)
    def _():
        h = (x_ref[...] * (1.0 + mod_ref[1:2, :]) + mod_ref[0:1, :]).astype(BF16)
        h_scr[...] = h
        g_ref[...] = _dot(h, wg_ref[...]) + bg_ref[...]
        u = _dot(h, w_ref[...]) + b_ref[...]
        for q in range(S5_Q):
            u_scr[q] = u[:, q * LANE:(q + 1) * LANE]
            for t in range(SUB):
                u4_ref[q, :, t * LANE:(t + 1) * LANE] = u_scr[q, pl.ds(t, tm // SUB, stride=SUB), :].astype(BF16)

    @pl.when(j > 0)
    def _():
        o_ref[...] = (_dot(h_scr[...], w_ref[...]) + b_ref[...]).astype(o_ref.dtype)


def _inproj(x, mod_l, w, b, wg, bg, cfg):
    t_tok = x.shape[0]
    tm, tn = cfg["tm_in"], 512
    nj = w.shape[1] // tn
    cond = functools.partial(_cond_row, tm=tm, n_ctx_tok=cfg["n_ctx_tok"], lat_len=cfg["lat_len"])
    return pl.pallas_call(
        _inproj_kernel,
        grid=(t_tok // tm, nj),
        in_specs=[pl.BlockSpec((tm, D_MODEL), lambda i, j: (i, 0)),
                  pl.BlockSpec((None, 6, D_MODEL), lambda i, j: (cond(i), 0, 0)),
                  pl.BlockSpec((D_MODEL, tn), lambda i, j: (0, j)),
                  pl.BlockSpec((1, tn), lambda i, j: (0, j)),
                  pl.BlockSpec((D_MODEL, LANE), lambda i, j: (0, 0)),
                  pl.BlockSpec((1, LANE), lambda i, j: (0, 0))],
        out_specs=[pl.BlockSpec((tm, tn), lambda i, j: (i, jnp.maximum(j - 1, 0))),
                   pl.BlockSpec((S5_Q, tm // SUB, S5_FLAT), lambda i, j: (0, i, 0)),
                   pl.BlockSpec((tm, LANE), lambda i, j: (i, 0))],
        out_shape=[jax.ShapeDtypeStruct((t_tok, (nj - 1) * tn), BF16),
                   jax.ShapeDtypeStruct((S5_Q, t_tok // SUB, S5_FLAT), BF16),
                   jax.ShapeDtypeStruct((t_tok, LANE), F32)],
        scratch_shapes=[pltpu.VMEM((tm, D_MODEL), BF16), pltpu.VMEM((S5_Q, tm, LANE), F32)],
        compiler_params=_params(("parallel", "arbitrary"), 48),
        name="inproj",
    )(x, mod_l, w, b, wg, bg)


def _s5_factors(lam_re, lam_im, log_step, b_re, b_im, c_re, c_im):
    lam = lax.complex(lam_re.astype(F32), lam_im.astype(F32))
    lam_dt = lam * jnp.exp(log_step.astype(F32))[..., None]
    lam_bar = jnp.exp(lam_dt)
    bbar = ((lam_bar - 1.0) / lam)[..., None] * lax.complex(b_re.astype(F32), b_im.astype(F32))
    cmat = lax.complex(c_re.astype(F32), c_im.astype(F32))
    ks = jnp.arange(SUB + 1, dtype=F32)
    pw = jnp.exp(lam_dt[None] * ks[:, None, None, None])
    kern = jnp.einsum('dgcp,tdgp,dgpe->dgtce', cmat, pw[:SUB], bbar).real
    pad = jnp.zeros_like(kern[0][:, :SUB - 1])
    ktab = jnp.concatenate([pad, kern[0]], 1) + jnp.concatenate([kern[1][:, ::-1], pad], 1)
    k_lag = ktab.reshape(S5_Q, S5_QG, N_LAG, S5_GROUP, S5_GROUP).transpose(0, 2, 3, 1, 4)
    k_lag = k_lag.reshape(S5_Q, N_LAG, S5_GROUP, LANE)
    pw_in = jnp.stack([pw[:SUB][::-1, 0], pw[:SUB][:, 1]], 0)
    wb = pw_in[..., None] * bbar[:, None]
    wb = jnp.stack([wb.real, wb.imag], 1).reshape(2, 2, SUB, S5_Q, S5_QG, S5_STATE, S5_GROUP)
    a_in = wb.transpose(3, 2, 0, 1, 5, 4, 6).reshape(S5_Q, SUB, 4, S5_STATE, LANE)
    pw_out = jnp.stack([pw[1:, 0], pw[1:][::-1, 1]], 0)
    ce = cmat[:, None] * pw_out[:, :, :, None, :]
    ce = jnp.stack([ce.real, -ce.imag], 1).reshape(2, 2, SUB, S5_Q, S5_QG, S5_GROUP, S5_STATE)
    b_out = ce.transpose(3, 0, 1, 2, 5, 4, 6).reshape(S5_Q, 4, SUB, S5_GROUP, S5_HALF)
    a = pw[SUB]
    a = jnp.stack([a.real, a.imag], 1).reshape(2, 2, S5_Q, 1, S5_HALF).transpose(2, 0, 1, 3, 4)
    return a_in.astype(BF16), k_lag.astype(BF16), b_out.astype(BF16), a


def _expand(src_t, n_rep, row_shift, col_shift):
    k, r = src_t.shape
    rep = jnp.where(_iota((k, n_rep * k), 0) == (_iota((k, n_rep * k), 1) & (k - 1)), 1.0, 0.0).astype(BF16)
    same = (_iota((r, n_rep * k), 0) >> row_shift) == (_iota((r, n_rep * k), 1) >> col_shift)
    return jnp.where(same, _dot_tn(src_t, rep), 0.0).astype(BF16)


def _s5a_kernel(u_ref, a_ref, o_ref, w_scr):
    @pl.when(pl.program_id(1) == 0)
    def _():
        for t in range(SUB):
            for k in range(4):
                w_scr[t * LANE:(t + 1) * LANE, k * S5_HALF:(k + 1) * S5_HALF] = _expand(a_ref[t, k], S5_QG, 4, 6)

    res = _dot(u_ref[...], w_scr[...])
    for d in range(2):
        for r in range(2):
            k = 2 * d + r
            o_ref[d, r] = res[:, k * S5_HALF:(k + 1) * S5_HALF]


def _s5_state_in(u4, a_in, cfg):
    rows = u4.shape[1]
    rt = cfg["s5_rows"]
    return pl.pallas_call(
        _s5a_kernel,
        grid=(S5_Q, rows // rt),
        in_specs=[pl.BlockSpec((None, rt, S5_FLAT), lambda q, i: (q, i, 0)),
                  pl.BlockSpec((None, SUB, 4, S5_STATE, LANE), lambda q, i: (q, 0, 0, 0, 0))],
        out_specs=pl.BlockSpec((None, 2, 2, rt, S5_HALF), lambda q, i: (q, 0, 0, i, 0)),
        out_shape=jax.ShapeDtypeStruct((S5_Q, 2, 2, rows, S5_HALF), F32),
        scratch_shapes=[pltpu.VMEM((S5_FLAT, S5_ST), BF16)],
        compiler_params=_params(("parallel", "arbitrary"), 48),
        name="s5_state_in",
    )(u4, a_in)


def _s5b_kernel(loc_ref, a_ref, x0_ref, xp_ref, fin_ref, *, n_ctx_seq, ctx_sub, n_lat_seq, lat_sub):
    d = pl.program_id(1)
    ar = jnp.broadcast_to(a_ref[0], (8, LANE))
    ai = jnp.broadcast_to(a_ref[1], (8, LANE))

    def run(base, nsub, xr0, xi0):
        def body(jj, carry):
            xr, xi = carry
            j = jnp.where(d == 0, jj, nsub - 1 - jj)
            idx = pl.ds(base + j, 8, stride=nsub)
            xp_ref[0, idx, :] = xr
            xp_ref[1, idx, :] = xi
            lr = loc_ref[0, idx, :]
            li = loc_ref[1, idx, :]
            return ar * xr - ai * xi + lr, ar * xi + ai * xr + li
        return lax.fori_loop(0, nsub, body, (xr0, xi0))

    zero = jnp.zeros((8, LANE), F32)
    for bg in range(n_ctx_seq // 8):
        xr, xi = run(bg * 8 * ctx_sub, ctx_sub, zero, zero)
        fin_ref[0, bg * 8:(bg + 1) * 8, :] = xr
        fin_ref[1, bg * 8:(bg + 1) * 8, :] = xi
    for bg in range(n_lat_seq // 8):
        run(n_ctx_seq * ctx_sub + bg * 8 * lat_sub, lat_sub, x0_ref[0, bg * 8:(bg + 1) * 8, :],
            x0_ref[1, bg * 8:(bg + 1) * 8, :])


def _s5_scan(loc, a, x0, cfg):
    rows = loc.shape[3]
    n_ctx_seq, n_lat_seq = cfg["n_ctx_seq"], cfg["n_lat_seq"]
    kern = functools.partial(_s5b_kernel, n_ctx_seq=n_ctx_seq, ctx_sub=cfg["ctx_len"] // SUB,
                             n_lat_seq=n_lat_seq, lat_sub=cfg["lat_len"] // SUB)
    nlb = S5_HALF // LANE
    return pl.pallas_call(
        kern,
        grid=(S5_Q, 2, nlb),
        in_specs=[pl.BlockSpec((None, None, 2, rows, LANE), lambda q, d, b: (q, d, 0, 0, b)),
                  pl.BlockSpec((None, None, 2, 1, LANE), lambda q, d, b: (q, d, 0, 0, b)),
                  pl.BlockSpec((None, None, 2, n_lat_seq, LANE), lambda q, d, b: (q, d, 0, 0, b))],
        out_specs=[pl.BlockSpec((None, None, 2, rows, LANE), lambda q, d, b: (q, d, 0, 0, b)),
                   pl.BlockSpec((None, None, 2, n_ctx_seq, LANE), lambda q, d, b: (q, d, 0, 0, b))],
        out_shape=[jax.ShapeDtypeStruct(loc.shape, F32),
                   jax.ShapeDtypeStruct((S5_Q, 2, 2, n_ctx_seq, S5_HALF), F32)],
        compiler_params=_params(("parallel", "parallel", "parallel"), 48),
        name="s5_scan",
    )(loc, a, x0)


def _s5c_kernel(u_ref, xp_ref, k_ref, b_ref, d_ref, z_ref, m_scr, wo_scr, bd_scr, z_scr):
    rt = u_ref.shape[0]

    @pl.when(pl.program_id(1) == 0)
    def _():
        for l in range(N_LAG):
            bd_scr[l] = _expand(k_ref[l], S5_QG, 4, 4)
        for t in range(SUB):
            for s in range(SUB):
                m_scr[t * LANE:(t + 1) * LANE, s * LANE:(s + 1) * LANE] = bd_scr[s - t + SUB - 1]
        for k in range(4):
            for t in range(SUB):
                wo_scr[k * S5_HALF:(k + 1) * S5_HALF, t * LANE:(t + 1) * LANE] = _expand(b_ref[k, t], S5_QG, 6, 4)

    u = u_ref[...]
    xcat = jnp.concatenate([xp_ref[0, 0], xp_ref[0, 1], xp_ref[1, 0], xp_ref[1, 1]], axis=1).astype(BF16)
    y = _dot(u, m_scr[...]) + _dot(xcat, wo_scr[...])
    z = jax.nn.gelu(d_ref[...] * u.astype(F32) + y)
    for t in range(SUB):
        z_scr[pl.ds(t, rt, stride=SUB), :] = z[:, t * LANE:(t + 1) * LANE]
    z_ref[...] = z_scr[...].astype(z_ref.dtype)


def _s5_output(u4, xprev, k_lag, b_out, d4, cfg):
    rows = u4.shape[1]
    rt = cfg["s5_rows"]
    return pl.pallas_call(
        _s5c_kernel,
        grid=(S5_Q, rows // rt),
        in_specs=[pl.BlockSpec((None, rt, S5_FLAT), lambda q, i: (q, i, 0)),
                  pl.BlockSpec((None, 2, 2, rt, S5_HALF), lambda q, i: (q, 0, 0, i, 0)),
                  pl.BlockSpec((None, N_LAG, S5_GROUP, LANE), lambda q, i: (q, 0, 0, 0)),
                  pl.BlockSpec((None, 4, SUB, S5_GROUP, S5_HALF), lambda q, i: (q, 0, 0, 0, 0)),
                  pl.BlockSpec((None, 1, S5_FLAT), lambda q, i: (q, 0, 0))],
        out_specs=pl.BlockSpec((rt * SUB, LANE), lambda q, i: (i, q)),
        out_shape=jax.ShapeDtypeStruct((rows * SUB, S5_WIDTH), BF16),
        scratch_shapes=[pltpu.VMEM((S5_FLAT, S5_FLAT), BF16), pltpu.VMEM((S5_ST, S5_FLAT), BF16),
                        pltpu.VMEM((N_LAG, LANE, LANE), BF16), pltpu.VMEM((rt * SUB, LANE), F32)],
        compiler_params=_params(("parallel", "arbitrary"), 56),
        name="s5_output",
    )(u4, xprev, k_lag, b_out, d4)


def _group_norm(o, g, b):
    mu = jnp.mean(o, axis=-1, keepdims=True)
    var = jnp.mean(jnp.square(o - mu), axis=-1, keepdims=True)
    return (o - mu) * lax.rsqrt(var + GN_EPS) * g + b


def _ret_kernel(dec_ref, q_ref, k_ref, v_ref, g_ref, cos_ref, sin_ref, s0_ref, gg_ref, gb_ref, y_ref, *rest,
                nc, use_rot, has_init, want_final):
    if want_final:
        sfin_ref, sf_scr, sb_scr, x_scr, kr_scr = rest
    else:
        sf_scr, sb_scr, x_scr, kr_scr = rest
    h = pl.program_id(1)
    row = _iota((CHUNK, CHUNK), 0).astype(F32)
    col = _iota((CHUNK, CHUNK), 1).astype(F32)
    lg_f = -jnp.exp(jnp.full((CHUNK, CHUNK), dec_ref[0, h], F32))
    lg_b = -jnp.exp(jnp.full((CHUNK, CHUNK), dec_ref[1, h], F32))
    lag = row - col
    scale = HEAD_DIM ** -0.5
    dmat = (jnp.where(lag >= 0, jnp.exp(lg_f * jnp.maximum(lag, 0.0)), 0.0)
            + jnp.where(lag <= 0, jnp.exp(lg_b * jnp.maximum(-lag, 0.0)), 0.0)) * scale
    qd_f = jnp.exp(lg_f * (row + 1.0))
    qd_b = jnp.exp(lg_b * (CHUNK - row))
    kd_f = jnp.exp(lg_f * (CHUNK - 1.0 - col)) * scale
    kd_b = jnp.exp(lg_b * col) * scale
    cd_f = jnp.exp(lg_f * CHUNK)
    cd_b = jnp.exp(lg_b * CHUNK)

    def chunk(j):
        return pl.ds(pl.multiple_of(j * CHUNK, CHUNK), CHUNK)

    def rot(ref, sl):
        x = ref[sl, :].astype(F32)
        if not use_rot:
            return x
        return x * cos_ref[sl, :] + pltpu.roll(x, HEAD_DIM // 2, 1) * sin_ref[sl, :]

    def local(j, carry):
        sl = chunk(j)
        k = rot(k_ref, sl)
        kr_scr[sl, :] = k.astype(BF16)
        k_t = k.T
        v = v_ref[sl, :]
        x_scr[0, j] = _dot((k_t * kd_f).astype(BF16), v)
        x_scr[1, j] = _dot((k_t * kd_b).astype(BF16), v)
        return carry

    lax.fori_loop(0, nc, local, 0, unroll=min(4, nc))
    zero = jnp.zeros((CHUNK, CHUNK), F32)

    def states(jj, carry):
        s_f, s_b = carry
        jf, jb = jj, nc - 1 - jj
        sf_scr[jf] = s_f.astype(BF16)
        sb_scr[jb] = s_b.astype(BF16)
        return s_f * cd_f + x_scr[0, jf], s_b * cd_b + x_scr[1, jb]

    init = (s0_ref[0], s0_ref[1]) if has_init else (zero, zero)
    s_f, s_b = lax.fori_loop(0, nc, states, init, unroll=2)
    if want_final:
        sfin_ref[0] = s_f
        sfin_ref[1] = s_b

    def outputs(j, carry):
        sl = chunk(j)
        q = rot(q_ref, sl)
        att = _dot_nt(q.astype(BF16), kr_scr[sl, :]) * dmat
        lhs = jnp.concatenate([att.astype(BF16), (q * qd_f).astype(BF16), (q * qd_b).astype(BF16)], axis=1)
        rhs = jnp.concatenate([v_ref[sl, :], sf_scr[j], sb_scr[j]], axis=0)
        o = _dot(lhs, rhs)
        y = jax.nn.silu(g_ref[sl, :].astype(F32)) * _group_norm(o, gg_ref[...], gb_ref[...])
        y_ref[sl, :] = y.astype(y_ref.dtype)
        return carry

    lax.fori_loop(0, nc, outputs, 0, unroll=min(4, nc))


def _retention(proj, dec, cos_t, sin_t, s0, gn_g, gn_b, *, n_seq, seq_len, row0, use_rot, has_init, want_final):
    nc = seq_len // CHUNK
    blk0 = row0 // seq_len
    kern = functools.partial(_ret_kernel, nc=nc, use_rot=use_rot, has_init=has_init, want_final=want_final)

    def tok(cb):
        return pl.BlockSpec((seq_len, HEAD_DIM), lambda s, h, cb=cb: (blk0 + s, cb + h))

    rot_spec = pl.BlockSpec((seq_len, HEAD_DIM), lambda s, h: (0, 0))
    st_spec = pl.BlockSpec((None, 2, None, HEAD_DIM, HEAD_DIM), lambda s, h: (s, 0, h, 0, 0))
    gn_spec = pl.BlockSpec((1, HEAD_DIM), lambda s, h: (0, h))
    out_specs = [pl.BlockSpec((seq_len, HEAD_DIM), lambda s, h: (s, h))]
    out_shape = [jax.ShapeDtypeStruct((n_seq * seq_len, WIDTH), BF16)]
    if want_final:
        out_specs.append(st_spec)
        out_shape.append(jax.ShapeDtypeStruct((n_seq, 2, HEADS, HEAD_DIM, HEAD_DIM), F32))
    return pl.pallas_call(
        kern,
        grid=(n_seq, HEADS),
        in_specs=[pl.BlockSpec(memory_space=pltpu.SMEM), tok(0), tok(4), tok(8), tok(12),
                  rot_spec, rot_spec, st_spec, gn_spec, gn_spec],
        out_specs=out_specs,
        out_shape=out_shape,
        scratch_shapes=[pltpu.VMEM((nc, HEAD_DIM, HEAD_DIM), BF16), pltpu.VMEM((nc, HEAD_DIM, HEAD_DIM), BF16),
                        pltpu.VMEM((2, nc, HEAD_DIM, HEAD_DIM), F32), pltpu.VMEM((seq_len, HEAD_DIM), BF16)],
        compiler_params=_params(("parallel", "parallel"), 48),
        name="retention",
    )(dec, proj, proj, proj, proj, cos_t, sin_t, s0, gn_g, gn_b)


def _gate_prep_kernel(g_ref, col_ref, row_ref):
    lane = _iota((CHUNK, LANE), 1)
    tri = jnp.where(_iota((CHUNK, CHUNK), 0) >= _iota((CHUNK, CHUNK), 1), 1.0, 0.0)
    for c in range(g_ref.shape[0] // CHUNK):
        sl = slice(c * CHUNK, (c + 1) * CHUNK)
        g = g_ref[sl, :]
        lf = jnp.where(lane < 4 * HEADS, jnp.minimum(g, 0.0) - jnp.log1p(jnp.exp(-jnp.abs(g))), 0.0)
        cs = _dot_hi(tri, lf)
        tot = cs[CHUNK - 1:CHUNK, :]
        bc = jnp.where(lane < 2 * HEADS, cs, tot - cs + lf)
        rest = jnp.where(lane < 2 * HEADS, tot - cs, cs - lf)
        pack = g + pltpu.roll(bc, GATE_BC, 1) + pltpu.roll(rest, GATE_REST, 1)
        col_ref[sl, :] = pack
        row_ref[sl, :] = pack.T


def _gate_prep(gates, cfg):
    t_tok = gates.shape[0]
    tm = cfg["tm_in"]
    spec = pl.BlockSpec((tm, LANE), lambda i: (i, 0))
    return pl.pallas_call(
        _gate_prep_kernel,
        grid=(t_tok // tm,),
        in_specs=[spec],
        out_specs=[spec, spec],
        out_shape=[jax.ShapeDtypeStruct((t_tok, LANE), F32)] * 2,
        compiler_params=_params(("parallel",), 32),
        name="gate_prep",
    )(gates)


def _mlstm_kernel(q_ref, k_ref, v_ref, o_ref, col_ref, row_ref, c0_ref, n0_ref, m0_ref, gg_ref, gb_ref, y_ref, *rest,
                  nc, has_init, want_final):
    if want_final:
        cfin_ref, nfin_ref, mfin_ref, c_scr, n_scr, m_scr, x_scr, nl_scr, ml_scr, bl_scr = rest
    else:
        c_scr, n_scr, m_scr, x_scr, nl_scr, ml_scr, bl_scr = rest
    src = _iota((CHUNK, CHUNK), 0)
    dst = _iota((CHUNK, CHUNK), 1)
    scale = HEAD_DIM ** -0.5

    def chunk(j):
        return pl.ds(pl.multiple_of(j * CHUNK, CHUNK), CHUNK)

    def gate_idx(d, h):
        return d * 2 * HEADS + h, GATE_BC + d * 2 * HEADS + HEADS + h, GATE_REST + d * 2 * HEADS + HEADS + h

    def head_body(h):
        def init(d):
            if has_init:
                return c0_ref[d], n0_ref[d], m0_ref[d]
            return (jnp.zeros((HEAD_DIM, HEAD_DIM), F32), jnp.zeros((1, HEAD_DIM), F32),
                    jnp.zeros((1, HEAD_DIM), F32))

        def local(j, carry):
            sl = chunk(j)
            k = k_ref[sl, :]
            v_t = v_ref[sl, :].astype(F32).T
            rp = row_ref[sl, :]
            for d in range(2):
                ii, bi, ri = gate_idx(d, h)
                bc_row = rp[bi:bi + 1, :]
                b_last = bc_row[:, CHUNK - 1:CHUNK] if d == 0 else bc_row[:, 0:1]
                log_k = rp[ri:ri + 1, :] + rp[ii:ii + 1, :]
                m_loc = jnp.max(log_k, axis=1, keepdims=True)
                kw = jnp.exp(log_k - m_loc)
                x_scr[d, j] = _dot((v_t * kw).astype(BF16), k) * scale
                kw_hi = kw.astype(BF16).astype(F32)
                kw2 = jnp.concatenate([jnp.broadcast_to(kw_hi, (8, CHUNK)), jnp.broadcast_to(kw - kw_hi, (8, CHUNK))], 0)
                nl = _dot(kw2.astype(BF16), k)
                nl_scr[d, j] = (nl[0:1, :] + nl[8:9, :]) * scale
                ml_scr[d, j] = jnp.broadcast_to(m_loc, (1, HEAD_DIM))
                bl_scr[d, j] = jnp.broadcast_to(b_last, (1, HEAD_DIM))
            return carry

        lax.fori_loop(0, nc, local, 0, unroll=min(4, nc))

        def states(jj, carry):
            out = []
            for d, j in ((0, jj), (1, nc - 1 - jj)):
                cmat, nvec, m = carry[d]
                c_scr[d, j] = cmat.astype(BF16)
                n_scr[d, j] = nvec
                m_scr[d, j] = m
                m_new = jnp.maximum(bl_scr[d, j] + m, ml_scr[d, j])
                keep = jnp.exp(bl_scr[d, j] + m - m_new)
                add = jnp.exp(ml_scr[d, j] - m_new)
                out.append((keep * cmat + add * x_scr[d, j], keep * nvec + add * nl_scr[d, j], m_new))
            return tuple(out)

        fin = lax.fori_loop(0, nc, states, (init(0), init(1)), unroll=2)
        if want_final:
            for d in range(2):
                cfin_ref[d] = fin[d][0]
                nfin_ref[d] = fin[d][1]
                mfin_ref[d] = fin[d][2]

        def outputs(j, carry):
            sl = chunk(j)
            q = q_ref[sl, :]
            v = v_ref[sl, :]
            s_t = _dot_nt(k_ref[sl, :], q) * scale
            cp = col_ref[sl, :]
            rp = row_ref[sl, :]
            h_t = None
            for d in range(2):
                ii, bi, _ = gate_idx(d, h)
                causal = (src <= dst) if d == 0 else (src >= dst)
                bc_row = rp[bi:bi + 1, :]
                log_d = jnp.where(causal, bc_row + (cp[:, ii:ii + 1] - cp[:, bi:bi + 1]), -jnp.inf)
                log_prev = bc_row + m_scr[d, j]
                m_t = jnp.maximum(log_prev, jnp.max(log_d, axis=0, keepdims=True))
                w = s_t * jnp.exp(log_d - m_t)
                w_prev = jnp.exp(log_prev - m_t)
                qn = _dot_nt(jnp.broadcast_to(n_scr[d, j], (16, HEAD_DIM)).astype(BF16), q)[0:1, :]
                den = jnp.sum(w, axis=0, keepdims=True) + w_prev * qn
                inv = 1.0 / jnp.maximum(jnp.abs(den), jnp.exp(-m_t))
                num = _dot_tn(v, w.astype(BF16)) + _dot_nt(c_scr[d, j], q) * w_prev
                h_t = num * inv if h_t is None else h_t + num * inv
            y = jax.nn.sigmoid(o_ref[sl, :].astype(F32).T) * h_t
            mu = jnp.mean(y, axis=0, keepdims=True)
            var = jnp.mean(jnp.square(y - mu), axis=0, keepdims=True)
            y = (y - mu) * lax.rsqrt(var + GN_EPS) * gg_ref[...] + gb_ref[...]
            y_ref[sl, :] = y.T.astype(y_ref.dtype)
            return carry

        lax.fori_loop(0, nc, outputs, 0, unroll=min(4, nc))

    hh = pl.program_id(1)
    for h in range(HEADS):
        pl.when(hh == h)(functools.partial(head_body, h))


def _mlstm(proj, gcol, grow, c0, n0, m0, gn_g, gn_b, *, n_seq, seq_len, row0, has_init, want_final):
    nc = seq_len // CHUNK
    blk0 = row0 // seq_len
    kern = functools.partial(_mlstm_kernel, nc=nc, has_init=has_init, want_final=want_final)

    def tok(cb):
        return pl.BlockSpec((seq_len, HEAD_DIM), lambda s, h, cb=cb: (blk0 + s, cb + h))

    gate_spec = pl.BlockSpec((seq_len, LANE), lambda s, h: (blk0 + s, 0))
    c_spec = pl.BlockSpec((None, 2, None, HEAD_DIM, HEAD_DIM), lambda s, h: (s, 0, h, 0, 0))
    v_spec = pl.BlockSpec((None, 2, None, 1, HEAD_DIM), lambda s, h: (s, 0, h, 0, 0))
    gn_spec = pl.BlockSpec((HEAD_DIM, LANE), lambda s, h: (h, 0))
    out_specs = [pl.BlockSpec((seq_len, HEAD_DIM), lambda s, h: (s, h))]
    out_shape = [jax.ShapeDtypeStruct((n_seq * seq_len, WIDTH), BF16)]
    if want_final:
        out_specs += [c_spec, v_spec, v_spec]
        out_shape += [jax.ShapeDtypeStruct((n_seq, 2, HEADS, HEAD_DIM, HEAD_DIM), F32),
                      jax.ShapeDtypeStruct((n_seq, 2, HEADS, 1, HEAD_DIM), F32),
                      jax.ShapeDtypeStruct((n_seq, 2, HEADS, 1, HEAD_DIM), F32)]
    return pl.pallas_call(
        kern,
        grid=(n_seq, HEADS),
        in_specs=[tok(16), tok(20), tok(24), tok(28), gate_spec, gate_spec,
                  c_spec, v_spec, v_spec, gn_spec, gn_spec],
        out_specs=out_specs,
        out_shape=out_shape,
        scratch_shapes=[pltpu.VMEM((2, nc, HEAD_DIM, HEAD_DIM), BF16)] + [pltpu.VMEM((2, nc, 1, HEAD_DIM), F32)] * 2
        + [pltpu.VMEM((2, nc, HEAD_DIM, HEAD_DIM), F32)] + [pltpu.VMEM((2, nc, 1, HEAD_DIM), F32)] * 3,
        compiler_params=_params(("parallel", "parallel"), 48),
        name="mlstm",
    )(proj, proj, proj, proj, gcol, grow, c0, n0, m0, gn_g, gn_b)


def _layer_norm(x, g, b):
    mu = jnp.mean(x, axis=-1, keepdims=True)
    var = jnp.mean(jnp.square(x - mu), axis=-1, keepdims=True)
    return (x - mu) * lax.rsqrt(var + LN_EPS) * g + b


def _merge_kernel(x_ref, mod_ref, z_ref, ybc_ref, ybs_ref, ycc_ref, ycs_ref, wm_ref, bm_ref, wglu_ref, bglu_ref,
                  wa_ref, wb_ref, wc_ref, wo_ref, lg_ref, lb_ref, o_ref, *, n_ctx_tiles):
    is_ctx = pl.program_id(0) < n_ctx_tiles
    x = x_ref[...]
    h = (x * (1.0 + mod_ref[1:2, :]) + mod_ref[0:1, :]).astype(BF16)
    z = z_ref[...]
    ya = (z.astype(F32) * jax.nn.sigmoid(_dot(z, wglu_ref[...]) + bglu_ref[...])).astype(BF16)
    yb = jnp.where(is_ctx, ybc_ref[...], ybs_ref[...])
    yc = jnp.where(is_ctx, ycc_ref[...], ycs_ref[...])
    merged = None
    for j, (y, w_ref) in enumerate(((ya, wa_ref), (yb, wb_ref), (yc, wc_ref))):
        gate = jax.nn.sigmoid(_dot(h, wm_ref[:, j * D_MODEL:(j + 1) * D_MODEL]) + bm_ref[:, j * D_MODEL:(j + 1) * D_MODEL])
        term = gate * _dot(y, w_ref[...])
        merged = term if merged is None else merged + term
    mix = _dot(merged.astype(BF16), wo_ref[...])
    o_ref[...] = _layer_norm(ALPHA * x + mod_ref[2:3, :] * mix, lg_ref[...], lb_ref[...])


def _merge(x, mod_l, z, yb_c, yb_s, yc_c, yc_s, wm, bm, wglu, bglu, wa, wb, wc, wo, lg, lb, cfg):
    t_tok = x.shape[0]
    tm = cfg["tm_merge"]
    n_ctx_tiles = cfg["n_ctx_tok"] // tm
    cond = functools.partial(_cond_row, tm=tm, n_ctx_tok=cfg["n_ctx_tok"], lat_len=cfg["lat_len"])

    def full(shape):
        return pl.BlockSpec(shape, lambda i: (0,) * len(shape))

    def tok(w):
        return pl.BlockSpec((tm, w), lambda i: (i, 0))

    ctx_spec = pl.BlockSpec((tm, WIDTH), lambda i: (jnp.minimum(i, n_ctx_tiles - 1), 0))
    lat_spec = pl.BlockSpec((tm, WIDTH), lambda i: (jnp.maximum(i - n_ctx_tiles, 0), 0))
    return pl.pallas_call(
        functools.partial(_merge_kernel, n_ctx_tiles=n_ctx_tiles),
        grid=(t_tok // tm,),
        in_specs=[tok(D_MODEL), pl.BlockSpec((None, 6, D_MODEL), lambda i: (cond(i), 0, 0)),
                  tok(WIDTH), ctx_spec, lat_spec, ctx_spec, lat_spec,
                  full((D_MODEL, 3 * D_MODEL)), full((1, 3 * D_MODEL)), full((WIDTH, WIDTH)), full((1, WIDTH)),
                  full((WIDTH, D_MODEL)), full((WIDTH, D_MODEL)), full((WIDTH, D_MODEL)),
                  full((D_MODEL, D_MODEL)), full((1, D_MODEL)), full((1, D_MODEL))],
        out_specs=tok(D_MODEL),
        out_shape=jax.ShapeDtypeStruct((t_tok, D_MODEL), F32),
        compiler_params=_params(("parallel",), 56),
        name="merge",
    )(x, mod_l, z, yb_c, yb_s, yc_c, yc_s, wm, bm, wglu, bglu, wa, wb, wc, wo, lg, lb)


def _ffn_kernel(x_ref, mod_ref, w1_ref, w3_ref, w2_ref, lg_ref, lb_ref, o_ref, h_scr, acc_scr):
    f = pl.program_id(1)

    @pl.when(f == 0)
    def _():
        h_scr[...] = (x_ref[...] * (1.0 + mod_ref[4:5, :]) + mod_ref[3:4, :]).astype(BF16)
        acc_scr[...] = jnp.zeros_like(acc_scr)

    h = h_scr[...]
    act = (jax.nn.silu(_dot(h, w1_ref[...])) * _dot(h, w3_ref[...])).astype(BF16)
    acc_scr[...] += _dot(act, w2_ref[...])

    @pl.when(f == pl.num_programs(1) - 1)
    def _():
        o_ref[...] = _layer_norm(ALPHA * x_ref[...] + mod_ref[5:6, :] * acc_scr[...], lg_ref[...], lb_ref[...])


def _dense_ffn(x, mod_l, w1, w3, w2, lg, lb, cfg):
    t_tok = x.shape[0]
    tm, tf = cfg["tm_ffn"], cfg["tf"]
    cond = functools.partial(_cond_row, tm=tm, n_ctx_tok=cfg["n_ctx_tok"], lat_len=cfg["lat_len"])
    return pl.pallas_call(
        _ffn_kernel,
        grid=(t_tok // tm, D_FF // tf),
        in_specs=[pl.BlockSpec((tm, D_MODEL), lambda i, f: (i, 0)),
                  pl.BlockSpec((None, 6, D_MODEL), lambda i, f: (cond(i), 0, 0)),
                  pl.BlockSpec((D_MODEL, tf), lambda i, f: (0, f)),
                  pl.BlockSpec((D_MODEL, tf), lambda i, f: (0, f)),
                  pl.BlockSpec((tf, D_MODEL), lambda i, f: (f, 0)),
                  pl.BlockSpec((1, D_MODEL), lambda i, f: (0, 0)),
                  pl.BlockSpec((1, D_MODEL), lambda i, f: (0, 0))],
        out_specs=pl.BlockSpec((tm, D_MODEL), lambda i, f: (i, 0)),
        out_shape=jax.ShapeDtypeStruct((t_tok, D_MODEL), F32),
        scratch_shapes=[pltpu.VMEM((tm, D_MODEL), BF16), pltpu.VMEM((tm, D_MODEL), F32)],
        compiler_params=_params(("parallel", "arbitrary"), 56),
        name="dense_ffn",
    )(x, mod_l, w1, w3, w2, lg, lb)


def _router_kernel(x_ref, mod_ref, w_ref, b_ref, g_ref):
    h = x_ref[...] * (1.0 + mod_ref[4:5, :]) + mod_ref[3:4, :]
    lane = _iota(g_ref.shape, 1)
    logits = jnp.where(lane < N_EXPERTS, _dot_hi(h, w_ref[...]) + b_ref[...], NEG_BIG)
    m1 = jnp.max(logits, axis=1, keepdims=True)
    i1 = jnp.min(jnp.where(logits == m1, lane, LANE), axis=1, keepdims=True)
    rest = jnp.where(lane == i1, NEG_BIG, logits)
    m2 = jnp.max(rest, axis=1, keepdims=True)
    i2 = jnp.min(jnp.where(rest == m2, lane, LANE), axis=1, keepdims=True)
    e2 = jnp.exp(m2 - m1)
    den = 1.0 + e2
    g_ref[...] = jnp.where(lane == i1, 1.0 / den, 0.0) + jnp.where(lane == i2, e2 / den, 0.0)


def _router(x, mod_l, w, b, cfg):
    t_tok = x.shape[0]
    tm = cfg["tm_ffn"]
    cond = functools.partial(_cond_row, tm=tm, n_ctx_tok=cfg["n_ctx_tok"], lat_len=cfg["lat_len"])
    return pl.pallas_call(
        _router_kernel,
        grid=(t_tok // tm,),
        in_specs=[pl.BlockSpec((tm, D_MODEL), lambda i: (i, 0)),
                  pl.BlockSpec((None, 6, D_MODEL), lambda i: (cond(i), 0, 0)),
                  pl.BlockSpec((D_MODEL, LANE), lambda i: (0, 0)),
                  pl.BlockSpec((1, LANE), lambda i: (0, 0))],
        out_specs=pl.BlockSpec((tm, LANE), lambda i: (i, 0)),
        out_shape=jax.ShapeDtypeStruct((t_tok, LANE), F32),
        compiler_params=_params(("parallel",), 40),
        name="router",
    )(x, mod_l, w, b)


def _moe_kernel(x_ref, mod_ref, gate_ref, w1_ref, w3_ref, w2_ref, lg_ref, lb_ref, o_ref,
                h_scr, acc_scr, hc_scr, ob_scr, sp_scr, gt_scr, cnt_smem):
    e = pl.program_id(1)
    f = pl.program_id(2)
    last_f = pl.num_programs(2) - 1
    tm = x_ref.shape[0]

    @pl.when((e == 0) & (f == 0))
    def _():
        h_scr[...] = (x_ref[...] * (1.0 + mod_ref[4:5, :]) + mod_ref[3:4, :]).astype(BF16)
        acc_scr[...] = jnp.zeros_like(acc_scr)
        g = gate_ref[...]
        sel = g > 0.0
        ones = jnp.where(sel, 1.0, 0.0)
        before = jnp.where(_iota((tm, tm), 0) > _iota((tm, tm), 1), 1.0, 0.0).astype(BF16)
        pos = _dot(before, ones.astype(BF16))
        spt = jnp.where(sel, pos, -1.0).T
        gt = g.T
        cnt = jnp.sum(ones, axis=0, keepdims=True)
        for ee in range(N_EXPERTS):
            sp_scr[ee] = spt[ee:ee + 1, :]
            gt_scr[ee] = gt[ee:ee + 1, :]
            cnt_smem[ee] = cnt[0, ee].astype(jnp.int32)

    nb = (cnt_smem[e] + (MOE_BLK - 1)) // MOE_BLK
    row = _iota((MOE_BLK, tm), 0)

    def onehot(b):
        return sp_scr[e] == (row + b * MOE_BLK).astype(F32)

    @pl.when(f == 0)
    def _():
        def gather(b, carry):
            p = jnp.where(onehot(b), 1.0, 0.0).astype(BF16)
            hc_scr[b] = _dot(p, h_scr[...]).astype(BF16)
            ob_scr[b] = jnp.zeros((MOE_BLK, D_MODEL), F32)
            return carry
        lax.fori_loop(0, nb, gather, 0)

    def ffn(b, carry):
        hc = hc_scr[b]
        act = (jax.nn.silu(_dot(hc, w1_ref[...])) * _dot(hc, w3_ref[...])).astype(BF16)
        ob_scr[b] += _dot(act, w2_ref[...])
        return carry
    lax.fori_loop(0, nb, ffn, 0)

    @pl.when(f == last_f)
    def _():
        def scatter(b, carry):
            m = onehot(b)
            gc = jnp.sum(jnp.where(m, gt_scr[e], 0.0), axis=1, keepdims=True)
            og = (ob_scr[b] * gc).astype(BF16)
            acc_scr[...] += _dot_tn(jnp.where(m, 1.0, 0.0).astype(BF16), og)
            return carry
        lax.fori_loop(0, nb, scatter, 0)

    @pl.when((e == pl.num_programs(1) - 1) & (f == last_f))
    def _():
        o_ref[...] = _layer_norm(ALPHA * x_ref[...] + mod_ref[5:6, :] * acc_scr[...], lg_ref[...], lb_ref[...])


def _moe_ffn(x, mod_l, gates, w1, w3, w2, lg, lb, cfg):
    t_tok = x.shape[0]
    tm, tf = cfg["tm_moe"], cfg["tf"]
    cond = functools.partial(_cond_row, tm=tm, n_ctx_tok=cfg["n_ctx_tok"], lat_len=cfg["lat_len"])
    return pl.pallas_call(
        _moe_kernel,
        grid=(t_tok // tm, N_EXPERTS, D_FF // tf),
        in_specs=[pl.BlockSpec((tm, D_MODEL), lambda i, e, f: (i, 0)),
                  pl.BlockSpec((None, 6, D_MODEL), lambda i, e, f: (cond(i), 0, 0)),
                  pl.BlockSpec((tm, LANE), lambda i, e, f: (i, 0)),
                  pl.BlockSpec((None, D_MODEL, tf), lambda i, e, f: (e, 0, f)),
                  pl.BlockSpec((None, D_MODEL, tf), lambda i, e, f: (e, 0, f)),
                  pl.BlockSpec((None, tf, D_MODEL), lambda i, e, f: (e, f, 0)),
                  pl.BlockSpec((1, D_MODEL), lambda i, e, f: (0, 0)),
                  pl.BlockSpec((1, D_MODEL), lambda i, e, f: (0, 0))],
        out_specs=pl.BlockSpec((tm, D_MODEL), lambda i, e, f: (i, 0)),
        out_shape=jax.ShapeDtypeStruct((t_tok, D_MODEL), F32),
        scratch_shapes=[pltpu.VMEM((tm, D_MODEL), BF16), pltpu.VMEM((tm, D_MODEL), F32),
                        pltpu.VMEM((tm // MOE_BLK, MOE_BLK, D_MODEL), BF16),
                        pltpu.VMEM((tm // MOE_BLK, MOE_BLK, D_MODEL), F32),
                        pltpu.VMEM((N_EXPERTS, 1, tm), F32), pltpu.VMEM((N_EXPERTS, 1, tm), F32),
                        pltpu.SMEM((N_EXPERTS,), jnp.int32)],
        compiler_params=_params(("parallel", "arbitrary", "arbitrary"), 56),
        name="moe_ffn",
    )(x, mod_l, gates, w1, w3, w2, lg, lb)


def _rotary_tables(n_tok):
    rows = n_tok // GRID_W
    r = jnp.repeat(jnp.arange(rows, dtype=F32), GRID_W)
    col = jnp.tile(jnp.arange(GRID_W, dtype=F32), rows)
    n_freq = HEAD_DIM // 4
    inv = ROPE_BASE ** (-jnp.arange(n_freq, dtype=F32) / n_freq)
    ang = jnp.concatenate([r[:, None] * inv, col[:, None] * inv], -1)
    cos, sin = jnp.cos(ang), jnp.sin(ang)
    return jnp.concatenate([cos, cos], -1), jnp.concatenate([-sin, sin], -1)


def _row2(v):
    return v.reshape(1, -1).astype(F32)


def kernel(x_prompt, x_sample, cache_s5_re, cache_s5_im, cache_ret, cache_ml_c, cache_ml_n, cache_ml_m, c, c_ctx, ada_w, ada_b, w_in, b_in, s5_lam_re, s5_lam_im, s5_log_step, s5_b_re, s5_b_im, s5_c_re, s5_c_im, s5_d, s5_glu_w, s5_glu_b, ret_decay, ret_gn_g, ret_gn_b, ml_gn_g, ml_gn_b, w_a, w_b, w_c, w_o, ln1_g, ln1_b, ln2_g, ln2_b, ffn_w1, ffn_w3, ffn_w2, moe_router, moe_router_b, moe_w1, moe_w3, moe_w2):
    n_ctx_seq, ctx_len, _ = x_prompt.shape
    n_lat_seq, lat_len, _ = x_sample.shape
    n_ctx_tok = n_ctx_seq * ctx_len
    n_lat_tok = n_lat_seq * lat_len
    t_tok = n_ctx_tok + n_lat_tok
    assert n_lat_seq + 1 <= N_COND and n_ctx_seq % 8 == 0 and n_lat_seq % 8 == 0
    assert ctx_len % CHUNK == 0 and lat_len % CHUNK == 0 and n_ctx_tok % lat_len == 0
    tm = min(1024, lat_len)
    cfg = dict(n_ctx_seq=n_ctx_seq, ctx_len=ctx_len, n_lat_seq=n_lat_seq, lat_len=lat_len, n_ctx_tok=n_ctx_tok,
               tm_in=tm, tm_merge=min(256, lat_len), tm_ffn=min(512, lat_len), tm_moe=min(1024, lat_len), tf=1408,
               s5_rows=min(256, t_tok // SUB // 8))

    x = jnp.concatenate([x_prompt.reshape(n_ctx_tok, D_MODEL), x_sample.reshape(n_lat_tok, D_MODEL)], 0)
    cond = jnp.zeros((N_COND, D_MODEL), F32).at[0].set(c_ctx).at[1:1 + n_lat_seq].set(c)
    mod = _modulation(cond, ada_w, ada_b).reshape(DEPTH, N_COND, 6, D_MODEL)
    cos_t, sin_t = _rotary_tables(lat_len)

    n_main = S5_WIDTH + 8 * WIDTH
    gate_off = n_main
    merge_off = gate_off + 4 * HEADS
    s5_fac = jax.vmap(_s5_factors)(s5_lam_re, s5_lam_im, s5_log_step, s5_b_re, s5_b_im, s5_c_re, s5_c_im)

    st_s5, st_ret, st_c, st_n, st_m = [], [], [], [], []
    zero_ret = jnp.zeros((n_ctx_seq, 2, HEADS, HEAD_DIM, HEAD_DIM), F32)
    zero_vec = jnp.zeros((n_ctx_seq, 2, HEADS, 1, HEAD_DIM), F32)
    for l in range(DEPTH):
        mod_l = mod[l]
        w_main = w_in[l][:, :n_main].astype(BF16)
        b_main = _row2(b_in[l][:n_main])
        w_gate = jnp.zeros((D_MODEL, LANE), F32).at[:, :4 * HEADS].set(w_in[l][:, gate_off:merge_off]).astype(BF16)
        b_gate = jnp.zeros((1, LANE), F32).at[0, :4 * HEADS].set(b_in[l][gate_off:merge_off])
        proj, u4, gates = _inproj(x, mod_l, w_main, b_main, w_gate, b_gate, cfg)

        s5_ain, s5_klag, s5_bout, s5_a = (m[l] for m in s5_fac)
        loc = _s5_state_in(u4, s5_ain, cfg)
        x0 = jnp.stack([cache_s5_re[:, l], cache_s5_im[:, l]], 0)
        x0 = x0.reshape(2, n_lat_seq, 2, S5_Q, S5_HALF).transpose(3, 2, 0, 1, 4).astype(F32)
        xprev, s5_fin = _s5_scan(loc, s5_a, x0, cfg)
        d4 = jnp.tile(s5_d[l].astype(F32).reshape(S5_Q, 1, LANE), (1, SUB, 1)).reshape(S5_Q, 1, S5_FLAT)
        z = _s5_output(u4, xprev, s5_klag, s5_bout, d4, cfg)
        st_s5.append(s5_fin)

        gg, gb = _row2(ret_gn_g[l]), _row2(ret_gn_b[l])
        dec = ret_decay[l].astype(F32)
        yb_c, ret_fin = _retention(proj, dec, cos_t, sin_t, zero_ret, gg, gb, n_seq=n_ctx_seq, seq_len=ctx_len,
                                   row0=0, use_rot=False, has_init=False, want_final=True)
        yb_s, = _retention(proj, dec, cos_t, sin_t, cache_ret[:, l].astype(F32), gg, gb, n_seq=n_lat_seq,
                           seq_len=lat_len, row0=n_ctx_tok, use_rot=True, has_init=True, want_final=False)
        st_ret.append(ret_fin)

        gg = jnp.broadcast_to(ml_gn_g[l].astype(F32)[:, None], (WIDTH, LANE))
        gb = jnp.broadcast_to(ml_gn_b[l].astype(F32)[:, None], (WIDTH, LANE))
        gcol, grow = _gate_prep(gates, cfg)
        yc_c, c_fin, n_fin, m_fin = _mlstm(proj, gcol, grow, zero_ret, zero_vec, zero_vec, gg, gb, n_seq=n_ctx_seq,
                                           seq_len=ctx_len, row0=0, has_init=False, want_final=True)
        n0 = cache_ml_n[:, l].astype(F32)[:, :, :, None, :]
        m0 = jnp.broadcast_to(cache_ml_m[:, l].astype(F32)[:, :, :, None, None], n0.shape)
        yc_s, = _mlstm(proj, gcol, grow, cache_ml_c[:, l].astype(F32), n0, m0, gg, gb, n_seq=n_lat_seq,
                       seq_len=lat_len, row0=n_ctx_tok, has_init=True, want_final=False)
        st_c.append(c_fin)
        st_n.append(n_fin[:, :, :, 0, :])
        st_m.append(m_fin[:, :, :, 0, 0])

        x = _merge(x, mod_l, z, yb_c, yb_s, yc_c, yc_s, w_in[l][:, merge_off:].astype(BF16),
                   _row2(b_in[l][merge_off:]), s5_glu_w[l].astype(BF16), _row2(s5_glu_b[l]), w_a[l].astype(BF16),
                   w_b[l].astype(BF16), w_c[l].astype(BF16), w_o[l].astype(BF16), _row2(ln1_g[l]), _row2(ln1_b[l]), cfg)

        j = l // 2
        if l % 2 == 0:
            x = _dense_ffn(x, mod_l, ffn_w1[j].astype(BF16), ffn_w3[j].astype(BF16), ffn_w2[j].astype(BF16),
                           _row2(ln2_g[l]), _row2(ln2_b[l]), cfg)
        else:
            rw = jnp.zeros((D_MODEL, LANE), F32).at[:, :N_EXPERTS].set(moe_router[j])
            rb = jnp.zeros((1, LANE), F32).at[0, :N_EXPERTS].set(moe_router_b[j])
            mg = _router(x, mod_l, rw, rb, cfg)
            x = _moe_ffn(x, mod_l, mg, moe_w1[j].astype(BF16), moe_w3[j].astype(BF16), moe_w2[j].astype(BF16),
                         _row2(ln2_g[l]), _row2(ln2_b[l]), cfg)

    y_p = x[:n_ctx_tok].reshape(n_ctx_seq, ctx_len, D_MODEL)
    y_s = x[n_ctx_tok:].reshape(n_lat_seq, lat_len, D_MODEL)
    s5 = jnp.stack(st_s5, 0)
    s5 = s5.reshape(DEPTH, S5_Q, 2, 2, n_ctx_seq, S5_QG, S5_STATE).transpose(3, 4, 0, 2, 1, 5, 6)
    s5 = s5.reshape(2, n_ctx_seq, DEPTH, 2, S5_GROUPS, S5_STATE)
    return (y_p, y_s, s5[0], s5[1], jnp.stack(st_ret, 1), jnp.stack(st_c, 1), jnp.stack(st_n, 1),
            jnp.stack(st_m, 1))
```

```python
import functools

import jax
import jax.numpy as jnp
from jax import lax
from jax.experimental import pallas as pl
from jax.experimental.pallas import tpu as pltpu

F32 = jnp.float32
BF16 = jnp.bfloat16

D_MODEL = 1024
DEPTH = 4
GRID_W = 64
CHUNK = 128
S5_WIDTH = 512
S5_GROUP = 16
S5_GROUPS = 32
S5_STATE = 64
HEADS = 4
HEAD_DIM = 128
WIDTH = 512
ROPE_BASE = 10000.0
D_FF = 2816
N_EXPERTS = 8
ALPHA = (2.0 * DEPTH) ** 0.25
LN_EPS = 1e-5
GN_EPS = 1e-5
N_COND = 16
SUB = 16
N_LAG = 2 * SUB - 1
LANE = 128
S5_Q = S5_WIDTH // LANE
S5_QG = LANE // S5_GROUP
S5_FLAT = SUB * LANE
S5_HALF = S5_QG * S5_STATE
S5_ST = 4 * S5_HALF
NEG_BIG = -1e30
MOE_BLK = 128
GATE_BC = 16
GATE_REST = 32


def _dot(a, b):
    return jnp.dot(a, b, preferred_element_type=F32)


def _dot_hi(a, b):
    return jnp.dot(a, b, preferred_element_type=F32, precision=lax.Precision.HIGHEST)


def _dot_nt(a, b):
    return lax.dot_general(a, b, (((1,), (1,)), ((), ())), preferred_element_type=F32)


def _dot_tn(a, b):
    return lax.dot_general(a, b, (((0,), (0,)), ((), ())), preferred_element_type=F32)


def _params(sem, vmem_mb):
    return pltpu.CompilerParams(dimension_semantics=sem, vmem_limit_bytes=vmem_mb << 20)


def _cond_row(tile, tm, n_ctx_tok, lat_len):
    start = tile * tm
    return jnp.where(start < n_ctx_tok, 0, 1 + (start - n_ctx_tok) // lat_len)


def _iota(shape, axis):
    return lax.broadcasted_iota(jnp.int32, shape, axis)


def _mod_kernel(c_ref, w_ref, b_ref, o_ref):
    o_ref[...] = _dot_hi(jax.nn.silu(c_ref[...]), w_ref[...]) + b_ref[...]


def _modulation(cond, ada_w, ada_b):
    tn = 1536
    n = ada_w.shape[-1]
    return pl.pallas_call(
        _mod_kernel,
        grid=(DEPTH, n // tn),
        in_specs=[pl.BlockSpec((N_COND, D_MODEL), lambda l, j: (0, 0)),
                  pl.BlockSpec((None, D_MODEL, tn), lambda l, j: (l, 0, j)),
                  pl.BlockSpec((None, 1, tn), lambda l, j: (l, 0, j))],
        out_specs=pl.BlockSpec((None, N_COND, tn), lambda l, j: (l, 0, j)),
        out_shape=jax.ShapeDtypeStruct((DEPTH, N_COND, n), F32),
        compiler_params=_params(("parallel", "parallel"), 40),
        name="modulation",
    )(cond, ada_w, ada_b.reshape(DEPTH, 1, n))


def _inproj_kernel(x_ref, mod_ref, w_ref, b_ref, wg_ref, bg_ref, o_ref, u4_ref, g_ref, h_scr, u_scr):
    j = pl.program_id(1)
    tm = x_ref.shape[0]

    @pl.when(j == 0)
    def _():
        h = (x_ref[...] * (1.0 + mod_ref[1:2, :]) + mod_ref[0:1, :]).astype(BF16)
        h_scr[...] = h
        g_ref[...] = _dot(h, wg_ref[...]) + bg_ref[...]
        u = _dot(h, w_ref[...]) + b_ref[...]
        for q in range(S5_Q):
            u_scr[q] = u[:, q * LANE:(q + 1) * LANE]
            for t in range(SUB):
                u4_ref[q, :, t * LANE:(t + 1) * LANE] = u_scr[q, pl.ds(t, tm // SUB, stride=SUB), :].astype(BF16)

    @pl.when(j > 0)
    def _():
        o_ref[...] = (_dot(h_scr[...], w_ref[...]) + b_ref[...]).astype(o_ref.dtype)


def _inproj(x, mod_l, w, b, wg, bg, cfg):
    t_tok = x.shape[0]
    tm, tn = cfg["tm_in"], 512
    nj = w.shape[1] // tn
    cond = functools.partial(_cond_row, tm=tm, n_ctx_tok=cfg["n_ctx_tok"], lat_len=cfg["lat_len"])
    return pl.pallas_call(
        _inproj_kernel,
        grid=(t_tok // tm, nj),
        in_specs=[pl.BlockSpec((tm, D_MODEL), lambda i, j: (i, 0)),
                  pl.BlockSpec((None, 6, D_MODEL), lambda i, j: (cond(i), 0, 0)),
                  pl.BlockSpec((D_MODEL, tn), lambda i, j: (0, j)),
                  pl.BlockSpec((1, tn), lambda i, j: (0, j)),
                  pl.BlockSpec((D_MODEL, LANE), lambda i, j: (0, 0)),
                  pl.BlockSpec((1, LANE), lambda i, j: (0, 0))],
        out_specs=[pl.BlockSpec((tm, tn), lambda i, j: (i, jnp.maximum(j - 1, 0))),
                   pl.BlockSpec((S5_Q, tm // SUB, S5_FLAT), lambda i, j: (0, i, 0)),
                   pl.BlockSpec((tm, LANE), lambda i, j: (i, 0))],
        out_shape=[jax.ShapeDtypeStruct((t_tok, (nj - 1) * tn), BF16),
                   jax.ShapeDtypeStruct((S5_Q, t_tok // SUB, S5_FLAT), BF16),
                   jax.ShapeDtypeStruct((t_tok, LANE), F32)],
        scratch_shapes=[pltpu.VMEM((tm, D_MODEL), BF16), pltpu.VMEM((S5_Q, tm, LANE), F32)],
        compiler_params=_params(("parallel", "arbitrary"), 48),
        name="inproj",
    )(x, mod_l, w, b, wg, bg)


def _s5_factors(lam_re, lam_im, log_step, b_re, b_im, c_re, c_im):
    lam = lax.complex(lam_re.astype(F32), lam_im.astype(F32))
    lam_dt = lam * jnp.exp(log_step.astype(F32))[..., None]
    lam_bar = jnp.exp(lam_dt)
    bbar = ((lam_bar - 1.0) / lam)[..., None] * lax.complex(b_re.astype(F32), b_im.astype(F32))
    cmat = lax.complex(c_re.astype(F32), c_im.astype(F32))
    ks = jnp.arange(SUB + 1, dtype=F32)
    pw = jnp.exp(lam_dt[None] * ks[:, None, None, None])
    kern = jnp.einsum('dgcp,tdgp,dgpe->dgtce', cmat, pw[:SUB], bbar).real
    pad = jnp.zeros_like(kern[0][:, :SUB - 1])
    ktab = jnp.concatenate([pad, kern[0]], 1) + jnp.concatenate([kern[1][:, ::-1], pad], 1)
    k_lag = ktab.reshape(S5_Q, S5_QG, N_LAG, S5_GROUP, S5_GROUP).transpose(0, 2, 3, 1, 4)
    k_lag = k_lag.reshape(S5_Q, N_LAG, S5_GROUP, LANE)
    pw_in = jnp.stack([pw[:SUB][::-1, 0], pw[:SUB][:, 1]], 0)
    wb = pw_in[..., None] * bbar[:, None]
    wb = jnp.stack([wb.real, wb.imag], 1).reshape(2, 2, SUB, S5_Q, S5_QG, S5_STATE, S5_GROUP)
    a_in = wb.transpose(3, 2, 0, 1, 5, 4, 6).reshape(S5_Q, SUB, 4, S5_STATE, LANE)
    pw_out = jnp.stack([pw[1:, 0], pw[1:][::-1, 1]], 0)
    ce = cmat[:, None] * pw_out[:, :, :, None, :]
    ce = jnp.stack([ce.real, -ce.imag], 1).reshape(2, 2, SUB, S5_Q, S5_QG, S5_GROUP, S5_STATE)
    b_out = ce.transpose(3, 0, 1, 2, 5, 4, 6).reshape(S5_Q, 4, SUB, S5_GROUP, S5_HALF)
    a = pw[SUB]
    a = jnp.stack([a.real, a.imag], 1).reshape(2, 2, S5_Q, 1, S5_HALF).transpose(2, 0, 1, 3, 4)
    return a_in.astype(BF16), k_lag.astype(BF16), b_out.astype(BF16), a


def _expand(src_t, n_rep, row_shift, col_shift):
    k, r = src_t.shape
    rep = jnp.where(_iota((k, n_rep * k), 0) == (_iota((k, n_rep * k), 1) & (k - 1)), 1.0, 0.0).astype(BF16)
    same = (_iota((r, n_rep * k), 0) >> row_shift) == (_iota((r, n_rep * k), 1) >> col_shift)
    return jnp.where(same, _dot_tn(src_t, rep), 0.0).astype(BF16)


def _s5a_kernel(u_ref, a_ref, o_ref, w_scr):
    @pl.when(pl.program_id(1) == 0)
    def _():
        for t in range(SUB):
            for k in range(4):
                w_scr[t * LANE:(t + 1) * LANE, k * S5_HALF:(k + 1) * S5_HALF] = _expand(a_ref[t, k], S5_QG, 4, 6)

    res = _dot(u_ref[...], w_scr[...])
    for d in range(2):
        for r in range(2):
            k = 2 * d + r
            o_ref[d, r] = res[:, k * S5_HALF:(k + 1) * S5_HALF]


def _s5_state_in(u4, a_in, cfg):
    rows = u4.shape[1]
    rt = cfg["s5_rows"]
    return pl.pallas_call(
        _s5a_kernel,
        grid=(S5_Q, rows // rt),
        in_specs=[pl.BlockSpec((None, rt, S5_FLAT), lambda q, i: (q, i, 0)),
                  pl.BlockSpec((None, SUB, 4, S5_STATE, LANE), lambda q, i: (q, 0, 0, 0, 0))],
        out_specs=pl.BlockSpec((None, 2, 2, rt, S5_HALF), lambda q, i: (q, 0, 0, i, 0)),
        out_shape=jax.ShapeDtypeStruct((S5_Q, 2, 2, rows, S5_HALF), F32),
        scratch_shapes=[pltpu.VMEM((S5_FLAT, S5_ST), BF16)],
        compiler_params=_params(("parallel", "arbitrary"), 48),
        name="s5_state_in",
    )(u4, a_in)


def _s5b_kernel(loc_ref, a_ref, x0_ref, xp_ref, fin_ref, *, n_ctx_seq, ctx_sub, n_lat_seq, lat_sub):
    d = pl.program_id(1)
    ar = jnp.broadcast_to(a_ref[0], (8, LANE))
    ai = jnp.broadcast_to(a_ref[1], (8, LANE))

    def run(base, nsub, xr0, xi0):
        def body(jj, carry):
            xr, xi = carry
            j = jnp.where(d == 0, jj, nsub - 1 - jj)
            idx = pl.ds(base + j, 8, stride=nsub)
            xp_ref[0, idx, :] = xr
            xp_ref[1, idx, :] = xi
            lr = loc_ref[0, idx, :]
            li = loc_ref[1, idx, :]
            return ar * xr - ai * xi + lr, ar * xi + ai * xr + li
        return lax.fori_loop(0, nsub, body, (xr0, xi0))

    zero = jnp.zeros((8, LANE), F32)
    for bg in range(n_ctx_seq // 8):
        xr, xi = run(bg * 8 * ctx_sub, ctx_sub, zero, zero)
        fin_ref[0, bg * 8:(bg + 1) * 8, :] = xr
        fin_ref[1, bg * 8:(bg + 1) * 8, :] = xi
    for bg in range(n_lat_seq // 8):
        run(n_ctx_seq * ctx_sub + bg * 8 * lat_sub, lat_sub, x0_ref[0, bg * 8:(bg + 1) * 8, :],
            x0_ref[1, bg * 8:(bg + 1) * 8, :])


def _s5_scan(loc, a, x0, cfg):
    rows = loc.shape[3]
    n_ctx_seq, n_lat_seq = cfg["n_ctx_seq"], cfg["n_lat_seq"]
    kern = functools.partial(_s5b_kernel, n_ctx_seq=n_ctx_seq, ctx_sub=cfg["ctx_len"] // SUB,
                             n_lat_seq=n_lat_seq, lat_sub=cfg["lat_len"] // SUB)
    nlb = S5_HALF // LANE
    return pl.pallas_call(
        kern,
        grid=(S5_Q, 2, nlb),
        in_specs=[pl.BlockSpec((None, None, 2, rows, LANE), lambda q, d, b: (q, d, 0, 0, b)),
                  pl.BlockSpec((None, None, 2, 1, LANE), lambda q, d, b: (q, d, 0, 0, b)),
                  pl.BlockSpec((None, None, 2, n_lat_seq, LANE), lambda q, d, b: (q, d, 0, 0, b))],
        out_specs=[pl.BlockSpec((None, None, 2, rows, LANE), lambda q, d, b: (q, d, 0, 0, b)),
                   pl.BlockSpec((None, None, 2, n_ctx_seq, LANE), lambda q, d, b: (q, d, 0, 0, b))],
        out_shape=[jax.ShapeDtypeStruct(loc.shape, F32),
                   jax.ShapeDtypeStruct((S5_Q, 2, 2, n_ctx_seq, S5_HALF), F32)],
        compiler_params=_params(("parallel", "parallel", "parallel"), 48),
        name="s5_scan",
    )(loc, a, x0)


def _s5c_kernel(u_ref, xp_ref, k_ref, b_ref, d_ref, z_ref, m_scr, wo_scr, bd_scr, z_scr):
    rt = u_ref.shape[0]

    @pl.when(pl.program_id(1) == 0)
    def _():
        for l in range(N_LAG):
            bd_scr[l] = _expand(k_ref[l], S5_QG, 4, 4)
        for t in range(SUB):
            for s in range(SUB):
                m_scr[t * LANE:(t + 1) * LANE, s * LANE:(s + 1) * LANE] = bd_scr[s - t + SUB - 1]
        for k in range(4):
            for t in range(SUB):
                wo_scr[k * S5_HALF:(k + 1) * S5_HALF, t * LANE:(t + 1) * LANE] = _expand(b_ref[k, t], S5_QG, 6, 4)

    u = u_ref[...]
    xcat = jnp.concatenate([xp_ref[0, 0], xp_ref[0, 1], xp_ref[1, 0], xp_ref[1, 1]], axis=1).astype(BF16)
    y = _dot(u, m_scr[...]) + _dot(xcat, wo_scr[...])
    z = jax.nn.gelu(d_ref[...] * u.astype(F32) + y)
    for t in range(SUB):
        z_scr[pl.ds(t, rt, stride=SUB), :] = z[:, t * LANE:(t + 1) * LANE]
    z_ref[...] = z_scr[...].astype(z_ref.dtype)


def _s5_output(u4, xprev, k_lag, b_out, d4, cfg):
    rows = u4.shape[1]
    rt = cfg["s5_rows"]
    return pl.pallas_call(
        _s5c_kernel,
        grid=(S5_Q, rows // rt),
        in_specs=[pl.BlockSpec((None, rt, S5_FLAT), lambda q, i: (q, i, 0)),
                  pl.BlockSpec((None, 2, 2, rt, S5_HALF), lambda q, i: (q, 0, 0, i, 0)),
                  pl.BlockSpec((None, N_LAG, S5_GROUP, LANE), lambda q, i: (q, 0, 0, 0)),
                  pl.BlockSpec((None, 4, SUB, S5_GROUP, S5_HALF), lambda q, i: (q, 0, 0, 0, 0)),
                  pl.BlockSpec((None, 1, S5_FLAT), lambda q, i: (q, 0, 0))],
        out_specs=pl.BlockSpec((rt * SUB, LANE), lambda q, i: (i, q)),
        out_shape=jax.ShapeDtypeStruct((rows * SUB, S5_WIDTH), BF16),
        scratch_shapes=[pltpu.VMEM((S5_FLAT, S5_FLAT), BF16), pltpu.VMEM((S5_ST, S5_FLAT), BF16),
                        pltpu.VMEM((N_LAG, LANE, LANE), BF16), pltpu.VMEM((rt * SUB, LANE), F32)],
        compiler_params=_params(("parallel", "arbitrary"), 56),
        name="s5_output",
    )(u4, xprev, k_lag, b_out, d4)


def _group_norm(o, g, b):
    mu = jnp.mean(o, axis=-1, keepdims=True)
    var = jnp.mean(jnp.square(o - mu), axis=-1, keepdims=True)
    return (o - mu) * lax.rsqrt(var + GN_EPS) * g + b


def _ret_kernel(dec_ref, q_ref, k_ref, v_ref, g_ref, cos_ref, sin_ref, s0_ref, gg_ref, gb_ref, y_ref, *rest,
                nc, use_rot, has_init, want_final):
    if want_final:
        sfin_ref, sf_scr, sb_scr, x_scr, kr_scr = rest
    else:
        sf_scr, sb_scr, x_scr, kr_scr = rest
    h = pl.program_id(1)
    row = _iota((CHUNK, CHUNK), 0).astype(F32)
    col = _iota((CHUNK, CHUNK), 1).astype(F32)
    lg_f = -jnp.exp(jnp.full((CHUNK, CHUNK), dec_ref[0, h], F32))
    lg_b = -jnp.exp(jnp.full((CHUNK, CHUNK), dec_ref[1, h], F32))
    lag = row - col
    scale = HEAD_DIM ** -0.5
    dmat = (jnp.where(lag >= 0, jnp.exp(lg_f * jnp.maximum(lag, 0.0)), 0.0)
            + jnp.where(lag <= 0, jnp.exp(lg_b * jnp.maximum(-lag, 0.0)), 0.0)) * scale
    qd_f = jnp.exp(lg_f * (row + 1.0))
    qd_b = jnp.exp(lg_b * (CHUNK - row))
    kd_f = jnp.exp(lg_f * (CHUNK - 1.0 - col)) * scale
    kd_b = jnp.exp(lg_b * col) * scale
    cd_f = jnp.exp(lg_f * CHUNK)
    cd_b = jnp.exp(lg_b * CHUNK)

    def chunk(j):
        return pl.ds(pl.multiple_of(j * CHUNK, CHUNK), CHUNK)

    def rot(ref, sl):
        x = ref[sl, :].astype(F32)
        if not use_rot:
            return x
        return x * cos_ref[sl, :] + pltpu.roll(x, HEAD_DIM // 2, 1) * sin_ref[sl, :]

    def local(j, carry):
        sl = chunk(j)
        k = rot(k_ref, sl)
        kr_scr[sl, :] = k.astype(BF16)
        k_t = k.T
        v = v_ref[sl, :]
        x_scr[0, j] = _dot((k_t * kd_f).astype(BF16), v)
        x_scr[1, j] = _dot((k_t * kd_b).astype(BF16), v)
        return carry

    lax.fori_loop(0, nc, local, 0, unroll=min(4, nc))
    zero = jnp.zeros((CHUNK, CHUNK), F32)

    def states(jj, carry):
        s_f, s_b = carry
        jf, jb = jj, nc - 1 - jj
        sf_scr[jf] = s_f.astype(BF16)
        sb_scr[jb] = s_b.astype(BF16)
        return s_f * cd_f + x_scr[0, jf], s_b * cd_b + x_scr[1, jb]

    init = (s0_ref[0], s0_ref[1]) if has_init else (zero, zero)
    s_f, s_b = lax.fori_loop(0, nc, states, init, unroll=2)
    if want_final:
        sfin_ref[0] = s_f
        sfin_ref[1] = s_b

    def outputs(j, carry):
        sl = chunk(j)
        q = rot(q_ref, sl)
        att = _dot_nt(q.astype(BF16), kr_scr[sl, :]) * dmat
        lhs = jnp.concatenate([att.astype(BF16), (q * qd_f).astype(BF16), (q * qd_b).astype(BF16)], axis=1)
        rhs = jnp.concatenate([v_ref[sl, :], sf_scr[j], sb_scr[j]], axis=0)
        o = _dot(lhs, rhs)
        y = jax.nn.silu(g_ref[sl, :].astype(F32)) * _group_norm(o, gg_ref[...], gb_ref[...])
        y_ref[sl, :] = y.astype(y_ref.dtype)
        return carry

    lax.fori_loop(0, nc, outputs, 0, unroll=min(4, nc))


def _retention(proj, dec, cos_t, sin_t, s0, gn_g, gn_b, *, n_seq, seq_len, row0, use_rot, has_init, want_final):
    nc = seq_len // CHUNK
    blk0 = row0 // seq_len
    kern = functools.partial(_ret_kernel, nc=nc, use_rot=use_rot, has_init=has_init, want_final=want_final)

    def tok(cb):
        return pl.BlockSpec((seq_len, HEAD_DIM), lambda s, h, cb=cb: (blk0 + s, cb + h))

    rot_spec = pl.BlockSpec((seq_len, HEAD_DIM), lambda s, h: (0, 0))
    st_spec = pl.BlockSpec((None, 2, None, HEAD_DIM, HEAD_DIM), lambda s, h: (s, 0, h, 0, 0))
    gn_spec = pl.BlockSpec((1, HEAD_DIM), lambda s, h: (0, h))
    out_specs = [pl.BlockSpec((seq_len, HEAD_DIM), lambda s, h: (s, h))]
    out_shape = [jax.ShapeDtypeStruct((n_seq * seq_len, WIDTH), BF16)]
    if want_final:
        out_specs.append(st_spec)
        out_shape.append(jax.ShapeDtypeStruct((n_seq, 2, HEADS, HEAD_DIM, HEAD_DIM), F32))
    return pl.pallas_call(
        kern,
        grid=(n_seq, HEADS),
        in_specs=[pl.BlockSpec(memory_space=pltpu.SMEM), tok(0), tok(4), tok(8), tok(12),
                  rot_spec, rot_spec, st_spec, gn_spec, gn_spec],
        out_specs=out_specs,
        out_shape=out_shape,
        scratch_shapes=[pltpu.VMEM((nc, HEAD_DIM, HEAD_DIM), BF16), pltpu.VMEM((nc, HEAD_DIM, HEAD_DIM), BF16),
                        pltpu.VMEM((2, nc, HEAD_DIM, HEAD_DIM), F32), pltpu.VMEM((seq_len, HEAD_DIM), BF16)],
        compiler_params=_params(("parallel", "parallel"), 48),
        name="retention",
    )(dec, proj, proj, proj, proj, cos_t, sin_t, s0, gn_g, gn_b)


def _gate_prep_kernel(g_ref, col_ref, row_ref):
    lane = _iota((CHUNK, LANE), 1)
    tri = jnp.where(_iota((CHUNK, CHUNK), 0) >= _iota((CHUNK, CHUNK), 1), 1.0, 0.0)
    for c in range(g_ref.shape[0] // CHUNK):
        sl = slice(c * CHUNK, (c + 1) * CHUNK)
        g = g_ref[sl, :]
        lf = jnp.where(lane < 4 * HEADS, jnp.minimum(g, 0.0) - jnp.log1p(jnp.exp(-jnp.abs(g))), 0.0)
        cs = _dot_hi(tri, lf)
        tot = cs[CHUNK - 1:CHUNK, :]
        bc = jnp.where(lane < 2 * HEADS, cs, tot - cs + lf)
        rest = jnp.where(lane < 2 * HEADS, tot - cs, cs - lf)
        pack = g + pltpu.roll(bc, GATE_BC, 1) + pltpu.roll(rest, GATE_REST, 1)
        col_ref[sl, :] = pack
        row_ref[sl, :] = pack.T


def _gate_prep(gates, cfg):
    t_tok = gates.shape[0]
    tm = cfg["tm_in"]
    spec = pl.BlockSpec((tm, LANE), lambda i: (i, 0))
    return pl.pallas_call(
        _gate_prep_kernel,
        grid=(t_tok // tm,),
        in_specs=[spec],
        out_specs=[spec, spec],
        out_shape=[jax.ShapeDtypeStruct((t_tok, LANE), F32)] * 2,
        compiler_params=_params(("parallel",), 32),
        name="gate_prep",
    )(gates)


def _mlstm_kernel(q_ref, k_ref, v_ref, o_ref, col_ref, row_ref, c0_ref, n0_ref, m0_ref, gg_ref, gb_ref, y_ref, *rest,
                  nc, has_init, want_final):
    if want_final:
        cfin_ref, nfin_ref, mfin_ref, c_scr, n_scr, m_scr, x_scr, nl_scr, ml_scr, bl_scr = rest
    else:
        c_scr, n_scr, m_scr, x_scr, nl_scr, ml_scr, bl_scr = rest
    src = _iota((CHUNK, CHUNK), 0)
    dst = _iota((CHUNK, CHUNK), 1)
    scale = HEAD_DIM ** -0.5

    def chunk(j):
        return pl.ds(pl.multiple_of(j * CHUNK, CHUNK), CHUNK)

    def gate_idx(d, h):
        return d * 2 * HEADS + h, GATE_BC + d * 2 * HEADS + HEADS + h, GATE_REST + d * 2 * HEADS + HEADS + h

    def head_body(h):
        def init(d):
            if has_init:
                return c0_ref[d], n0_ref[d], m0_ref[d]
            return (jnp.zeros((HEAD_DIM, HEAD_DIM), F32), jnp.zeros((1, HEAD_DIM), F32),
                    jnp.zeros((1, HEAD_DIM), F32))

        def local(j, carry):
            sl = chunk(j)
            k = k_ref[sl, :]
            v_t = v_ref[sl, :].astype(F32).T
            rp = row_ref[sl, :]
            for d in range(2):
                ii, bi, ri = gate_idx(d, h)
                bc_row = rp[bi:bi + 1, :]
                b_last = bc_row[:, CHUNK - 1:CHUNK] if d == 0 else bc_row[:, 0:1]
                log_k = rp[ri:ri + 1, :] + rp[ii:ii + 1, :]
                m_loc = jnp.max(log_k, axis=1, keepdims=True)
                kw = jnp.exp(log_k - m_loc)
                x_scr[d, j] = _dot((v_t * kw).astype(BF16), k) * scale
                kw_hi = kw.astype(BF16).astype(F32)
                kw2 = jnp.concatenate([jnp.broadcast_to(kw_hi, (8, CHUNK)), jnp.broadcast_to(kw - kw_hi, (8, CHUNK))], 0)
                nl = _dot(kw2.astype(BF16), k)
                nl_scr[d, j] = (nl[0:1, :] + nl[8:9, :]) * scale
                ml_scr[d, j] = jnp.broadcast_to(m_loc, (1, HEAD_DIM))
                bl_scr[d, j] = jnp.broadcast_to(b_last, (1, HEAD_DIM))
            return carry

        lax.fori_loop(0, nc, local, 0, unroll=min(4, nc))

        def states(jj, carry):
            out = []
            for d, j in ((0, jj), (1, nc - 1 - jj)):
                cmat, nvec, m = carry[d]
                c_scr[d, j] = cmat.astype(BF16)
                n_scr[d, j] = nvec
                m_scr[d, j] = m
                m_new = jnp.maximum(bl_scr[d, j] + m, ml_scr[d, j])
                keep = jnp.exp(bl_scr[d, j] + m - m_new)
                add = jnp.exp(ml_scr[d, j] - m_new)
                out.append((keep * cmat + add * x_scr[d, j], keep * nvec + add * nl_scr[d, j], m_new))
            return tuple(out)

        fin = lax.fori_loop(0, nc, states, (init(0), init(1)), unroll=2)
        if want_final:
            for d in range(2):
                cfin_ref[d] = fin[d][0]
                nfin_ref[d] = fin[d][1]
                mfin_ref[d] = fin[d][2]

        def outputs(j, carry):
            sl = chunk(j)
            q = q_ref[sl, :]
            v = v_ref[sl, :]
            s_t = _dot_nt(k_ref[sl, :], q) * scale
            cp = col_ref[sl, :]
            rp = row_ref[sl, :]
            h_t = None
            for d in range(2):
                ii, bi, _ = gate_idx(d, h)
                causal = (src <= dst) if d == 0 else (src >= dst)
                bc_row = rp[bi:bi + 1, :]
                log_d = jnp.where(causal, bc_row + (cp[:, ii:ii + 1] - cp[:, bi:bi + 1]), -jnp.inf)
                log_prev = bc_row + m_scr[d, j]
                m_t = jnp.maximum(log_prev, jnp.max(log_d, axis=0, keepdims=True))
                w = s_t * jnp.exp(log_d - m_t)
                w_prev = jnp.exp(log_prev - m_t)
                qn = _dot_nt(jnp.broadcast_to(n_scr[d, j], (16, HEAD_DIM)).astype(BF16), q)[0:1, :]
                den = jnp.sum(w, axis=0, keepdims=True) + w_prev * qn
                inv = 1.0 / jnp.maximum(jnp.abs(den), jnp.exp(-m_t))
                num = _dot_tn(v, w.astype(BF16)) + _dot_nt(c_scr[d, j], q) * w_prev
                h_t = num * inv if h_t is None else h_t + num * inv
            y = jax.nn.sigmoid(o_ref[sl, :].astype(F32).T) * h_t
            mu = jnp.mean(y, axis=0, keepdims=True)
            var = jnp.mean(jnp.square(y - mu), axis=0, keepdims=True)
            y = (y - mu) * lax.rsqrt(var + GN_EPS) * gg_ref[...] + gb_ref[...]
            y_ref[sl, :] = y.T.astype(y_ref.dtype)
            return carry

        lax.fori_loop(0, nc, outputs, 0, unroll=min(4, nc))

    hh = pl.program_id(1)
    for h in range(HEADS):
        pl.when(hh == h)(functools.partial(head_body, h))


def _mlstm(proj, gcol, grow, c0, n0, m0, gn_g, gn_b, *, n_seq, seq_len, row0, has_init, want_final):
    nc = seq_len // CHUNK
    blk0 = row0 // seq_len
    kern = functools.partial(_mlstm_kernel, nc=nc, has_init=has_init, want_final=want_final)

    def tok(cb):
        return pl.BlockSpec((seq_len, HEAD_DIM), lambda s, h, cb=cb: (blk0 + s, cb + h))

    gate_spec = pl.BlockSpec((seq_len, LANE), lambda s, h: (blk0 + s, 0))
    c_spec = pl.BlockSpec((None, 2, None, HEAD_DIM, HEAD_DIM), lambda s, h: (s, 0, h, 0, 0))
    v_spec = pl.BlockSpec((None, 2, None, 1, HEAD_DIM), lambda s, h: (s, 0, h, 0, 0))
    gn_spec = pl.BlockSpec((HEAD_DIM, LANE), lambda s, h: (h, 0))
    out_specs = [pl.BlockSpec((seq_len, HEAD_DIM), lambda s, h: (s, h))]
    out_shape = [jax.ShapeDtypeStruct((n_seq * seq_len, WIDTH), BF16)]
    if want_final:
        out_specs += [c_spec, v_spec, v_spec]
        out_shape += [jax.ShapeDtypeStruct((n_seq, 2, HEADS, HEAD_DIM, HEAD_DIM), F32),
                      jax.ShapeDtypeStruct((n_seq, 2, HEADS, 1, HEAD_DIM), F32),
                      jax.ShapeDtypeStruct((n_seq, 2, HEADS, 1, HEAD_DIM), F32)]
    return pl.pallas_call(
        kern,
        grid=(n_seq, HEADS),
        in_specs=[tok(16), tok(20), tok(24), tok(28), gate_spec, gate_spec,
                  c_spec, v_spec, v_spec, gn_spec, gn_spec],
        out_specs=out_specs,
        out_shape=out_shape,
        scratch_shapes=[pltpu.VMEM((2, nc, HEAD_DIM, HEAD_DIM), BF16)] + [pltpu.VMEM((2, nc, 1, HEAD_DIM), F32)] * 2
        + [pltpu.VMEM((2, nc, HEAD_DIM, HEAD_DIM), F32)] + [pltpu.VMEM((2, nc, 1, HEAD_DIM), F32)] * 3,
        compiler_params=_params(("parallel", "parallel"), 48),
        name="mlstm",
    )(proj, proj, proj, proj, gcol, grow, c0, n0, m0, gn_g, gn_b)


def _layer_norm(x, g, b):
    mu = jnp.mean(x, axis=-1, keepdims=True)
    var = jnp.mean(jnp.square(x - mu), axis=-1, keepdims=True)
    return (x - mu) * lax.rsqrt(var + LN_EPS) * g + b


def _top2_gates(h, w_ref, b_ref):
    lane = _iota((h.shape[0], LANE), 1)
    logits = jnp.where(lane < N_EXPERTS, _dot_hi(h, w_ref[...]) + b_ref[...], NEG_BIG)
    m1 = jnp.max(logits, axis=1, keepdims=True)
    i1 = jnp.min(jnp.where(logits == m1, lane, LANE), axis=1, keepdims=True)
    rest = jnp.where(lane == i1, NEG_BIG, logits)
    m2 = jnp.max(rest, axis=1, keepdims=True)
    i2 = jnp.min(jnp.where(rest == m2, lane, LANE), axis=1, keepdims=True)
    e2 = jnp.exp(m2 - m1)
    den = 1.0 + e2
    return jnp.where(lane == i1, 1.0 / den, 0.0) + jnp.where(lane == i2, e2 / den, 0.0)


def _merge_kernel(x_ref, mod_ref, z_ref, ybc_ref, ybs_ref, ycc_ref, ycs_ref, wm_ref, bm_ref, wglu_ref, bglu_ref,
                  wa_ref, wb_ref, wc_ref, wo_ref, lg_ref, lb_ref, *rest, n_ctx_tiles, with_router):
    if with_router:
        rw_ref, rb_ref, o_ref, g_ref = rest
    else:
        o_ref, = rest
    is_ctx = pl.program_id(0) < n_ctx_tiles
    x = x_ref[...]
    h = (x * (1.0 + mod_ref[1:2, :]) + mod_ref[0:1, :]).astype(BF16)
    z = z_ref[...]
    ya = (z.astype(F32) * jax.nn.sigmoid(_dot(z, wglu_ref[...]) + bglu_ref[...])).astype(BF16)
    yb = jnp.where(is_ctx, ybc_ref[...], ybs_ref[...])
    yc = jnp.where(is_ctx, ycc_ref[...], ycs_ref[...])
    merged = None
    for j, (y, w_ref) in enumerate(((ya, wa_ref), (yb, wb_ref), (yc, wc_ref))):
        gate = jax.nn.sigmoid(_dot(h, wm_ref[:, j * D_MODEL:(j + 1) * D_MODEL]) + bm_ref[:, j * D_MODEL:(j + 1) * D_MODEL])
        term = gate * _dot(y, w_ref[...])
        merged = term if merged is None else merged + term
    mix = _dot(merged.astype(BF16), wo_ref[...])
    x1 = _layer_norm(ALPHA * x + mod_ref[2:3, :] * mix, lg_ref[...], lb_ref[...])
    o_ref[...] = x1
    if with_router:
        g_ref[...] = _top2_gates(x1 * (1.0 + mod_ref[4:5, :]) + mod_ref[3:4, :], rw_ref, rb_ref)


def _merge(x, mod_l, z, yb_c, yb_s, yc_c, yc_s, wm, bm, wglu, bglu, wa, wb, wc, wo, lg, lb, router, cfg):
    t_tok = x.shape[0]
    tm = cfg["tm_merge"]
    n_ctx_tiles = cfg["n_ctx_tok"] // tm
    cond = functools.partial(_cond_row, tm=tm, n_ctx_tok=cfg["n_ctx_tok"], lat_len=cfg["lat_len"])

    def full(shape):
        return pl.BlockSpec(shape, lambda i: (0,) * len(shape))

    def tok(w):
        return pl.BlockSpec((tm, w), lambda i: (i, 0))

    ctx_spec = pl.BlockSpec((tm, WIDTH), lambda i: (jnp.minimum(i, n_ctx_tiles - 1), 0))
    lat_spec = pl.BlockSpec((tm, WIDTH), lambda i: (jnp.maximum(i - n_ctx_tiles, 0), 0))
    in_specs = [tok(D_MODEL), pl.BlockSpec((None, 6, D_MODEL), lambda i: (cond(i), 0, 0)),
                tok(WIDTH), ctx_spec, lat_spec, ctx_spec, lat_spec,
                full((D_MODEL, 3 * D_MODEL)), full((1, 3 * D_MODEL)), full((WIDTH, WIDTH)), full((1, WIDTH)),
                full((WIDTH, D_MODEL)), full((WIDTH, D_MODEL)), full((WIDTH, D_MODEL)),
                full((D_MODEL, D_MODEL)), full((1, D_MODEL)), full((1, D_MODEL))]
    args = [x, mod_l, z, yb_c, yb_s, yc_c, yc_s, wm, bm, wglu, bglu, wa, wb, wc, wo, lg, lb]
    out_specs = [tok(D_MODEL)]
    out_shape = [jax.ShapeDtypeStruct((t_tok, D_MODEL), F32)]
    if router is not None:
        in_specs += [full((D_MODEL, LANE)), full((1, LANE))]
        args += list(router)
        out_specs.append(tok(LANE))
        out_shape.append(jax.ShapeDtypeStruct((t_tok, LANE), F32))
    return pl.pallas_call(
        functools.partial(_merge_kernel, n_ctx_tiles=n_ctx_tiles, with_router=router is not None),
        grid=(t_tok // tm,),
        in_specs=in_specs,
        out_specs=out_specs,
        out_shape=out_shape,
        compiler_params=_params(("parallel",), 56),
        name="merge",
    )(*args)


def _ffn_kernel(x_ref, mod_ref, w1_ref, w3_ref, w2_ref, lg_ref, lb_ref, o_ref, h_scr, acc_scr):
    f = pl.program_id(1)

    @pl.when(f == 0)
    def _():
        h_scr[...] = (x_ref[...] * (1.0 + mod_ref[4:5, :]) + mod_ref[3:4, :]).astype(BF16)
        acc_scr[...] = jnp.zeros_like(acc_scr)

    h = h_scr[...]
    act = (jax.nn.silu(_dot(h, w1_ref[...])) * _dot(h, w3_ref[...])).astype(BF16)
    acc_scr[...] += _dot(act, w2_ref[...])

    @pl.when(f == pl.num_programs(1) - 1)
    def _():
        o_ref[...] = _layer_norm(ALPHA * x_ref[...] + mod_ref[5:6, :] * acc_scr[...], lg_ref[...], lb_ref[...])


def _dense_ffn(x, mod_l, w1, w3, w2, lg, lb, cfg):
    t_tok = x.shape[0]
    tm, tf = cfg["tm_ffn"], cfg["tf"]
    cond = functools.partial(_cond_row, tm=tm, n_ctx_tok=cfg["n_ctx_tok"], lat_len=cfg["lat_len"])
    return pl.pallas_call(
        _ffn_kernel,
        grid=(t_tok // tm, D_FF // tf),
        in_specs=[pl.BlockSpec((tm, D_MODEL), lambda i, f: (i, 0)),
                  pl.BlockSpec((None, 6, D_MODEL), lambda i, f: (cond(i), 0, 0)),
                  pl.BlockSpec((D_MODEL, tf), lambda i, f: (0, f)),
                  pl.BlockSpec((D_MODEL, tf), lambda i, f: (0, f)),
                  pl.BlockSpec((tf, D_MODEL), lambda i, f: (f, 0)),
                  pl.BlockSpec((1, D_MODEL), lambda i, f: (0, 0)),
                  pl.BlockSpec((1, D_MODEL), lambda i, f: (0, 0))],
        out_specs=pl.BlockSpec((tm, D_MODEL), lambda i, f: (i, 0)),
        out_shape=jax.ShapeDtypeStruct((t_tok, D_MODEL), F32),
        scratch_shapes=[pltpu.VMEM((tm, D_MODEL), BF16), pltpu.VMEM((tm, D_MODEL), F32)],
        compiler_params=_params(("parallel", "arbitrary"), 56),
        name="dense_ffn",
    )(x, mod_l, w1, w3, w2, lg, lb)


def _moe_kernel(x_ref, mod_ref, gate_ref, w1_ref, w3_ref, w2_ref, lg_ref, lb_ref, o_ref,
                h_scr, acc_scr, hc_scr, ob_scr, sp_scr, gt_scr, cnt_smem):
    e = pl.program_id(1)
    f = pl.program_id(2)
    last_f = pl.num_programs(2) - 1
    tm = x_ref.shape[0]

    @pl.when((e == 0) & (f == 0))
    def _():
        h_scr[...] = (x_ref[...] * (1.0 + mod_ref[4:5, :]) + mod_ref[3:4, :]).astype(BF16)
        acc_scr[...] = jnp.zeros_like(acc_scr)
        g = gate_ref[...]
        sel = g > 0.0
        ones = jnp.where(sel, 1.0, 0.0)
        before = jnp.where(_iota((tm, tm), 0) > _iota((tm, tm), 1), 1.0, 0.0).astype(BF16)
        pos = _dot(before, ones.astype(BF16))
        spt = jnp.where(sel, pos, -1.0).T
        gt = g.T
        cnt = jnp.sum(ones, axis=0, keepdims=True)
        for ee in range(N_EXPERTS):
            sp_scr[ee] = spt[ee:ee + 1, :]
            gt_scr[ee] = gt[ee:ee + 1, :]
            cnt_smem[ee] = cnt[0, ee].astype(jnp.int32)

    nb = (cnt_smem[e] + (MOE_BLK - 1)) // MOE_BLK
    row = _iota((MOE_BLK, tm), 0)

    def onehot(b):
        return sp_scr[e] == (row + b * MOE_BLK).astype(F32)

    @pl.when(f == 0)
    def _():
        def gather(b, carry):
            p = jnp.where(onehot(b), 1.0, 0.0).astype(BF16)
            hc_scr[b] = _dot(p, h_scr[...]).astype(BF16)
            ob_scr[b] = jnp.zeros((MOE_BLK, D_MODEL), F32)
            return carry
        lax.fori_loop(0, nb, gather, 0)

        @pl.when(nb % 2 == 1)
        def _():
            ob_scr[nb] = jnp.zeros((MOE_BLK, D_MODEL), F32)

    def ffn(b, carry):
        hc = hc_scr[b]
        act = (jax.nn.silu(_dot(hc, w1_ref[...])) * _dot(hc, w3_ref[...])).astype(BF16)
        ob_scr[b] += _dot(act, w2_ref[...])
        return carry
    lax.fori_loop(0, nb, ffn, 0)

    @pl.when(f == last_f)
    def _():
        row2 = _iota((2 * MOE_BLK, tm), 0)

        def scatter(p, carry):
            m = sp_scr[e] == (row2 + p * (2 * MOE_BLK)).astype(F32)
            gc = jnp.sum(jnp.where(m, gt_scr[e], 0.0), axis=1, keepdims=True)
            og = (ob_scr[pl.ds(2 * p, 2)].reshape(2 * MOE_BLK, D_MODEL) * gc).astype(BF16)
            acc_scr[...] += _dot_tn(jnp.where(m, 1.0, 0.0).astype(BF16), og)
            return carry
        lax.fori_loop(0, (nb + 1) // 2, scatter, 0)

    @pl.when((e == pl.num_programs(1) - 1) & (f == last_f))
    def _():
        o_ref[...] = _layer_norm(ALPHA * x_ref[...] + mod_ref[5:6, :] * acc_scr[...], lg_ref[...], lb_ref[...])


def _moe_ffn(x, mod_l, gates, w1, w3, w2, lg, lb, cfg):
    t_tok = x.shape[0]
    tm, tf = cfg["tm_moe"], cfg["tf"]
    cond = functools.partial(_cond_row, tm=tm, n_ctx_tok=cfg["n_ctx_tok"], lat_len=cfg["lat_len"])
    return pl.pallas_call(
        _moe_kernel,
        grid=(t_tok // tm, N_EXPERTS, D_FF // tf),
        in_specs=[pl.BlockSpec((tm, D_MODEL), lambda i, e, f: (i, 0)),
                  pl.BlockSpec((None, 6, D_MODEL), lambda i, e, f: (cond(i), 0, 0)),
                  pl.BlockSpec((tm, LANE), lambda i, e, f: (i, 0)),
                  pl.BlockSpec((None, D_MODEL, tf), lambda i, e, f: (e, 0, f)),
                  pl.BlockSpec((None, D_MODEL, tf), lambda i, e, f: (e, 0, f)),
                  pl.BlockSpec((None, tf, D_MODEL), lambda i, e, f: (e, f, 0)),
                  pl.BlockSpec((1, D_MODEL), lambda i, e, f: (0, 0)),
                  pl.BlockSpec((1, D_MODEL), lambda i, e, f: (0, 0))],
        out_specs=pl.BlockSpec((tm, D_MODEL), lambda i, e, f: (i, 0)),
        out_shape=jax.ShapeDtypeStruct((t_tok, D_MODEL), F32),
        scratch_shapes=[pltpu.VMEM((tm, D_MODEL), BF16), pltpu.VMEM((tm, D_MODEL), F32),
                        pltpu.VMEM((tm // MOE_BLK, MOE_BLK, D_MODEL), BF16),
                        pltpu.VMEM((tm // MOE_BLK, MOE_BLK, D_MODEL), F32),
                        pltpu.VMEM((N_EXPERTS, 1, tm), F32), pltpu.VMEM((N_EXPERTS, 1, tm), F32),
                        pltpu.SMEM((N_EXPERTS,), jnp.int32)],
        compiler_params=_params(("parallel", "arbitrary", "arbitrary"), 56),
        name="moe_ffn",
    )(x, mod_l, gates, w1, w3, w2, lg, lb)


def _rotary_tables(n_tok):
    rows = n_tok // GRID_W
    r = jnp.repeat(jnp.arange(rows, dtype=F32), GRID_W)
    col = jnp.tile(jnp.arange(GRID_W, dtype=F32), rows)
    n_freq = HEAD_DIM // 4
    inv = ROPE_BASE ** (-jnp.arange(n_freq, dtype=F32) / n_freq)
    ang = jnp.concatenate([r[:, None] * inv, col[:, None] * inv], -1)
    cos, sin = jnp.cos(ang), jnp.sin(ang)
    return jnp.concatenate([cos, cos], -1), jnp.concatenate([-sin, sin], -1)


def _row2(v):
    return v.reshape(1, -1).astype(F32)


def kernel(x_prompt, x_sample, cache_s5_re, cache_s5_im, cache_ret, cache_ml_c, cache_ml_n, cache_ml_m, c, c_ctx, ada_w, ada_b, w_in, b_in, s5_lam_re, s5_lam_im, s5_log_step, s5_b_re, s5_b_im, s5_c_re, s5_c_im, s5_d, s5_glu_w, s5_glu_b, ret_decay, ret_gn_g, ret_gn_b, ml_gn_g, ml_gn_b, w_a, w_b, w_c, w_o, ln1_g, ln1_b, ln2_g, ln2_b, ffn_w1, ffn_w3, ffn_w2, moe_router, moe_router_b, moe_w1, moe_w3, moe_w2):
    n_ctx_seq, ctx_len, _ = x_prompt.shape
    n_lat_seq, lat_len, _ = x_sample.shape
    n_ctx_tok = n_ctx_seq * ctx_len
    n_lat_tok = n_lat_seq * lat_len
    t_tok = n_ctx_tok + n_lat_tok
    assert n_lat_seq + 1 <= N_COND and n_ctx_seq % 8 == 0 and n_lat_seq % 8 == 0
    assert ctx_len % CHUNK == 0 and lat_len % CHUNK == 0 and n_ctx_tok % lat_len == 0
    tm = min(1024, lat_len)
    cfg = dict(n_ctx_seq=n_ctx_seq, ctx_len=ctx_len, n_lat_seq=n_lat_seq, lat_len=lat_len, n_ctx_tok=n_ctx_tok,
               tm_in=tm, tm_merge=min(256, lat_len), tm_ffn=min(512, lat_len), tm_moe=min(1024, lat_len), tf=1408,
               s5_rows=min(256, t_tok // SUB // 8))

    x = jnp.concatenate([x_prompt.reshape(n_ctx_tok, D_MODEL), x_sample.reshape(n_lat_tok, D_MODEL)], 0)
    cond = jnp.zeros((N_COND, D_MODEL), F32).at[0].set(c_ctx).at[1:1 + n_lat_seq].set(c)
    mod = _modulation(cond, ada_w, ada_b).reshape(DEPTH, N_COND, 6, D_MODEL)
    cos_t, sin_t = _rotary_tables(lat_len)

    n_main = S5_WIDTH + 8 * WIDTH
    gate_off = n_main
    merge_off = gate_off + 4 * HEADS
    s5_fac = jax.vmap(_s5_factors)(s5_lam_re, s5_lam_im, s5_log_step, s5_b_re, s5_b_im, s5_c_re, s5_c_im)

    st_s5, st_ret, st_c, st_n, st_m = [], [], [], [], []
    zero_ret = jnp.zeros((n_ctx_seq, 2, HEADS, HEAD_DIM, HEAD_DIM), F32)
    zero_vec = jnp.zeros((n_ctx_seq, 2, HEADS, 1, HEAD_DIM), F32)
    for l in range(DEPTH):
        mod_l = mod[l]
        w_main = w_in[l][:, :n_main].astype(BF16)
        b_main = _row2(b_in[l][:n_main])
        w_gate = jnp.zeros((D_MODEL, LANE), F32).at[:, :4 * HEADS].set(w_in[l][:, gate_off:merge_off]).astype(BF16)
        b_gate = jnp.zeros((1, LANE), F32).at[0, :4 * HEADS].set(b_in[l][gate_off:merge_off])
        proj, u4, gates = _inproj(x, mod_l, w_main, b_main, w_gate, b_gate, cfg)

        s5_ain, s5_klag, s5_bout, s5_a = (m[l] for m in s5_fac)
        loc = _s5_state_in(u4, s5_ain, cfg)
        x0 = jnp.stack([cache_s5_re[:, l], cache_s5_im[:, l]], 0)
        x0 = x0.reshape(2, n_lat_seq, 2, S5_Q, S5_HALF).transpose(3, 2, 0, 1, 4).astype(F32)
        xprev, s5_fin = _s5_scan(loc, s5_a, x0, cfg)
        d4 = jnp.tile(s5_d[l].astype(F32).reshape(S5_Q, 1, LANE), (1, SUB, 1)).reshape(S5_Q, 1, S5_FLAT)
        z = _s5_output(u4, xprev, s5_klag, s5_bout, d4, cfg)
        st_s5.append(s5_fin)

        gg, gb = _row2(ret_gn_g[l]), _row2(ret_gn_b[l])
        dec = ret_decay[l].astype(F32)
        yb_c, ret_fin = _retention(proj, dec, cos_t, sin_t, zero_ret, gg, gb, n_seq=n_ctx_seq, seq_len=ctx_len,
                                   row0=0, use_rot=False, has_init=False, want_final=True)
        yb_s, = _retention(proj, dec, cos_t, sin_t, cache_ret[:, l].astype(F32), gg, gb, n_seq=n_lat_seq,
                           seq_len=lat_len, row0=n_ctx_tok, use_rot=True, has_init=True, want_final=False)
        st_ret.append(ret_fin)

        gg = jnp.broadcast_to(ml_gn_g[l].astype(F32)[:, None], (WIDTH, LANE))
        gb = jnp.broadcast_to(ml_gn_b[l].astype(F32)[:, None], (WIDTH, LANE))
        gcol, grow = _gate_prep(gates, cfg)
        yc_c, c_fin, n_fin, m_fin = _mlstm(proj, gcol, grow, zero_ret, zero_vec, zero_vec, gg, gb, n_seq=n_ctx_seq,
                                           seq_len=ctx_len, row0=0, has_init=False, want_final=True)
        n0 = cache_ml_n[:, l].astype(F32)[:, :, :, None, :]
        m0 = jnp.broadcast_to(cache_ml_m[:, l].astype(F32)[:, :, :, None, None], n0.shape)
        yc_s, = _mlstm(proj, gcol, grow, cache_ml_c[:, l].astype(F32), n0, m0, gg, gb, n_seq=n_lat_seq,
                       seq_len=lat_len, row0=n_ctx_tok, has_init=True, want_final=False)
        st_c.append(c_fin)
        st_n.append(n_fin[:, :, :, 0, :])
        st_m.append(m_fin[:, :, :, 0, 0])

        j = l // 2
        router = None
        if l % 2 == 1:
            router = (jnp.zeros((D_MODEL, LANE), F32).at[:, :N_EXPERTS].set(moe_router[j]),
                      jnp.zeros((1, LANE), F32).at[0, :N_EXPERTS].set(moe_router_b[j]))
        merged = _merge(x, mod_l, z, yb_c, yb_s, yc_c, yc_s, w_in[l][:, merge_off:].astype(BF16),
                        _row2(b_in[l][merge_off:]), s5_glu_w[l].astype(BF16), _row2(s5_glu_b[l]), w_a[l].astype(BF16),
                        w_b[l].astype(BF16), w_c[l].astype(BF16), w_o[l].astype(BF16), _row2(ln1_g[l]),
                        _row2(ln1_b[l]), router, cfg)

        if router is None:
            x = _dense_ffn(merged[0], mod_l, ffn_w1[j].astype(BF16), ffn_w3[j].astype(BF16), ffn_w2[j].astype(BF16),
                           _row2(ln2_g[l]), _row2(ln2_b[l]), cfg)
        else:
            x = _moe_ffn(merged[0], mod_l, merged[1], moe_w1[j].astype(BF16), moe_w3[j].astype(BF16),
                         moe_w2[j].astype(BF16), _row2(ln2_g[l]), _row2(ln2_b[l]), cfg)

    y_p = x[:n_ctx_tok].reshape(n_ctx_seq, ctx_len, D_MODEL)
    y_s = x[n_ctx_tok:].reshape(n_lat_seq, lat_len, D_MODEL)
    s5 = jnp.stack(st_s5, 0)
    s5 = s5.reshape(DEPTH, S5_Q, 2, 2, n_ctx_seq, S5_QG, S5_STATE).transpose(3, 4, 0, 2, 1, 5, 6)
    s5 = s5.reshape(2, n_ctx_seq, DEPTH, 2, S5_GROUPS, S5_STATE)
    return (y_p, y_s, s5[0], s5[1], jnp.stack(st_ret, 1), jnp.stack(st_c, 1), jnp.stack(st_n, 1),
            jnp.stack(st_m, 1))
```

```python
import functools

import jax
import jax.numpy as jnp
from jax import lax
from jax.experimental import pallas as pl
from jax.experimental.pallas import tpu as pltpu

F32 = jnp.float32
BF16 = jnp.bfloat16

D_MODEL = 1024
DEPTH = 4
GRID_W = 64
CHUNK = 128
S5_WIDTH = 512
S5_GROUP = 16
S5_GROUPS = 32
S5_STATE = 64
HEADS = 4
HEAD_DIM = 128
WIDTH = 512
ROPE_BASE = 10000.0
D_FF = 2816
N_EXPERTS = 8
ALPHA = (2.0 * DEPTH) ** 0.25
LN_EPS = 1e-5
GN_EPS = 1e-5
N_COND = 16
SUB = 16
N_LAG = 2 * SUB - 1
LANE = 128
S5_Q = S5_WIDTH // LANE
S5_QG = LANE // S5_GROUP
S5_FLAT = SUB * LANE
S5_HALF = S5_QG * S5_STATE
S5_ST = 4 * S5_HALF
NEG_BIG = -1e30
MOE_BLK = 128
GATE_BC = 16
GATE_REST = 32


def _dot(a, b):
    return jnp.dot(a, b, preferred_element_type=F32)


def _dot_hi(a, b):
    return jnp.dot(a, b, preferred_element_type=F32, precision=lax.Precision.HIGHEST)


def _dot_nt(a, b):
    return lax.dot_general(a, b, (((1,), (1,)), ((), ())), preferred_element_type=F32)


def _dot_tn(a, b):
    return lax.dot_general(a, b, (((0,), (0,)), ((), ())), preferred_element_type=F32)


def _params(sem, vmem_mb):
    return pltpu.CompilerParams(dimension_semantics=sem, vmem_limit_bytes=vmem_mb << 20)


def _cond_row(tile, tm, n_ctx_tok, lat_len):
    start = tile * tm
    return jnp.where(start < n_ctx_tok, 0, 1 + (start - n_ctx_tok) // lat_len)


def _iota(shape, axis):
    return lax.broadcasted_iota(jnp.int32, shape, axis)


def _mod_kernel(c_ref, w_ref, b_ref, o_ref):
    o_ref[...] = _dot_hi(jax.nn.silu(c_ref[...]), w_ref[...]) + b_ref[...]


def _modulation(cond, ada_w, ada_b):
    tn = 1536
    n = ada_w.shape[-1]
    return pl.pallas_call(
        _mod_kernel,
        grid=(DEPTH, n // tn),
        in_specs=[pl.BlockSpec((N_COND, D_MODEL), lambda l, j: (0, 0)),
                  pl.BlockSpec((None, D_MODEL, tn), lambda l, j: (l, 0, j)),
                  pl.BlockSpec((None, 1, tn), lambda l, j: (l, 0, j))],
        out_specs=pl.BlockSpec((None, N_COND, tn), lambda l, j: (l, 0, j)),
        out_shape=jax.ShapeDtypeStruct((DEPTH, N_COND, n), F32),
        compiler_params=_params(("parallel", "parallel"), 40),
        name="modulation",
    )(cond, ada_w, ada_b.reshape(DEPTH, 1, n))


def _inproj_kernel(x_ref, mod_ref, w_ref, b_ref, wg_ref, bg_ref, o_ref, u4_ref, g_ref, h_scr, u_scr):
    j = pl.program_id(1)
    tm = x_ref.shape[0]

    @pl.when(j == 0)
    def _():
        h = (x_ref[...] * (1.0 + mod_ref[1:2, :]) + mod_ref[0:1, :]).astype(BF16)
        h_scr[...] = h
        g_ref[...] = _dot(h, wg_ref[...]) + bg_ref[...]
        u = _dot(h, w_ref[...]) + b_ref[...]
        for q in range(S5_Q):
            u_scr[q] = u[:, q * LANE:(q + 1) * LANE]
            for t in range(SUB):
                u4_ref[q, :, t * LANE:(t + 1) * LANE] = u_scr[q, pl.ds(t, tm // SUB, stride=SUB), :].astype(BF16)

    @pl.when(j > 0)
    def _():
        o_ref[...] = (_dot(h_scr[...], w_ref[...]) + b_ref[...]).astype(o_ref.dtype)


def _inproj(x, mod_l, w, b, wg, bg, cfg):
    t_tok = x.shape[0]
    tm, tn = cfg["tm_in"], 512
    nj = w.shape[1] // tn
    cond = functools.partial(_cond_row, tm=tm, n_ctx_tok=cfg["n_ctx_tok"], lat_len=cfg["lat_len"])
    return pl.pallas_call(
        _inproj_kernel,
        grid=(t_tok // tm, nj),
        in_specs=[pl.BlockSpec((tm, D_MODEL), lambda i, j: (i, 0)),
                  pl.BlockSpec((None, 6, D_MODEL), lambda i, j: (cond(i), 0, 0)),
                  pl.BlockSpec((D_MODEL, tn), lambda i, j: (0, j)),
                  pl.BlockSpec((1, tn), lambda i, j: (0, j)),
                  pl.BlockSpec((D_MODEL, LANE), lambda i, j: (0, 0)),
                  pl.BlockSpec((1, LANE), lambda i, j: (0, 0))],
        out_specs=[pl.BlockSpec((tm, tn), lambda i, j: (i, jnp.maximum(j - 1, 0))),
                   pl.BlockSpec((S5_Q, tm // SUB, S5_FLAT), lambda i, j: (0, i, 0)),
                   pl.BlockSpec((tm, LANE), lambda i, j: (i, 0))],
        out_shape=[jax.ShapeDtypeStruct((t_tok, (nj - 1) * tn), BF16),
                   jax.ShapeDtypeStruct((S5_Q, t_tok // SUB, S5_FLAT), BF16),
                   jax.ShapeDtypeStruct((t_tok, LANE), F32)],
        scratch_shapes=[pltpu.VMEM((tm, D_MODEL), BF16), pltpu.VMEM((S5_Q, tm, LANE), F32)],
        compiler_params=_params(("parallel", "arbitrary"), 48),
        name="inproj",
    )(x, mod_l, w, b, wg, bg)


def _s5_factors(lam_re, lam_im, log_step, b_re, b_im, c_re, c_im):
    lam = lax.complex(lam_re.astype(F32), lam_im.astype(F32))
    lam_dt = lam * jnp.exp(log_step.astype(F32))[..., None]
    lam_bar = jnp.exp(lam_dt)
    bbar = ((lam_bar - 1.0) / lam)[..., None] * lax.complex(b_re.astype(F32), b_im.astype(F32))
    cmat = lax.complex(c_re.astype(F32), c_im.astype(F32))
    ks = jnp.arange(SUB + 1, dtype=F32)
    pw = jnp.exp(lam_dt[None] * ks[:, None, None, None])
    kern = jnp.einsum('dgcp,tdgp,dgpe->dgtce', cmat, pw[:SUB], bbar).real
    pad = jnp.zeros_like(kern[0][:, :SUB - 1])
    ktab = jnp.concatenate([pad, kern[0]], 1) + jnp.concatenate([kern[1][:, ::-1], pad], 1)
    k_lag = ktab.reshape(S5_Q, S5_QG, N_LAG, S5_GROUP, S5_GROUP).transpose(0, 2, 3, 1, 4)
    k_lag = k_lag.reshape(S5_Q, N_LAG, S5_GROUP, LANE)
    pw_in = jnp.stack([pw[:SUB][::-1, 0], pw[:SUB][:, 1]], 0)
    wb = pw_in[..., None] * bbar[:, None]
    wb = jnp.stack([wb.real, wb.imag], 1).reshape(2, 2, SUB, S5_Q, S5_QG, S5_STATE, S5_GROUP)
    a_in = wb.transpose(3, 2, 0, 1, 5, 4, 6).reshape(S5_Q, SUB, 4, S5_STATE, LANE)
    pw_out = jnp.stack([pw[1:, 0], pw[1:][::-1, 1]], 0)
    ce = cmat[:, None] * pw_out[:, :, :, None, :]
    ce = jnp.stack([ce.real, -ce.imag], 1).reshape(2, 2, SUB, S5_Q, S5_QG, S5_GROUP, S5_STATE)
    b_out = ce.transpose(3, 0, 1, 2, 5, 4, 6).reshape(S5_Q, 4, SUB, S5_GROUP, S5_HALF)
    a = pw[SUB]
    a = jnp.stack([a.real, a.imag], 1).reshape(2, 2, S5_Q, 1, S5_HALF).transpose(2, 0, 1, 3, 4)
    return a_in.astype(BF16), k_lag.astype(BF16), b_out.astype(BF16), a


def _expand(src_t, n_rep, row_shift, col_shift):
    k, r = src_t.shape
    rep = jnp.where(_iota((k, n_rep * k), 0) == (_iota((k, n_rep * k), 1) & (k - 1)), 1.0, 0.0).astype(BF16)
    same = (_iota((r, n_rep * k), 0) >> row_shift) == (_iota((r, n_rep * k), 1) >> col_shift)
    return jnp.where(same, _dot_tn(src_t, rep), 0.0).astype(BF16)


def _s5a_kernel(u_ref, a_ref, o_ref, w_scr):
    @pl.when(pl.program_id(1) == 0)
    def _():
        for t in range(SUB):
            for k in range(4):
                w_scr[t * LANE:(t + 1) * LANE, k * S5_HALF:(k + 1) * S5_HALF] = _expand(a_ref[t, k], S5_QG, 4, 6)

    res = _dot(u_ref[...], w_scr[...])
    for d in range(2):
        for r in range(2):
            k = 2 * d + r
            o_ref[d, r] = res[:, k * S5_HALF:(k + 1) * S5_HALF]


def _s5_state_in(u4, a_in, cfg):
    rows = u4.shape[1]
    rt = cfg["s5_rows"]
    return pl.pallas_call(
        _s5a_kernel,
        grid=(S5_Q, rows // rt),
        in_specs=[pl.BlockSpec((None, rt, S5_FLAT), lambda q, i: (q, i, 0)),
                  pl.BlockSpec((None, SUB, 4, S5_STATE, LANE), lambda q, i: (q, 0, 0, 0, 0))],
        out_specs=pl.BlockSpec((None, 2, 2, rt, S5_HALF), lambda q, i: (q, 0, 0, i, 0)),
        out_shape=jax.ShapeDtypeStruct((S5_Q, 2, 2, rows, S5_HALF), F32),
        scratch_shapes=[pltpu.VMEM((S5_FLAT, S5_ST), BF16)],
        compiler_params=_params(("parallel", "arbitrary"), 48),
        name="s5_state_in",
    )(u4, a_in)


def _s5b_kernel(loc_ref, a_ref, x0_ref, xp_ref, fin_ref, *, n_ctx_seq, ctx_sub, n_lat_seq, lat_sub):
    d = pl.program_id(1)
    ar = jnp.broadcast_to(a_ref[0], (8, LANE))
    ai = jnp.broadcast_to(a_ref[1], (8, LANE))

    def run(base, nsub, xr0, xi0):
        def body(jj, carry):
            xr, xi = carry
            j = jnp.where(d == 0, jj, nsub - 1 - jj)
            idx = pl.ds(base + j, 8, stride=nsub)
            xp_ref[0, idx, :] = xr
            xp_ref[1, idx, :] = xi
            lr = loc_ref[0, idx, :]
            li = loc_ref[1, idx, :]
            return ar * xr - ai * xi + lr, ar * xi + ai * xr + li
        return lax.fori_loop(0, nsub, body, (xr0, xi0))

    zero = jnp.zeros((8, LANE), F32)
    for bg in range(n_ctx_seq // 8):
        xr, xi = run(bg * 8 * ctx_sub, ctx_sub, zero, zero)
        fin_ref[0, bg * 8:(bg + 1) * 8, :] = xr
        fin_ref[1, bg * 8:(bg + 1) * 8, :] = xi
    for bg in range(n_lat_seq // 8):
        run(n_ctx_seq * ctx_sub + bg * 8 * lat_sub, lat_sub, x0_ref[0, bg * 8:(bg + 1) * 8, :],
            x0_ref[1, bg * 8:(bg + 1) * 8, :])


def _s5_scan(loc, a, x0, cfg):
    rows = loc.shape[3]
    n_ctx_seq, n_lat_seq = cfg["n_ctx_seq"], cfg["n_lat_seq"]
    kern = functools.partial(_s5b_kernel, n_ctx_seq=n_ctx_seq, ctx_sub=cfg["ctx_len"] // SUB,
                             n_lat_seq=n_lat_seq, lat_sub=cfg["lat_len"] // SUB)
    nlb = S5_HALF // LANE
    return pl.pallas_call(
        kern,
        grid=(S5_Q, 2, nlb),
        in_specs=[pl.BlockSpec((None, None, 2, rows, LANE), lambda q, d, b: (q, d, 0, 0, b)),
                  pl.BlockSpec((None, None, 2, 1, LANE), lambda q, d, b: (q, d, 0, 0, b)),
                  pl.BlockSpec((None, None, 2, n_lat_seq, LANE), lambda q, d, b: (q, d, 0, 0, b))],
        out_specs=[pl.BlockSpec((None, None, 2, rows, LANE), lambda q, d, b: (q, d, 0, 0, b)),
                   pl.BlockSpec((None, None, 2, n_ctx_seq, LANE), lambda q, d, b: (q, d, 0, 0, b))],
        out_shape=[jax.ShapeDtypeStruct(loc.shape, F32),
                   jax.ShapeDtypeStruct((S5_Q, 2, 2, n_ctx_seq, S5_HALF), F32)],
        compiler_params=_params(("parallel", "parallel", "parallel"), 48),
        name="s5_scan",
    )(loc, a, x0)


def _s5c_kernel(u_ref, xp_ref, k_ref, b_ref, d_ref, z_ref, m_scr, wo_scr, bd_scr, z_scr):
    rt = u_ref.shape[0]

    @pl.when(pl.program_id(1) == 0)
    def _():
        for l in range(N_LAG):
            bd_scr[l] = _expand(k_ref[l], S5_QG, 4, 4)
        for t in range(SUB):
            for s in range(SUB):
                m_scr[t * LANE:(t + 1) * LANE, s * LANE:(s + 1) * LANE] = bd_scr[s - t + SUB - 1]
        for k in range(4):
            for t in range(SUB):
                wo_scr[k * S5_HALF:(k + 1) * S5_HALF, t * LANE:(t + 1) * LANE] = _expand(b_ref[k, t], S5_QG, 6, 4)

    u = u_ref[...]
    xcat = jnp.concatenate([xp_ref[0, 0], xp_ref[0, 1], xp_ref[1, 0], xp_ref[1, 1]], axis=1).astype(BF16)
    y = _dot(u, m_scr[...]) + _dot(xcat, wo_scr[...])
    z = jax.nn.gelu(d_ref[...] * u.astype(F32) + y)
    for t in range(SUB):
        z_scr[pl.ds(t, rt, stride=SUB), :] = z[:, t * LANE:(t + 1) * LANE]
    z_ref[...] = z_scr[...].astype(z_ref.dtype)


def _s5_output(u4, xprev, k_lag, b_out, d4, cfg):
    rows = u4.shape[1]
    rt = cfg["s5_rows"]
    return pl.pallas_call(
        _s5c_kernel,
        grid=(S5_Q, rows // rt),
        in_specs=[pl.BlockSpec((None, rt, S5_FLAT), lambda q, i: (q, i, 0)),
                  pl.BlockSpec((None, 2, 2, rt, S5_HALF), lambda q, i: (q, 0, 0, i, 0)),
                  pl.BlockSpec((None, N_LAG, S5_GROUP, LANE), lambda q, i: (q, 0, 0, 0)),
                  pl.BlockSpec((None, 4, SUB, S5_GROUP, S5_HALF), lambda q, i: (q, 0, 0, 0, 0)),
                  pl.BlockSpec((None, 1, S5_FLAT), lambda q, i: (q, 0, 0))],
        out_specs=pl.BlockSpec((rt * SUB, LANE), lambda q, i: (i, q)),
        out_shape=jax.ShapeDtypeStruct((rows * SUB, S5_WIDTH), BF16),
        scratch_shapes=[pltpu.VMEM((S5_FLAT, S5_FLAT), BF16), pltpu.VMEM((S5_ST, S5_FLAT), BF16),
                        pltpu.VMEM((N_LAG, LANE, LANE), BF16), pltpu.VMEM((rt * SUB, LANE), F32)],
        compiler_params=_params(("parallel", "arbitrary"), 56),
        name="s5_output",
    )(u4, xprev, k_lag, b_out, d4)


def _group_norm(o, g, b):
    mu = jnp.mean(o, axis=-1, keepdims=True)
    var = jnp.mean(jnp.square(o - mu), axis=-1, keepdims=True)
    return (o - mu) * lax.rsqrt(var + GN_EPS) * g + b


def _ret_kernel(dec_ref, q_ref, k_ref, v_ref, g_ref, cos_ref, sin_ref, s0_ref, gg_ref, gb_ref, y_ref, *rest,
                nc, use_rot, has_init, want_final):
    if want_final:
        sfin_ref, sf_scr, sb_scr, x_scr, kr_scr = rest
    else:
        sf_scr, sb_scr, x_scr, kr_scr = rest
    h = pl.program_id(1)
    row = _iota((CHUNK, CHUNK), 0).astype(F32)
    col = _iota((CHUNK, CHUNK), 1).astype(F32)
    lg_f = -jnp.exp(jnp.full((CHUNK, CHUNK), dec_ref[0, h], F32))
    lg_b = -jnp.exp(jnp.full((CHUNK, CHUNK), dec_ref[1, h], F32))
    lag = row - col
    scale = HEAD_DIM ** -0.5
    dmat = (jnp.where(lag >= 0, jnp.exp(lg_f * jnp.maximum(lag, 0.0)), 0.0)
            + jnp.where(lag <= 0, jnp.exp(lg_b * jnp.maximum(-lag, 0.0)), 0.0)) * scale
    qd_f = jnp.exp(lg_f * (row + 1.0))
    qd_b = jnp.exp(lg_b * (CHUNK - row))
    kd_f = jnp.exp(lg_f * (CHUNK - 1.0 - col)) * scale
    kd_b = jnp.exp(lg_b * col) * scale
    cd_f = jnp.exp(lg_f * CHUNK)
    cd_b = jnp.exp(lg_b * CHUNK)

    def chunk(j):
        return pl.ds(pl.multiple_of(j * CHUNK, CHUNK), CHUNK)

    def rot(ref, sl):
        x = ref[sl, :].astype(F32)
        if not use_rot:
            return x
        return x * cos_ref[sl, :] + pltpu.roll(x, HEAD_DIM // 2, 1) * sin_ref[sl, :]

    def local(j, carry):
        sl = chunk(j)
        k = rot(k_ref, sl)
        kr_scr[sl, :] = k.astype(BF16)
        k_t = k.T
        v = v_ref[sl, :]
        x_scr[0, j] = _dot((k_t * kd_f).astype(BF16), v)
        x_scr[1, j] = _dot((k_t * kd_b).astype(BF16), v)
        return carry

    lax.fori_loop(0, nc, local, 0, unroll=min(4, nc))
    zero = jnp.zeros((CHUNK, CHUNK), F32)

    def states(jj, carry):
        s_f, s_b = carry
        jf, jb = jj, nc - 1 - jj
        sf_scr[jf] = s_f.astype(BF16)
        sb_scr[jb] = s_b.astype(BF16)
        return s_f * cd_f + x_scr[0, jf], s_b * cd_b + x_scr[1, jb]

    init = (s0_ref[0], s0_ref[1]) if has_init else (zero, zero)
    s_f, s_b = lax.fori_loop(0, nc, states, init, unroll=2)
    if want_final:
        sfin_ref[0] = s_f
        sfin_ref[1] = s_b

    def outputs(j, carry):
        sl = chunk(j)
        q = rot(q_ref, sl)
        att = _dot_nt(q.astype(BF16), kr_scr[sl, :]) * dmat
        lhs = jnp.concatenate([att.astype(BF16), (q * qd_f).astype(BF16), (q * qd_b).astype(BF16)], axis=1)
        rhs = jnp.concatenate([v_ref[sl, :], sf_scr[j], sb_scr[j]], axis=0)
        o = _dot(lhs, rhs)
        y = jax.nn.silu(g_ref[sl, :].astype(F32)) * _group_norm(o, gg_ref[...], gb_ref[...])
        y_ref[sl, :] = y.astype(y_ref.dtype)
        return carry

    lax.fori_loop(0, nc, outputs, 0, unroll=min(8, nc))


def _retention(proj, dec, cos_t, sin_t, s0, gn_g, gn_b, *, n_seq, seq_len, row0, use_rot, has_init, want_final):
    nc = seq_len // CHUNK
    blk0 = row0 // seq_len
    kern = functools.partial(_ret_kernel, nc=nc, use_rot=use_rot, has_init=has_init, want_final=want_final)

    def tok(cb):
        return pl.BlockSpec((seq_len, HEAD_DIM), lambda s, h, cb=cb: (blk0 + s, cb + h))

    rot_spec = pl.BlockSpec((seq_len, HEAD_DIM), lambda s, h: (0, 0))
    st_spec = pl.BlockSpec((None, 2, None, HEAD_DIM, HEAD_DIM), lambda s, h: (s, 0, h, 0, 0))
    gn_spec = pl.BlockSpec((1, HEAD_DIM), lambda s, h: (0, h))
    out_specs = [pl.BlockSpec((seq_len, HEAD_DIM), lambda s, h: (s, h))]
    out_shape = [jax.ShapeDtypeStruct((n_seq * seq_len, WIDTH), BF16)]
    if want_final:
        out_specs.append(st_spec)
        out_shape.append(jax.ShapeDtypeStruct((n_seq, 2, HEADS, HEAD_DIM, HEAD_DIM), F32))
    return pl.pallas_call(
        kern,
        grid=(n_seq, HEADS),
        in_specs=[pl.BlockSpec(memory_space=pltpu.SMEM), tok(0), tok(4), tok(8), tok(12),
                  rot_spec, rot_spec, st_spec, gn_spec, gn_spec],
        out_specs=out_specs,
        out_shape=out_shape,
        scratch_shapes=[pltpu.VMEM((nc, HEAD_DIM, HEAD_DIM), BF16), pltpu.VMEM((nc, HEAD_DIM, HEAD_DIM), BF16),
                        pltpu.VMEM((2, nc, HEAD_DIM, HEAD_DIM), F32), pltpu.VMEM((seq_len, HEAD_DIM), BF16)],
        compiler_params=_params(("parallel", "parallel"), 48),
        name="retention",
    )(dec, proj, proj, proj, proj, cos_t, sin_t, s0, gn_g, gn_b)


def _gate_prep_kernel(g_ref, col_ref, row_ref):
    lane = _iota((CHUNK, LANE), 1)
    tri = jnp.where(_iota((CHUNK, CHUNK), 0) >= _iota((CHUNK, CHUNK), 1), 1.0, 0.0)
    for c in range(g_ref.shape[0] // CHUNK):
        sl = slice(c * CHUNK, (c + 1) * CHUNK)
        g = g_ref[sl, :]
        lf = jnp.where(lane < 4 * HEADS, jnp.minimum(g, 0.0) - jnp.log1p(jnp.exp(-jnp.abs(g))), 0.0)
        cs = _dot_hi(tri, lf)
        tot = cs[CHUNK - 1:CHUNK, :]
        bc = jnp.where(lane < 2 * HEADS, cs, tot - cs + lf)
        rest = jnp.where(lane < 2 * HEADS, tot - cs, cs - lf)
        pack = g + pltpu.roll(bc, GATE_BC, 1) + pltpu.roll(rest, GATE_REST, 1)
        col_ref[sl, :] = pack
        row_ref[sl, :] = pack.T


def _gate_prep(gates, cfg):
    t_tok = gates.shape[0]
    tm = cfg["tm_in"]
    spec = pl.BlockSpec((tm, LANE), lambda i: (i, 0))
    return pl.pallas_call(
        _gate_prep_kernel,
        grid=(t_tok // tm,),
        in_specs=[spec],
        out_specs=[spec, spec],
        out_shape=[jax.ShapeDtypeStruct((t_tok, LANE), F32)] * 2,
        compiler_params=_params(("parallel",), 32),
        name="gate_prep",
    )(gates)


def _mlstm_kernel(q_ref, k_ref, v_ref, o_ref, col_ref, row_ref, c0_ref, n0_ref, m0_ref, gg_ref, gb_ref, y_ref, *rest,
                  nc, has_init, want_final):
    if want_final:
        cfin_ref, nfin_ref, mfin_ref, c_scr, n_scr, m_scr, x_scr, nl_scr, ml_scr, bl_scr = rest
    else:
        c_scr, n_scr, m_scr, x_scr, nl_scr, ml_scr, bl_scr = rest
    src = _iota((CHUNK, CHUNK), 0)
    dst = _iota((CHUNK, CHUNK), 1)
    scale = HEAD_DIM ** -0.5

    def chunk(j):
        return pl.ds(pl.multiple_of(j * CHUNK, CHUNK), CHUNK)

    def gate_idx(d, h):
        return d * 2 * HEADS + h, GATE_BC + d * 2 * HEADS + HEADS + h, GATE_REST + d * 2 * HEADS + HEADS + h

    def head_body(h):
        def init(d):
            if has_init:
                return c0_ref[d], n0_ref[d], m0_ref[d]
            return (jnp.zeros((HEAD_DIM, HEAD_DIM), F32), jnp.zeros((1, HEAD_DIM), F32),
                    jnp.zeros((1, HEAD_DIM), F32))

        def local(j, carry):
            sl = chunk(j)
            k = k_ref[sl, :]
            v_t = v_ref[sl, :].astype(F32).T
            rp = row_ref[sl, :]
            for d in range(2):
                ii, bi, ri = gate_idx(d, h)
                bc_row = rp[bi:bi + 1, :]
                b_last = bc_row[:, CHUNK - 1:CHUNK] if d == 0 else bc_row[:, 0:1]
                log_k = rp[ri:ri + 1, :] + rp[ii:ii + 1, :]
                m_loc = jnp.max(log_k, axis=1, keepdims=True)
                kw = jnp.exp(log_k - m_loc)
                x_scr[d, j] = _dot((v_t * kw).astype(BF16), k) * scale
                kw_hi = kw.astype(BF16).astype(F32)
                kw2 = jnp.concatenate([jnp.broadcast_to(kw_hi, (8, CHUNK)), jnp.broadcast_to(kw - kw_hi, (8, CHUNK))], 0)
                nl = _dot(kw2.astype(BF16), k)
                nl_scr[d, j] = (nl[0:1, :] + nl[8:9, :]) * scale
                ml_scr[d, j] = jnp.broadcast_to(m_loc, (1, HEAD_DIM))
                bl_scr[d, j] = jnp.broadcast_to(b_last, (1, HEAD_DIM))
            return carry

        lax.fori_loop(0, nc, local, 0, unroll=min(4, nc))

        def states(jj, carry):
            out = []
            for d, j in ((0, jj), (1, nc - 1 - jj)):
                cmat, nvec, m = carry[d]
                c_scr[d, j] = cmat.astype(BF16)
                n_scr[d, j] = nvec
                m_scr[d, j] = m
                m_new = jnp.maximum(bl_scr[d, j] + m, ml_scr[d, j])
                keep = jnp.exp(bl_scr[d, j] + m - m_new)
                add = jnp.exp(ml_scr[d, j] - m_new)
                out.append((keep * cmat + add * x_scr[d, j], keep * nvec + add * nl_scr[d, j], m_new))
            return tuple(out)

        fin = lax.fori_loop(0, nc, states, (init(0), init(1)), unroll=2)
        if want_final:
            for d in range(2):
                cfin_ref[d] = fin[d][0]
                nfin_ref[d] = fin[d][1]
                mfin_ref[d] = fin[d][2]

        def outputs(j, carry):
            sl = chunk(j)
            q = q_ref[sl, :]
            v = v_ref[sl, :]
            s_t = _dot_nt(k_ref[sl, :], q) * scale
            cp = col_ref[sl, :]
            rp = row_ref[sl, :]
            h_t = None
            for d in range(2):
                ii, bi, _ = gate_idx(d, h)
                causal = (src <= dst) if d == 0 else (src >= dst)
                bc_row = rp[bi:bi + 1, :]
                log_d = jnp.where(causal, bc_row + (cp[:, ii:ii + 1] - cp[:, bi:bi + 1]), -jnp.inf)
                log_prev = bc_row + m_scr[d, j]
                m_t = jnp.maximum(log_prev, jnp.max(log_d, axis=0, keepdims=True))
                w = s_t * jnp.exp(log_d - m_t)
                w_prev = jnp.exp(log_prev - m_t)
                qn = _dot_nt(jnp.broadcast_to(n_scr[d, j], (16, HEAD_DIM)).astype(BF16), q)[0:1, :]
                den = jnp.sum(w, axis=0, keepdims=True) + w_prev * qn
                inv = 1.0 / jnp.maximum(jnp.abs(den), jnp.exp(-m_t))
                num = _dot_tn(v, w.astype(BF16)) + _dot_nt(c_scr[d, j], q) * w_prev
                h_t = num * inv if h_t is None else h_t + num * inv
            y = jax.nn.sigmoid(o_ref[sl, :].astype(F32).T) * h_t
            mu = jnp.mean(y, axis=0, keepdims=True)
            var = jnp.mean(jnp.square(y - mu), axis=0, keepdims=True)
            y = (y - mu) * lax.rsqrt(var + GN_EPS) * gg_ref[...] + gb_ref[...]
            y_ref[sl, :] = y.T.astype(y_ref.dtype)
            return carry

        lax.fori_loop(0, nc, outputs, 0, unroll=min(4, nc))

    hh = pl.program_id(1)
    for h in range(HEADS):
        pl.when(hh == h)(functools.partial(head_body, h))


def _mlstm(proj, gcol, grow, c0, n0, m0, gn_g, gn_b, *, n_seq, seq_len, row0, has_init, want_final):
    nc = seq_len // CHUNK
    blk0 = row0 // seq_len
    kern = functools.partial(_mlstm_kernel, nc=nc, has_init=has_init, want_final=want_final)

    def tok(cb):
        return pl.BlockSpec((seq_len, HEAD_DIM), lambda s, h, cb=cb: (blk0 + s, cb + h))

    gate_spec = pl.BlockSpec((seq_len, LANE), lambda s, h: (blk0 + s, 0))
    c_spec = pl.BlockSpec((None, 2, None, HEAD_DIM, HEAD_DIM), lambda s, h: (s, 0, h, 0, 0))
    v_spec = pl.BlockSpec((None, 2, None, 1, HEAD_DIM), lambda s, h: (s, 0, h, 0, 0))
    gn_spec = pl.BlockSpec((HEAD_DIM, LANE), lambda s, h: (h, 0))
    out_specs = [pl.BlockSpec((seq_len, HEAD_DIM), lambda s, h: (s, h))]
    out_shape = [jax.ShapeDtypeStruct((n_seq * seq_len, WIDTH), BF16)]
    if want_final:
        out_specs += [c_spec, v_spec, v_spec]
        out_shape += [jax.ShapeDtypeStruct((n_seq, 2, HEADS, HEAD_DIM, HEAD_DIM), F32),
                      jax.ShapeDtypeStruct((n_seq, 2, HEADS, 1, HEAD_DIM), F32),
                      jax.ShapeDtypeStruct((n_seq, 2, HEADS, 1, HEAD_DIM), F32)]
    return pl.pallas_call(
        kern,
        grid=(n_seq, HEADS),
        in_specs=[tok(16), tok(20), tok(24), tok(28), gate_spec, gate_spec,
                  c_spec, v_spec, v_spec, gn_spec, gn_spec],
        out_specs=out_specs,
        out_shape=out_shape,
        scratch_shapes=[pltpu.VMEM((2, nc, HEAD_DIM, HEAD_DIM), BF16)] + [pltpu.VMEM((2, nc, 1, HEAD_DIM), F32)] * 2
        + [pltpu.VMEM((2, nc, HEAD_DIM, HEAD_DIM), F32)] + [pltpu.VMEM((2, nc, 1, HEAD_DIM), F32)] * 3,
        compiler_params=_params(("parallel", "parallel"), 48),
        name="mlstm",
    )(proj, proj, proj, proj, gcol, grow, c0, n0, m0, gn_g, gn_b)


def _layer_norm(x, g, b):
    mu = jnp.mean(x, axis=-1, keepdims=True)
    var = jnp.mean(jnp.square(x - mu), axis=-1, keepdims=True)
    return (x - mu) * lax.rsqrt(var + LN_EPS) * g + b


def _top2_gates(h, whi_ref, wlo_ref, b_ref):
    lane = _iota((h.shape[0], LANE), 1)
    h_hi = h.astype(BF16)
    h_lo = (h - h_hi.astype(F32)).astype(BF16)
    logits = _dot(h_hi, whi_ref[...]) + _dot(h_hi, wlo_ref[...]) + _dot(h_lo, whi_ref[...]) + b_ref[...]
    logits = jnp.where(lane < N_EXPERTS, logits, NEG_BIG)
    m1 = jnp.max(logits, axis=1, keepdims=True)
    i1 = jnp.min(jnp.where(logits == m1, lane, LANE), axis=1, keepdims=True)
    rest = jnp.where(lane == i1, NEG_BIG, logits)
    m2 = jnp.max(rest, axis=1, keepdims=True)
    i2 = jnp.min(jnp.where(rest == m2, lane, LANE), axis=1, keepdims=True)
    e2 = jnp.exp(m2 - m1)
    den = 1.0 + e2
    return jnp.where(lane == i1, 1.0 / den, 0.0) + jnp.where(lane == i2, e2 / den, 0.0)


def _merge_kernel(x_ref, mod_ref, z_ref, ybc_ref, ybs_ref, ycc_ref, ycs_ref, wm_ref, bm_ref, wglu_ref, bglu_ref,
                  wa_ref, wb_ref, wc_ref, wo_ref, lg_ref, lb_ref, *rest, n_ctx_tiles, with_router):
    if with_router:
        rwh_ref, rwl_ref, rb_ref, o_ref, g_ref = rest
    else:
        o_ref, = rest
    is_ctx = pl.program_id(0) < n_ctx_tiles
    x = x_ref[...]
    h = (x * (1.0 + mod_ref[1:2, :]) + mod_ref[0:1, :]).astype(BF16)
    z = z_ref[...]
    ya = (z.astype(F32) * jax.nn.sigmoid(_dot(z, wglu_ref[...]) + bglu_ref[...])).astype(BF16)
    yb = jnp.where(is_ctx, ybc_ref[...], ybs_ref[...])
    yc = jnp.where(is_ctx, ycc_ref[...], ycs_ref[...])
    merged = None
    for j, (y, w_ref) in enumerate(((ya, wa_ref), (yb, wb_ref), (yc, wc_ref))):
        gate = jax.nn.sigmoid(_dot(h, wm_ref[:, j * D_MODEL:(j + 1) * D_MODEL]) + bm_ref[:, j * D_MODEL:(j + 1) * D_MODEL])
        term = gate * _dot(y, w_ref[...])
        merged = term if merged is None else merged + term
    mix = _dot(merged.astype(BF16), wo_ref[...])
    x1 = _layer_norm(ALPHA * x + mod_ref[2:3, :] * mix, lg_ref[...], lb_ref[...])
    o_ref[...] = x1
    if with_router:
        g_ref[...] = _top2_gates(x1 * (1.0 + mod_ref[4:5, :]) + mod_ref[3:4, :], rwh_ref, rwl_ref, rb_ref)


def _merge(x, mod_l, z, yb_c, yb_s, yc_c, yc_s, wm, bm, wglu, bglu, wa, wb, wc, wo, lg, lb, router, cfg):
    t_tok = x.shape[0]
    tm = cfg["tm_merge"]
    n_ctx_tiles = cfg["n_ctx_tok"] // tm
    cond = functools.partial(_cond_row, tm=tm, n_ctx_tok=cfg["n_ctx_tok"], lat_len=cfg["lat_len"])

    def full(shape):
        return pl.BlockSpec(shape, lambda i: (0,) * len(shape))

    def tok(w):
        return pl.BlockSpec((tm, w), lambda i: (i, 0))

    ctx_spec = pl.BlockSpec((tm, WIDTH), lambda i: (jnp.minimum(i, n_ctx_tiles - 1), 0))
    lat_spec = pl.BlockSpec((tm, WIDTH), lambda i: (jnp.maximum(i - n_ctx_tiles, 0), 0))
    in_specs = [tok(D_MODEL), pl.BlockSpec((None, 6, D_MODEL), lambda i: (cond(i), 0, 0)),
                tok(WIDTH), ctx_spec, lat_spec, ctx_spec, lat_spec,
                full((D_MODEL, 3 * D_MODEL)), full((1, 3 * D_MODEL)), full((WIDTH, WIDTH)), full((1, WIDTH)),
                full((WIDTH, D_MODEL)), full((WIDTH, D_MODEL)), full((WIDTH, D_MODEL)),
                full((D_MODEL, D_MODEL)), full((1, D_MODEL)), full((1, D_MODEL))]
    args = [x, mod_l, z, yb_c, yb_s, yc_c, yc_s, wm, bm, wglu, bglu, wa, wb, wc, wo, lg, lb]
    out_specs = [tok(D_MODEL)]
    out_shape = [jax.ShapeDtypeStruct((t_tok, D_MODEL), F32)]
    if router is not None:
        in_specs += [full((D_MODEL, LANE)), full((D_MODEL, LANE)), full((1, LANE))]
        args += list(router)
        out_specs.append(tok(LANE))
        out_shape.append(jax.ShapeDtypeStruct((t_tok, LANE), F32))
    return pl.pallas_call(
        functools.partial(_merge_kernel, n_ctx_tiles=n_ctx_tiles, with_router=router is not None),
        grid=(t_tok // tm,),
        in_specs=in_specs,
        out_specs=out_specs,
        out_shape=out_shape,
        compiler_params=_params(("parallel",), 56),
        name="merge",
    )(*args)


def _ffn_kernel(x_ref, mod_ref, w1_ref, w3_ref, w2_ref, lg_ref, lb_ref, o_ref, h_scr, acc_scr):
    f = pl.program_id(1)

    @pl.when(f == 0)
    def _():
        h_scr[...] = (x_ref[...] * (1.0 + mod_ref[4:5, :]) + mod_ref[3:4, :]).astype(BF16)
        acc_scr[...] = jnp.zeros_like(acc_scr)

    h = h_scr[...]
    act = (jax.nn.silu(_dot(h, w1_ref[...])) * _dot(h, w3_ref[...])).astype(BF16)
    acc_scr[...] += _dot(act, w2_ref[...])

    @pl.when(f == pl.num_programs(1) - 1)
    def _():
        o_ref[...] = _layer_norm(ALPHA * x_ref[...] + mod_ref[5:6, :] * acc_scr[...], lg_ref[...], lb_ref[...])


def _dense_ffn(x, mod_l, w1, w3, w2, lg, lb, cfg):
    t_tok = x.shape[0]
    tm, tf = cfg["tm_ffn"], cfg["tf"]
    cond = functools.partial(_cond_row, tm=tm, n_ctx_tok=cfg["n_ctx_tok"], lat_len=cfg["lat_len"])
    return pl.pallas_call(
        _ffn_kernel,
        grid=(t_tok // tm, D_FF // tf),
        in_specs=[pl.BlockSpec((tm, D_MODEL), lambda i, f: (i, 0)),
                  pl.BlockSpec((None, 6, D_MODEL), lambda i, f: (cond(i), 0, 0)),
                  pl.BlockSpec((D_MODEL, tf), lambda i, f: (0, f)),
                  pl.BlockSpec((D_MODEL, tf), lambda i, f: (0, f)),
                  pl.BlockSpec((tf, D_MODEL), lambda i, f: (f, 0)),
                  pl.BlockSpec((1, D_MODEL), lambda i, f: (0, 0)),
                  pl.BlockSpec((1, D_MODEL), lambda i, f: (0, 0))],
        out_specs=pl.BlockSpec((tm, D_MODEL), lambda i, f: (i, 0)),
        out_shape=jax.ShapeDtypeStruct((t_tok, D_MODEL), F32),
        scratch_shapes=[pltpu.VMEM((tm, D_MODEL), BF16), pltpu.VMEM((tm, D_MODEL), F32)],
        compiler_params=_params(("parallel", "arbitrary"), 56),
        name="dense_ffn",
    )(x, mod_l, w1, w3, w2, lg, lb)


def _moe_kernel(x_ref, mod_ref, gate_ref, w1_ref, w3_ref, w2_ref, lg_ref, lb_ref, o_ref,
                h_scr, acc_scr, hc_scr, ob_scr, sp_scr, gt_scr, cnt_smem):
    e = pl.program_id(1)
    f = pl.program_id(2)
    last_f = pl.num_programs(2) - 1
    tm = x_ref.shape[0]

    @pl.when((e == 0) & (f == 0))
    def _():
        h_scr[...] = (x_ref[...] * (1.0 + mod_ref[4:5, :]) + mod_ref[3:4, :]).astype(BF16)
        acc_scr[...] = jnp.zeros_like(acc_scr)
        g = gate_ref[...]
        sel = g > 0.0
        ones = jnp.where(sel, 1.0, 0.0)
        before = jnp.where(_iota((tm, tm), 0) > _iota((tm, tm), 1), 1.0, 0.0).astype(BF16)
        pos = _dot(before, ones.astype(BF16))
        spt = jnp.where(sel, pos, -1.0).T
        gt = g.T
        cnt = jnp.sum(ones, axis=0, keepdims=True)
        for ee in range(N_EXPERTS):
            sp_scr[ee] = spt[ee:ee + 1, :]
            gt_scr[ee] = gt[ee:ee + 1, :]
            cnt_smem[ee] = cnt[0, ee].astype(jnp.int32)

    nb = (cnt_smem[e] + (MOE_BLK - 1)) // MOE_BLK
    row = _iota((MOE_BLK, tm), 0)

    def onehot(b):
        return sp_scr[e] == (row + b * MOE_BLK).astype(F32)

    @pl.when(f == 0)
    def _():
        def gather(b, carry):
            p = jnp.where(onehot(b), 1.0, 0.0).astype(BF16)
            hc_scr[b] = _dot(p, h_scr[...]).astype(BF16)
            ob_scr[b] = jnp.zeros((MOE_BLK, D_MODEL), F32)
            return carry
        lax.fori_loop(0, nb, gather, 0)

        @pl.when(nb % 2 == 1)
        def _():
            ob_scr[nb] = jnp.zeros((MOE_BLK, D_MODEL), F32)

    def ffn(b, carry):
        hc = hc_scr[b]
        act = (jax.nn.silu(_dot(hc, w1_ref[...])) * _dot(hc, w3_ref[...])).astype(BF16)
        ob_scr[b] += _dot(act, w2_ref[...])
        return carry
    lax.fori_loop(0, nb, ffn, 0)

    @pl.when(f == last_f)
    def _():
        row2 = _iota((2 * MOE_BLK, tm), 0)

        def scatter(p, carry):
            m = sp_scr[e] == (row2 + p * (2 * MOE_BLK)).astype(F32)
            gc = jnp.sum(jnp.where(m, gt_scr[e], 0.0), axis=1, keepdims=True)
            og = (ob_scr[pl.ds(2 * p, 2)].reshape(2 * MOE_BLK, D_MODEL) * gc).astype(BF16)
            acc_scr[...] += _dot_tn(jnp.where(m, 1.0, 0.0).astype(BF16), og)
            return carry
        lax.fori_loop(0, (nb + 1) // 2, scatter, 0)

    @pl.when((e == pl.num_programs(1) - 1) & (f == last_f))
    def _():
        o_ref[...] = _layer_norm(ALPHA * x_ref[...] + mod_ref[5:6, :] * acc_scr[...], lg_ref[...], lb_ref[...])


def _moe_ffn(x, mod_l, gates, w1, w3, w2, lg, lb, cfg):
    t_tok = x.shape[0]
    tm, tf = cfg["tm_moe"], cfg["tf"]
    cond = functools.partial(_cond_row, tm=tm, n_ctx_tok=cfg["n_ctx_tok"], lat_len=cfg["lat_len"])
    return pl.pallas_call(
        _moe_kernel,
        grid=(t_tok // tm, N_EXPERTS, D_FF // tf),
        in_specs=[pl.BlockSpec((tm, D_MODEL), lambda i, e, f: (i, 0)),
                  pl.BlockSpec((None, 6, D_MODEL), lambda i, e, f: (cond(i), 0, 0)),
                  pl.BlockSpec((tm, LANE), lambda i, e, f: (i, 0)),
                  pl.BlockSpec((None, D_MODEL, tf), lambda i, e, f: (e, 0, f)),
                  pl.BlockSpec((None, D_MODEL, tf), lambda i, e, f: (e, 0, f)),
                  pl.BlockSpec((None, tf, D_MODEL), lambda i, e, f: (e, f, 0)),
                  pl.BlockSpec((1, D_MODEL), lambda i, e, f: (0, 0)),
                  pl.BlockSpec((1, D_MODEL), lambda i, e, f: (0, 0))],
        out_specs=pl.BlockSpec((tm, D_MODEL), lambda i, e, f: (i, 0)),
        out_shape=jax.ShapeDtypeStruct((t_tok, D_MODEL), F32),
        scratch_shapes=[pltpu.VMEM((tm, D_MODEL), BF16), pltpu.VMEM((tm, D_MODEL), F32),
                        pltpu.VMEM((tm // MOE_BLK, MOE_BLK, D_MODEL), BF16),
                        pltpu.VMEM((tm // MOE_BLK, MOE_BLK, D_MODEL), F32),
                        pltpu.VMEM((N_EXPERTS, 1, tm), F32), pltpu.VMEM((N_EXPERTS, 1, tm), F32),
                        pltpu.SMEM((N_EXPERTS,), jnp.int32)],
        compiler_params=_params(("parallel", "arbitrary", "arbitrary"), 56),
        name="moe_ffn",
    )(x, mod_l, gates, w1, w3, w2, lg, lb)


def _rotary_tables(n_tok):
    rows = n_tok // GRID_W
    r = jnp.repeat(jnp.arange(rows, dtype=F32), GRID_W)
    col = jnp.tile(jnp.arange(GRID_W, dtype=F32), rows)
    n_freq = HEAD_DIM // 4
    inv = ROPE_BASE ** (-jnp.arange(n_freq, dtype=F32) / n_freq)
    ang = jnp.concatenate([r[:, None] * inv, col[:, None] * inv], -1)
    cos, sin = jnp.cos(ang), jnp.sin(ang)
    return jnp.concatenate([cos, cos], -1), jnp.concatenate([-sin, sin], -1)


def _row2(v):
    return v.reshape(1, -1).astype(F32)


def kernel(x_prompt, x_sample, cache_s5_re, cache_s5_im, cache_ret, cache_ml_c, cache_ml_n, cache_ml_m, c, c_ctx, ada_w, ada_b, w_in, b_in, s5_lam_re, s5_lam_im, s5_log_step, s5_b_re, s5_b_im, s5_c_re, s5_c_im, s5_d, s5_glu_w, s5_glu_b, ret_decay, ret_gn_g, ret_gn_b, ml_gn_g, ml_gn_b, w_a, w_b, w_c, w_o, ln1_g, ln1_b, ln2_g, ln2_b, ffn_w1, ffn_w3, ffn_w2, moe_router, moe_router_b, moe_w1, moe_w3, moe_w2):
    n_ctx_seq, ctx_len, _ = x_prompt.shape
    n_lat_seq, lat_len, _ = x_sample.shape
    n_ctx_tok = n_ctx_seq * ctx_len
    n_lat_tok = n_lat_seq * lat_len
    t_tok = n_ctx_tok + n_lat_tok
    assert n_lat_seq + 1 <= N_COND and n_ctx_seq % 8 == 0 and n_lat_seq % 8 == 0
    assert ctx_len % CHUNK == 0 and lat_len % CHUNK == 0 and n_ctx_tok % lat_len == 0
    cfg = dict(n_ctx_seq=n_ctx_seq, ctx_len=ctx_len, n_lat_seq=n_lat_seq, lat_len=lat_len, n_ctx_tok=n_ctx_tok,
               tm_in=min(2048, lat_len), tm_merge=min(512, lat_len), tm_ffn=min(512, lat_len), tm_moe=min(1024, lat_len), tf=1408,
               s5_rows=min(256, t_tok // SUB // 8))

    x = jnp.concatenate([x_prompt.reshape(n_ctx_tok, D_MODEL), x_sample.reshape(n_lat_tok, D_MODEL)], 0)
    cond = jnp.zeros((N_COND, D_MODEL), F32).at[0].set(c_ctx).at[1:1 + n_lat_seq].set(c)
    mod = _modulation(cond, ada_w, ada_b).reshape(DEPTH, N_COND, 6, D_MODEL)
    cos_t, sin_t = _rotary_tables(lat_len)

    n_main = S5_WIDTH + 8 * WIDTH
    gate_off = n_main
    merge_off = gate_off + 4 * HEADS
    s5_fac = jax.vmap(_s5_factors)(s5_lam_re, s5_lam_im, s5_log_step, s5_b_re, s5_b_im, s5_c_re, s5_c_im)

    st_s5, st_ret, st_c, st_n, st_m = [], [], [], [], []
    zero_ret = jnp.zeros((n_ctx_seq, 2, HEADS, HEAD_DIM, HEAD_DIM), F32)
    zero_vec = jnp.zeros((n_ctx_seq, 2, HEADS, 1, HEAD_DIM), F32)
    for l in range(DEPTH):
        mod_l = mod[l]
        w_main = w_in[l][:, :n_main].astype(BF16)
        b_main = _row2(b_in[l][:n_main])
        w_gate = jnp.zeros((D_MODEL, LANE), F32).at[:, :4 * HEADS].set(w_in[l][:, gate_off:merge_off]).astype(BF16)
        b_gate = jnp.zeros((1, LANE), F32).at[0, :4 * HEADS].set(b_in[l][gate_off:merge_off])
        proj, u4, gates = _inproj(x, mod_l, w_main, b_main, w_gate, b_gate, cfg)

        s5_ain, s5_klag, s5_bout, s5_a = (m[l] for m in s5_fac)
        loc = _s5_state_in(u4, s5_ain, cfg)
        x0 = jnp.stack([cache_s5_re[:, l], cache_s5_im[:, l]], 0)
        x0 = x0.reshape(2, n_lat_seq, 2, S5_Q, S5_HALF).transpose(3, 2, 0, 1, 4).astype(F32)
        xprev, s5_fin = _s5_scan(loc, s5_a, x0, cfg)
        d4 = jnp.tile(s5_d[l].astype(F32).reshape(S5_Q, 1, LANE), (1, SUB, 1)).reshape(S5_Q, 1, S5_FLAT)
        z = _s5_output(u4, xprev, s5_klag, s5_bout, d4, cfg)
        st_s5.append(s5_fin)

        gg, gb = _row2(ret_gn_g[l]), _row2(ret_gn_b[l])
        dec = ret_decay[l].astype(F32)
        yb_c, ret_fin = _retention(proj, dec, cos_t, sin_t, zero_ret, gg, gb, n_seq=n_ctx_seq, seq_len=ctx_len,
                                   row0=0, use_rot=False, has_init=False, want_final=True)
        yb_s, = _retention(proj, dec, cos_t, sin_t, cache_ret[:, l].astype(F32), gg, gb, n_seq=n_lat_seq,
                           seq_len=lat_len, row0=n_ctx_tok, use_rot=True, has_init=True, want_final=False)
        st_ret.append(ret_fin)

        gg = jnp.broadcast_to(ml_gn_g[l].astype(F32)[:, None], (WIDTH, LANE))
        gb = jnp.broadcast_to(ml_gn_b[l].astype(F32)[:, None], (WIDTH, LANE))
        gcol, grow = _gate_prep(gates, cfg)
        yc_c, c_fin, n_fin, m_fin = _mlstm(proj, gcol, grow, zero_ret, zero_vec, zero_vec, gg, gb, n_seq=n_ctx_seq,
                                           seq_len=ctx_len, row0=0, has_init=False, want_final=True)
        n0 = cache_ml_n[:, l].astype(F32)[:, :, :, None, :]
        m0 = jnp.broadcast_to(cache_ml_m[:, l].astype(F32)[:, :, :, None, None], n0.shape)
        yc_s, = _mlstm(proj, gcol, grow, cache_ml_c[:, l].astype(F32), n0, m0, gg, gb, n_seq=n_lat_seq,
                       seq_len=lat_len, row0=n_ctx_tok, has_init=True, want_final=False)
        st_c.append(c_fin)
        st_n.append(n_fin[:, :, :, 0, :])
        st_m.append(m_fin[:, :, :, 0, 0])

        j = l // 2
        router = None
        if l % 2 == 1:
            rw = jnp.zeros((D_MODEL, LANE), F32).at[:, :N_EXPERTS].set(moe_router[j])
            rw_hi = rw.astype(BF16)
            router = (rw_hi, (rw - rw_hi.astype(F32)).astype(BF16),
                      jnp.zeros((1, LANE), F32).at[0, :N_EXPERTS].set(moe_router_b[j]))
        merged = _merge(x, mod_l, z, yb_c, yb_s, yc_c, yc_s, w_in[l][:, merge_off:].astype(BF16),
                        _row2(b_in[l][merge_off:]), s5_glu_w[l].astype(BF16), _row2(s5_glu_b[l]), w_a[l].astype(BF16),
                        w_b[l].astype(BF16), w_c[l].astype(BF16), w_o[l].astype(BF16), _row2(ln1_g[l]),
                        _row2(ln1_b[l]), router, cfg)

        if router is None:
            x = _dense_ffn(merged[0], mod_l, ffn_w1[j].astype(BF16), ffn_w3[j].astype(BF16), ffn_w2[j].astype(BF16),
                           _row2(ln2_g[l]), _row2(ln2_b[l]), cfg)
        else:
            x = _moe_ffn(merged[0], mod_l, merged[1], moe_w1[j].astype(BF16), moe_w3[j].astype(BF16),
                         moe_w2[j].astype(BF16), _row2(ln2_g[l]), _row2(ln2_b[l]), cfg)

    y_p = x[:n_ctx_tok].reshape(n_ctx_seq, ctx_len, D_MODEL)
    y_s = x[n_ctx_tok:].reshape(n_lat_seq, lat_len, D_MODEL)
    s5 = jnp.stack(st_s5, 0)
    s5 = s5.reshape(DEPTH, S5_Q, 2, 2, n_ctx_seq, S5_QG, S5_STATE).transpose(3, 4, 0, 2, 1, 5, 6)
    s5 = s5.reshape(2, n_ctx_seq, DEPTH, 2, S5_GROUPS, S5_STATE)
    return (y_p, y_s, s5[0], s5[1], jnp.stack(st_ret, 1), jnp.stack(st_c, 1), jnp.stack(st_n, 1),
            jnp.stack(st_m, 1))
```

```python
import functools

import jax
import jax.numpy as jnp
from jax import lax
from jax.experimental import pallas as pl
from jax.experimental.pallas import tpu as pltpu

F32 = jnp.float32
BF16 = jnp.bfloat16

D_MODEL = 1024
DEPTH = 4
GRID_W = 64
CHUNK = 128
S5_WIDTH = 512
S5_GROUP = 16
S5_GROUPS = 32
S5_STATE = 64
HEADS = 4
HEAD_DIM = 128
WIDTH = 512
ROPE_BASE = 10000.0
D_FF = 2816
N_EXPERTS = 8
ALPHA = (2.0 * DEPTH) ** 0.25
LN_EPS = 1e-5
GN_EPS = 1e-5
N_COND = 16
SUB = 16
N_LAG = 2 * SUB - 1
LANE = 128
S5_Q = S5_WIDTH // LANE
S5_QG = LANE // S5_GROUP
S5_FLAT = SUB * LANE
S5_HALF = S5_QG * S5_STATE
S5_ST = 4 * S5_HALF
NEG_BIG = -1e30
MOE_BLK = 128
GATE_BC = 16
GATE_REST = 32


def _dot(a, b):
    return jnp.dot(a, b, preferred_element_type=F32)


def _dot_hi(a, b):
    return jnp.dot(a, b, preferred_element_type=F32, precision=lax.Precision.HIGHEST)


def _dot_nt(a, b):
    return lax.dot_general(a, b, (((1,), (1,)), ((), ())), preferred_element_type=F32)


def _dot_tn(a, b):
    return lax.dot_general(a, b, (((0,), (0,)), ((), ())), preferred_element_type=F32)


def _params(sem, vmem_mb):
    return pltpu.CompilerParams(dimension_semantics=sem, vmem_limit_bytes=vmem_mb << 20)


def _cond_row(tile, tm, n_ctx_tok, lat_len):
    start = tile * tm
    return jnp.where(start < n_ctx_tok, 0, 1 + (start - n_ctx_tok) // lat_len)


def _iota(shape, axis):
    return lax.broadcasted_iota(jnp.int32, shape, axis)


def _group_specs(tm, width, n_ctx_tiles):
    return (pl.BlockSpec((tm, width), lambda i, *_: (jnp.minimum(i, n_ctx_tiles - 1), 0)),
            pl.BlockSpec((tm, width), lambda i, *_: (jnp.maximum(i - n_ctx_tiles, 0), 0)))


def _mod_kernel(c_ref, w_ref, b_ref, o_ref):
    o_ref[...] = _dot_hi(jax.nn.silu(c_ref[...]), w_ref[...]) + b_ref[...]


def _modulation(cond, ada_w, ada_b):
    tn = 1536
    n = ada_w.shape[-1]
    return pl.pallas_call(
        _mod_kernel,
        grid=(DEPTH, n // tn),
        in_specs=[pl.BlockSpec((N_COND, D_MODEL), lambda l, j: (0, 0)),
                  pl.BlockSpec((None, D_MODEL, tn), lambda l, j: (l, 0, j)),
                  pl.BlockSpec((None, 1, tn), lambda l, j: (l, 0, j))],
        out_specs=pl.BlockSpec((None, N_COND, tn), lambda l, j: (l, 0, j)),
        out_shape=jax.ShapeDtypeStruct((DEPTH, N_COND, n), F32),
        compiler_params=_params(("parallel", "parallel"), 40),
        name="modulation",
    )(cond, ada_w, ada_b.reshape(DEPTH, 1, n))


def _inproj_kernel(*refs, n_ctx_tiles, split):
    if split:
        xc_ref, xs_ref = refs[:2]
    x_ref, mod_ref, w_ref, b_ref, wg_ref, bg_ref, o_ref, u4_ref, g_ref, h_scr, u_scr = refs[1 if split else 0:]
    j = pl.program_id(1)
    tm = x_ref.shape[0]

    @pl.when(j == 0)
    def _():
        x = jnp.where(pl.program_id(0) < n_ctx_tiles, xc_ref[...], xs_ref[...]) if split else x_ref[...]
        h = (x * (1.0 + mod_ref[1:2, :]) + mod_ref[0:1, :]).astype(BF16)
        h_scr[...] = h
        g_ref[...] = _dot(h, wg_ref[...]) + bg_ref[...]
        u = _dot(h, w_ref[...]) + b_ref[...]
        for q in range(S5_Q):
            u_scr[q] = u[:, q * LANE:(q + 1) * LANE]
            for t in range(SUB):
                u4_ref[q, :, t * LANE:(t + 1) * LANE] = u_scr[q, pl.ds(t, tm // SUB, stride=SUB), :].astype(BF16)

    @pl.when(j > 0)
    def _():
        o_ref[...] = (_dot(h_scr[...], w_ref[...]) + b_ref[...]).astype(o_ref.dtype)


def _inproj(x, mod_l, w, b, wg, bg, cfg):
    split = isinstance(x, tuple)
    xs = list(x) if split else [x]
    t_tok = sum(a.shape[0] for a in xs)
    tm, tn = cfg["tm_in"] // (2 if split else 1), 512
    nj = w.shape[1] // tn
    n_ctx_tiles = cfg["n_ctx_tok"] // tm
    cond = functools.partial(_cond_row, tm=tm, n_ctx_tok=cfg["n_ctx_tok"], lat_len=cfg["lat_len"])
    x_specs = list(_group_specs(tm, D_MODEL, n_ctx_tiles)) if split else [pl.BlockSpec((tm, D_MODEL), lambda i, j: (i, 0))]
    return pl.pallas_call(
        functools.partial(_inproj_kernel, n_ctx_tiles=n_ctx_tiles, split=split),
        grid=(t_tok // tm, nj),
        in_specs=x_specs + [
                  pl.BlockSpec((None, 6, D_MODEL), lambda i, j: (cond(i), 0, 0)),
                  pl.BlockSpec((D_MODEL, tn), lambda i, j: (0, j)),
                  pl.BlockSpec((1, tn), lambda i, j: (0, j)),
                  pl.BlockSpec((D_MODEL, LANE), lambda i, j: (0, 0)),
                  pl.BlockSpec((1, LANE), lambda i, j: (0, 0))],
        out_specs=[pl.BlockSpec((tm, tn), lambda i, j: (i, jnp.maximum(j - 1, 0))),
                   pl.BlockSpec((S5_Q, tm // SUB, S5_FLAT), lambda i, j: (0, i, 0)),
                   pl.BlockSpec((tm, LANE), lambda i, j: (i, 0))],
        out_shape=[jax.ShapeDtypeStruct((t_tok, (nj - 1) * tn), BF16),
                   jax.ShapeDtypeStruct((S5_Q, t_tok // SUB, S5_FLAT), BF16),
                   jax.ShapeDtypeStruct((t_tok, LANE), F32)],
        scratch_shapes=[pltpu.VMEM((tm, D_MODEL), BF16), pltpu.VMEM((S5_Q, tm, LANE), F32)],
        compiler_params=_params(("parallel", "arbitrary"), 48),
        name="inproj",
    )(*xs, mod_l, w, b, wg, bg)


def _s5_factors(lam_re, lam_im, log_step, b_re, b_im, c_re, c_im):
    lam = lax.complex(lam_re.astype(F32), lam_im.astype(F32))
    lam_dt = lam * jnp.exp(log_step.astype(F32))[..., None]
    lam_bar = jnp.exp(lam_dt)
    bbar = ((lam_bar - 1.0) / lam)[..., None] * lax.complex(b_re.astype(F32), b_im.astype(F32))
    cmat = lax.complex(c_re.astype(F32), c_im.astype(F32))
    ks = jnp.arange(SUB + 1, dtype=F32)
    pw = jnp.exp(lam_dt[None] * ks[:, None, None, None])
    kern = jnp.einsum('dgcp,tdgp,dgpe->dgtce', cmat, pw[:SUB], bbar).real
    pad = jnp.zeros_like(kern[0][:, :SUB - 1])
    ktab = jnp.concatenate([pad, kern[0]], 1) + jnp.concatenate([kern[1][:, ::-1], pad], 1)
    k_lag = ktab.reshape(S5_Q, S5_QG, N_LAG, S5_GROUP, S5_GROUP).transpose(0, 2, 3, 1, 4)
    k_lag = k_lag.reshape(S5_Q, N_LAG, S5_GROUP, LANE)
    pw_in = jnp.stack([pw[:SUB][::-1, 0], pw[:SUB][:, 1]], 0)
    wb = pw_in[..., None] * bbar[:, None]
    wb = jnp.stack([wb.real, wb.imag], 1).reshape(2, 2, SUB, S5_Q, S5_QG, S5_STATE, S5_GROUP)
    a_in = wb.transpose(3, 2, 0, 1, 5, 4, 6).reshape(S5_Q, SUB, 4, S5_STATE, LANE)
    pw_out = jnp.stack([pw[1:, 0], pw[1:][::-1, 1]], 0)
    ce = cmat[:, None] * pw_out[:, :, :, None, :]
    ce = jnp.stack([ce.real, -ce.imag], 1).reshape(2, 2, SUB, S5_Q, S5_QG, S5_GROUP, S5_STATE)
    b_out = ce.transpose(3, 0, 1, 2, 5, 4, 6).reshape(S5_Q, 4, SUB, S5_GROUP, S5_HALF)
    a = pw[SUB]
    a = jnp.stack([a.real, a.imag], 1).reshape(2, 2, S5_Q, 1, S5_HALF).transpose(2, 0, 1, 3, 4)
    return a_in.astype(BF16), k_lag.astype(BF16), b_out.astype(BF16), a


def _expand(src_t, n_rep, row_shift, col_shift):
    k, r = src_t.shape
    rep = jnp.where(_iota((k, n_rep * k), 0) == (_iota((k, n_rep * k), 1) & (k - 1)), 1.0, 0.0).astype(BF16)
    same = (_iota((r, n_rep * k), 0) >> row_shift) == (_iota((r, n_rep * k), 1) >> col_shift)
    return jnp.where(same, _dot_tn(src_t, rep), 0.0).astype(BF16)


def _s5a_kernel(u_ref, a_ref, o_ref, w_scr):
    @pl.when(pl.program_id(1) == 0)
    def _():
        for t in range(SUB):
            for k in range(4):
                w_scr[t * LANE:(t + 1) * LANE, k * S5_HALF:(k + 1) * S5_HALF] = _expand(a_ref[t, k], S5_QG, 4, 6)

    res = _dot(u_ref[...], w_scr[...])
    for d in range(2):
        for r in range(2):
            k = 2 * d + r
            o_ref[d, r] = res[:, k * S5_HALF:(k + 1) * S5_HALF]


def _s5_state_in(u4, a_in, cfg):
    rows = u4.shape[1]
    rt = cfg["s5_rows"]
    return pl.pallas_call(
        _s5a_kernel,
        grid=(S5_Q, rows // rt),
        in_specs=[pl.BlockSpec((None, rt, S5_FLAT), lambda q, i: (q, i, 0)),
                  pl.BlockSpec((None, SUB, 4, S5_STATE, LANE), lambda q, i: (q, 0, 0, 0, 0))],
        out_specs=pl.BlockSpec((None, 2, 2, rt, S5_HALF), lambda q, i: (q, 0, 0, i, 0)),
        out_shape=jax.ShapeDtypeStruct((S5_Q, 2, 2, rows, S5_HALF), F32),
        scratch_shapes=[pltpu.VMEM((S5_FLAT, S5_ST), BF16)],
        compiler_params=_params(("parallel", "arbitrary"), 48),
        name="s5_state_in",
    )(u4, a_in)


def _s5b_kernel(loc_ref, a_ref, x0_ref, xp_ref, fin_ref, *, n_ctx_seq, ctx_sub, n_lat_seq, lat_sub):
    d = pl.program_id(1)
    ar = jnp.broadcast_to(a_ref[0], (8, LANE))
    ai = jnp.broadcast_to(a_ref[1], (8, LANE))

    def run(base, nsub, xr0, xi0):
        def body(jj, carry):
            xr, xi = carry
            j = jnp.where(d == 0, jj, nsub - 1 - jj)
            idx = pl.ds(base + j, 8, stride=nsub)
            xp_ref[0, idx, :] = xr
            xp_ref[1, idx, :] = xi
            lr = loc_ref[0, idx, :]
            li = loc_ref[1, idx, :]
            return ar * xr - ai * xi + lr, ar * xi + ai * xr + li
        return lax.fori_loop(0, nsub, body, (xr0, xi0))

    zero = jnp.zeros((8, LANE), F32)
    for bg in range(n_ctx_seq // 8):
        xr, xi = run(bg * 8 * ctx_sub, ctx_sub, zero, zero)
        fin_ref[0, bg * 8:(bg + 1) * 8, :] = xr
        fin_ref[1, bg * 8:(bg + 1) * 8, :] = xi
    for bg in range(n_lat_seq // 8):
        run(n_ctx_seq * ctx_sub + bg * 8 * lat_sub, lat_sub, x0_ref[0, bg * 8:(bg + 1) * 8, :],
            x0_ref[1, bg * 8:(bg + 1) * 8, :])


def _s5_scan(loc, a, x0, cfg):
    rows = loc.shape[3]
    n_ctx_seq, n_lat_seq = cfg["n_ctx_seq"], cfg["n_lat_seq"]
    kern = functools.partial(_s5b_kernel, n_ctx_seq=n_ctx_seq, ctx_sub=cfg["ctx_len"] // SUB,
                             n_lat_seq=n_lat_seq, lat_sub=cfg["lat_len"] // SUB)
    nlb = S5_HALF // LANE
    return pl.pallas_call(
        kern,
        grid=(S5_Q, 2, nlb),
        in_specs=[pl.BlockSpec((None, None, 2, rows, LANE), lambda q, d, b: (q, d, 0, 0, b)),
                  pl.BlockSpec((None, None, 2, 1, LANE), lambda q, d, b: (q, d, 0, 0, b)),
                  pl.BlockSpec((None, None, 2, n_lat_seq, LANE), lambda q, d, b: (q, d, 0, 0, b))],
        out_specs=[pl.BlockSpec((None, None, 2, rows, LANE), lambda q, d, b: (q, d, 0, 0, b)),
                   pl.BlockSpec((None, None, 2, n_ctx_seq, LANE), lambda q, d, b: (q, d, 0, 0, b))],
        out_shape=[jax.ShapeDtypeStruct(loc.shape, F32),
                   jax.ShapeDtypeStruct((S5_Q, 2, 2, n_ctx_seq, S5_HALF), F32)],
        compiler_params=_params(("parallel", "parallel", "parallel"), 48),
        name="s5_scan",
    )(loc, a, x0)


def _s5c_kernel(u_ref, xp_ref, k_ref, b_ref, d_ref, z_ref, m_scr, wo_scr, bd_scr, z_scr):
    rt = u_ref.shape[0]

    @pl.when(pl.program_id(1) == 0)
    def _():
        for l in range(N_LAG):
            bd_scr[l] = _expand(k_ref[l], S5_QG, 4, 4)
        for t in range(SUB):
            for s in range(SUB):
                m_scr[t * LANE:(t + 1) * LANE, s * LANE:(s + 1) * LANE] = bd_scr[s - t + SUB - 1]
        for k in range(4):
            for t in range(SUB):
                wo_scr[k * S5_HALF:(k + 1) * S5_HALF, t * LANE:(t + 1) * LANE] = _expand(b_ref[k, t], S5_QG, 6, 4)

    u = u_ref[...]
    xcat = jnp.concatenate([xp_ref[0, 0], xp_ref[0, 1], xp_ref[1, 0], xp_ref[1, 1]], axis=1).astype(BF16)
    y = _dot(u, m_scr[...]) + _dot(xcat, wo_scr[...])
    z = jax.nn.gelu(d_ref[...] * u.astype(F32) + y)
    for t in range(SUB):
        z_scr[pl.ds(t, rt, stride=SUB), :] = z[:, t * LANE:(t + 1) * LANE]
    z_ref[...] = z_scr[...].astype(z_ref.dtype)


def _s5_output(u4, xprev, k_lag, b_out, d4, cfg):
    rows = u4.shape[1]
    rt = cfg["s5_rows"]
    return pl.pallas_call(
        _s5c_kernel,
        grid=(S5_Q, rows // rt),
        in_specs=[pl.BlockSpec((None, rt, S5_FLAT), lambda q, i: (q, i, 0)),
                  pl.BlockSpec((None, 2, 2, rt, S5_HALF), lambda q, i: (q, 0, 0, i, 0)),
                  pl.BlockSpec((None, N_LAG, S5_GROUP, LANE), lambda q, i: (q, 0, 0, 0)),
                  pl.BlockSpec((None, 4, SUB, S5_GROUP, S5_HALF), lambda q, i: (q, 0, 0, 0, 0)),
                  pl.BlockSpec((None, 1, S5_FLAT), lambda q, i: (q, 0, 0))],
        out_specs=pl.BlockSpec((rt * SUB, LANE), lambda q, i: (i, q)),
        out_shape=jax.ShapeDtypeStruct((rows * SUB, S5_WIDTH), BF16),
        scratch_shapes=[pltpu.VMEM((S5_FLAT, S5_FLAT), BF16), pltpu.VMEM((S5_ST, S5_FLAT), BF16),
                        pltpu.VMEM((N_LAG, LANE, LANE), BF16), pltpu.VMEM((rt * SUB, LANE), F32)],
        compiler_params=_params(("parallel", "arbitrary"), 56),
        name="s5_output",
    )(u4, xprev, k_lag, b_out, d4)


def _group_norm(o, g, b):
    mu = jnp.mean(o, axis=-1, keepdims=True)
    var = jnp.mean(jnp.square(o - mu), axis=-1, keepdims=True)
    return (o - mu) * lax.rsqrt(var + GN_EPS) * g + b


def _ret_kernel(dec_ref, q_ref, k_ref, v_ref, g_ref, cos_ref, sin_ref, s0_ref, gg_ref, gb_ref, y_ref, *rest,
                nc, use_rot, has_init, want_final):
    if want_final:
        sfin_ref, sf_scr, sb_scr, x_scr, kr_scr = rest
    else:
        sf_scr, sb_scr, x_scr, kr_scr = rest
    h = pl.program_id(1)
    row = _iota((CHUNK, CHUNK), 0).astype(F32)
    col = _iota((CHUNK, CHUNK), 1).astype(F32)
    lg_f = -jnp.exp(jnp.full((CHUNK, CHUNK), dec_ref[0, h], F32))
    lg_b = -jnp.exp(jnp.full((CHUNK, CHUNK), dec_ref[1, h], F32))
    lag = row - col
    scale = HEAD_DIM ** -0.5
    dmat = (jnp.where(lag >= 0, jnp.exp(lg_f * jnp.maximum(lag, 0.0)), 0.0)
            + jnp.where(lag <= 0, jnp.exp(lg_b * jnp.maximum(-lag, 0.0)), 0.0)) * scale
    qd_f = jnp.exp(lg_f * (row + 1.0))
    qd_b = jnp.exp(lg_b * (CHUNK - row))
    kd_f = jnp.exp(lg_f * (CHUNK - 1.0 - col)) * scale
    kd_b = jnp.exp(lg_b * col) * scale
    cd_f = jnp.exp(lg_f * CHUNK)
    cd_b = jnp.exp(lg_b * CHUNK)

    def chunk(j):
        return pl.ds(pl.multiple_of(j * CHUNK, CHUNK), CHUNK)

    def rot(ref, sl):
        x = ref[sl, :].astype(F32)
        if not use_rot:
            return x
        return x * cos_ref[sl, :] + pltpu.roll(x, HEAD_DIM // 2, 1) * sin_ref[sl, :]

    def local(j, carry):
        sl = chunk(j)
        k = rot(k_ref, sl)
        kr_scr[sl, :] = k.astype(BF16)
        k_t = k.T
        v = v_ref[sl, :]
        x_scr[0, j] = _dot((k_t * kd_f).astype(BF16), v)
        x_scr[1, j] = _dot((k_t * kd_b).astype(BF16), v)
        return carry

    lax.fori_loop(0, nc, local, 0, unroll=min(4, nc))
    zero = jnp.zeros((CHUNK, CHUNK), F32)

    def states(jj, carry):
        s_f, s_b = carry
        jf, jb = jj, nc - 1 - jj
        sf_scr[jf] = s_f.astype(BF16)
        sb_scr[jb] = s_b.astype(BF16)
        return s_f * cd_f + x_scr[0, jf], s_b * cd_b + x_scr[1, jb]

    init = (s0_ref[0], s0_ref[1]) if has_init else (zero, zero)
    s_f, s_b = lax.fori_loop(0, nc, states, init, unroll=2)
    if want_final:
        sfin_ref[0] = s_f
        sfin_ref[1] = s_b

    def outputs(j, carry):
        sl = chunk(j)
        q = rot(q_ref, sl)
        att = _dot_nt(q.astype(BF16), kr_scr[sl, :]) * dmat
        lhs = jnp.concatenate([att.astype(BF16), (q * qd_f).astype(BF16), (q * qd_b).astype(BF16)], axis=1)
        rhs = jnp.concatenate([v_ref[sl, :], sf_scr[j], sb_scr[j]], axis=0)
        o = _dot(lhs, rhs)
        y = jax.nn.silu(g_ref[sl, :].astype(F32)) * _group_norm(o, gg_ref[...], gb_ref[...])
        y_ref[sl, :] = y.astype(y_ref.dtype)
        return carry

    lax.fori_loop(0, nc, outputs, 0, unroll=min(8, nc))


def _retention(proj, dec, cos_t, sin_t, s0, gn_g, gn_b, *, n_seq, seq_len, row0, use_rot, has_init, want_final):
    nc = seq_len // CHUNK
    blk0 = row0 // seq_len
    kern = functools.partial(_ret_kernel, nc=nc, use_rot=use_rot, has_init=has_init, want_final=want_final)

    def tok(cb):
        return pl.BlockSpec((seq_len, HEAD_DIM), lambda s, h, cb=cb: (blk0 + s, cb + h))

    rot_spec = pl.BlockSpec((seq_len, HEAD_DIM), lambda s, h: (0, 0))
    st_spec = pl.BlockSpec((None, 2, None, HEAD_DIM, HEAD_DIM), lambda s, h: (s, 0, h, 0, 0))
    gn_spec = pl.BlockSpec((1, HEAD_DIM), lambda s, h: (0, h))
    out_specs = [pl.BlockSpec((seq_len, HEAD_DIM), lambda s, h: (s, h))]
    out_shape = [jax.ShapeDtypeStruct((n_seq * seq_len, WIDTH), BF16)]
    if want_final:
        out_specs.append(st_spec)
        out_shape.append(jax.ShapeDtypeStruct((n_seq, 2, HEADS, HEAD_DIM, HEAD_DIM), F32))
    return pl.pallas_call(
        kern,
        grid=(n_seq, HEADS),
        in_specs=[pl.BlockSpec(memory_space=pltpu.SMEM), tok(0), tok(4), tok(8), tok(12),
                  rot_spec, rot_spec, st_spec, gn_spec, gn_spec],
        out_specs=out_specs,
        out_shape=out_shape,
        scratch_shapes=[pltpu.VMEM((nc, HEAD_DIM, HEAD_DIM), BF16), pltpu.VMEM((nc, HEAD_DIM, HEAD_DIM), BF16),
                        pltpu.VMEM((2, nc, HEAD_DIM, HEAD_DIM), F32), pltpu.VMEM((seq_len, HEAD_DIM), BF16)],
        compiler_params=_params(("parallel", "parallel"), 48),
        name="retention",
    )(dec, proj, proj, proj, proj, cos_t, sin_t, s0, gn_g, gn_b)


def _gate_prep_kernel(g_ref, col_ref, row_ref):
    lane = _iota((CHUNK, LANE), 1)
    tri = jnp.where(_iota((CHUNK, CHUNK), 0) >= _iota((CHUNK, CHUNK), 1), 1.0, 0.0)
    for c in range(g_ref.shape[0] // CHUNK):
        sl = slice(c * CHUNK, (c + 1) * CHUNK)
        g = g_ref[sl, :]
        lf = jnp.where(lane < 4 * HEADS, jnp.minimum(g, 0.0) - jnp.log1p(jnp.exp(-jnp.abs(g))), 0.0)
        cs = _dot_hi(tri, lf)
        tot = cs[CHUNK - 1:CHUNK, :]
        bc = jnp.where(lane < 2 * HEADS, cs, tot - cs + lf)
        rest = jnp.where(lane < 2 * HEADS, tot - cs, cs - lf)
        pack = g + pltpu.roll(bc, GATE_BC, 1) + pltpu.roll(rest, GATE_REST, 1)
        col_ref[sl, :] = pack
        row_ref[sl, :] = pack.T


def _gate_prep(gates, cfg):
    t_tok = gates.shape[0]
    tm = cfg["tm_in"]
    spec = pl.BlockSpec((tm, LANE), lambda i: (i, 0))
    return pl.pallas_call(
        _gate_prep_kernel,
        grid=(t_tok // tm,),
        in_specs=[spec],
        out_specs=[spec, spec],
        out_shape=[jax.ShapeDtypeStruct((t_tok, LANE), F32)] * 2,
        compiler_params=_params(("parallel",), 32),
        name="gate_prep",
    )(gates)


def _mlstm_kernel(q_ref, k_ref, v_ref, o_ref, col_ref, row_ref, c0_ref, n0_ref, m0_ref, gg_ref, gb_ref, y_ref, *rest,
                  nc, has_init, want_final):
    if want_final:
        cfin_ref, nfin_ref, mfin_ref, c_scr, n_scr, m_scr, x_scr, nl_scr, ml_scr, bl_scr = rest
    else:
        c_scr, n_scr, m_scr, x_scr, nl_scr, ml_scr, bl_scr = rest
    src = _iota((CHUNK, CHUNK), 0)
    dst = _iota((CHUNK, CHUNK), 1)
    scale = HEAD_DIM ** -0.5

    def chunk(j):
        return pl.ds(pl.multiple_of(j * CHUNK, CHUNK), CHUNK)

    def gate_idx(d, h):
        return d * 2 * HEADS + h, GATE_BC + d * 2 * HEADS + HEADS + h, GATE_REST + d * 2 * HEADS + HEADS + h

    def head_body(h):
        def init(d):
            if has_init:
                return c0_ref[d], n0_ref[d], m0_ref[d]
            return (jnp.zeros((HEAD_DIM, HEAD_DIM), F32), jnp.zeros((1, HEAD_DIM), F32),
                    jnp.zeros((1, HEAD_DIM), F32))

        def local(j, carry):
            sl = chunk(j)
            k = k_ref[sl, :]
            v_t = v_ref[sl, :].astype(F32).T
            rp = row_ref[sl, :]
            for d in range(2):
                ii, bi, ri = gate_idx(d, h)
                bc_row = rp[bi:bi + 1, :]
                b_last = bc_row[:, CHUNK - 1:CHUNK] if d == 0 else bc_row[:, 0:1]
                log_k = rp[ri:ri + 1, :] + rp[ii:ii + 1, :]
                m_loc = jnp.max(log_k, axis=1, keepdims=True)
                kw = jnp.exp(log_k - m_loc)
                x_scr[d, j] = _dot((v_t * kw).astype(BF16), k) * scale
                kw_hi = kw.astype(BF16).astype(F32)
                kw2 = jnp.concatenate([jnp.broadcast_to(kw_hi, (8, CHUNK)), jnp.broadcast_to(kw - kw_hi, (8, CHUNK))], 0)
                nl = _dot(kw2.astype(BF16), k)
                nl_scr[d, j] = (nl[0:1, :] + nl[8:9, :]) * scale
                ml_scr[d, j] = jnp.broadcast_to(m_loc, (1, HEAD_DIM))
                bl_scr[d, j] = jnp.broadcast_to(b_last, (1, HEAD_DIM))
            return carry

        lax.fori_loop(0, nc, local, 0, unroll=min(4, nc))

        def states(jj, carry):
            out = []
            for d, j in ((0, jj), (1, nc - 1 - jj)):
                cmat, nvec, m = carry[d]
                c_scr[d, j] = cmat.astype(BF16)
                n_scr[d, j] = nvec
                m_scr[d, j] = m
                m_new = jnp.maximum(bl_scr[d, j] + m, ml_scr[d, j])
                keep = jnp.exp(bl_scr[d, j] + m - m_new)
                add = jnp.exp(ml_scr[d, j] - m_new)
                out.append((keep * cmat + add * x_scr[d, j], keep * nvec + add * nl_scr[d, j], m_new))
            return tuple(out)

        fin = lax.fori_loop(0, nc, states, (init(0), init(1)), unroll=2)
        if want_final:
            for d in range(2):
                cfin_ref[d] = fin[d][0]
                nfin_ref[d] = fin[d][1]
                mfin_ref[d] = fin[d][2]

        def outputs(j, carry):
            sl = chunk(j)
            q = q_ref[sl, :]
            v = v_ref[sl, :]
            s_t = _dot_nt(k_ref[sl, :], q) * scale
            cp = col_ref[sl, :]
            rp = row_ref[sl, :]
            h_t = None
            for d in range(2):
                ii, bi, _ = gate_idx(d, h)
                causal = (src <= dst) if d == 0 else (src >= dst)
                bc_row = rp[bi:bi + 1, :]
                log_d = jnp.where(causal, bc_row + (cp[:, ii:ii + 1] - cp[:, bi:bi + 1]), -jnp.inf)
                log_prev = bc_row + m_scr[d, j]
                m_t = jnp.maximum(log_prev, jnp.max(log_d, axis=0, keepdims=True))
                w = s_t * jnp.exp(log_d - m_t)
                w_prev = jnp.exp(log_prev - m_t)
                qn = _dot_nt(jnp.broadcast_to(n_scr[d, j], (16, HEAD_DIM)).astype(BF16), q)[0:1, :]
                den = jnp.sum(w, axis=0, keepdims=True) + w_prev * qn
                inv = 1.0 / jnp.maximum(jnp.abs(den), jnp.exp(-m_t))
                num = _dot_tn(v, w.astype(BF16)) + _dot_nt(c_scr[d, j], q) * w_prev
                h_t = num * inv if h_t is None else h_t + num * inv
            y = jax.nn.sigmoid(o_ref[sl, :].astype(F32).T) * h_t
            mu = jnp.mean(y, axis=0, keepdims=True)
            var = jnp.mean(jnp.square(y - mu), axis=0, keepdims=True)
            y = (y - mu) * lax.rsqrt(var + GN_EPS) * gg_ref[...] + gb_ref[...]
            y_ref[sl, :] = y.T.astype(y_ref.dtype)
            return carry

        lax.fori_loop(0, nc, outputs, 0, unroll=min(4, nc))

    hh = pl.program_id(1)
    for h in range(HEADS):
        pl.when(hh == h)(functools.partial(head_body, h))


def _mlstm(proj, gcol, grow, c0, n0, m0, gn_g, gn_b, *, n_seq, seq_len, row0, has_init, want_final):
    nc = seq_len // CHUNK
    blk0 = row0 // seq_len
    kern = functools.partial(_mlstm_kernel, nc=nc, has_init=has_init, want_final=want_final)

    def tok(cb):
        return pl.BlockSpec((seq_len, HEAD_DIM), lambda s, h, cb=cb: (blk0 + s, cb + h))

    gate_spec = pl.BlockSpec((seq_len, LANE), lambda s, h: (blk0 + s, 0))
    c_spec = pl.BlockSpec((None, 2, None, HEAD_DIM, HEAD_DIM), lambda s, h: (s, 0, h, 0, 0))
    v_spec = pl.BlockSpec((None, 2, None, 1, HEAD_DIM), lambda s, h: (s, 0, h, 0, 0))
    gn_spec = pl.BlockSpec((HEAD_DIM, LANE), lambda s, h: (h, 0))
    out_specs = [pl.BlockSpec((seq_len, HEAD_DIM), lambda s, h: (s, h))]
    out_shape = [jax.ShapeDtypeStruct((n_seq * seq_len, WIDTH), BF16)]
    if want_final:
        out_specs += [c_spec, v_spec, v_spec]
        out_shape += [jax.ShapeDtypeStruct((n_seq, 2, HEADS, HEAD_DIM, HEAD_DIM), F32),
                      jax.ShapeDtypeStruct((n_seq, 2, HEADS, 1, HEAD_DIM), F32),
                      jax.ShapeDtypeStruct((n_seq, 2, HEADS, 1, HEAD_DIM), F32)]
    return pl.pallas_call(
        kern,
        grid=(n_seq, HEADS),
        in_specs=[tok(16), tok(20), tok(24), tok(28), gate_spec, gate_spec,
                  c_spec, v_spec, v_spec, gn_spec, gn_spec],
        out_specs=out_specs,
        out_shape=out_shape,
        scratch_shapes=[pltpu.VMEM((2, nc, HEAD_DIM, HEAD_DIM), BF16)] + [pltpu.VMEM((2, nc, 1, HEAD_DIM), F32)] * 2
        + [pltpu.VMEM((2, nc, HEAD_DIM, HEAD_DIM), F32)] + [pltpu.VMEM((2, nc, 1, HEAD_DIM), F32)] * 3,
        compiler_params=_params(("parallel", "parallel"), 48),
        name="mlstm",
    )(proj, proj, proj, proj, gcol, grow, c0, n0, m0, gn_g, gn_b)


def _layer_norm(x, g, b):
    mu = jnp.mean(x, axis=-1, keepdims=True)
    var = jnp.mean(jnp.square(x - mu), axis=-1, keepdims=True)
    return (x - mu) * lax.rsqrt(var + LN_EPS) * g + b


def _top2_gates(h, whi_ref, wlo_ref, b_ref):
    lane = _iota((h.shape[0], LANE), 1)
    h_hi = h.astype(BF16)
    h_lo = (h - h_hi.astype(F32)).astype(BF16)
    logits = _dot(h_hi, whi_ref[...]) + _dot(h_hi, wlo_ref[...]) + _dot(h_lo, whi_ref[...]) + b_ref[...]
    logits = jnp.where(lane < N_EXPERTS, logits, NEG_BIG)
    m1 = jnp.max(logits, axis=1, keepdims=True)
    i1 = jnp.min(jnp.where(logits == m1, lane, LANE), axis=1, keepdims=True)
    rest = jnp.where(lane == i1, NEG_BIG, logits)
    m2 = jnp.max(rest, axis=1, keepdims=True)
    i2 = jnp.min(jnp.where(rest == m2, lane, LANE), axis=1, keepdims=True)
    e2 = jnp.exp(m2 - m1)
    den = 1.0 + e2
    return jnp.where(lane == i1, 1.0 / den, 0.0) + jnp.where(lane == i2, e2 / den, 0.0)


def _merge_kernel(*refs, n_ctx_tiles, with_router, split):
    if split:
        xc_ref, xs_ref = refs[:2]
    (x_ref, mod_ref, z_ref, ybc_ref, ybs_ref, ycc_ref, ycs_ref, wm_ref, bm_ref, wglu_ref, bglu_ref,
     wa_ref, wb_ref, wc_ref, wo_ref, lg_ref, lb_ref, *rest) = refs[1 if split else 0:]
    if with_router:
        rwh_ref, rwl_ref, rb_ref, o_ref, g_ref = rest
    else:
        o_ref, = rest
    is_ctx = pl.program_id(0) < n_ctx_tiles
    x = jnp.where(is_ctx, xc_ref[...], xs_ref[...]) if split else x_ref[...]
    h = (x * (1.0 + mod_ref[1:2, :]) + mod_ref[0:1, :]).astype(BF16)
    z = z_ref[...]
    ya = (z.astype(F32) * jax.nn.sigmoid(_dot(z, wglu_ref[...]) + bglu_ref[...])).astype(BF16)
    yb = jnp.where(is_ctx, ybc_ref[...], ybs_ref[...])
    yc = jnp.where(is_ctx, ycc_ref[...], ycs_ref[...])
    merged = None
    for j, (y, w_ref) in enumerate(((ya, wa_ref), (yb, wb_ref), (yc, wc_ref))):
        gate = jax.nn.sigmoid(_dot(h, wm_ref[:, j * D_MODEL:(j + 1) * D_MODEL]) + bm_ref[:, j * D_MODEL:(j + 1) * D_MODEL])
        term = gate * _dot(y, w_ref[...])
        merged = term if merged is None else merged + term
    mix = _dot(merged.astype(BF16), wo_ref[...])
    x1 = _layer_norm(ALPHA * x + mod_ref[2:3, :] * mix, lg_ref[...], lb_ref[...])
    o_ref[...] = x1
    if with_router:
        g_ref[...] = _top2_gates(x1 * (1.0 + mod_ref[4:5, :]) + mod_ref[3:4, :], rwh_ref, rwl_ref, rb_ref)


def _merge(x, mod_l, z, yb_c, yb_s, yc_c, yc_s, wm, bm, wglu, bglu, wa, wb, wc, wo, lg, lb, router, cfg):
    split = isinstance(x, tuple)
    xs = list(x) if split else [x]
    t_tok = sum(a.shape[0] for a in xs)
    tm = cfg["tm_merge"]
    n_ctx_tiles = cfg["n_ctx_tok"] // tm
    cond = functools.partial(_cond_row, tm=tm, n_ctx_tok=cfg["n_ctx_tok"], lat_len=cfg["lat_len"])

    def full(shape):
        return pl.BlockSpec(shape, lambda i: (0,) * len(shape))

    def tok(w):
        return pl.BlockSpec((tm, w), lambda i: (i, 0))

    ctx_spec, lat_spec = _group_specs(tm, WIDTH, n_ctx_tiles)
    x_specs = list(_group_specs(tm, D_MODEL, n_ctx_tiles)) if split else [tok(D_MODEL)]
    in_specs = x_specs + [pl.BlockSpec((None, 6, D_MODEL), lambda i: (cond(i), 0, 0)),
                tok(WIDTH), ctx_spec, lat_spec, ctx_spec, lat_spec,
                full((D_MODEL, 3 * D_MODEL)), full((1, 3 * D_MODEL)), full((WIDTH, WIDTH)), full((1, WIDTH)),
                full((WIDTH, D_MODEL)), full((WIDTH, D_MODEL)), full((WIDTH, D_MODEL)),
                full((D_MODEL, D_MODEL)), full((1, D_MODEL)), full((1, D_MODEL))]
    args = xs + [mod_l, z, yb_c, yb_s, yc_c, yc_s, wm, bm, wglu, bglu, wa, wb, wc, wo, lg, lb]
    out_specs = [tok(D_MODEL)]
    out_shape = [jax.ShapeDtypeStruct((t_tok, D_MODEL), F32)]
    if router is not None:
        in_specs += [full((D_MODEL, LANE)), full((D_MODEL, LANE)), full((1, LANE))]
        args += list(router)
        out_specs.append(tok(LANE))
        out_shape.append(jax.ShapeDtypeStruct((t_tok, LANE), F32))
    return pl.pallas_call(
        functools.partial(_merge_kernel, n_ctx_tiles=n_ctx_tiles, with_router=router is not None, split=split),
        grid=(t_tok // tm,),
        in_specs=in_specs,
        out_specs=out_specs,
        out_shape=out_shape,
        compiler_params=_params(("parallel",), 56),
        name="merge",
    )(*args)


def _ffn_kernel(x_ref, mod_ref, w1_ref, w3_ref, w2_ref, lg_ref, lb_ref, o_ref, h_scr, acc_scr):
    f = pl.program_id(1)

    @pl.when(f == 0)
    def _():
        h_scr[...] = (x_ref[...] * (1.0 + mod_ref[4:5, :]) + mod_ref[3:4, :]).astype(BF16)
        acc_scr[...] = jnp.zeros_like(acc_scr)

    h = h_scr[...]
    act = (jax.nn.silu(_dot(h, w1_ref[...])) * _dot(h, w3_ref[...])).astype(BF16)
    acc_scr[...] += _dot(act, w2_ref[...])

    @pl.when(f == pl.num_programs(1) - 1)
    def _():
        o_ref[...] = _layer_norm(ALPHA * x_ref[...] + mod_ref[5:6, :] * acc_scr[...], lg_ref[...], lb_ref[...])


def _dense_ffn(x, mod_l, w1, w3, w2, lg, lb, cfg):
    t_tok = x.shape[0]
    tm, tf = cfg["tm_ffn"], cfg["tf"]
    cond = functools.partial(_cond_row, tm=tm, n_ctx_tok=cfg["n_ctx_tok"], lat_len=cfg["lat_len"])
    return pl.pallas_call(
        _ffn_kernel,
        grid=(t_tok // tm, D_FF // tf),
        in_specs=[pl.BlockSpec((tm, D_MODEL), lambda i, f: (i, 0)),
                  pl.BlockSpec((None, 6, D_MODEL), lambda i, f: (cond(i), 0, 0)),
                  pl.BlockSpec((D_MODEL, tf), lambda i, f: (0, f)),
                  pl.BlockSpec((D_MODEL, tf), lambda i, f: (0, f)),
                  pl.BlockSpec((tf, D_MODEL), lambda i, f: (f, 0)),
                  pl.BlockSpec((1, D_MODEL), lambda i, f: (0, 0)),
                  pl.BlockSpec((1, D_MODEL), lambda i, f: (0, 0))],
        out_specs=pl.BlockSpec((tm, D_MODEL), lambda i, f: (i, 0)),
        out_shape=jax.ShapeDtypeStruct((t_tok, D_MODEL), F32),
        scratch_shapes=[pltpu.VMEM((tm, D_MODEL), BF16), pltpu.VMEM((tm, D_MODEL), F32)],
        compiler_params=_params(("parallel", "arbitrary"), 56),
        name="dense_ffn",
    )(x, mod_l, w1, w3, w2, lg, lb)


def _moe_kernel(x_ref, mod_ref, gate_ref, w1_ref, w3_ref, w2_ref, lg_ref, lb_ref, oc_ref, os_ref,
                h_scr, acc_scr, hc_scr, ob_scr, sp_scr, gt_scr, cnt_smem, *, n_ctx_tiles):
    e = pl.program_id(1)
    f = pl.program_id(2)
    last_f = pl.num_programs(2) - 1
    tm = x_ref.shape[0]

    @pl.when((e == 0) & (f == 0))
    def _():
        h_scr[...] = (x_ref[...] * (1.0 + mod_ref[4:5, :]) + mod_ref[3:4, :]).astype(BF16)
        acc_scr[...] = jnp.zeros_like(acc_scr)
        g = gate_ref[...]
        sel = g > 0.0
        ones = jnp.where(sel, 1.0, 0.0)
        before = jnp.where(_iota((tm, tm), 0) > _iota((tm, tm), 1), 1.0, 0.0).astype(BF16)
        pos = _dot(before, ones.astype(BF16))
        spt = jnp.where(sel, pos, -1.0).T
        gt = g.T
        cnt = jnp.sum(ones, axis=0, keepdims=True)
        for ee in range(N_EXPERTS):
            sp_scr[ee] = spt[ee:ee + 1, :]
            gt_scr[ee] = gt[ee:ee + 1, :]
            cnt_smem[ee] = cnt[0, ee].astype(jnp.int32)

    nb = (cnt_smem[e] + (MOE_BLK - 1)) // MOE_BLK
    row = _iota((MOE_BLK, tm), 0)

    def onehot(b):
        return sp_scr[e] == (row + b * MOE_BLK).astype(F32)

    @pl.when(f == 0)
    def _():
        def gather(b, carry):
            p = jnp.where(onehot(b), 1.0, 0.0).astype(BF16)
            hc_scr[b] = _dot(p, h_scr[...]).astype(BF16)
            ob_scr[b] = jnp.zeros((MOE_BLK, D_MODEL), F32)
            return carry
        lax.fori_loop(0, nb, gather, 0)

        @pl.when(nb % 2 == 1)
        def _():
            ob_scr[nb] = jnp.zeros((MOE_BLK, D_MODEL), F32)

    def ffn(hc):
        act = (jax.nn.silu(_dot(hc, w1_ref[...])) * _dot(hc, w3_ref[...])).astype(BF16)
        return _dot(act, w2_ref[...])

    def ffn_pair(p, carry):
        two = pl.ds(2 * p, 2)
        ob_scr[two] += ffn(hc_scr[two].reshape(2 * MOE_BLK, D_MODEL)).reshape(2, MOE_BLK, D_MODEL)
        return carry
    lax.fori_loop(0, nb // 2, ffn_pair, 0)

    @pl.when(nb % 2 == 1)
    def _():
        ob_scr[nb - 1] += ffn(hc_scr[nb - 1])

    @pl.when(f == last_f)
    def _():
        row2 = _iota((2 * MOE_BLK, tm), 0)

        def scatter(p, carry):
            m = sp_scr[e] == (row2 + p * (2 * MOE_BLK)).astype(F32)
            gc = jnp.sum(jnp.where(m, gt_scr[e], 0.0), axis=1, keepdims=True)
            og = (ob_scr[pl.ds(2 * p, 2)].reshape(2 * MOE_BLK, D_MODEL) * gc).astype(BF16)
            acc_scr[...] += _dot_tn(jnp.where(m, 1.0, 0.0).astype(BF16), og)
            return carry
        lax.fori_loop(0, (nb + 1) // 2, scatter, 0)

    last = (e == pl.num_programs(1) - 1) & (f == last_f)
    is_ctx = pl.program_id(0) < n_ctx_tiles
    for o_ref, mine in ((oc_ref, is_ctx), (os_ref, jnp.logical_not(is_ctx))):
        @pl.when(last & mine)
        def _():
            o_ref[...] = _layer_norm(ALPHA * x_ref[...] + mod_ref[5:6, :] * acc_scr[...], lg_ref[...], lb_ref[...])


def _moe_ffn(x, mod_l, gates, w1, w3, w2, lg, lb, cfg):
    t_tok = x.shape[0]
    tm, tf = cfg["tm_moe"], cfg["tf"]
    n_ctx_tiles = cfg["n_ctx_tok"] // tm
    cond = functools.partial(_cond_row, tm=tm, n_ctx_tok=cfg["n_ctx_tok"], lat_len=cfg["lat_len"])
    return pl.pallas_call(
        functools.partial(_moe_kernel, n_ctx_tiles=n_ctx_tiles),
        grid=(t_tok // tm, N_EXPERTS, D_FF // tf),
        in_specs=[pl.BlockSpec((tm, D_MODEL), lambda i, e, f: (i, 0), pipeline_mode=pl.Buffered(1)),
                  pl.BlockSpec((None, 6, D_MODEL), lambda i, e, f: (cond(i), 0, 0)),
                  pl.BlockSpec((tm, LANE), lambda i, e, f: (i, 0), pipeline_mode=pl.Buffered(1)),
                  pl.BlockSpec((None, D_MODEL, tf), lambda i, e, f: (e, 0, f)),
                  pl.BlockSpec((None, D_MODEL, tf), lambda i, e, f: (e, 0, f)),
                  pl.BlockSpec((None, tf, D_MODEL), lambda i, e, f: (e, f, 0)),
                  pl.BlockSpec((1, D_MODEL), lambda i, e, f: (0, 0)),
                  pl.BlockSpec((1, D_MODEL), lambda i, e, f: (0, 0))],
        out_specs=list(_group_specs(tm, D_MODEL, n_ctx_tiles)),
        out_shape=[jax.ShapeDtypeStruct((cfg["n_ctx_tok"], D_MODEL), F32),
                   jax.ShapeDtypeStruct((t_tok - cfg["n_ctx_tok"], D_MODEL), F32)],
        scratch_shapes=[pltpu.VMEM((tm, D_MODEL), BF16), pltpu.VMEM((tm, D_MODEL), F32),
                        pltpu.VMEM((tm // MOE_BLK, MOE_BLK, D_MODEL), BF16),
                        pltpu.VMEM((tm // MOE_BLK, MOE_BLK, D_MODEL), F32),
                        pltpu.VMEM((N_EXPERTS, 1, tm), F32), pltpu.VMEM((N_EXPERTS, 1, tm), F32),
                        pltpu.SMEM((N_EXPERTS,), jnp.int32)],
        compiler_params=_params(("arbitrary", "arbitrary", "arbitrary"), 56),
        name="moe_ffn",
    )(x, mod_l, gates, w1, w3, w2, lg, lb)


def _rotary_tables(n_tok):
    rows = n_tok // GRID_W
    r = jnp.repeat(jnp.arange(rows, dtype=F32), GRID_W)
    col = jnp.tile(jnp.arange(GRID_W, dtype=F32), rows)
    n_freq = HEAD_DIM // 4
    inv = ROPE_BASE ** (-jnp.arange(n_freq, dtype=F32) / n_freq)
    ang = jnp.concatenate([r[:, None] * inv, col[:, None] * inv], -1)
    cos, sin = jnp.cos(ang), jnp.sin(ang)
    return jnp.concatenate([cos, cos], -1), jnp.concatenate([-sin, sin], -1)


def _row2(v):
    return v.reshape(1, -1).astype(F32)


def kernel(x_prompt, x_sample, cache_s5_re, cache_s5_im, cache_ret, cache_ml_c, cache_ml_n, cache_ml_m, c, c_ctx, ada_w, ada_b, w_in, b_in, s5_lam_re, s5_lam_im, s5_log_step, s5_b_re, s5_b_im, s5_c_re, s5_c_im, s5_d, s5_glu_w, s5_glu_b, ret_decay, ret_gn_g, ret_gn_b, ml_gn_g, ml_gn_b, w_a, w_b, w_c, w_o, ln1_g, ln1_b, ln2_g, ln2_b, ffn_w1, ffn_w3, ffn_w2, moe_router, moe_router_b, moe_w1, moe_w3, moe_w2):
    n_ctx_seq, ctx_len, _ = x_prompt.shape
    n_lat_seq, lat_len, _ = x_sample.shape
    n_ctx_tok = n_ctx_seq * ctx_len
    n_lat_tok = n_lat_seq * lat_len
    t_tok = n_ctx_tok + n_lat_tok
    assert n_lat_seq + 1 <= N_COND and n_ctx_seq % 8 == 0 and n_lat_seq % 8 == 0
    assert ctx_len % CHUNK == 0 and lat_len % CHUNK == 0 and n_ctx_tok % lat_len == 0
    cfg = dict(n_ctx_seq=n_ctx_seq, ctx_len=ctx_len, n_lat_seq=n_lat_seq, lat_len=lat_len, n_ctx_tok=n_ctx_tok,
               tm_in=min(2048, lat_len), tm_merge=min(512, lat_len), tm_ffn=min(512, lat_len), tm_moe=min(1024, lat_len), tf=1408,
               s5_rows=min(256, t_tok // SUB // 8))

    x = (x_prompt.reshape(n_ctx_tok, D_MODEL), x_sample.reshape(n_lat_tok, D_MODEL))
    cond = jnp.zeros((N_COND, D_MODEL), F32).at[0].set(c_ctx).at[1:1 + n_lat_seq].set(c)
    mod = _modulation(cond, ada_w, ada_b).reshape(DEPTH, N_COND, 6, D_MODEL)
    cos_t, sin_t = _rotary_tables(lat_len)

    n_main = S5_WIDTH + 8 * WIDTH
    gate_off = n_main
    merge_off = gate_off + 4 * HEADS
    s5_fac = jax.vmap(_s5_factors)(s5_lam_re, s5_lam_im, s5_log_step, s5_b_re, s5_b_im, s5_c_re, s5_c_im)

    st_s5, st_ret, st_c, st_n, st_m = [], [], [], [], []
    zero_ret = jnp.zeros((n_ctx_seq, 2, HEADS, HEAD_DIM, HEAD_DIM), F32)
    zero_vec = jnp.zeros((n_ctx_seq, 2, HEADS, 1, HEAD_DIM), F32)
    for l in range(DEPTH):
        mod_l = mod[l]
        w_main = w_in[l][:, :n_main].astype(BF16)
        b_main = _row2(b_in[l][:n_main])
        w_gate = jnp.zeros((D_MODEL, LANE), F32).at[:, :4 * HEADS].set(w_in[l][:, gate_off:merge_off]).astype(BF16)
        b_gate = jnp.zeros((1, LANE), F32).at[0, :4 * HEADS].set(b_in[l][gate_off:merge_off])
        proj, u4, gates = _inproj(x, mod_l, w_main, b_main, w_gate, b_gate, cfg)

        s5_ain, s5_klag, s5_bout, s5_a = (m[l] for m in s5_fac)
        loc = _s5_state_in(u4, s5_ain, cfg)
        x0 = jnp.stack([cache_s5_re[:, l], cache_s5_im[:, l]], 0)
        x0 = x0.reshape(2, n_lat_seq, 2, S5_Q, S5_HALF).transpose(3, 2, 0, 1, 4).astype(F32)
        xprev, s5_fin = _s5_scan(loc, s5_a, x0, cfg)
        d4 = jnp.tile(s5_d[l].astype(F32).reshape(S5_Q, 1, LANE), (1, SUB, 1)).reshape(S5_Q, 1, S5_FLAT)
        z = _s5_output(u4, xprev, s5_klag, s5_bout, d4, cfg)
        st_s5.append(s5_fin)

        gg, gb = _row2(ret_gn_g[l]), _row2(ret_gn_b[l])
        dec = ret_decay[l].astype(F32)
        yb_c, ret_fin = _retention(proj, dec, cos_t, sin_t, zero_ret, gg, gb, n_seq=n_ctx_seq, seq_len=ctx_len,
                                   row0=0, use_rot=False, has_init=False, want_final=True)
        yb_s, = _retention(proj, dec, cos_t, sin_t, cache_ret[:, l].astype(F32), gg, gb, n_seq=n_lat_seq,
                           seq_len=lat_len, row0=n_ctx_tok, use_rot=True, has_init=True, want_final=False)
        st_ret.append(ret_fin)

        gg = jnp.broadcast_to(ml_gn_g[l].astype(F32)[:, None], (WIDTH, LANE))
        gb = jnp.broadcast_to(ml_gn_b[l].astype(F32)[:, None], (WIDTH, LANE))
        gcol, grow = _gate_prep(gates, cfg)
        yc_c, c_fin, n_fin, m_fin = _mlstm(proj, gcol, grow, zero_ret, zero_vec, zero_vec, gg, gb, n_seq=n_ctx_seq,
                                           seq_len=ctx_len, row0=0, has_init=False, want_final=True)
        n0 = cache_ml_n[:, l].astype(F32)[:, :, :, None, :]
        m0 = jnp.broadcast_to(cache_ml_m[:, l].astype(F32)[:, :, :, None, None], n0.shape)
        yc_s, = _mlstm(proj, gcol, grow, cache_ml_c[:, l].astype(F32), n0, m0, gg, gb, n_seq=n_lat_seq,
                       seq_len=lat_len, row0=n_ctx_tok, has_init=True, want_final=False)
        st_c.append(c_fin)
        st_n.append(n_fin[:, :, :, 0, :])
        st_m.append(m_fin[:, :, :, 0, 0])

        j = l // 2
        router = None
        if l % 2 == 1:
            rw = jnp.zeros((D_MODEL, LANE), F32).at[:, :N_EXPERTS].set(moe_router[j])
            rw_hi = rw.astype(BF16)
            router = (rw_hi, (rw - rw_hi.astype(F32)).astype(BF16),
                      jnp.zeros((1, LANE), F32).at[0, :N_EXPERTS].set(moe_router_b[j]))
        merged = _merge(x, mod_l, z, yb_c, yb_s, yc_c, yc_s, w_in[l][:, merge_off:].astype(BF16),
                        _row2(b_in[l][merge_off:]), s5_glu_w[l].astype(BF16), _row2(s5_glu_b[l]), w_a[l].astype(BF16),
                        w_b[l].astype(BF16), w_c[l].astype(BF16), w_o[l].astype(BF16), _row2(ln1_g[l]),
                        _row2(ln1_b[l]), router, cfg)

        if router is None:
            x = _dense_ffn(merged[0], mod_l, ffn_w1[j].astype(BF16), ffn_w3[j].astype(BF16), ffn_w2[j].astype(BF16),
                           _row2(ln2_g[l]), _row2(ln2_b[l]), cfg)
        else:
            x = tuple(_moe_ffn(merged[0], mod_l, merged[1], moe_w1[j].astype(BF16), moe_w3[j].astype(BF16),
                               moe_w2[j].astype(BF16), _row2(ln2_g[l]), _row2(ln2_b[l]), cfg))

    y_p = x[0].reshape(n_ctx_seq, ctx_len, D_MODEL)
    y_s = x[1].reshape(n_lat_seq, lat_len, D_MODEL)
    s5 = jnp.stack(st_s5, 0)
    s5 = s5.reshape(DEPTH, S5_Q, 2, 2, n_ctx_seq, S5_QG, S5_STATE).transpose(3, 4, 0, 2, 1, 5, 6)
    s5 = s5.reshape(2, n_ctx_seq, DEPTH, 2, S5_GROUPS, S5_STATE)
    return (y_p, y_s, s5[0], s5[1], jnp.stack(st_ret, 1), jnp.stack(st_c, 1), jnp.stack(st_n, 1),
            jnp.stack(st_m, 1))
```

```python
import functools

import jax
import jax.numpy as jnp
from jax import lax
from jax.experimental import pallas as pl
from jax.experimental.pallas import tpu as pltpu

F32 = jnp.float32
BF16 = jnp.bfloat16

D_MODEL = 1024
DEPTH = 4
GRID_W = 64
CHUNK = 128
S5_WIDTH = 512
S5_GROUP = 16
S5_GROUPS = 32
S5_STATE = 64
HEADS = 4
HEAD_DIM = 128
WIDTH = 512
ROPE_BASE = 10000.0
D_FF = 2816
N_EXPERTS = 8
ALPHA = (2.0 * DEPTH) ** 0.25
LN_EPS = 1e-5
GN_EPS = 1e-5
N_COND = 16
SUB = 16
N_LAG = 2 * SUB - 1
LANE = 128
S5_Q = S5_WIDTH // LANE
S5_QG = LANE // S5_GROUP
S5_FLAT = SUB * LANE
S5_HALF = S5_QG * S5_STATE
S5_ST = 4 * S5_HALF
NEG_BIG = -1e30
MOE_BLK = 128
GATE_BC = 16
GATE_REST = 32


def _dot(a, b):
    return jnp.dot(a, b, preferred_element_type=F32)


def _dot_hi(a, b):
    return jnp.dot(a, b, preferred_element_type=F32, precision=lax.Precision.HIGHEST)


def _dot_nt(a, b):
    return lax.dot_general(a, b, (((1,), (1,)), ((), ())), preferred_element_type=F32)


def _dot_tn(a, b):
    return lax.dot_general(a, b, (((0,), (0,)), ((), ())), preferred_element_type=F32)


def _params(sem, vmem_mb):
    return pltpu.CompilerParams(dimension_semantics=sem, vmem_limit_bytes=vmem_mb << 20)


def _cond_row(tile, tm, n_ctx_tok, lat_len):
    start = tile * tm
    return jnp.where(start < n_ctx_tok, 0, 1 + (start - n_ctx_tok) // lat_len)


def _iota(shape, axis):
    return lax.broadcasted_iota(jnp.int32, shape, axis)


def _group_specs(tm, width, n_ctx_tiles):
    return (pl.BlockSpec((tm, width), lambda i, *_: (jnp.minimum(i, n_ctx_tiles - 1), 0)),
            pl.BlockSpec((tm, width), lambda i, *_: (jnp.maximum(i - n_ctx_tiles, 0), 0)))


def _mod_kernel(c_ref, w_ref, b_ref, o_ref):
    o_ref[...] = _dot_hi(jax.nn.silu(c_ref[...]), w_ref[...]) + b_ref[...]


def _modulation(cond, ada_w, ada_b):
    tn = 1536
    n = ada_w.shape[-1]
    return pl.pallas_call(
        _mod_kernel,
        grid=(DEPTH, n // tn),
        in_specs=[pl.BlockSpec((N_COND, D_MODEL), lambda l, j: (0, 0)),
                  pl.BlockSpec((None, D_MODEL, tn), lambda l, j: (l, 0, j)),
                  pl.BlockSpec((None, 1, tn), lambda l, j: (l, 0, j))],
        out_specs=pl.BlockSpec((None, N_COND, tn), lambda l, j: (l, 0, j)),
        out_shape=jax.ShapeDtypeStruct((DEPTH, N_COND, n), F32),
        compiler_params=_params(("parallel", "parallel"), 40),
        name="modulation",
    )(cond, ada_w, ada_b.reshape(DEPTH, 1, n))


def _inproj_kernel(*refs, n_ctx_tiles, split):
    if split:
        xc_ref, xs_ref = refs[:2]
    x_ref, mod_ref, w_ref, b_ref, wg_ref, bg_ref, o_ref, u4_ref, g_ref, h_scr, u_scr = refs[1 if split else 0:]
    j = pl.program_id(1)
    tm = x_ref.shape[0]

    @pl.when(j == 0)
    def _():
        x = jnp.where(pl.program_id(0) < n_ctx_tiles, xc_ref[...], xs_ref[...]) if split else x_ref[...]
        h = (x * (1.0 + mod_ref[1:2, :]) + mod_ref[0:1, :]).astype(BF16)
        h_scr[...] = h
        g_ref[...] = _dot(h, wg_ref[...]) + bg_ref[...]
        u = _dot(h, w_ref[...]) + b_ref[...]
        for q in range(S5_Q):
            u_scr[q] = u[:, q * LANE:(q + 1) * LANE]
            for t in range(SUB):
                u4_ref[q, :, t * LANE:(t + 1) * LANE] = u_scr[q, pl.ds(t, tm // SUB, stride=SUB), :].astype(BF16)

    @pl.when(j > 0)
    def _():
        o_ref[...] = (_dot(h_scr[...], w_ref[...]) + b_ref[...]).astype(o_ref.dtype)


def _inproj(x, mod_l, w, b, wg, bg, cfg):
    split = isinstance(x, tuple)
    xs = list(x) if split else [x]
    t_tok = sum(a.shape[0] for a in xs)
    tm, tn = cfg["tm_in"] // (2 if split else 1), 512
    nj = w.shape[1] // tn
    n_ctx_tiles = cfg["n_ctx_tok"] // tm
    cond = functools.partial(_cond_row, tm=tm, n_ctx_tok=cfg["n_ctx_tok"], lat_len=cfg["lat_len"])
    x_specs = list(_group_specs(tm, D_MODEL, n_ctx_tiles)) if split else [pl.BlockSpec((tm, D_MODEL), lambda i, j: (i, 0))]
    return pl.pallas_call(
        functools.partial(_inproj_kernel, n_ctx_tiles=n_ctx_tiles, split=split),
        grid=(t_tok // tm, nj),
        in_specs=x_specs + [
                  pl.BlockSpec((None, 6, D_MODEL), lambda i, j: (cond(i), 0, 0)),
                  pl.BlockSpec((D_MODEL, tn), lambda i, j: (0, j)),
                  pl.BlockSpec((1, tn), lambda i, j: (0, j)),
                  pl.BlockSpec((D_MODEL, LANE), lambda i, j: (0, 0)),
                  pl.BlockSpec((1, LANE), lambda i, j: (0, 0))],
        out_specs=[pl.BlockSpec((tm, tn), lambda i, j: (i, jnp.maximum(j - 1, 0))),
                   pl.BlockSpec((S5_Q, tm // SUB, S5_FLAT), lambda i, j: (0, i, 0)),
                   pl.BlockSpec((tm, LANE), lambda i, j: (i, 0))],
        out_shape=[jax.ShapeDtypeStruct((t_tok, (nj - 1) * tn), BF16),
                   jax.ShapeDtypeStruct((S5_Q, t_tok // SUB, S5_FLAT), BF16),
                   jax.ShapeDtypeStruct((t_tok, LANE), F32)],
        scratch_shapes=[pltpu.VMEM((tm, D_MODEL), BF16), pltpu.VMEM((S5_Q, tm, LANE), F32)],
        compiler_params=_params(("parallel", "arbitrary"), 48),
        name="inproj",
    )(*xs, mod_l, w, b, wg, bg)


def _s5_factors(lam_re, lam_im, log_step, b_re, b_im, c_re, c_im):
    lam = lax.complex(lam_re.astype(F32), lam_im.astype(F32))
    lam_dt = lam * jnp.exp(log_step.astype(F32))[..., None]
    lam_bar = jnp.exp(lam_dt)
    bbar = ((lam_bar - 1.0) / lam)[..., None] * lax.complex(b_re.astype(F32), b_im.astype(F32))
    cmat = lax.complex(c_re.astype(F32), c_im.astype(F32))
    ks = jnp.arange(SUB + 1, dtype=F32)
    pw = jnp.exp(lam_dt[None] * ks[:, None, None, None])
    kern = jnp.einsum('dgcp,tdgp,dgpe->dgtce', cmat, pw[:SUB], bbar).real
    pad = jnp.zeros_like(kern[0][:, :SUB - 1])
    ktab = jnp.concatenate([pad, kern[0]], 1) + jnp.concatenate([kern[1][:, ::-1], pad], 1)
    k_lag = ktab.reshape(S5_Q, S5_QG, N_LAG, S5_GROUP, S5_GROUP).transpose(0, 2, 3, 1, 4)
    k_lag = k_lag.reshape(S5_Q, N_LAG, S5_GROUP, LANE)
    pw_in = jnp.stack([pw[:SUB][::-1, 0], pw[:SUB][:, 1]], 0)
    wb = pw_in[..., None] * bbar[:, None]
    wb = jnp.stack([wb.real, wb.imag], 1).reshape(2, 2, SUB, S5_Q, S5_QG, S5_STATE, S5_GROUP)
    a_in = wb.transpose(3, 2, 0, 1, 5, 4, 6).reshape(S5_Q, SUB, 4, S5_STATE, LANE)
    pw_out = jnp.stack([pw[1:, 0], pw[1:][::-1, 1]], 0)
    ce = cmat[:, None] * pw_out[:, :, :, None, :]
    ce = jnp.stack([ce.real, -ce.imag], 1).reshape(2, 2, SUB, S5_Q, S5_QG, S5_GROUP, S5_STATE)
    b_out = ce.transpose(3, 0, 1, 2, 5, 4, 6).reshape(S5_Q, 4, SUB, S5_GROUP, S5_HALF)
    a = pw[SUB]
    a = jnp.stack([a.real, a.imag], 1).reshape(2, 2, S5_Q, 1, S5_HALF).transpose(2, 0, 1, 3, 4)
    return a_in.astype(BF16), k_lag.astype(BF16), b_out.astype(BF16), a


def _expand(src_t, n_rep, row_shift, col_shift):
    k, r = src_t.shape
    rep = jnp.where(_iota((k, n_rep * k), 0) == (_iota((k, n_rep * k), 1) & (k - 1)), 1.0, 0.0).astype(BF16)
    same = (_iota((r, n_rep * k), 0) >> row_shift) == (_iota((r, n_rep * k), 1) >> col_shift)
    return jnp.where(same, _dot_tn(src_t, rep), 0.0).astype(BF16)


def _s5a_kernel(u_ref, a_ref, o_ref, w_scr):
    @pl.when(pl.program_id(1) == 0)
    def _():
        for t in range(SUB):
            for k in range(4):
                w_scr[t * LANE:(t + 1) * LANE, k * S5_HALF:(k + 1) * S5_HALF] = _expand(a_ref[t, k], S5_QG, 4, 6)

    res = _dot(u_ref[...], w_scr[...])
    for d in range(2):
        for r in range(2):
            k = 2 * d + r
            o_ref[d, r] = res[:, k * S5_HALF:(k + 1) * S5_HALF]


def _s5_state_in(u4, a_in, cfg):
    rows = u4.shape[1]
    rt = cfg["s5_rows"]
    return pl.pallas_call(
        _s5a_kernel,
        grid=(S5_Q, rows // rt),
        in_specs=[pl.BlockSpec((None, rt, S5_FLAT), lambda q, i: (q, i, 0)),
                  pl.BlockSpec((None, SUB, 4, S5_STATE, LANE), lambda q, i: (q, 0, 0, 0, 0))],
        out_specs=pl.BlockSpec((None, 2, 2, rt, S5_HALF), lambda q, i: (q, 0, 0, i, 0)),
        out_shape=jax.ShapeDtypeStruct((S5_Q, 2, 2, rows, S5_HALF), F32),
        scratch_shapes=[pltpu.VMEM((S5_FLAT, S5_ST), BF16)],
        compiler_params=_params(("parallel", "arbitrary"), 48),
        name="s5_state_in",
    )(u4, a_in)


def _s5b_kernel(loc_ref, a_ref, x0_ref, xp_ref, fin_ref, *, n_ctx_seq, ctx_sub, n_lat_seq, lat_sub):
    d = pl.program_id(1)
    ar = jnp.broadcast_to(a_ref[0], (8, LANE))
    ai = jnp.broadcast_to(a_ref[1], (8, LANE))

    def run(base, nsub, xr0, xi0):
        def body(jj, carry):
            xr, xi = carry
            j = jnp.where(d == 0, jj, nsub - 1 - jj)
            idx = pl.ds(base + j, 8, stride=nsub)
            xp_ref[0, idx, :] = xr
            xp_ref[1, idx, :] = xi
            lr = loc_ref[0, idx, :]
            li = loc_ref[1, idx, :]
            return ar * xr - ai * xi + lr, ar * xi + ai * xr + li
        return lax.fori_loop(0, nsub, body, (xr0, xi0))

    zero = jnp.zeros((8, LANE), F32)
    for bg in range(n_ctx_seq // 8):
        xr, xi = run(bg * 8 * ctx_sub, ctx_sub, zero, zero)
        fin_ref[0, bg * 8:(bg + 1) * 8, :] = xr
        fin_ref[1, bg * 8:(bg + 1) * 8, :] = xi
    for bg in range(n_lat_seq // 8):
        run(n_ctx_seq * ctx_sub + bg * 8 * lat_sub, lat_sub, x0_ref[0, bg * 8:(bg + 1) * 8, :],
            x0_ref[1, bg * 8:(bg + 1) * 8, :])


def _s5_scan(loc, a, x0, cfg):
    rows = loc.shape[3]
    n_ctx_seq, n_lat_seq = cfg["n_ctx_seq"], cfg["n_lat_seq"]
    kern = functools.partial(_s5b_kernel, n_ctx_seq=n_ctx_seq, ctx_sub=cfg["ctx_len"] // SUB,
                             n_lat_seq=n_lat_seq, lat_sub=cfg["lat_len"] // SUB)
    nlb = S5_HALF // LANE
    return pl.pallas_call(
        kern,
        grid=(S5_Q, 2, nlb),
        in_specs=[pl.BlockSpec((None, None, 2, rows, LANE), lambda q, d, b: (q, d, 0, 0, b)),
                  pl.BlockSpec((None, None, 2, 1, LANE), lambda q, d, b: (q, d, 0, 0, b)),
                  pl.BlockSpec((None, None, 2, n_lat_seq, LANE), lambda q, d, b: (q, d, 0, 0, b))],
        out_specs=[pl.BlockSpec((None, None, 2, rows, LANE), lambda q, d, b: (q, d, 0, 0, b)),
                   pl.BlockSpec((None, None, 2, n_ctx_seq, LANE), lambda q, d, b: (q, d, 0, 0, b))],
        out_shape=[jax.ShapeDtypeStruct(loc.shape, F32),
                   jax.ShapeDtypeStruct((S5_Q, 2, 2, n_ctx_seq, S5_HALF), F32)],
        compiler_params=_params(("parallel", "parallel", "parallel"), 48),
        name="s5_scan",
    )(loc, a, x0)


def _s5c_kernel(u_ref, xp_ref, k_ref, b_ref, d_ref, z_ref, m_scr, wo_scr, bd_scr, z_scr):
    rt = u_ref.shape[0]

    @pl.when(pl.program_id(1) == 0)
    def _():
        for l in range(N_LAG):
            bd_scr[l] = _expand(k_ref[l], S5_QG, 4, 4)
        for t in range(SUB):
            for s in range(SUB):
                m_scr[t * LANE:(t + 1) * LANE, s * LANE:(s + 1) * LANE] = bd_scr[s - t + SUB - 1]
        for k in range(4):
            for t in range(SUB):
                wo_scr[k * S5_HALF:(k + 1) * S5_HALF, t * LANE:(t + 1) * LANE] = _expand(b_ref[k, t], S5_QG, 6, 4)

    u = u_ref[...]
    xcat = jnp.concatenate([xp_ref[0, 0], xp_ref[0, 1], xp_ref[1, 0], xp_ref[1, 1]], axis=1).astype(BF16)
    y = _dot(u, m_scr[...]) + _dot(xcat, wo_scr[...])
    z = jax.nn.gelu(d_ref[...] * u.astype(F32) + y)
    for t in range(SUB):
        z_scr[pl.ds(t, rt, stride=SUB), :] = z[:, t * LANE:(t + 1) * LANE]
    z_ref[...] = z_scr[...].astype(z_ref.dtype)


def _s5_output(u4, xprev, k_lag, b_out, d4, cfg):
    rows = u4.shape[1]
    rt = cfg["s5_rows"]
    return pl.pallas_call(
        _s5c_kernel,
        grid=(S5_Q, rows // rt),
        in_specs=[pl.BlockSpec((None, rt, S5_FLAT), lambda q, i: (q, i, 0)),
                  pl.BlockSpec((None, 2, 2, rt, S5_HALF), lambda q, i: (q, 0, 0, i, 0)),
                  pl.BlockSpec((None, N_LAG, S5_GROUP, LANE), lambda q, i: (q, 0, 0, 0)),
                  pl.BlockSpec((None, 4, SUB, S5_GROUP, S5_HALF), lambda q, i: (q, 0, 0, 0, 0)),
                  pl.BlockSpec((None, 1, S5_FLAT), lambda q, i: (q, 0, 0))],
        out_specs=pl.BlockSpec((rt * SUB, LANE), lambda q, i: (i, q)),
        out_shape=jax.ShapeDtypeStruct((rows * SUB, S5_WIDTH), BF16),
        scratch_shapes=[pltpu.VMEM((S5_FLAT, S5_FLAT), BF16), pltpu.VMEM((S5_ST, S5_FLAT), BF16),
                        pltpu.VMEM((N_LAG, LANE, LANE), BF16), pltpu.VMEM((rt * SUB, LANE), F32)],
        compiler_params=_params(("parallel", "arbitrary"), 56),
        name="s5_output",
    )(u4, xprev, k_lag, b_out, d4)


def _group_norm(o, g, b):
    mu = jnp.mean(o, axis=-1, keepdims=True)
    var = jnp.mean(jnp.square(o - mu), axis=-1, keepdims=True)
    return (o - mu) * lax.rsqrt(var + GN_EPS) * g + b


def _ret_kernel(dec_ref, q_ref, k_ref, v_ref, g_ref, cos_ref, sin_ref, s0_ref, gg_ref, gb_ref, y_ref, *rest,
                nc, n_sub, use_rot, has_init, want_final):
    if want_final:
        sfin_ref, sf_scr, sb_scr, x_scr, kr_scr = rest
    else:
        sf_scr, sb_scr, x_scr, kr_scr = rest
    h = pl.program_id(1)
    row = _iota((CHUNK, CHUNK), 0).astype(F32)
    col = _iota((CHUNK, CHUNK), 1).astype(F32)
    lg_f = -jnp.exp(jnp.full((CHUNK, CHUNK), dec_ref[0, h], F32))
    lg_b = -jnp.exp(jnp.full((CHUNK, CHUNK), dec_ref[1, h], F32))
    lag = row - col
    scale = HEAD_DIM ** -0.5
    dmat = (jnp.where(lag >= 0, jnp.exp(lg_f * jnp.maximum(lag, 0.0)), 0.0)
            + jnp.where(lag <= 0, jnp.exp(lg_b * jnp.maximum(-lag, 0.0)), 0.0)) * scale
    qd_f = jnp.exp(lg_f * (row + 1.0))
    qd_b = jnp.exp(lg_b * (CHUNK - row))
    kd_f = jnp.exp(lg_f * (CHUNK - 1.0 - col)) * scale
    kd_b = jnp.exp(lg_b * col) * scale
    cd_f = jnp.exp(lg_f * CHUNK)
    cd_b = jnp.exp(lg_b * CHUNK)

    def chunk(s, j):
        return pl.ds(pl.multiple_of((s * nc + j) * CHUNK, CHUNK), CHUNK)

    def rot(ref, sl):
        x = ref[sl, :].astype(F32)
        if not use_rot:
            return x
        return x * cos_ref[sl, :] + pltpu.roll(x, HEAD_DIM // 2, 1) * sin_ref[sl, :]

    def local(s, j, carry):
        sl = chunk(s, j)
        k = rot(k_ref, sl)
        kr_scr[sl, :] = k.astype(BF16)
        k_t = k.T
        v = v_ref[sl, :]
        x_scr[0, j] = _dot((k_t * kd_f).astype(BF16), v)
        x_scr[1, j] = _dot((k_t * kd_b).astype(BF16), v)
        return carry

    zero = jnp.zeros((CHUNK, CHUNK), F32)

    def states(jj, carry):
        s_f, s_b = carry
        jf, jb = jj, nc - 1 - jj
        sf_scr[jf] = s_f.astype(BF16)
        sb_scr[jb] = s_b.astype(BF16)
        return s_f * cd_f + x_scr[0, jf], s_b * cd_b + x_scr[1, jb]

    def outputs(s, j, carry):
        sl = chunk(s, j)
        q = rot(q_ref, sl)
        att = _dot_nt(q.astype(BF16), kr_scr[sl, :]) * dmat
        lhs = jnp.concatenate([att.astype(BF16), (q * qd_f).astype(BF16), (q * qd_b).astype(BF16)], axis=1)
        rhs = jnp.concatenate([v_ref[sl, :], sf_scr[j], sb_scr[j]], axis=0)
        o = _dot(lhs, rhs)
        y = jax.nn.silu(g_ref[sl, :].astype(F32)) * _group_norm(o, gg_ref[...], gb_ref[...])
        y_ref[sl, :] = y.astype(y_ref.dtype)
        return carry

    def sequence(s, carry):
        lax.fori_loop(0, nc, functools.partial(local, s), 0, unroll=min(4, nc))
        init = (s0_ref[s, 0], s0_ref[s, 1]) if has_init else (zero, zero)
        s_f, s_b = lax.fori_loop(0, nc, states, init, unroll=2)
        if want_final:
            sfin_ref[s, 0] = s_f
            sfin_ref[s, 1] = s_b
        lax.fori_loop(0, nc, functools.partial(outputs, s), 0, unroll=min(8, nc))
        return carry

    if n_sub == 1:
        sequence(0, 0)
    else:
        lax.fori_loop(0, n_sub, sequence, 0)


def _retention(proj, dec, cos_t, sin_t, s0, gn_g, gn_b, *, n_seq, seq_len, n_sub, row0, use_rot, has_init,
               want_final):
    assert n_seq % n_sub == 0 and not (use_rot and n_sub > 1)
    nc = seq_len // CHUNK
    rows = n_sub * seq_len
    blk0 = row0 // rows
    kern = functools.partial(_ret_kernel, nc=nc, n_sub=n_sub, use_rot=use_rot, has_init=has_init, want_final=want_final)

    def tok(cb):
        return pl.BlockSpec((rows, HEAD_DIM), lambda s, h, cb=cb: (blk0 + s, cb + h))

    rot_spec = pl.BlockSpec((seq_len, HEAD_DIM), lambda s, h: (0, 0))
    st_spec = pl.BlockSpec((n_sub, 2, None, HEAD_DIM, HEAD_DIM), lambda s, h: (s, 0, h, 0, 0))
    gn_spec = pl.BlockSpec((1, HEAD_DIM), lambda s, h: (0, h))
    out_specs = [pl.BlockSpec((rows, HEAD_DIM), lambda s, h: (s, h))]
    out_shape = [jax.ShapeDtypeStruct((n_seq * seq_len, WIDTH), BF16)]
    if want_final:
        out_specs.append(st_spec)
        out_shape.append(jax.ShapeDtypeStruct((n_seq, 2, HEADS, HEAD_DIM, HEAD_DIM), F32))
    return pl.pallas_call(
        kern,
        grid=(n_seq // n_sub, HEADS),
        in_specs=[pl.BlockSpec(memory_space=pltpu.SMEM), tok(0), tok(4), tok(8), tok(12),
                  rot_spec, rot_spec, st_spec, gn_spec, gn_spec],
        out_specs=out_specs,
        out_shape=out_shape,
        scratch_shapes=[pltpu.VMEM((nc, HEAD_DIM, HEAD_DIM), BF16), pltpu.VMEM((nc, HEAD_DIM, HEAD_DIM), BF16),
                        pltpu.VMEM((2, nc, HEAD_DIM, HEAD_DIM), F32), pltpu.VMEM((rows, HEAD_DIM), BF16)],
        compiler_params=_params(("parallel", "parallel"), 48),
        name="retention",
    )(dec, proj, proj, proj, proj, cos_t, sin_t, s0, gn_g, gn_b)


def _gate_prep_kernel(g_ref, col_ref, row_ref):
    lane = _iota((CHUNK, LANE), 1)
    tri = jnp.where(_iota((CHUNK, CHUNK), 0) >= _iota((CHUNK, CHUNK), 1), 1.0, 0.0)
    for c in range(g_ref.shape[0] // CHUNK):
        sl = slice(c * CHUNK, (c + 1) * CHUNK)
        g = g_ref[sl, :]
        lf = jnp.where(lane < 4 * HEADS, jnp.minimum(g, 0.0) - jnp.log1p(jnp.exp(-jnp.abs(g))), 0.0)
        cs = _dot_hi(tri, lf)
        tot = cs[CHUNK - 1:CHUNK, :]
        bc = jnp.where(lane < 2 * HEADS, cs, tot - cs + lf)
        rest = jnp.where(lane < 2 * HEADS, tot - cs, cs - lf)
        pack = g + pltpu.roll(bc, GATE_BC, 1) + pltpu.roll(rest, GATE_REST, 1)
        col_ref[sl, :] = pack
        row_ref[sl, :] = pack.T


def _gate_prep(gates, cfg):
    t_tok = gates.shape[0]
    tm = cfg["tm_in"]
    spec = pl.BlockSpec((tm, LANE), lambda i: (i, 0))
    return pl.pallas_call(
        _gate_prep_kernel,
        grid=(t_tok // tm,),
        in_specs=[spec],
        out_specs=[spec, spec],
        out_shape=[jax.ShapeDtypeStruct((t_tok, LANE), F32)] * 2,
        compiler_params=_params(("parallel",), 32),
        name="gate_prep",
    )(gates)


def _mlstm_kernel(q_ref, k_ref, v_ref, o_ref, col_ref, row_ref, c0_ref, n0_ref, m0_ref, gg_ref, gb_ref, y_ref, *rest,
                  nc, n_sub, has_init, want_final):
    if want_final:
        cfin_ref, nfin_ref, mfin_ref, c_scr, n_scr, m_scr, x_scr, nl_scr, ml_scr, bl_scr = rest
    else:
        c_scr, n_scr, m_scr, x_scr, nl_scr, ml_scr, bl_scr = rest
    src = _iota((CHUNK, CHUNK), 0)
    dst = _iota((CHUNK, CHUNK), 1)
    scale = HEAD_DIM ** -0.5

    def chunk(s, j):
        return pl.ds(pl.multiple_of((s * nc + j) * CHUNK, CHUNK), CHUNK)

    def gate_idx(d, h):
        return d * 2 * HEADS + h, GATE_BC + d * 2 * HEADS + HEADS + h, GATE_REST + d * 2 * HEADS + HEADS + h

    def head_body(h):
        def init(s, d):
            if has_init:
                return c0_ref[s, d], n0_ref[s, d], m0_ref[s, d]
            return (jnp.zeros((HEAD_DIM, HEAD_DIM), F32), jnp.zeros((1, HEAD_DIM), F32),
                    jnp.zeros((1, HEAD_DIM), F32))

        def local(s, j, carry):
            sl = chunk(s, j)
            k = k_ref[sl, :]
            v_t = v_ref[sl, :].astype(F32).T
            rp = row_ref[sl, :]
            for d in range(2):
                ii, bi, ri = gate_idx(d, h)
                bc_row = rp[bi:bi + 1, :]
                b_last = bc_row[:, CHUNK - 1:CHUNK] if d == 0 else bc_row[:, 0:1]
                log_k = rp[ri:ri + 1, :] + rp[ii:ii + 1, :]
                m_loc = jnp.max(log_k, axis=1, keepdims=True)
                kw = jnp.exp(log_k - m_loc)
                x_scr[d, j] = _dot((v_t * kw).astype(BF16), k) * scale
                kw_hi = kw.astype(BF16).astype(F32)
                kw2 = jnp.concatenate([jnp.broadcast_to(kw_hi, (8, CHUNK)), jnp.broadcast_to(kw - kw_hi, (8, CHUNK))], 0)
                nl = _dot(kw2.astype(BF16), k)
                nl_scr[d, j] = (nl[0:1, :] + nl[8:9, :]) * scale
                ml_scr[d, j] = jnp.broadcast_to(m_loc, (1, HEAD_DIM))
                bl_scr[d, j] = jnp.broadcast_to(b_last, (1, HEAD_DIM))
            return carry

        def states(jj, carry):
            out = []
            for d, j in ((0, jj), (1, nc - 1 - jj)):
                cmat, nvec, m = carry[d]
                c_scr[d, j] = cmat.astype(BF16)
                n_scr[d, j] = nvec
                m_scr[d, j] = m
                m_new = jnp.maximum(bl_scr[d, j] + m, ml_scr[d, j])
                keep = jnp.exp(bl_scr[d, j] + m - m_new)
                add = jnp.exp(ml_scr[d, j] - m_new)
                out.append((keep * cmat + add * x_scr[d, j], keep * nvec + add * nl_scr[d, j], m_new))
            return tuple(out)

        def outputs(s, j, carry):
            sl = chunk(s, j)
            q = q_ref[sl, :]
            v = v_ref[sl, :]
            s_t = _dot_nt(k_ref[sl, :], q) * scale
            cp = col_ref[sl, :]
            rp = row_ref[sl, :]
            h_t = None
            for d in range(2):
                ii, bi, _ = gate_idx(d, h)
                causal = (src <= dst) if d == 0 else (src >= dst)
                bc_row = rp[bi:bi + 1, :]
                log_d = jnp.where(causal, bc_row + (cp[:, ii:ii + 1] - cp[:, bi:bi + 1]), -jnp.inf)
                log_prev = bc_row + m_scr[d, j]
                m_t = jnp.maximum(log_prev, jnp.max(log_d, axis=0, keepdims=True))
                w = s_t * jnp.exp(log_d - m_t)
                w_prev = jnp.exp(log_prev - m_t)
                qn = _dot_nt(jnp.broadcast_to(n_scr[d, j], (16, HEAD_DIM)).astype(BF16), q)[0:1, :]
                den = jnp.sum(w, axis=0, keepdims=True) + w_prev * qn
                inv = 1.0 / jnp.maximum(jnp.abs(den), jnp.exp(-m_t))
                num = _dot_tn(v, w.astype(BF16)) + _dot_nt(c_scr[d, j], q) * w_prev
                h_t = num * inv if h_t is None else h_t + num * inv
            y = jax.nn.sigmoid(o_ref[sl, :].astype(F32).T) * h_t
            mu = jnp.mean(y, axis=0, keepdims=True)
            var = jnp.mean(jnp.square(y - mu), axis=0, keepdims=True)
            y = (y - mu) * lax.rsqrt(var + GN_EPS) * gg_ref[...] + gb_ref[...]
            y_ref[sl, :] = y.T.astype(y_ref.dtype)
            return carry

        def sequence(s, carry):
            lax.fori_loop(0, nc, functools.partial(local, s), 0, unroll=min(4, nc))
            fin = lax.fori_loop(0, nc, states, (init(s, 0), init(s, 1)), unroll=2)
            if want_final:
                for d in range(2):
                    cfin_ref[s, d] = fin[d][0]
                    nfin_ref[s, d] = fin[d][1]
                    mfin_ref[s, d] = fin[d][2]
            lax.fori_loop(0, nc, functools.partial(outputs, s), 0, unroll=min(4, nc))
            return carry

        if n_sub == 1:
            sequence(0, 0)
        else:
            lax.fori_loop(0, n_sub, sequence, 0)

    hh = pl.program_id(1)
    for h in range(HEADS):
        pl.when(hh == h)(functools.partial(head_body, h))


def _mlstm(proj, gcol, grow, c0, n0, m0, gn_g, gn_b, *, n_seq, seq_len, n_sub, row0, has_init, want_final):
    assert n_seq % n_sub == 0
    nc = seq_len // CHUNK
    rows = n_sub * seq_len
    blk0 = row0 // rows
    kern = functools.partial(_mlstm_kernel, nc=nc, n_sub=n_sub, has_init=has_init, want_final=want_final)

    def tok(cb):
        return pl.BlockSpec((rows, HEAD_DIM), lambda s, h, cb=cb: (blk0 + s, cb + h))

    gate_spec = pl.BlockSpec((rows, LANE), lambda s, h: (blk0 + s, 0))
    c_spec = pl.BlockSpec((n_sub, 2, None, HEAD_DIM, HEAD_DIM), lambda s, h: (s, 0, h, 0, 0))
    v_spec = pl.BlockSpec((n_sub, 2, None, 1, HEAD_DIM), lambda s, h: (s, 0, h, 0, 0))
    gn_spec = pl.BlockSpec((HEAD_DIM, LANE), lambda s, h: (h, 0))
    out_specs = [pl.BlockSpec((rows, HEAD_DIM), lambda s, h: (s, h))]
    out_shape = [jax.ShapeDtypeStruct((n_seq * seq_len, WIDTH), BF16)]
    if want_final:
        out_specs += [c_spec, v_spec, v_spec]
        out_shape += [jax.ShapeDtypeStruct((n_seq, 2, HEADS, HEAD_DIM, HEAD_DIM), F32),
                      jax.ShapeDtypeStruct((n_seq, 2, HEADS, 1, HEAD_DIM), F32),
                      jax.ShapeDtypeStruct((n_seq, 2, HEADS, 1, HEAD_DIM), F32)]
    return pl.pallas_call(
        kern,
        grid=(n_seq // n_sub, HEADS),
        in_specs=[tok(16), tok(20), tok(24), tok(28), gate_spec, gate_spec,
                  c_spec, v_spec, v_spec, gn_spec, gn_spec],
        out_specs=out_specs,
        out_shape=out_shape,
        scratch_shapes=[pltpu.VMEM((2, nc, HEAD_DIM, HEAD_DIM), BF16)] + [pltpu.VMEM((2, nc, 1, HEAD_DIM), F32)] * 2
        + [pltpu.VMEM((2, nc, HEAD_DIM, HEAD_DIM), F32)] + [pltpu.VMEM((2, nc, 1, HEAD_DIM), F32)] * 3,
        compiler_params=_params(("parallel", "parallel"), 48),
        name="mlstm",
    )(proj, proj, proj, proj, gcol, grow, c0, n0, m0, gn_g, gn_b)


def _layer_norm(x, g, b):
    mu = jnp.mean(x, axis=-1, keepdims=True)
    var = jnp.mean(jnp.square(x - mu), axis=-1, keepdims=True)
    return (x - mu) * lax.rsqrt(var + LN_EPS) * g + b


def _top2_gates(h, whi_ref, wlo_ref, b_ref):
    lane = _iota((h.shape[0], LANE), 1)
    h_hi = h.astype(BF16)
    h_lo = (h - h_hi.astype(F32)).astype(BF16)
    logits = _dot(h_hi, whi_ref[...]) + _dot(h_hi, wlo_ref[...]) + _dot(h_lo, whi_ref[...]) + b_ref[...]
    logits = jnp.where(lane < N_EXPERTS, logits, NEG_BIG)
    m1 = jnp.max(logits, axis=1, keepdims=True)
    i1 = jnp.min(jnp.where(logits == m1, lane, LANE), axis=1, keepdims=True)
    rest = jnp.where(lane == i1, NEG_BIG, logits)
    m2 = jnp.max(rest, axis=1, keepdims=True)
    i2 = jnp.min(jnp.where(rest == m2, lane, LANE), axis=1, keepdims=True)
    e2 = jnp.exp(m2 - m1)
    den = 1.0 + e2
    return jnp.where(lane == i1, 1.0 / den, 0.0) + jnp.where(lane == i2, e2 / den, 0.0)


def _merge_kernel(*refs, n_ctx_tiles, with_router, split):
    if split:
        xc_ref, xs_ref = refs[:2]
    (x_ref, mod_ref, z_ref, ybc_ref, ybs_ref, ycc_ref, ycs_ref, wm_ref, bm_ref, wglu_ref, bglu_ref,
     wa_ref, wb_ref, wc_ref, wo_ref, lg_ref, lb_ref, *rest) = refs[1 if split else 0:]
    if with_router:
        rwh_ref, rwl_ref, rb_ref, o_ref, g_ref = rest
    else:
        o_ref, = rest
    is_ctx = pl.program_id(0) < n_ctx_tiles
    x = jnp.where(is_ctx, xc_ref[...], xs_ref[...]) if split else x_ref[...]
    h = (x * (1.0 + mod_ref[1:2, :]) + mod_ref[0:1, :]).astype(BF16)
    z = z_ref[...]
    ya = (z.astype(F32) * jax.nn.sigmoid(_dot(z, wglu_ref[...]) + bglu_ref[...])).astype(BF16)
    yb = jnp.where(is_ctx, ybc_ref[...], ybs_ref[...])
    yc = jnp.where(is_ctx, ycc_ref[...], ycs_ref[...])
    merged = None
    for j, (y, w_ref) in enumerate(((ya, wa_ref), (yb, wb_ref), (yc, wc_ref))):
        gate = jax.nn.sigmoid(_dot(h, wm_ref[:, j * D_MODEL:(j + 1) * D_MODEL]) + bm_ref[:, j * D_MODEL:(j + 1) * D_MODEL])
        term = gate * _dot(y, w_ref[...])
        merged = term if merged is None else merged + term
    mix = _dot(merged.astype(BF16), wo_ref[...])
    x1 = _layer_norm(ALPHA * x + mod_ref[2:3, :] * mix, lg_ref[...], lb_ref[...])
    o_ref[...] = x1
    if with_router:
        g_ref[...] = _top2_gates(x1 * (1.0 + mod_ref[4:5, :]) + mod_ref[3:4, :], rwh_ref, rwl_ref, rb_ref)


def _merge(x, mod_l, z, yb_c, yb_s, yc_c, yc_s, wm, bm, wglu, bglu, wa, wb, wc, wo, lg, lb, router, cfg):
    split = isinstance(x, tuple)
    xs = list(x) if split else [x]
    t_tok = sum(a.shape[0] for a in xs)
    tm = cfg["tm_merge"]
    n_ctx_tiles = cfg["n_ctx_tok"] // tm
    cond = functools.partial(_cond_row, tm=tm, n_ctx_tok=cfg["n_ctx_tok"], lat_len=cfg["lat_len"])

    def full(shape):
        return pl.BlockSpec(shape, lambda i: (0,) * len(shape))

    def tok(w):
        return pl.BlockSpec((tm, w), lambda i: (i, 0))

    ctx_spec, lat_spec = _group_specs(tm, WIDTH, n_ctx_tiles)
    x_specs = list(_group_specs(tm, D_MODEL, n_ctx_tiles)) if split else [tok(D_MODEL)]
    in_specs = x_specs + [pl.BlockSpec((None, 6, D_MODEL), lambda i: (cond(i), 0, 0)),
                tok(WIDTH), ctx_spec, lat_spec, ctx_spec, lat_spec,
                full((D_MODEL, 3 * D_MODEL)), full((1, 3 * D_MODEL)), full((WIDTH, WIDTH)), full((1, WIDTH)),
                full((WIDTH, D_MODEL)), full((WIDTH, D_MODEL)), full((WIDTH, D_MODEL)),
                full((D_MODEL, D_MODEL)), full((1, D_MODEL)), full((1, D_MODEL))]
    args = xs + [mod_l, z, yb_c, yb_s, yc_c, yc_s, wm, bm, wglu, bglu, wa, wb, wc, wo, lg, lb]
    out_specs = [tok(D_MODEL)]
    out_shape = [jax.ShapeDtypeStruct((t_tok, D_MODEL), F32)]
    if router is not None:
        in_specs += [full((D_MODEL, LANE)), full((D_MODEL, LANE)), full((1, LANE))]
        args += list(router)
        out_specs.append(tok(LANE))
        out_shape.append(jax.ShapeDtypeStruct((t_tok, LANE), F32))
    return pl.pallas_call(
        functools.partial(_merge_kernel, n_ctx_tiles=n_ctx_tiles, with_router=router is not None, split=split),
        grid=(t_tok // tm,),
        in_specs=in_specs,
        out_specs=out_specs,
        out_shape=out_shape,
        compiler_params=_params(("parallel",), 56),
        name="merge",
    )(*args)


def _ffn_kernel(x_ref, mod_ref, w1_ref, w3_ref, w2_ref, lg_ref, lb_ref, o_ref, h_scr, acc_scr):
    f = pl.program_id(1)

    @pl.when(f == 0)
    def _():
        h_scr[...] = (x_ref[...] * (1.0 + mod_ref[4:5, :]) + mod_ref[3:4, :]).astype(BF16)
        acc_scr[...] = jnp.zeros_like(acc_scr)

    h = h_scr[...]
    act = (jax.nn.silu(_dot(h, w1_ref[...])) * _dot(h, w3_ref[...])).astype(BF16)
    acc_scr[...] += _dot(act, w2_ref[...])

    @pl.when(f == pl.num_programs(1) - 1)
    def _():
        o_ref[...] = _layer_norm(ALPHA * x_ref[...] + mod_ref[5:6, :] * acc_scr[...], lg_ref[...], lb_ref[...])


def _dense_ffn(x, mod_l, w1, w3, w2, lg, lb, cfg):
    t_tok = x.shape[0]
    tm, tf = cfg["tm_ffn"], cfg["tf"]
    cond = functools.partial(_cond_row, tm=tm, n_ctx_tok=cfg["n_ctx_tok"], lat_len=cfg["lat_len"])
    return pl.pallas_call(
        _ffn_kernel,
        grid=(t_tok // tm, D_FF // tf),
        in_specs=[pl.BlockSpec((tm, D_MODEL), lambda i, f: (i, 0)),
                  pl.BlockSpec((None, 6, D_MODEL), lambda i, f: (cond(i), 0, 0)),
                  pl.BlockSpec((D_MODEL, tf), lambda i, f: (0, f)),
                  pl.BlockSpec((D_MODEL, tf), lambda i, f: (0, f)),
                  pl.BlockSpec((tf, D_MODEL), lambda i, f: (f, 0)),
                  pl.BlockSpec((1, D_MODEL), lambda i, f: (0, 0)),
                  pl.BlockSpec((1, D_MODEL), lambda i, f: (0, 0))],
        out_specs=pl.BlockSpec((tm, D_MODEL), lambda i, f: (i, 0)),
        out_shape=jax.ShapeDtypeStruct((t_tok, D_MODEL), F32),
        scratch_shapes=[pltpu.VMEM((tm, D_MODEL), BF16), pltpu.VMEM((tm, D_MODEL), F32)],
        compiler_params=_params(("parallel", "arbitrary"), 56),
        name="dense_ffn",
    )(x, mod_l, w1, w3, w2, lg, lb)


def _moe_kernel(x_ref, mod_ref, gate_ref, w1_ref, w3_ref, w2_ref, lg_ref, lb_ref, oc_ref, os_ref,
                h_scr, acc_scr, hc_scr, ob_scr, sp_scr, gt_scr, cnt_smem, *, n_ctx_tiles):
    e = pl.program_id(1)
    f = pl.program_id(2)
    last_f = pl.num_programs(2) - 1
    tm = x_ref.shape[0]

    @pl.when((e == 0) & (f == 0))
    def _():
        h_scr[...] = (x_ref[...] * (1.0 + mod_ref[4:5, :]) + mod_ref[3:4, :]).astype(BF16)
        acc_scr[...] = jnp.zeros_like(acc_scr)
        g = gate_ref[...]
        sel = g > 0.0
        ones = jnp.where(sel, 1.0, 0.0)
        before = jnp.where(_iota((tm, tm), 0) > _iota((tm, tm), 1), 1.0, 0.0).astype(BF16)
        pos = _dot(before, ones.astype(BF16))
        spt = jnp.where(sel, pos, -1.0).T
        gt = g.T
        cnt = jnp.sum(ones, axis=0, keepdims=True)
        for ee in range(N_EXPERTS):
            sp_scr[ee] = spt[ee:ee + 1, :]
            gt_scr[ee] = gt[ee:ee + 1, :]
            cnt_smem[ee] = cnt[0, ee].astype(jnp.int32)

    nb = (cnt_smem[e] + (MOE_BLK - 1)) // MOE_BLK
    row = _iota((MOE_BLK, tm), 0)

    def onehot(b):
        return sp_scr[e] == (row + b * MOE_BLK).astype(F32)

    @pl.when(f == 0)
    def _():
        def gather(b, carry):
            p = jnp.where(onehot(b), 1.0, 0.0).astype(BF16)
            hc_scr[b] = _dot(p, h_scr[...]).astype(BF16)
            ob_scr[b] = jnp.zeros((MOE_BLK, D_MODEL), F32)
            return carry
        lax.fori_loop(0, nb, gather, 0)

        @pl.when(nb % 2 == 1)
        def _():
            ob_scr[nb] = jnp.zeros((MOE_BLK, D_MODEL), F32)

    def ffn(hc):
        act = (jax.nn.silu(_dot(hc, w1_ref[...])) * _dot(hc, w3_ref[...])).astype(BF16)
        return _dot(act, w2_ref[...])

    def ffn_pair(p, carry):
        two = pl.ds(2 * p, 2)
        ob_scr[two] += ffn(hc_scr[two].reshape(2 * MOE_BLK, D_MODEL)).reshape(2, MOE_BLK, D_MODEL)
        return carry
    lax.fori_loop(0, nb // 2, ffn_pair, 0)

    @pl.when(nb % 2 == 1)
    def _():
        ob_scr[nb - 1] += ffn(hc_scr[nb - 1])

    @pl.when(f == last_f)
    def _():
        row2 = _iota((2 * MOE_BLK, tm), 0)

        def scatter(p, carry):
            m = sp_scr[e] == (row2 + p * (2 * MOE_BLK)).astype(F32)
            gc = jnp.sum(jnp.where(m, gt_scr[e], 0.0), axis=1, keepdims=True)
            og = (ob_scr[pl.ds(2 * p, 2)].reshape(2 * MOE_BLK, D_MODEL) * gc).astype(BF16)
            acc_scr[...] += _dot_tn(jnp.where(m, 1.0, 0.0).astype(BF16), og)
            return carry
        lax.fori_loop(0, (nb + 1) // 2, scatter, 0)

    last = (e == pl.num_programs(1) - 1) & (f == last_f)
    is_ctx = pl.program_id(0) < n_ctx_tiles
    for o_ref, mine in ((oc_ref, is_ctx), (os_ref, jnp.logical_not(is_ctx))):
        @pl.when(last & mine)
        def _():
            o_ref[...] = _layer_norm(ALPHA * x_ref[...] + mod_ref[5:6, :] * acc_scr[...], lg_ref[...], lb_ref[...])


def _moe_ffn(x, mod_l, gates, w1, w3, w2, lg, lb, cfg):
    t_tok = x.shape[0]
    tm, tf = cfg["tm_moe"], cfg["tf"]
    n_ctx_tiles = cfg["n_ctx_tok"] // tm
    cond = functools.partial(_cond_row, tm=tm, n_ctx_tok=cfg["n_ctx_tok"], lat_len=cfg["lat_len"])
    return pl.pallas_call(
        functools.partial(_moe_kernel, n_ctx_tiles=n_ctx_tiles),
        grid=(t_tok // tm, N_EXPERTS, D_FF // tf),
        in_specs=[pl.BlockSpec((tm, D_MODEL), lambda i, e, f: (i, 0), pipeline_mode=pl.Buffered(1)),
                  pl.BlockSpec((None, 6, D_MODEL), lambda i, e, f: (cond(i), 0, 0)),
                  pl.BlockSpec((tm, LANE), lambda i, e, f: (i, 0), pipeline_mode=pl.Buffered(1)),
                  pl.BlockSpec((None, D_MODEL, tf), lambda i, e, f: (e, 0, f)),
                  pl.BlockSpec((None, D_MODEL, tf), lambda i, e, f: (e, 0, f)),
                  pl.BlockSpec((None, tf, D_MODEL), lambda i, e, f: (e, f, 0)),
                  pl.BlockSpec((1, D_MODEL), lambda i, e, f: (0, 0)),
                  pl.BlockSpec((1, D_MODEL), lambda i, e, f: (0, 0))],
        out_specs=list(_group_specs(tm, D_MODEL, n_ctx_tiles)),
        out_shape=[jax.ShapeDtypeStruct((cfg["n_ctx_tok"], D_MODEL), F32),
                   jax.ShapeDtypeStruct((t_tok - cfg["n_ctx_tok"], D_MODEL), F32)],
        scratch_shapes=[pltpu.VMEM((tm, D_MODEL), BF16), pltpu.VMEM((tm, D_MODEL), F32),
                        pltpu.VMEM((tm // MOE_BLK, MOE_BLK, D_MODEL), BF16),
                        pltpu.VMEM((tm // MOE_BLK, MOE_BLK, D_MODEL), F32),
                        pltpu.VMEM((N_EXPERTS, 1, tm), F32), pltpu.VMEM((N_EXPERTS, 1, tm), F32),
                        pltpu.SMEM((N_EXPERTS,), jnp.int32)],
        compiler_params=_params(("arbitrary", "arbitrary", "arbitrary"), 56),
        name="moe_ffn",
    )(x, mod_l, gates, w1, w3, w2, lg, lb)


def _rotary_tables(n_tok):
    rows = n_tok // GRID_W
    r = jnp.repeat(jnp.arange(rows, dtype=F32), GRID_W)
    col = jnp.tile(jnp.arange(GRID_W, dtype=F32), rows)
    n_freq = HEAD_DIM // 4
    inv = ROPE_BASE ** (-jnp.arange(n_freq, dtype=F32) / n_freq)
    ang = jnp.concatenate([r[:, None] * inv, col[:, None] * inv], -1)
    cos, sin = jnp.cos(ang), jnp.sin(ang)
    return jnp.concatenate([cos, cos], -1), jnp.concatenate([-sin, sin], -1)


def _row2(v):
    return v.reshape(1, -1).astype(F32)


def kernel(x_prompt, x_sample, cache_s5_re, cache_s5_im, cache_ret, cache_ml_c, cache_ml_n, cache_ml_m, c, c_ctx, ada_w, ada_b, w_in, b_in, s5_lam_re, s5_lam_im, s5_log_step, s5_b_re, s5_b_im, s5_c_re, s5_c_im, s5_d, s5_glu_w, s5_glu_b, ret_decay, ret_gn_g, ret_gn_b, ml_gn_g, ml_gn_b, w_a, w_b, w_c, w_o, ln1_g, ln1_b, ln2_g, ln2_b, ffn_w1, ffn_w3, ffn_w2, moe_router, moe_router_b, moe_w1, moe_w3, moe_w2):
    n_ctx_seq, ctx_len, _ = x_prompt.shape
    n_lat_seq, lat_len, _ = x_sample.shape
    n_ctx_tok = n_ctx_seq * ctx_len
    n_lat_tok = n_lat_seq * lat_len
    t_tok = n_ctx_tok + n_lat_tok
    assert n_lat_seq + 1 <= N_COND and n_ctx_seq % 8 == 0 and n_lat_seq % 8 == 0
    assert ctx_len % CHUNK == 0 and lat_len % CHUNK == 0 and n_ctx_tok % lat_len == 0
    cfg = dict(n_ctx_seq=n_ctx_seq, ctx_len=ctx_len, n_lat_seq=n_lat_seq, lat_len=lat_len, n_ctx_tok=n_ctx_tok,
               tm_in=min(2048, lat_len), tm_merge=min(512, lat_len), tm_ffn=min(512, lat_len), tm_moe=min(1024, lat_len), tf=1408,
               s5_rows=min(256, t_tok // SUB // 8))

    x = (x_prompt.reshape(n_ctx_tok, D_MODEL), x_sample.reshape(n_lat_tok, D_MODEL))
    cond = jnp.zeros((N_COND, D_MODEL), F32).at[0].set(c_ctx).at[1:1 + n_lat_seq].set(c)
    mod = _modulation(cond, ada_w, ada_b).reshape(DEPTH, N_COND, 6, D_MODEL)
    cos_t, sin_t = _rotary_tables(lat_len)

    n_main = S5_WIDTH + 8 * WIDTH
    gate_off = n_main
    merge_off = gate_off + 4 * HEADS
    s5_fac = jax.vmap(_s5_factors)(s5_lam_re, s5_lam_im, s5_log_step, s5_b_re, s5_b_im, s5_c_re, s5_c_im)

    st_s5, st_ret, st_c, st_n, st_m = [], [], [], [], []
    zero_ret = jnp.zeros((n_ctx_seq, 2, HEADS, HEAD_DIM, HEAD_DIM), F32)
    zero_vec = jnp.zeros((n_ctx_seq, 2, HEADS, 1, HEAD_DIM), F32)
    for l in range(DEPTH):
        mod_l = mod[l]
        w_main = w_in[l][:, :n_main].astype(BF16)
        b_main = _row2(b_in[l][:n_main])
        w_gate = jnp.zeros((D_MODEL, LANE), F32).at[:, :4 * HEADS].set(w_in[l][:, gate_off:merge_off]).astype(BF16)
        b_gate = jnp.zeros((1, LANE), F32).at[0, :4 * HEADS].set(b_in[l][gate_off:merge_off])
        proj, u4, gates = _inproj(x, mod_l, w_main, b_main, w_gate, b_gate, cfg)

        s5_ain, s5_klag, s5_bout, s5_a = (m[l] for m in s5_fac)
        loc = _s5_state_in(u4, s5_ain, cfg)
        x0 = jnp.stack([cache_s5_re[:, l], cache_s5_im[:, l]], 0)
        x0 = x0.reshape(2, n_lat_seq, 2, S5_Q, S5_HALF).transpose(3, 2, 0, 1, 4).astype(F32)
        xprev, s5_fin = _s5_scan(loc, s5_a, x0, cfg)
        d4 = jnp.tile(s5_d[l].astype(F32).reshape(S5_Q, 1, LANE), (1, SUB, 1)).reshape(S5_Q, 1, S5_FLAT)
        z = _s5_output(u4, xprev, s5_klag, s5_bout, d4, cfg)
        st_s5.append(s5_fin)

        gg, gb = _row2(ret_gn_g[l]), _row2(ret_gn_b[l])
        dec = ret_decay[l].astype(F32)
        yb_c, ret_fin = _retention(proj, dec, cos_t, sin_t, zero_ret, gg, gb, n_seq=n_ctx_seq, seq_len=ctx_len,
                                   n_sub=8, row0=0, use_rot=False, has_init=False, want_final=True)
        yb_s, = _retention(proj, dec, cos_t, sin_t, cache_ret[:, l].astype(F32), gg, gb, n_seq=n_lat_seq,
                           seq_len=lat_len, n_sub=1, row0=n_ctx_tok, use_rot=True, has_init=True, want_final=False)
        st_ret.append(ret_fin)

        gg = jnp.broadcast_to(ml_gn_g[l].astype(F32)[:, None], (WIDTH, LANE))
        gb = jnp.broadcast_to(ml_gn_b[l].astype(F32)[:, None], (WIDTH, LANE))
        gcol, grow = _gate_prep(gates, cfg)
        yc_c, c_fin, n_fin, m_fin = _mlstm(proj, gcol, grow, zero_ret, zero_vec, zero_vec, gg, gb, n_seq=n_ctx_seq,
                                           seq_len=ctx_len, n_sub=8, row0=0, has_init=False, want_final=True)
        n0 = cache_ml_n[:, l].astype(F32)[:, :, :, None, :]
        m0 = jnp.broadcast_to(cache_ml_m[:, l].astype(F32)[:, :, :, None, None], n0.shape)
        yc_s, = _mlstm(proj, gcol, grow, cache_ml_c[:, l].astype(F32), n0, m0, gg, gb, n_seq=n_lat_seq,
                       seq_len=lat_len, n_sub=1, row0=n_ctx_tok, has_init=True, want_final=False)
        st_c.append(c_fin)
        st_n.append(n_fin[:, :, :, 0, :])
        st_m.append(m_fin[:, :, :, 0, 0])

        j = l // 2
        router = None
        if l % 2 == 1:
            rw = jnp.zeros((D_MODEL, LANE), F32).at[:, :N_EXPERTS].set(moe_router[j])
            rw_hi = rw.astype(BF16)
            router = (rw_hi, (rw - rw_hi.astype(F32)).astype(BF16),
                      jnp.zeros((1, LANE), F32).at[0, :N_EXPERTS].set(moe_router_b[j]))
        merged = _merge(x, mod_l, z, yb_c, yb_s, yc_c, yc_s, w_in[l][:, merge_off:].astype(BF16),
                        _row2(b_in[l][merge_off:]), s5_glu_w[l].astype(BF16), _row2(s5_glu_b[l]), w_a[l].astype(BF16),
                        w_b[l].astype(BF16), w_c[l].astype(BF16), w_o[l].astype(BF16), _row2(ln1_g[l]),
                        _row2(ln1_b[l]), router, cfg)

        if router is None:
            x = _dense_ffn(merged[0], mod_l, ffn_w1[j].astype(BF16), ffn_w3[j].astype(BF16), ffn_w2[j].astype(BF16),
                           _row2(ln2_g[l]), _row2(ln2_b[l]), cfg)
        else:
            x = tuple(_moe_ffn(merged[0], mod_l, merged[1], moe_w1[j].astype(BF16), moe_w3[j].astype(BF16),
                               moe_w2[j].astype(BF16), _row2(ln2_g[l]), _row2(ln2_b[l]), cfg))

    y_p = x[0].reshape(n_ctx_seq, ctx_len, D_MODEL)
    y_s = x[1].reshape(n_lat_seq, lat_len, D_MODEL)
    s5 = jnp.stack(st_s5, 0)
    s5 = s5.reshape(DEPTH, S5_Q, 2, 2, n_ctx_seq, S5_QG, S5_STATE).transpose(3, 4, 0, 2, 1, 5, 6)
    s5 = s5.reshape(2, n_ctx_seq, DEPTH, 2, S5_GROUPS, S5_STATE)
    return (y_p, y_s, s5[0], s5[1], jnp.stack(st_ret, 1), jnp.stack(st_c, 1), jnp.stack(st_n, 1),
            jnp.stack(st_m, 1))
```

```python
import functools

import jax
import jax.numpy as jnp
from jax import lax
from jax.experimental import pallas as pl
from jax.experimental.pallas import tpu as pltpu

F32 = jnp.float32
BF16 = jnp.bfloat16

D_MODEL = 1024
DEPTH = 4
GRID_W = 64
CHUNK = 128
S5_WIDTH = 512
S5_GROUP = 16
S5_GROUPS = 32
S5_STATE = 64
HEADS = 4
HEAD_DIM = 128
WIDTH = 512
ROPE_BASE = 10000.0
D_FF = 2816
N_EXPERTS = 8
ALPHA = (2.0 * DEPTH) ** 0.25
LN_EPS = 1e-5
GN_EPS = 1e-5
N_COND = 16
SUB = 16
N_LAG = 2 * SUB - 1
LANE = 128
S5_Q = S5_WIDTH // LANE
S5_QG = LANE // S5_GROUP
S5_FLAT = SUB * LANE
S5_HALF = S5_QG * S5_STATE
S5_ST = 4 * S5_HALF
S5_PG = 2
S5_P = S5_QG // S5_PG
S5_PB = S5_PG * S5_GROUP
S5_PFLAT = SUB * S5_PB
S5_PHALF = S5_PG * S5_STATE
NEG_BIG = -1e30
MOE_BLK = 128
GATE_BC = 16
GATE_REST = 32


def _dot(a, b):
    return jnp.dot(a, b, preferred_element_type=F32)


def _dot_hi(a, b):
    return jnp.dot(a, b, preferred_element_type=F32, precision=lax.Precision.HIGHEST)


def _dot_nt(a, b):
    return lax.dot_general(a, b, (((1,), (1,)), ((), ())), preferred_element_type=F32)


def _dot_tn(a, b):
    return lax.dot_general(a, b, (((0,), (0,)), ((), ())), preferred_element_type=F32)


def _params(sem, vmem_mb):
    return pltpu.CompilerParams(dimension_semantics=sem, vmem_limit_bytes=vmem_mb << 20)


def _cond_row(tile, tm, n_ctx_tok, lat_len):
    start = tile * tm
    return jnp.where(start < n_ctx_tok, 0, 1 + (start - n_ctx_tok) // lat_len)


def _iota(shape, axis):
    return lax.broadcasted_iota(jnp.int32, shape, axis)


def _group_specs(tm, width, n_ctx_tiles):
    return (pl.BlockSpec((tm, width), lambda i, *_: (jnp.minimum(i, n_ctx_tiles - 1), 0)),
            pl.BlockSpec((tm, width), lambda i, *_: (jnp.maximum(i - n_ctx_tiles, 0), 0)))


def _mod_kernel(c_ref, w_ref, b_ref, o_ref):
    o_ref[...] = _dot_hi(jax.nn.silu(c_ref[...]), w_ref[...]) + b_ref[...]


def _modulation(cond, ada_w, ada_b):
    tn = 1536
    n = ada_w.shape[-1]
    return pl.pallas_call(
        _mod_kernel,
        grid=(DEPTH, n // tn),
        in_specs=[pl.BlockSpec((N_COND, D_MODEL), lambda l, j: (0, 0)),
                  pl.BlockSpec((None, D_MODEL, tn), lambda l, j: (l, 0, j)),
                  pl.BlockSpec((None, 1, tn), lambda l, j: (l, 0, j))],
        out_specs=pl.BlockSpec((None, N_COND, tn), lambda l, j: (l, 0, j)),
        out_shape=jax.ShapeDtypeStruct((DEPTH, N_COND, n), F32),
        compiler_params=_params(("parallel", "parallel"), 40),
        name="modulation",
    )(cond, ada_w, ada_b.reshape(DEPTH, 1, n))


def _inproj_kernel(*refs, n_ctx_tiles, split):
    if split:
        xc_ref, xs_ref = refs[:2]
    x_ref, mod_ref, w_ref, b_ref, wg_ref, bg_ref, o_ref, u4_ref, g_ref, h_scr, u_scr = refs[1 if split else 0:]
    j = pl.program_id(1)
    tm = x_ref.shape[0]

    @pl.when(j == 0)
    def _():
        x = jnp.where(pl.program_id(0) < n_ctx_tiles, xc_ref[...], xs_ref[...]) if split else x_ref[...]
        h = (x * (1.0 + mod_ref[1:2, :]) + mod_ref[0:1, :]).astype(BF16)
        h_scr[...] = h
        g_ref[...] = _dot(h, wg_ref[...]) + bg_ref[...]
        u = _dot(h, w_ref[...]) + b_ref[...]
        for q in range(S5_Q):
            u_scr[q] = u[:, q * LANE:(q + 1) * LANE]
            rows = [u_scr[q, pl.ds(t, tm // SUB, stride=SUB), :].astype(BF16) for t in range(SUB)]
            for p in range(S5_P):
                u4_ref[q, :, p * S5_PFLAT:(p + 1) * S5_PFLAT] = jnp.concatenate(
                    [r[:, p * S5_PB:(p + 1) * S5_PB] for r in rows], axis=1)

    @pl.when(j > 0)
    def _():
        o_ref[...] = (_dot(h_scr[...], w_ref[...]) + b_ref[...]).astype(o_ref.dtype)


def _inproj(x, mod_l, w, b, wg, bg, cfg):
    split = isinstance(x, tuple)
    xs = list(x) if split else [x]
    t_tok = sum(a.shape[0] for a in xs)
    tm, tn = cfg["tm_in"] // (2 if split else 1), 512
    nj = w.shape[1] // tn
    n_ctx_tiles = cfg["n_ctx_tok"] // tm
    cond = functools.partial(_cond_row, tm=tm, n_ctx_tok=cfg["n_ctx_tok"], lat_len=cfg["lat_len"])
    x_specs = list(_group_specs(tm, D_MODEL, n_ctx_tiles)) if split else [pl.BlockSpec((tm, D_MODEL), lambda i, j: (i, 0))]
    return pl.pallas_call(
        functools.partial(_inproj_kernel, n_ctx_tiles=n_ctx_tiles, split=split),
        grid=(t_tok // tm, nj),
        in_specs=x_specs + [
                  pl.BlockSpec((None, 6, D_MODEL), lambda i, j: (cond(i), 0, 0)),
                  pl.BlockSpec((D_MODEL, tn), lambda i, j: (0, j)),
                  pl.BlockSpec((1, tn), lambda i, j: (0, j)),
                  pl.BlockSpec((D_MODEL, LANE), lambda i, j: (0, 0)),
                  pl.BlockSpec((1, LANE), lambda i, j: (0, 0))],
        out_specs=[pl.BlockSpec((tm, tn), lambda i, j: (i, jnp.maximum(j - 1, 0))),
                   pl.BlockSpec((S5_Q, tm // SUB, S5_FLAT), lambda i, j: (0, i, 0)),
                   pl.BlockSpec((tm, LANE), lambda i, j: (i, 0))],
        out_shape=[jax.ShapeDtypeStruct((t_tok, (nj - 1) * tn), BF16),
                   jax.ShapeDtypeStruct((S5_Q, t_tok // SUB, S5_FLAT), BF16),
                   jax.ShapeDtypeStruct((t_tok, LANE), F32)],
        scratch_shapes=[pltpu.VMEM((tm, D_MODEL), BF16), pltpu.VMEM((S5_Q, tm, LANE), F32)],
        compiler_params=_params(("parallel", "arbitrary"), 48),
        name="inproj",
    )(*xs, mod_l, w, b, wg, bg)


def _s5_factors(lam_re, lam_im, log_step, b_re, b_im, c_re, c_im):
    lam = lax.complex(lam_re.astype(F32), lam_im.astype(F32))
    lam_dt = lam * jnp.exp(log_step.astype(F32))[..., None]
    lam_bar = jnp.exp(lam_dt)
    bbar = ((lam_bar - 1.0) / lam)[..., None] * lax.complex(b_re.astype(F32), b_im.astype(F32))
    cmat = lax.complex(c_re.astype(F32), c_im.astype(F32))
    ks = jnp.arange(SUB + 1, dtype=F32)
    pw = jnp.exp(lam_dt[None] * ks[:, None, None, None])
    kern = jnp.einsum('dgcp,tdgp,dgpe->dgtce', cmat, pw[:SUB], bbar).real
    pad = jnp.zeros_like(kern[0][:, :SUB - 1])
    ktab = jnp.concatenate([pad, kern[0]], 1) + jnp.concatenate([kern[1][:, ::-1], pad], 1)
    k_lag = ktab.reshape(S5_Q, S5_QG, N_LAG, S5_GROUP, S5_GROUP).transpose(0, 2, 3, 1, 4)
    k_lag = k_lag.reshape(S5_Q, N_LAG, S5_GROUP, S5_P, S5_PB).transpose(0, 3, 1, 2, 4)
    pw_in = jnp.stack([pw[:SUB][::-1, 0], pw[:SUB][:, 1]], 0)
    wb = pw_in[..., None] * bbar[:, None]
    wb = jnp.stack([wb.real, wb.imag], 1).reshape(2, 2, SUB, S5_Q, S5_QG, S5_STATE, S5_GROUP)
    a_in = wb.transpose(3, 2, 0, 1, 5, 4, 6).reshape(S5_Q, SUB, 4, S5_STATE, S5_P, S5_PB).transpose(0, 4, 1, 2, 3, 5)
    pw_out = jnp.stack([pw[1:, 0], pw[1:][::-1, 1]], 0)
    ce = cmat[:, None] * pw_out[:, :, :, None, :]
    ce = jnp.stack([ce.real, -ce.imag], 1).reshape(2, 2, SUB, S5_Q, S5_QG, S5_GROUP, S5_STATE)
    b_out = ce.transpose(3, 0, 1, 2, 5, 4, 6).reshape(S5_Q, 4, SUB, S5_GROUP, S5_P, S5_PHALF).transpose(0, 4, 1, 2, 3, 5)
    a = pw[SUB]
    a = jnp.stack([a.real, a.imag], 1).reshape(2, 2, S5_Q, 1, S5_HALF).transpose(2, 0, 1, 3, 4)
    return a_in.astype(BF16), k_lag.astype(BF16), b_out.astype(BF16), a


def _expand(src_t, n_rep, row_shift, col_shift):
    k, r = src_t.shape
    rep = jnp.where(_iota((k, n_rep * k), 0) == (_iota((k, n_rep * k), 1) & (k - 1)), 1.0, 0.0).astype(BF16)
    same = (_iota((r, n_rep * k), 0) >> row_shift) == (_iota((r, n_rep * k), 1) >> col_shift)
    return jnp.where(same, _dot_tn(src_t, rep), 0.0).astype(BF16)


def _s5a_kernel(u_ref, a_ref, o_ref, w_scr):
    @pl.when(pl.program_id(1) == 0)
    def _():
        for p in range(S5_P):
            for k in range(4):
                w_scr[p, :, k * S5_PHALF:(k + 1) * S5_PHALF] = jnp.concatenate(
                    [_expand(a_ref[p, t, k], S5_PG, 4, 6) for t in range(SUB)], axis=0)

    for p in range(S5_P):
        res = _dot(u_ref[:, p * S5_PFLAT:(p + 1) * S5_PFLAT], w_scr[p])
        for d in range(2):
            for r in range(2):
                k = 2 * d + r
                o_ref[d, r, :, p * S5_PHALF:(p + 1) * S5_PHALF] = res[:, k * S5_PHALF:(k + 1) * S5_PHALF]


def _s5_state_in(u4, a_in, cfg):
    rows = u4.shape[1]
    rt = cfg["s5_rows"]
    return pl.pallas_call(
        _s5a_kernel,
        grid=(S5_Q, rows // rt),
        in_specs=[pl.BlockSpec((None, rt, S5_FLAT), lambda q, i: (q, i, 0)),
                  pl.BlockSpec((None, S5_P, SUB, 4, S5_STATE, S5_PB), lambda q, i: (q, 0, 0, 0, 0, 0))],
        out_specs=pl.BlockSpec((None, 2, 2, rt, S5_HALF), lambda q, i: (q, 0, 0, i, 0)),
        out_shape=jax.ShapeDtypeStruct((S5_Q, 2, 2, rows, S5_HALF), F32),
        scratch_shapes=[pltpu.VMEM((S5_P, S5_PFLAT, 4 * S5_PHALF), BF16)],
        compiler_params=_params(("parallel", "arbitrary"), 48),
        name="s5_state_in",
    )(u4, a_in)


def _s5b_kernel(loc_ref, a_ref, x0_ref, xp_ref, fin_ref, *, n_ctx_seq, ctx_sub, n_lat_seq, lat_sub):
    d = pl.program_id(1)
    ar = jnp.broadcast_to(a_ref[0], (8, LANE))
    ai = jnp.broadcast_to(a_ref[1], (8, LANE))

    def run(base, nsub, xr0, xi0):
        def body(jj, carry):
            xr, xi = carry
            j = jnp.where(d == 0, jj, nsub - 1 - jj)
            idx = pl.ds(base + j, 8, stride=nsub)
            xp_ref[0, idx, :] = xr
            xp_ref[1, idx, :] = xi
            lr = loc_ref[0, idx, :]
            li = loc_ref[1, idx, :]
            return ar * xr - ai * xi + lr, ar * xi + ai * xr + li
        return lax.fori_loop(0, nsub, body, (xr0, xi0))

    zero = jnp.zeros((8, LANE), F32)
    for bg in range(n_ctx_seq // 8):
        xr, xi = run(bg * 8 * ctx_sub, ctx_sub, zero, zero)
        fin_ref[0, bg * 8:(bg + 1) * 8, :] = xr
        fin_ref[1, bg * 8:(bg + 1) * 8, :] = xi
    for bg in range(n_lat_seq // 8):
        run(n_ctx_seq * ctx_sub + bg * 8 * lat_sub, lat_sub, x0_ref[0, bg * 8:(bg + 1) * 8, :],
            x0_ref[1, bg * 8:(bg + 1) * 8, :])


def _s5_scan(loc, a, x0, cfg):
    rows = loc.shape[3]
    n_ctx_seq, n_lat_seq = cfg["n_ctx_seq"], cfg["n_lat_seq"]
    kern = functools.partial(_s5b_kernel, n_ctx_seq=n_ctx_seq, ctx_sub=cfg["ctx_len"] // SUB,
                             n_lat_seq=n_lat_seq, lat_sub=cfg["lat_len"] // SUB)
    nlb = S5_HALF // LANE
    return pl.pallas_call(
        kern,
        grid=(S5_Q, 2, nlb),
        in_specs=[pl.BlockSpec((None, None, 2, rows, LANE), lambda q, d, b: (q, d, 0, 0, b)),
                  pl.BlockSpec((None, None, 2, 1, LANE), lambda q, d, b: (q, d, 0, 0, b)),
                  pl.BlockSpec((None, None, 2, n_lat_seq, LANE), lambda q, d, b: (q, d, 0, 0, b))],
        out_specs=[pl.BlockSpec((None, None, 2, rows, LANE), lambda q, d, b: (q, d, 0, 0, b)),
                   pl.BlockSpec((None, None, 2, n_ctx_seq, LANE), lambda q, d, b: (q, d, 0, 0, b))],
        out_shape=[jax.ShapeDtypeStruct(loc.shape, F32),
                   jax.ShapeDtypeStruct((S5_Q, 2, 2, n_ctx_seq, S5_HALF), F32)],
        compiler_params=_params(("parallel", "parallel", "parallel"), 48),
        name="s5_scan",
    )(loc, a, x0)


def _s5c_kernel(u_ref, xp_ref, k_ref, b_ref, d_ref, z_ref, m_scr, wo_scr, z_scr):
    rt = u_ref.shape[0]

    @pl.when(pl.program_id(1) == 0)
    def _():
        for p in range(S5_P):
            lags = jnp.concatenate([_expand(k_ref[p, l], S5_PG, 4, 4) for l in range(N_LAG)]
                                   + [jnp.zeros((S5_PB, S5_PB), BF16)], axis=1)
            for t in range(SUB):
                lo = (SUB - 1 - t) * S5_PB
                m_scr[p, t * S5_PB:(t + 1) * S5_PB, :] = lags[:, lo:lo + S5_PFLAT]
            for k in range(4):
                wo_scr[p, k * S5_PHALF:(k + 1) * S5_PHALF, :] = jnp.concatenate(
                    [_expand(b_ref[p, k, t], S5_PG, 6, 4) for t in range(SUB)], axis=1)

    z = []
    for p in range(S5_P):
        u = u_ref[:, p * S5_PFLAT:(p + 1) * S5_PFLAT]
        xcat = jnp.concatenate([xp_ref[d, r, :, p * S5_PHALF:(p + 1) * S5_PHALF] for d in range(2) for r in range(2)],
                               axis=1).astype(BF16)
        y = _dot(u, m_scr[p]) + _dot(xcat, wo_scr[p])
        z.append(jax.nn.gelu(d_ref[:, p * S5_PFLAT:(p + 1) * S5_PFLAT] * u.astype(F32) + y))
    for t in range(SUB):
        z_scr[pl.ds(t, rt, stride=SUB), :] = jnp.concatenate([zp[:, t * S5_PB:(t + 1) * S5_PB] for zp in z], axis=1)
    z_ref[...] = z_scr[...].astype(z_ref.dtype)


def _s5_output(u4, xprev, k_lag, b_out, d4, cfg):
    rows = u4.shape[1]
    rt = cfg["s5_rows"]
    return pl.pallas_call(
        _s5c_kernel,
        grid=(S5_Q, rows // rt),
        in_specs=[pl.BlockSpec((None, rt, S5_FLAT), lambda q, i: (q, i, 0)),
                  pl.BlockSpec((None, 2, 2, rt, S5_HALF), lambda q, i: (q, 0, 0, i, 0)),
                  pl.BlockSpec((None, S5_P, N_LAG, S5_GROUP, S5_PB), lambda q, i: (q, 0, 0, 0, 0)),
                  pl.BlockSpec((None, S5_P, 4, SUB, S5_GROUP, S5_PHALF), lambda q, i: (q, 0, 0, 0, 0, 0)),
                  pl.BlockSpec((None, 1, S5_FLAT), lambda q, i: (q, 0, 0))],
        out_specs=pl.BlockSpec((rt * SUB, LANE), lambda q, i: (i, q)),
        out_shape=jax.ShapeDtypeStruct((rows * SUB, S5_WIDTH), BF16),
        scratch_shapes=[pltpu.VMEM((S5_P, S5_PFLAT, S5_PFLAT), BF16), pltpu.VMEM((S5_P, 4 * S5_PHALF, S5_PFLAT), BF16),
                        pltpu.VMEM((rt * SUB, LANE), F32)],
        compiler_params=_params(("parallel", "arbitrary"), 56),
        name="s5_output",
    )(u4, xprev, k_lag, b_out, d4)


def _group_norm(o, g, b):
    mu = jnp.mean(o, axis=-1, keepdims=True)
    var = jnp.mean(jnp.square(o - mu), axis=-1, keepdims=True)
    return (o - mu) * lax.rsqrt(var + GN_EPS) * g + b


def _ret_kernel(dec_ref, q_ref, k_ref, v_ref, g_ref, cos_ref, sin_ref, s0_ref, gg_ref, gb_ref, y_ref, *rest,
                nc, n_sub, use_rot, has_init, want_final):
    if want_final:
        sfin_ref, sf_scr, sb_scr, x_scr, kr_scr = rest
    else:
        sf_scr, sb_scr, x_scr, kr_scr = rest
    h = pl.program_id(1)
    row = _iota((CHUNK, CHUNK), 0).astype(F32)
    col = _iota((CHUNK, CHUNK), 1).astype(F32)
    lg_f = -jnp.exp(jnp.full((CHUNK, CHUNK), dec_ref[0, h], F32))
    lg_b = -jnp.exp(jnp.full((CHUNK, CHUNK), dec_ref[1, h], F32))
    lag = row - col
    scale = HEAD_DIM ** -0.5
    dmat = (jnp.where(lag >= 0, jnp.exp(lg_f * jnp.maximum(lag, 0.0)), 0.0)
            + jnp.where(lag <= 0, jnp.exp(lg_b * jnp.maximum(-lag, 0.0)), 0.0)) * scale
    qd_f = jnp.exp(lg_f * (row + 1.0))
    qd_b = jnp.exp(lg_b * (CHUNK - row))
    kd_f = jnp.exp(lg_f * (CHUNK - 1.0 - col)) * scale
    kd_b = jnp.exp(lg_b * col) * scale
    cd_f = jnp.exp(lg_f * CHUNK)
    cd_b = jnp.exp(lg_b * CHUNK)

    def chunk(s, j):
        return pl.ds(pl.multiple_of((s * nc + j) * CHUNK, CHUNK), CHUNK)

    def rot(ref, sl):
        x = ref[sl, :].astype(F32)
        if not use_rot:
            return x
        return x * cos_ref[sl, :] + pltpu.roll(x, HEAD_DIM // 2, 1) * sin_ref[sl, :]

    def local(s, j, carry):
        sl = chunk(s, j)
        k = rot(k_ref, sl)
        kr_scr[sl, :] = k.astype(BF16)
        k_t = k.T
        v = v_ref[sl, :]
        x_scr[0, j] = _dot((k_t * kd_f).astype(BF16), v)
        x_scr[1, j] = _dot((k_t * kd_b).astype(BF16), v)
        return carry

    zero = jnp.zeros((CHUNK, CHUNK), F32)

    def states(jj, carry):
        s_f, s_b = carry
        jf, jb = jj, nc - 1 - jj
        sf_scr[jf] = s_f.astype(BF16)
        sb_scr[jb] = s_b.astype(BF16)
        return s_f * cd_f + x_scr[0, jf], s_b * cd_b + x_scr[1, jb]

    def outputs(s, j, carry):
        sl = chunk(s, j)
        q = rot(q_ref, sl)
        att = _dot_nt(q.astype(BF16), kr_scr[sl, :]) * dmat
        lhs = jnp.concatenate([att.astype(BF16), (q * qd_f).astype(BF16), (q * qd_b).astype(BF16)], axis=1)
        rhs = jnp.concatenate([v_ref[sl, :], sf_scr[j], sb_scr[j]], axis=0)
        o = _dot(lhs, rhs)
        y = jax.nn.silu(g_ref[sl, :].astype(F32)) * _group_norm(o, gg_ref[...], gb_ref[...])
        y_ref[sl, :] = y.astype(y_ref.dtype)
        return carry

    def sequence(s, carry):
        lax.fori_loop(0, nc, functools.partial(local, s), 0, unroll=min(4, nc))
        init = (s0_ref[s, 0], s0_ref[s, 1]) if has_init else (zero, zero)
        s_f, s_b = lax.fori_loop(0, nc, states, init, unroll=2)
        if want_final:
            sfin_ref[s, 0] = s_f
            sfin_ref[s, 1] = s_b
        lax.fori_loop(0, nc, functools.partial(outputs, s), 0, unroll=min(8, nc))
        return carry

    if n_sub == 1:
        sequence(0, 0)
    else:
        lax.fori_loop(0, n_sub, sequence, 0)


def _retention(proj, dec, cos_t, sin_t, s0, gn_g, gn_b, *, n_seq, seq_len, n_sub, row0, use_rot, has_init,
               want_final):
    assert n_seq % n_sub == 0 and not (use_rot and n_sub > 1)
    nc = seq_len // CHUNK
    rows = n_sub * seq_len
    blk0 = row0 // rows
    kern = functools.partial(_ret_kernel, nc=nc, n_sub=n_sub, use_rot=use_rot, has_init=has_init, want_final=want_final)

    def tok(cb):
        return pl.BlockSpec((rows, HEAD_DIM), lambda s, h, cb=cb: (blk0 + s, cb + h))

    rot_spec = pl.BlockSpec((seq_len, HEAD_DIM), lambda s, h: (0, 0))
    st_spec = pl.BlockSpec((n_sub, 2, None, HEAD_DIM, HEAD_DIM), lambda s, h: (s, 0, h, 0, 0))
    gn_spec = pl.BlockSpec((1, HEAD_DIM), lambda s, h: (0, h))
    out_specs = [pl.BlockSpec((rows, HEAD_DIM), lambda s, h: (s, h))]
    out_shape = [jax.ShapeDtypeStruct((n_seq * seq_len, WIDTH), BF16)]
    if want_final:
        out_specs.append(st_spec)
        out_shape.append(jax.ShapeDtypeStruct((n_seq, 2, HEADS, HEAD_DIM, HEAD_DIM), F32))
    return pl.pallas_call(
        kern,
        grid=(n_seq // n_sub, HEADS),
        in_specs=[pl.BlockSpec(memory_space=pltpu.SMEM), tok(0), tok(4), tok(8), tok(12),
                  rot_spec, rot_spec, st_spec, gn_spec, gn_spec],
        out_specs=out_specs,
        out_shape=out_shape,
        scratch_shapes=[pltpu.VMEM((nc, HEAD_DIM, HEAD_DIM), BF16), pltpu.VMEM((nc, HEAD_DIM, HEAD_DIM), BF16),
                        pltpu.VMEM((2, nc, HEAD_DIM, HEAD_DIM), F32), pltpu.VMEM((rows, HEAD_DIM), BF16)],
        compiler_params=_params(("parallel", "parallel"), 48),
        name="retention",
    )(dec, proj, proj, proj, proj, cos_t, sin_t, s0, gn_g, gn_b)


def _gate_prep_kernel(g_ref, col_ref, row_ref):
    lane = _iota((CHUNK, LANE), 1)
    tri = jnp.where(_iota((CHUNK, CHUNK), 0) >= _iota((CHUNK, CHUNK), 1), 1.0, 0.0)
    for c in range(g_ref.shape[0] // CHUNK):
        sl = slice(c * CHUNK, (c + 1) * CHUNK)
        g = g_ref[sl, :]
        lf = jnp.where(lane < 4 * HEADS, jnp.minimum(g, 0.0) - jnp.log1p(jnp.exp(-jnp.abs(g))), 0.0)
        cs = _dot_hi(tri, lf)
        tot = cs[CHUNK - 1:CHUNK, :]
        bc = jnp.where(lane < 2 * HEADS, cs, tot - cs + lf)
        rest = jnp.where(lane < 2 * HEADS, tot - cs, cs - lf)
        pack = g + pltpu.roll(bc, GATE_BC, 1) + pltpu.roll(rest, GATE_REST, 1)
        col_ref[sl, :] = pack
        row_ref[sl, :] = pack.T


def _gate_prep(gates, cfg):
    t_tok = gates.shape[0]
    tm = cfg["tm_in"]
    spec = pl.BlockSpec((tm, LANE), lambda i: (i, 0))
    return pl.pallas_call(
        _gate_prep_kernel,
        grid=(t_tok // tm,),
        in_specs=[spec],
        out_specs=[spec, spec],
        out_shape=[jax.ShapeDtypeStruct((t_tok, LANE), F32)] * 2,
        compiler_params=_params(("parallel",), 32),
        name="gate_prep",
    )(gates)


def _mlstm_kernel(q_ref, k_ref, v_ref, o_ref, col_ref, row_ref, c0_ref, n0_ref, m0_ref, gg_ref, gb_ref, y_ref, *rest,
                  nc, n_sub, has_init, want_final):
    if want_final:
        cfin_ref, nfin_ref, mfin_ref, c_scr, n_scr, m_scr, x_scr, nl_scr, ml_scr, bl_scr = rest
    else:
        c_scr, n_scr, m_scr, x_scr, nl_scr, ml_scr, bl_scr = rest
    src = _iota((CHUNK, CHUNK), 0)
    dst = _iota((CHUNK, CHUNK), 1)
    scale = HEAD_DIM ** -0.5

    def chunk(s, j):
        return pl.ds(pl.multiple_of((s * nc + j) * CHUNK, CHUNK), CHUNK)

    def gate_idx(d, h):
        return d * 2 * HEADS + h, GATE_BC + d * 2 * HEADS + HEADS + h, GATE_REST + d * 2 * HEADS + HEADS + h

    def head_body(h):
        def init(s, d):
            if has_init:
                return c0_ref[s, d], n0_ref[s, d], m0_ref[s, d]
            return (jnp.zeros((HEAD_DIM, HEAD_DIM), F32), jnp.zeros((1, HEAD_DIM), F32),
                    jnp.zeros((1, HEAD_DIM), F32))

        def local(s, j, carry):
            sl = chunk(s, j)
            k = k_ref[sl, :]
            v_t = v_ref[sl, :].astype(F32).T
            rp = row_ref[sl, :]
            for d in range(2):
                ii, bi, ri = gate_idx(d, h)
                bc_row = rp[bi:bi + 1, :]
                b_last = bc_row[:, CHUNK - 1:CHUNK] if d == 0 else bc_row[:, 0:1]
                log_k = rp[ri:ri + 1, :] + rp[ii:ii + 1, :]
                m_loc = jnp.max(log_k, axis=1, keepdims=True)
                kw = jnp.exp(log_k - m_loc)
                x_scr[d, j] = _dot((v_t * kw).astype(BF16), k) * scale
                kw_hi = kw.astype(BF16).astype(F32)
                kw2 = jnp.concatenate([jnp.broadcast_to(kw_hi, (8, CHUNK)), jnp.broadcast_to(kw - kw_hi, (8, CHUNK))], 0)
                nl = _dot(kw2.astype(BF16), k)
                nl_scr[d, j] = (nl[0:1, :] + nl[8:9, :]) * scale
                ml_scr[d, j] = jnp.broadcast_to(m_loc, (1, HEAD_DIM))
                bl_scr[d, j] = jnp.broadcast_to(b_last, (1, HEAD_DIM))
            return carry

        def states(jj, carry):
            out = []
            for d, j in ((0, jj), (1, nc - 1 - jj)):
                cmat, nvec, m = carry[d]
                c_scr[d, j] = cmat.astype(BF16)
                n_scr[d, j] = nvec
                m_scr[d, j] = m
                m_new = jnp.maximum(bl_scr[d, j] + m, ml_scr[d, j])
                keep = jnp.exp(bl_scr[d, j] + m - m_new)
                add = jnp.exp(ml_scr[d, j] - m_new)
                out.append((keep * cmat + add * x_scr[d, j], keep * nvec + add * nl_scr[d, j], m_new))
            return tuple(out)

        def outputs(s, j, carry):
            sl = chunk(s, j)
            q = q_ref[sl, :]
            v = v_ref[sl, :]
            s_t = _dot_nt(k_ref[sl, :], q) * scale
            cp = col_ref[sl, :]
            rp = row_ref[sl, :]
            h_t = None
            for d in range(2):
                ii, bi, _ = gate_idx(d, h)
                causal = (src <= dst) if d == 0 else (src >= dst)
                bc_row = rp[bi:bi + 1, :]
                log_d = jnp.where(causal, bc_row + (cp[:, ii:ii + 1] - cp[:, bi:bi + 1]), -jnp.inf)
                log_prev = bc_row + m_scr[d, j]
                m_t = jnp.maximum(log_prev, jnp.max(log_d, axis=0, keepdims=True))
                w = s_t * jnp.exp(log_d - m_t)
                w_prev = jnp.exp(log_prev - m_t)
                qn = _dot_nt(jnp.broadcast_to(n_scr[d, j], (16, HEAD_DIM)).astype(BF16), q)[0:1, :]
                den = jnp.sum(w, axis=0, keepdims=True) + w_prev * qn
                inv = 1.0 / jnp.maximum(jnp.abs(den), jnp.exp(-m_t))
                num = _dot_tn(v, w.astype(BF16)) + _dot_nt(c_scr[d, j], q) * w_prev
                h_t = num * inv if h_t is None else h_t + num * inv
            y = jax.nn.sigmoid(o_ref[sl, :].astype(F32).T) * h_t
            mu = jnp.mean(y, axis=0, keepdims=True)
            var = jnp.mean(jnp.square(y - mu), axis=0, keepdims=True)
            y = (y - mu) * lax.rsqrt(var + GN_EPS) * gg_ref[...] + gb_ref[...]
            y_ref[sl, :] = y.T.astype(y_ref.dtype)
            return carry

        def sequence(s, carry):
            lax.fori_loop(0, nc, functools.partial(local, s), 0, unroll=min(4, nc))
            fin = lax.fori_loop(0, nc, states, (init(s, 0), init(s, 1)), unroll=2)
            if want_final:
                for d in range(2):
                    cfin_ref[s, d] = fin[d][0]
                    nfin_ref[s, d] = fin[d][1]
                    mfin_ref[s, d] = fin[d][2]
            lax.fori_loop(0, nc, functools.partial(outputs, s), 0, unroll=min(4, nc))
            return carry

        if n_sub == 1:
            sequence(0, 0)
        else:
            lax.fori_loop(0, n_sub, sequence, 0)

    hh = pl.program_id(1)
    for h in range(HEADS):
        pl.when(hh == h)(functools.partial(head_body, h))


def _mlstm(proj, gcol, grow, c0, n0, m0, gn_g, gn_b, *, n_seq, seq_len, n_sub, row0, has_init, want_final):
    assert n_seq % n_sub == 0
    nc = seq_len // CHUNK
    rows = n_sub * seq_len
    blk0 = row0 // rows
    kern = functools.partial(_mlstm_kernel, nc=nc, n_sub=n_sub, has_init=has_init, want_final=want_final)

    def tok(cb):
        return pl.BlockSpec((rows, HEAD_DIM), lambda s, h, cb=cb: (blk0 + s, cb + h))

    gate_spec = pl.BlockSpec((rows, LANE), lambda s, h: (blk0 + s, 0))
    c_spec = pl.BlockSpec((n_sub, 2, None, HEAD_DIM, HEAD_DIM), lambda s, h: (s, 0, h, 0, 0))
    v_spec = pl.BlockSpec((n_sub, 2, None, 1, HEAD_DIM), lambda s, h: (s, 0, h, 0, 0))
    gn_spec = pl.BlockSpec((HEAD_DIM, LANE), lambda s, h: (h, 0))
    out_specs = [pl.BlockSpec((rows, HEAD_DIM), lambda s, h: (s, h))]
    out_shape = [jax.ShapeDtypeStruct((n_seq * seq_len, WIDTH), BF16)]
    if want_final:
        out_specs += [c_spec, v_spec, v_spec]
        out_shape += [jax.ShapeDtypeStruct((n_seq, 2, HEADS, HEAD_DIM, HEAD_DIM), F32),
                      jax.ShapeDtypeStruct((n_seq, 2, HEADS, 1, HEAD_DIM), F32),
                      jax.ShapeDtypeStruct((n_seq, 2, HEADS, 1, HEAD_DIM), F32)]
    return pl.pallas_call(
        kern,
        grid=(n_seq // n_sub, HEADS),
        in_specs=[tok(16), tok(20), tok(24), tok(28), gate_spec, gate_spec,
                  c_spec, v_spec, v_spec, gn_spec, gn_spec],
        out_specs=out_specs,
        out_shape=out_shape,
        scratch_shapes=[pltpu.VMEM((2, nc, HEAD_DIM, HEAD_DIM), BF16)] + [pltpu.VMEM((2, nc, 1, HEAD_DIM), F32)] * 2
        + [pltpu.VMEM((2, nc, HEAD_DIM, HEAD_DIM), F32)] + [pltpu.VMEM((2, nc, 1, HEAD_DIM), F32)] * 3,
        compiler_params=_params(("parallel", "parallel"), 48),
        name="mlstm",
    )(proj, proj, proj, proj, gcol, grow, c0, n0, m0, gn_g, gn_b)


def _layer_norm(x, g, b):
    mu = jnp.mean(x, axis=-1, keepdims=True)
    var = jnp.mean(jnp.square(x - mu), axis=-1, keepdims=True)
    return (x - mu) * lax.rsqrt(var + LN_EPS) * g + b


def _top2_gates(h, whi_ref, wlo_ref, b_ref):
    lane = _iota((h.shape[0], LANE), 1)
    h_hi = h.astype(BF16)
    h_lo = (h - h_hi.astype(F32)).astype(BF16)
    logits = _dot(h_hi, whi_ref[...]) + _dot(h_hi, wlo_ref[...]) + _dot(h_lo, whi_ref[...]) + b_ref[...]
    logits = jnp.where(lane < N_EXPERTS, logits, NEG_BIG)
    m1 = jnp.max(logits, axis=1, keepdims=True)
    i1 = jnp.min(jnp.where(logits == m1, lane, LANE), axis=1, keepdims=True)
    rest = jnp.where(lane == i1, NEG_BIG, logits)
    m2 = jnp.max(rest, axis=1, keepdims=True)
    i2 = jnp.min(jnp.where(rest == m2, lane, LANE), axis=1, keepdims=True)
    e2 = jnp.exp(m2 - m1)
    den = 1.0 + e2
    return jnp.where(lane == i1, 1.0 / den, 0.0) + jnp.where(lane == i2, e2 / den, 0.0)


def _merge_kernel(*refs, n_ctx_tiles, with_router, split):
    if split:
        xc_ref, xs_ref = refs[:2]
    (x_ref, mod_ref, z_ref, ybc_ref, ybs_ref, ycc_ref, ycs_ref, wm_ref, bm_ref, wglu_ref, bglu_ref,
     wa_ref, wb_ref, wc_ref, wo_ref, lg_ref, lb_ref, *rest) = refs[1 if split else 0:]
    if with_router:
        rwh_ref, rwl_ref, rb_ref, o_ref, g_ref = rest
    else:
        o_ref, = rest
    is_ctx = pl.program_id(0) < n_ctx_tiles
    x = jnp.where(is_ctx, xc_ref[...], xs_ref[...]) if split else x_ref[...]
    h = (x * (1.0 + mod_ref[1:2, :]) + mod_ref[0:1, :]).astype(BF16)
    z = z_ref[...]
    ya = (z.astype(F32) * jax.nn.sigmoid(_dot(z, wglu_ref[...]) + bglu_ref[...])).astype(BF16)
    yb = jnp.where(is_ctx, ybc_ref[...], ybs_ref[...])
    yc = jnp.where(is_ctx, ycc_ref[...], ycs_ref[...])
    merged = None
    for j, (y, w_ref) in enumerate(((ya, wa_ref), (yb, wb_ref), (yc, wc_ref))):
        gate = jax.nn.sigmoid(_dot(h, wm_ref[:, j * D_MODEL:(j + 1) * D_MODEL]) + bm_ref[:, j * D_MODEL:(j + 1) * D_MODEL])
        term = gate * _dot(y, w_ref[...])
        merged = term if merged is None else merged + term
    mix = _dot(merged.astype(BF16), wo_ref[...])
    x1 = _layer_norm(ALPHA * x + mod_ref[2:3, :] * mix, lg_ref[...], lb_ref[...])
    o_ref[...] = x1
    if with_router:
        g_ref[...] = _top2_gates(x1 * (1.0 + mod_ref[4:5, :]) + mod_ref[3:4, :], rwh_ref, rwl_ref, rb_ref)


def _merge(x, mod_l, z, yb_c, yb_s, yc_c, yc_s, wm, bm, wglu, bglu, wa, wb, wc, wo, lg, lb, router, cfg):
    split = isinstance(x, tuple)
    xs = list(x) if split else [x]
    t_tok = sum(a.shape[0] for a in xs)
    tm = cfg["tm_merge"]
    n_ctx_tiles = cfg["n_ctx_tok"] // tm
    cond = functools.partial(_cond_row, tm=tm, n_ctx_tok=cfg["n_ctx_tok"], lat_len=cfg["lat_len"])

    def full(shape):
        return pl.BlockSpec(shape, lambda i: (0,) * len(shape))

    def tok(w):
        return pl.BlockSpec((tm, w), lambda i: (i, 0))

    ctx_spec, lat_spec = _group_specs(tm, WIDTH, n_ctx_tiles)
    x_specs = list(_group_specs(tm, D_MODEL, n_ctx_tiles)) if split else [tok(D_MODEL)]
    in_specs = x_specs + [pl.BlockSpec((None, 6, D_MODEL), lambda i: (cond(i), 0, 0)),
                tok(WIDTH), ctx_spec, lat_spec, ctx_spec, lat_spec,
                full((D_MODEL, 3 * D_MODEL)), full((1, 3 * D_MODEL)), full((WIDTH, WIDTH)), full((1, WIDTH)),
                full((WIDTH, D_MODEL)), full((WIDTH, D_MODEL)), full((WIDTH, D_MODEL)),
                full((D_MODEL, D_MODEL)), full((1, D_MODEL)), full((1, D_MODEL))]
    args = xs + [mod_l, z, yb_c, yb_s, yc_c, yc_s, wm, bm, wglu, bglu, wa, wb, wc, wo, lg, lb]
    out_specs = [tok(D_MODEL)]
    out_shape = [jax.ShapeDtypeStruct((t_tok, D_MODEL), F32)]
    if router is not None:
        in_specs += [full((D_MODEL, LANE)), full((D_MODEL, LANE)), full((1, LANE))]
        args += list(router)
        out_specs.append(tok(LANE))
        out_shape.append(jax.ShapeDtypeStruct((t_tok, LANE), F32))
    return pl.pallas_call(
        functools.partial(_merge_kernel, n_ctx_tiles=n_ctx_tiles, with_router=router is not None, split=split),
        grid=(t_tok // tm,),
        in_specs=in_specs,
        out_specs=out_specs,
        out_shape=out_shape,
        compiler_params=_params(("parallel",), 56),
        name="merge",
    )(*args)


def _ffn_kernel(x_ref, mod_ref, w1_ref, w3_ref, w2_ref, lg_ref, lb_ref, o_ref, h_scr, acc_scr):
    f = pl.program_id(1)

    @pl.when(f == 0)
    def _():
        h_scr[...] = (x_ref[...] * (1.0 + mod_ref[4:5, :]) + mod_ref[3:4, :]).astype(BF16)
        acc_scr[...] = jnp.zeros_like(acc_scr)

    h = h_scr[...]
    act = (jax.nn.silu(_dot(h, w1_ref[...])) * _dot(h, w3_ref[...])).astype(BF16)
    acc_scr[...] += _dot(act, w2_ref[...])

    @pl.when(f == pl.num_programs(1) - 1)
    def _():
        o_ref[...] = _layer_norm(ALPHA * x_ref[...] + mod_ref[5:6, :] * acc_scr[...], lg_ref[...], lb_ref[...])


def _dense_ffn(x, mod_l, w1, w3, w2, lg, lb, cfg):
    t_tok = x.shape[0]
    tm, tf = cfg["tm_ffn"], cfg["tf"]
    cond = functools.partial(_cond_row, tm=tm, n_ctx_tok=cfg["n_ctx_tok"], lat_len=cfg["lat_len"])
    return pl.pallas_call(
        _ffn_kernel,
        grid=(t_tok // tm, D_FF // tf),
        in_specs=[pl.BlockSpec((tm, D_MODEL), lambda i, f: (i, 0)),
                  pl.BlockSpec((None, 6, D_MODEL), lambda i, f: (cond(i), 0, 0)),
                  pl.BlockSpec((D_MODEL, tf), lambda i, f: (0, f)),
                  pl.BlockSpec((D_MODEL, tf), lambda i, f: (0, f)),
                  pl.BlockSpec((tf, D_MODEL), lambda i, f: (f, 0)),
                  pl.BlockSpec((1, D_MODEL), lambda i, f: (0, 0)),
                  pl.BlockSpec((1, D_MODEL), lambda i, f: (0, 0))],
        out_specs=pl.BlockSpec((tm, D_MODEL), lambda i, f: (i, 0)),
        out_shape=jax.ShapeDtypeStruct((t_tok, D_MODEL), F32),
        scratch_shapes=[pltpu.VMEM((tm, D_MODEL), BF16), pltpu.VMEM((tm, D_MODEL), F32)],
        compiler_params=_params(("parallel", "arbitrary"), 56),
        name="dense_ffn",
    )(x, mod_l, w1, w3, w2, lg, lb)


def _moe_kernel(x_ref, mod_ref, gate_ref, w1_ref, w3_ref, w2_ref, lg_ref, lb_ref, oc_ref, os_ref,
                h_scr, acc_scr, hc_scr, ob_scr, sp_scr, gt_scr, cnt_smem, *, n_ctx_tiles):
    e = pl.program_id(1)
    f = pl.program_id(2)
    last_f = pl.num_programs(2) - 1
    tm = x_ref.shape[0]

    @pl.when((e == 0) & (f == 0))
    def _():
        h_scr[...] = (x_ref[...] * (1.0 + mod_ref[4:5, :]) + mod_ref[3:4, :]).astype(BF16)
        acc_scr[...] = jnp.zeros_like(acc_scr)
        g = gate_ref[...]
        sel = g > 0.0
        ones = jnp.where(sel, 1.0, 0.0)
        before = jnp.where(_iota((tm, tm), 0) > _iota((tm, tm), 1), 1.0, 0.0).astype(BF16)
        pos = _dot(before, ones.astype(BF16))
        spt = jnp.where(sel, pos, -1.0).T
        gt = g.T
        cnt = jnp.sum(ones, axis=0, keepdims=True)
        for ee in range(N_EXPERTS):
            sp_scr[ee] = spt[ee:ee + 1, :]
            gt_scr[ee] = gt[ee:ee + 1, :]
            cnt_smem[ee] = cnt[0, ee].astype(jnp.int32)

    nb = (cnt_smem[e] + (MOE_BLK - 1)) // MOE_BLK
    row = _iota((MOE_BLK, tm), 0)

    def onehot(b):
        return sp_scr[e] == (row + b * MOE_BLK).astype(F32)

    @pl.when(f == 0)
    def _():
        def gather(b, carry):
            p = jnp.where(onehot(b), 1.0, 0.0).astype(BF16)
            hc_scr[b] = _dot(p, h_scr[...]).astype(BF16)
            ob_scr[b] = jnp.zeros((MOE_BLK, D_MODEL), F32)
            return carry
        lax.fori_loop(0, nb, gather, 0)

        @pl.when(nb % 2 == 1)
        def _():
            ob_scr[nb] = jnp.zeros((MOE_BLK, D_MODEL), F32)

    def ffn(hc):
        act = (jax.nn.silu(_dot(hc, w1_ref[...])) * _dot(hc, w3_ref[...])).astype(BF16)
        return _dot(act, w2_ref[...])

    def ffn_pair(p, carry):
        two = pl.ds(2 * p, 2)
        ob_scr[two] += ffn(hc_scr[two].reshape(2 * MOE_BLK, D_MODEL)).reshape(2, MOE_BLK, D_MODEL)
        return carry
    lax.fori_loop(0, nb // 2, ffn_pair, 0)

    @pl.when(nb % 2 == 1)
    def _():
        ob_scr[nb - 1] += ffn(hc_scr[nb - 1])

    @pl.when(f == last_f)
    def _():
        row2 = _iota((2 * MOE_BLK, tm), 0)

        def scatter(p, carry):
            m = sp_scr[e] == (row2 + p * (2 * MOE_BLK)).astype(F32)
            gc = jnp.sum(jnp.where(m, gt_scr[e], 0.0), axis=1, keepdims=True)
            og = (ob_scr[pl.ds(2 * p, 2)].reshape(2 * MOE_BLK, D_MODEL) * gc).astype(BF16)
            acc_scr[...] += _dot_tn(jnp.where(m, 1.0, 0.0).astype(BF16), og)
            return carry
        lax.fori_loop(0, (nb + 1) // 2, scatter, 0)

    last = (e == pl.num_programs(1) - 1) & (f == last_f)
    is_ctx = pl.program_id(0) < n_ctx_tiles
    for o_ref, mine in ((oc_ref, is_ctx), (os_ref, jnp.logical_not(is_ctx))):
        @pl.when(last & mine)
        def _():
            o_ref[...] = _layer_norm(ALPHA * x_ref[...] + mod_ref[5:6, :] * acc_scr[...], lg_ref[...], lb_ref[...])


def _moe_ffn(x, mod_l, gates, w1, w3, w2, layer, lg, lb, cfg):
    t_tok = x.shape[0]
    tm, tf = cfg["tm_moe"], cfg["tf"]
    n_ctx_tiles = cfg["n_ctx_tok"] // tm
    cond = functools.partial(_cond_row, tm=tm, n_ctx_tok=cfg["n_ctx_tok"], lat_len=cfg["lat_len"])
    return pl.pallas_call(
        functools.partial(_moe_kernel, n_ctx_tiles=n_ctx_tiles),
        grid=(t_tok // tm, N_EXPERTS, D_FF // tf),
        in_specs=[pl.BlockSpec((tm, D_MODEL), lambda i, e, f: (i, 0), pipeline_mode=pl.Buffered(1)),
                  pl.BlockSpec((None, 6, D_MODEL), lambda i, e, f: (cond(i), 0, 0)),
                  pl.BlockSpec((tm, LANE), lambda i, e, f: (i, 0), pipeline_mode=pl.Buffered(1)),
                  pl.BlockSpec((None, None, D_MODEL, tf), lambda i, e, f: (layer, e, 0, f)),
                  pl.BlockSpec((None, None, D_MODEL, tf), lambda i, e, f: (layer, e, 0, f)),
                  pl.BlockSpec((None, None, tf, D_MODEL), lambda i, e, f: (layer, e, f, 0)),
                  pl.BlockSpec((1, D_MODEL), lambda i, e, f: (0, 0)),
                  pl.BlockSpec((1, D_MODEL), lambda i, e, f: (0, 0))],
        out_specs=list(_group_specs(tm, D_MODEL, n_ctx_tiles)),
        out_shape=[jax.ShapeDtypeStruct((cfg["n_ctx_tok"], D_MODEL), F32),
                   jax.ShapeDtypeStruct((t_tok - cfg["n_ctx_tok"], D_MODEL), F32)],
        scratch_shapes=[pltpu.VMEM((tm, D_MODEL), BF16), pltpu.VMEM((tm, D_MODEL), F32),
                        pltpu.VMEM((tm // MOE_BLK, MOE_BLK, D_MODEL), BF16),
                        pltpu.VMEM((tm // MOE_BLK, MOE_BLK, D_MODEL), F32),
                        pltpu.VMEM((N_EXPERTS, 1, tm), F32), pltpu.VMEM((N_EXPERTS, 1, tm), F32),
                        pltpu.SMEM((N_EXPERTS,), jnp.int32)],
        compiler_params=_params(("arbitrary", "arbitrary", "arbitrary"), 56),
        name="moe_ffn",
    )(x, mod_l, gates, w1, w3, w2, lg, lb)


def _rotary_tables(n_tok):
    rows = n_tok // GRID_W
    r = jnp.repeat(jnp.arange(rows, dtype=F32), GRID_W)
    col = jnp.tile(jnp.arange(GRID_W, dtype=F32), rows)
    n_freq = HEAD_DIM // 4
    inv = ROPE_BASE ** (-jnp.arange(n_freq, dtype=F32) / n_freq)
    ang = jnp.concatenate([r[:, None] * inv, col[:, None] * inv], -1)
    cos, sin = jnp.cos(ang), jnp.sin(ang)
    return jnp.concatenate([cos, cos], -1), jnp.concatenate([-sin, sin], -1)


def _row2(v):
    return v.reshape(1, -1).astype(F32)


def kernel(x_prompt, x_sample, cache_s5_re, cache_s5_im, cache_ret, cache_ml_c, cache_ml_n, cache_ml_m, c, c_ctx, ada_w, ada_b, w_in, b_in, s5_lam_re, s5_lam_im, s5_log_step, s5_b_re, s5_b_im, s5_c_re, s5_c_im, s5_d, s5_glu_w, s5_glu_b, ret_decay, ret_gn_g, ret_gn_b, ml_gn_g, ml_gn_b, w_a, w_b, w_c, w_o, ln1_g, ln1_b, ln2_g, ln2_b, ffn_w1, ffn_w3, ffn_w2, moe_router, moe_router_b, moe_w1, moe_w3, moe_w2):
    n_ctx_seq, ctx_len, _ = x_prompt.shape
    n_lat_seq, lat_len, _ = x_sample.shape
    n_ctx_tok = n_ctx_seq * ctx_len
    n_lat_tok = n_lat_seq * lat_len
    t_tok = n_ctx_tok + n_lat_tok
    assert n_lat_seq + 1 <= N_COND and n_ctx_seq % 8 == 0 and n_lat_seq % 8 == 0
    assert ctx_len % CHUNK == 0 and lat_len % CHUNK == 0 and n_ctx_tok % lat_len == 0
    cfg = dict(n_ctx_seq=n_ctx_seq, ctx_len=ctx_len, n_lat_seq=n_lat_seq, lat_len=lat_len, n_ctx_tok=n_ctx_tok,
               tm_in=min(2048, lat_len), tm_merge=min(512, lat_len), tm_ffn=min(512, lat_len), tm_moe=min(1024, lat_len), tf=1408,
               s5_rows=min(256, t_tok // SUB // 8))

    x = (x_prompt.reshape(n_ctx_tok, D_MODEL), x_sample.reshape(n_lat_tok, D_MODEL))
    cond = jnp.zeros((N_COND, D_MODEL), F32).at[0].set(c_ctx).at[1:1 + n_lat_seq].set(c)
    mod = _modulation(cond, ada_w, ada_b).reshape(DEPTH, N_COND, 6, D_MODEL)
    cos_t, sin_t = _rotary_tables(lat_len)

    n_main = S5_WIDTH + 8 * WIDTH
    gate_off = n_main
    merge_off = gate_off + 4 * HEADS
    s5_fac = jax.vmap(_s5_factors)(s5_lam_re, s5_lam_im, s5_log_step, s5_b_re, s5_b_im, s5_c_re, s5_c_im)

    st_s5, st_ret, st_c, st_n, st_m = [], [], [], [], []
    zero_ret = jnp.zeros((n_ctx_seq, 2, HEADS, HEAD_DIM, HEAD_DIM), F32)
    zero_vec = jnp.zeros((n_ctx_seq, 2, HEADS, 1, HEAD_DIM), F32)
    for l in range(DEPTH):
        mod_l = mod[l]
        w_main = w_in[l][:, :n_main].astype(BF16)
        b_main = _row2(b_in[l][:n_main])
        w_gate = jnp.zeros((D_MODEL, LANE), F32).at[:, :4 * HEADS].set(w_in[l][:, gate_off:merge_off]).astype(BF16)
        b_gate = jnp.zeros((1, LANE), F32).at[0, :4 * HEADS].set(b_in[l][gate_off:merge_off])
        proj, u4, gates = _inproj(x, mod_l, w_main, b_main, w_gate, b_gate, cfg)

        s5_ain, s5_klag, s5_bout, s5_a = (m[l] for m in s5_fac)
        loc = _s5_state_in(u4, s5_ain, cfg)
        x0 = jnp.stack([cache_s5_re[:, l], cache_s5_im[:, l]], 0)
        x0 = x0.reshape(2, n_lat_seq, 2, S5_Q, S5_HALF).transpose(3, 2, 0, 1, 4).astype(F32)
        xprev, s5_fin = _s5_scan(loc, s5_a, x0, cfg)
        d4 = jnp.broadcast_to(s5_d[l].astype(F32).reshape(S5_Q, S5_P, 1, S5_PB), (S5_Q, S5_P, SUB, S5_PB))
        d4 = d4.reshape(S5_Q, 1, S5_FLAT)
        z = _s5_output(u4, xprev, s5_klag, s5_bout, d4, cfg)
        st_s5.append(s5_fin)

        gg, gb = _row2(ret_gn_g[l]), _row2(ret_gn_b[l])
        dec = ret_decay[l].astype(F32)
        yb_c, ret_fin = _retention(proj, dec, cos_t, sin_t, zero_ret, gg, gb, n_seq=n_ctx_seq, seq_len=ctx_len,
                                   n_sub=8, row0=0, use_rot=False, has_init=False, want_final=True)
        yb_s, = _retention(proj, dec, cos_t, sin_t, cache_ret[:, l].astype(F32), gg, gb, n_seq=n_lat_seq,
                           seq_len=lat_len, n_sub=1, row0=n_ctx_tok, use_rot=True, has_init=True, want_final=False)
        st_ret.append(ret_fin)

        gg = jnp.broadcast_to(ml_gn_g[l].astype(F32)[:, None], (WIDTH, LANE))
        gb = jnp.broadcast_to(ml_gn_b[l].astype(F32)[:, None], (WIDTH, LANE))
        gcol, grow = _gate_prep(gates, cfg)
        yc_c, c_fin, n_fin, m_fin = _mlstm(proj, gcol, grow, zero_ret, zero_vec, zero_vec, gg, gb, n_seq=n_ctx_seq,
                                           seq_len=ctx_len, n_sub=8, row0=0, has_init=False, want_final=True)
        n0 = cache_ml_n[:, l].astype(F32)[:, :, :, None, :]
        m0 = jnp.broadcast_to(cache_ml_m[:, l].astype(F32)[:, :, :, None, None], n0.shape)
        yc_s, = _mlstm(proj, gcol, grow, cache_ml_c[:, l].astype(F32), n0, m0, gg, gb, n_seq=n_lat_seq,
                       seq_len=lat_len, n_sub=1, row0=n_ctx_tok, has_init=True, want_final=False)
        st_c.append(c_fin)
        st_n.append(n_fin[:, :, :, 0, :])
        st_m.append(m_fin[:, :, :, 0, 0])

        j = l // 2
        router = None
        if l % 2 == 1:
            rw = jnp.zeros((D_MODEL, LANE), F32).at[:, :N_EXPERTS].set(moe_router[j])
            rw_hi = rw.astype(BF16)
            router = (rw_hi, (rw - rw_hi.astype(F32)).astype(BF16),
                      jnp.zeros((1, LANE), F32).at[0, :N_EXPERTS].set(moe_router_b[j]))
        merged = _merge(x, mod_l, z, yb_c, yb_s, yc_c, yc_s, w_in[l][:, merge_off:].astype(BF16),
                        _row2(b_in[l][merge_off:]), s5_glu_w[l].astype(BF16), _row2(s5_glu_b[l]), w_a[l].astype(BF16),
                        w_b[l].astype(BF16), w_c[l].astype(BF16), w_o[l].astype(BF16), _row2(ln1_g[l]),
                        _row2(ln1_b[l]), router, cfg)

        if router is None:
            x = _dense_ffn(merged[0], mod_l, ffn_w1[j].astype(BF16), ffn_w3[j].astype(BF16), ffn_w2[j].astype(BF16),
                           _row2(ln2_g[l]), _row2(ln2_b[l]), cfg)
        else:
            x = tuple(_moe_ffn(merged[0], mod_l, merged[1], moe_w1.astype(BF16), moe_w3.astype(BF16),
                               moe_w2.astype(BF16), j, _row2(ln2_g[l]), _row2(ln2_b[l]), cfg))

    y_p = x[0].reshape(n_ctx_seq, ctx_len, D_MODEL)
    y_s = x[1].reshape(n_lat_seq, lat_len, D_MODEL)
    s5 = jnp.stack(st_s5, 0)
    s5 = s5.reshape(DEPTH, S5_Q, 2, 2, n_ctx_seq, S5_QG, S5_STATE).transpose(3, 4, 0, 2, 1, 5, 6)
    s5 = s5.reshape(2, n_ctx_seq, DEPTH, 2, S5_GROUPS, S5_STATE)
    return (y_p, y_s, s5[0], s5[1], jnp.stack(st_ret, 1), jnp.stack(st_c, 1), jnp.stack(st_n, 1),
            jnp.stack(st_m, 1))
```

```python
import functools

import jax
import jax.numpy as jnp
from jax import lax
from jax.experimental import pallas as pl
from jax.experimental.pallas import tpu as pltpu

F32 = jnp.float32
BF16 = jnp.bfloat16

D_MODEL = 1024
DEPTH = 4
GRID_W = 64
CHUNK = 128
S5_WIDTH = 512
S5_GROUP = 16
S5_GROUPS = 32
S5_STATE = 64
HEADS = 4
HEAD_DIM = 128
WIDTH = 512
ROPE_BASE = 10000.0
D_FF = 2816
N_EXPERTS = 8
ALPHA = (2.0 * DEPTH) ** 0.25
LN_EPS = 1e-5
GN_EPS = 1e-5
N_COND = 16
SUB = 16
N_LAG = 2 * SUB - 1
LANE = 128
S5_Q = S5_WIDTH // LANE
S5_QG = LANE // S5_GROUP
S5_FLAT = SUB * LANE
S5_HALF = S5_QG * S5_STATE
S5_ST = 4 * S5_HALF
S5_PG = 2
S5_P = S5_QG // S5_PG
S5_PB = S5_PG * S5_GROUP
S5_PFLAT = SUB * S5_PB
S5_PHALF = S5_PG * S5_STATE
NEG_BIG = -1e30
MOE_BLK = 128
GATE_BC = 16
GATE_REST = 32


def _dot(a, b):
    return jnp.dot(a, b, preferred_element_type=F32)


def _dot_hi(a, b):
    return jnp.dot(a, b, preferred_element_type=F32, precision=lax.Precision.HIGHEST)


def _dot_nt(a, b):
    return lax.dot_general(a, b, (((1,), (1,)), ((), ())), preferred_element_type=F32)


def _dot_tn(a, b):
    return lax.dot_general(a, b, (((0,), (0,)), ((), ())), preferred_element_type=F32)


def _params(sem, vmem_mb):
    return pltpu.CompilerParams(dimension_semantics=sem, vmem_limit_bytes=vmem_mb << 20)


def _cond_row(tile, tm, n_ctx_tok, lat_len):
    start = tile * tm
    return jnp.where(start < n_ctx_tok, 0, 1 + (start - n_ctx_tok) // lat_len)


def _iota(shape, axis):
    return lax.broadcasted_iota(jnp.int32, shape, axis)


def _group_specs(tm, width, n_ctx_tiles):
    return (pl.BlockSpec((tm, width), lambda i, *_: (jnp.minimum(i, n_ctx_tiles - 1), 0)),
            pl.BlockSpec((tm, width), lambda i, *_: (jnp.maximum(i - n_ctx_tiles, 0), 0)))


def _mod_kernel(c_ref, w_ref, b_ref, o_ref):
    o_ref[...] = _dot_hi(jax.nn.silu(c_ref[...]), w_ref[...]) + b_ref[...]


def _modulation(cond, ada_w, ada_b):
    tn = 1536
    n = ada_w.shape[-1]
    return pl.pallas_call(
        _mod_kernel,
        grid=(DEPTH, n // tn),
        in_specs=[pl.BlockSpec((N_COND, D_MODEL), lambda l, j: (0, 0)),
                  pl.BlockSpec((None, D_MODEL, tn), lambda l, j: (l, 0, j)),
                  pl.BlockSpec((None, 1, tn), lambda l, j: (l, 0, j))],
        out_specs=pl.BlockSpec((None, N_COND, tn), lambda l, j: (l, 0, j)),
        out_shape=jax.ShapeDtypeStruct((DEPTH, N_COND, n), F32),
        compiler_params=_params(("parallel", "parallel"), 40),
        name="modulation",
    )(cond, ada_w, ada_b.reshape(DEPTH, 1, n))


def _inproj_kernel(*refs, n_ctx_tiles, split):
    if split:
        xc_ref, xs_ref = refs[:2]
    x_ref, mod_ref, w_ref, b_ref, wg_ref, bg_ref, o_ref, u4_ref, g_ref, h_scr, u_scr = refs[1 if split else 0:]
    j = pl.program_id(1)
    tm = x_ref.shape[0]

    @pl.when(j == 0)
    def _():
        x = jnp.where(pl.program_id(0) < n_ctx_tiles, xc_ref[...], xs_ref[...]) if split else x_ref[...]
        h = (x * (1.0 + mod_ref[1:2, :]) + mod_ref[0:1, :]).astype(BF16)
        h_scr[...] = h
        g_ref[...] = _dot(h, wg_ref[...]) + bg_ref[...]
        u = _dot(h, w_ref[...]) + b_ref[...]
        for q in range(S5_Q):
            u_scr[q] = u[:, q * LANE:(q + 1) * LANE]
            rows = [u_scr[q, pl.ds(t, tm // SUB, stride=SUB), :].astype(BF16) for t in range(SUB)]
            for p in range(S5_P):
                u4_ref[q, :, p * S5_PFLAT:(p + 1) * S5_PFLAT] = jnp.concatenate(
                    [r[:, p * S5_PB:(p + 1) * S5_PB] for r in rows], axis=1)

    @pl.when(j > 0)
    def _():
        o_ref[...] = (_dot(h_scr[...], w_ref[...]) + b_ref[...]).astype(o_ref.dtype)


def _inproj(x, mod_l, w, b, wg, bg, cfg):
    split = isinstance(x, tuple)
    xs = list(x) if split else [x]
    t_tok = sum(a.shape[0] for a in xs)
    tm, tn = cfg["tm_in"] // (2 if split else 1), 512
    nj = w.shape[1] // tn
    n_ctx_tiles = cfg["n_ctx_tok"] // tm
    cond = functools.partial(_cond_row, tm=tm, n_ctx_tok=cfg["n_ctx_tok"], lat_len=cfg["lat_len"])
    x_specs = list(_group_specs(tm, D_MODEL, n_ctx_tiles)) if split else [pl.BlockSpec((tm, D_MODEL), lambda i, j: (i, 0))]
    return pl.pallas_call(
        functools.partial(_inproj_kernel, n_ctx_tiles=n_ctx_tiles, split=split),
        grid=(t_tok // tm, nj),
        in_specs=x_specs + [
                  pl.BlockSpec((None, 6, D_MODEL), lambda i, j: (cond(i), 0, 0)),
                  pl.BlockSpec((D_MODEL, tn), lambda i, j: (0, j)),
                  pl.BlockSpec((1, tn), lambda i, j: (0, j)),
                  pl.BlockSpec((D_MODEL, LANE), lambda i, j: (0, 0)),
                  pl.BlockSpec((1, LANE), lambda i, j: (0, 0))],
        out_specs=[pl.BlockSpec((tm, tn), lambda i, j: (i, jnp.maximum(j - 1, 0))),
                   pl.BlockSpec((S5_Q, tm // SUB, S5_FLAT), lambda i, j: (0, i, 0)),
                   pl.BlockSpec((tm, LANE), lambda i, j: (i, 0))],
        out_shape=[jax.ShapeDtypeStruct((t_tok, (nj - 1) * tn), BF16),
                   jax.ShapeDtypeStruct((S5_Q, t_tok // SUB, S5_FLAT), BF16),
                   jax.ShapeDtypeStruct((t_tok, LANE), F32)],
        scratch_shapes=[pltpu.VMEM((tm, D_MODEL), BF16), pltpu.VMEM((S5_Q, tm, LANE), F32)],
        compiler_params=_params(("parallel", "arbitrary"), 48),
        name="inproj",
    )(*xs, mod_l, w, b, wg, bg)


def _s5_factors(lam_re, lam_im, log_step, b_re, b_im, c_re, c_im):
    lam = lax.complex(lam_re.astype(F32), lam_im.astype(F32))
    lam_dt = lam * jnp.exp(log_step.astype(F32))[..., None]
    lam_bar = jnp.exp(lam_dt)
    bbar = ((lam_bar - 1.0) / lam)[..., None] * lax.complex(b_re.astype(F32), b_im.astype(F32))
    cmat = lax.complex(c_re.astype(F32), c_im.astype(F32))
    ks = jnp.arange(SUB + 1, dtype=F32)
    pw = jnp.exp(lam_dt[None] * ks[:, None, None, None])
    kern = jnp.einsum('dgcp,tdgp,dgpe->dgtce', cmat, pw[:SUB], bbar).real
    pad = jnp.zeros_like(kern[0][:, :SUB - 1])
    ktab = jnp.concatenate([pad, kern[0]], 1) + jnp.concatenate([kern[1][:, ::-1], pad], 1)
    k_lag = ktab.reshape(S5_Q, S5_QG, N_LAG, S5_GROUP, S5_GROUP).transpose(0, 2, 3, 1, 4)
    k_lag = k_lag.reshape(S5_Q, N_LAG, S5_GROUP, S5_P, S5_PB).transpose(0, 3, 1, 2, 4)
    pw_in = jnp.stack([pw[:SUB][::-1, 0], pw[:SUB][:, 1]], 0)
    wb = pw_in[..., None] * bbar[:, None]
    wb = jnp.stack([wb.real, wb.imag], 1).reshape(2, 2, SUB, S5_Q, S5_QG, S5_STATE, S5_GROUP)
    a_in = wb.transpose(3, 2, 0, 1, 5, 4, 6).reshape(S5_Q, SUB, 4, S5_STATE, S5_P, S5_PB).transpose(0, 4, 1, 2, 3, 5)
    pw_out = jnp.stack([pw[1:, 0], pw[1:][::-1, 1]], 0)
    ce = cmat[:, None] * pw_out[:, :, :, None, :]
    ce = jnp.stack([ce.real, -ce.imag], 1).reshape(2, 2, SUB, S5_Q, S5_QG, S5_GROUP, S5_STATE)
    b_out = ce.transpose(3, 0, 1, 2, 5, 4, 6).reshape(S5_Q, 4, SUB, S5_GROUP, S5_P, S5_PHALF).transpose(0, 4, 1, 2, 3, 5)
    a = pw[SUB]
    a = jnp.stack([a.real, a.imag], 1).reshape(2, 2, S5_Q, 1, S5_HALF).transpose(2, 0, 1, 3, 4)
    return a_in.astype(BF16), k_lag.astype(BF16), b_out.astype(BF16), a


def _expand(src_t, n_rep, row_shift, col_shift):
    k, r = src_t.shape
    rep = jnp.where(_iota((k, n_rep * k), 0) == (_iota((k, n_rep * k), 1) & (k - 1)), 1.0, 0.0).astype(BF16)
    same = (_iota((r, n_rep * k), 0) >> row_shift) == (_iota((r, n_rep * k), 1) >> col_shift)
    return jnp.where(same, _dot_tn(src_t, rep), 0.0).astype(BF16)


def _s5a_kernel(u_ref, a_ref, o_ref, w_scr):
    @pl.when(pl.program_id(1) == 0)
    def _():
        for p in range(S5_P):
            for k in range(4):
                w_scr[p, :, k * S5_PHALF:(k + 1) * S5_PHALF] = jnp.concatenate(
                    [_expand(a_ref[p, t, k], S5_PG, 4, 6) for t in range(SUB)], axis=0)

    for p in range(S5_P):
        res = _dot(u_ref[:, p * S5_PFLAT:(p + 1) * S5_PFLAT], w_scr[p])
        for d in range(2):
            for r in range(2):
                k = 2 * d + r
                o_ref[d, r, :, p * S5_PHALF:(p + 1) * S5_PHALF] = res[:, k * S5_PHALF:(k + 1) * S5_PHALF]


def _s5_state_in(u4, a_in, cfg):
    rows = u4.shape[1]
    rt = cfg["s5_rows"]
    return pl.pallas_call(
        _s5a_kernel,
        grid=(S5_Q, rows // rt),
        in_specs=[pl.BlockSpec((None, rt, S5_FLAT), lambda q, i: (q, i, 0)),
                  pl.BlockSpec((None, S5_P, SUB, 4, S5_STATE, S5_PB), lambda q, i: (q, 0, 0, 0, 0, 0))],
        out_specs=pl.BlockSpec((None, 2, 2, rt, S5_HALF), lambda q, i: (q, 0, 0, i, 0)),
        out_shape=jax.ShapeDtypeStruct((S5_Q, 2, 2, rows, S5_HALF), F32),
        scratch_shapes=[pltpu.VMEM((S5_P, S5_PFLAT, 4 * S5_PHALF), BF16)],
        compiler_params=_params(("parallel", "arbitrary"), 48),
        name="s5_state_in",
    )(u4, a_in)


def _s5b_kernel(loc_ref, a_ref, x0_ref, xp_ref, fin_ref, *, n_ctx_seq, ctx_sub, n_lat_seq, lat_sub):
    d = pl.program_id(1)
    ar = jnp.broadcast_to(a_ref[0], (8, LANE))
    ai = jnp.broadcast_to(a_ref[1], (8, LANE))

    def run(base, nsub, xr0, xi0):
        def body(jj, carry):
            xr, xi = carry
            j = jnp.where(d == 0, jj, nsub - 1 - jj)
            idx = pl.ds(base + j, 8, stride=nsub)
            xp_ref[0, idx, :] = xr
            xp_ref[1, idx, :] = xi
            lr = loc_ref[0, idx, :]
            li = loc_ref[1, idx, :]
            return ar * xr - ai * xi + lr, ar * xi + ai * xr + li
        return lax.fori_loop(0, nsub, body, (xr0, xi0))

    zero = jnp.zeros((8, LANE), F32)
    for bg in range(n_ctx_seq // 8):
        xr, xi = run(bg * 8 * ctx_sub, ctx_sub, zero, zero)
        fin_ref[0, bg * 8:(bg + 1) * 8, :] = xr
        fin_ref[1, bg * 8:(bg + 1) * 8, :] = xi
    for bg in range(n_lat_seq // 8):
        run(n_ctx_seq * ctx_sub + bg * 8 * lat_sub, lat_sub, x0_ref[0, bg * 8:(bg + 1) * 8, :],
            x0_ref[1, bg * 8:(bg + 1) * 8, :])


def _s5_scan(loc, a, x0, cfg):
    rows = loc.shape[3]
    n_ctx_seq, n_lat_seq = cfg["n_ctx_seq"], cfg["n_lat_seq"]
    kern = functools.partial(_s5b_kernel, n_ctx_seq=n_ctx_seq, ctx_sub=cfg["ctx_len"] // SUB,
                             n_lat_seq=n_lat_seq, lat_sub=cfg["lat_len"] // SUB)
    nlb = S5_HALF // LANE
    return pl.pallas_call(
        kern,
        grid=(S5_Q, 2, nlb),
        in_specs=[pl.BlockSpec((None, None, 2, rows, LANE), lambda q, d, b: (q, d, 0, 0, b)),
                  pl.BlockSpec((None, None, 2, 1, LANE), lambda q, d, b: (q, d, 0, 0, b)),
                  pl.BlockSpec((None, None, 2, n_lat_seq, LANE), lambda q, d, b: (q, d, 0, 0, b))],
        out_specs=[pl.BlockSpec((None, None, 2, rows, LANE), lambda q, d, b: (q, d, 0, 0, b)),
                   pl.BlockSpec((None, None, 2, n_ctx_seq, LANE), lambda q, d, b: (q, d, 0, 0, b))],
        out_shape=[jax.ShapeDtypeStruct(loc.shape, F32),
                   jax.ShapeDtypeStruct((S5_Q, 2, 2, n_ctx_seq, S5_HALF), F32)],
        compiler_params=_params(("parallel", "parallel", "parallel"), 48),
        name="s5_scan",
    )(loc, a, x0)


def _s5c_kernel(u_ref, xp_ref, k_ref, b_ref, d_ref, z_ref, m_scr, wo_scr, z_scr):
    rt = u_ref.shape[0]

    @pl.when(pl.program_id(1) == 0)
    def _():
        for p in range(S5_P):
            lags = jnp.concatenate([_expand(k_ref[p, l], S5_PG, 4, 4) for l in range(N_LAG)]
                                   + [jnp.zeros((S5_PB, S5_PB), BF16)], axis=1)
            for t in range(SUB):
                lo = (SUB - 1 - t) * S5_PB
                m_scr[p, t * S5_PB:(t + 1) * S5_PB, :] = lags[:, lo:lo + S5_PFLAT]
            for k in range(4):
                wo_scr[p, k * S5_PHALF:(k + 1) * S5_PHALF, :] = jnp.concatenate(
                    [_expand(b_ref[p, k, t], S5_PG, 6, 4) for t in range(SUB)], axis=1)

    z = []
    for p in range(S5_P):
        u = u_ref[:, p * S5_PFLAT:(p + 1) * S5_PFLAT]
        xcat = jnp.concatenate([xp_ref[d, r, :, p * S5_PHALF:(p + 1) * S5_PHALF] for d in range(2) for r in range(2)],
                               axis=1).astype(BF16)
        y = _dot(u, m_scr[p]) + _dot(xcat, wo_scr[p])
        z.append(jax.nn.gelu(d_ref[:, p * S5_PFLAT:(p + 1) * S5_PFLAT] * u.astype(F32) + y))
    for t in range(SUB):
        z_scr[pl.ds(t, rt, stride=SUB), :] = jnp.concatenate([zp[:, t * S5_PB:(t + 1) * S5_PB] for zp in z], axis=1)
    z_ref[...] = z_scr[...].astype(z_ref.dtype)


def _s5_output(u4, xprev, k_lag, b_out, d4, cfg):
    rows = u4.shape[1]
    rt = cfg["s5_rows"]
    return pl.pallas_call(
        _s5c_kernel,
        grid=(S5_Q, rows // rt),
        in_specs=[pl.BlockSpec((None, rt, S5_FLAT), lambda q, i: (q, i, 0)),
                  pl.BlockSpec((None, 2, 2, rt, S5_HALF), lambda q, i: (q, 0, 0, i, 0)),
                  pl.BlockSpec((None, S5_P, N_LAG, S5_GROUP, S5_PB), lambda q, i: (q, 0, 0, 0, 0)),
                  pl.BlockSpec((None, S5_P, 4, SUB, S5_GROUP, S5_PHALF), lambda q, i: (q, 0, 0, 0, 0, 0)),
                  pl.BlockSpec((None, 1, S5_FLAT), lambda q, i: (q, 0, 0))],
        out_specs=pl.BlockSpec((rt * SUB, LANE), lambda q, i: (i, q)),
        out_shape=jax.ShapeDtypeStruct((rows * SUB, S5_WIDTH), BF16),
        scratch_shapes=[pltpu.VMEM((S5_P, S5_PFLAT, S5_PFLAT), BF16), pltpu.VMEM((S5_P, 4 * S5_PHALF, S5_PFLAT), BF16),
                        pltpu.VMEM((rt * SUB, LANE), F32)],
        compiler_params=_params(("parallel", "arbitrary"), 56),
        name="s5_output",
    )(u4, xprev, k_lag, b_out, d4)


def _group_norm(o, g, b):
    mu = jnp.mean(o, axis=-1, keepdims=True)
    var = jnp.mean(jnp.square(o - mu), axis=-1, keepdims=True)
    return (o - mu) * lax.rsqrt(var + GN_EPS) * g + b


def _ret_kernel(dec_ref, q_ref, k_ref, v_ref, g_ref, cos_ref, sin_ref, s0_ref, gg_ref, gb_ref, y_ref, *rest,
                nc, n_sub, use_rot, has_init, want_final):
    if want_final:
        sfin_ref, sf_scr, sb_scr, x_scr, kr_scr = rest
    else:
        sf_scr, sb_scr, x_scr, kr_scr = rest
    h = pl.program_id(1)
    row = _iota((CHUNK, CHUNK), 0).astype(F32)
    col = _iota((CHUNK, CHUNK), 1).astype(F32)
    lg_f = -jnp.exp(jnp.full((CHUNK, CHUNK), dec_ref[0, h], F32))
    lg_b = -jnp.exp(jnp.full((CHUNK, CHUNK), dec_ref[1, h], F32))
    lag = row - col
    scale = HEAD_DIM ** -0.5
    dmat = (jnp.where(lag >= 0, jnp.exp(lg_f * jnp.maximum(lag, 0.0)), 0.0)
            + jnp.where(lag <= 0, jnp.exp(lg_b * jnp.maximum(-lag, 0.0)), 0.0)) * scale
    qd_f = jnp.exp(lg_f * (row + 1.0))
    qd_b = jnp.exp(lg_b * (CHUNK - row))
    kd_f = jnp.exp(lg_f * (CHUNK - 1.0 - col)) * scale
    kd_b = jnp.exp(lg_b * col) * scale
    cd_f = jnp.exp(lg_f * CHUNK)
    cd_b = jnp.exp(lg_b * CHUNK)

    def chunk(s, j):
        return pl.ds(pl.multiple_of((s * nc + j) * CHUNK, CHUNK), CHUNK)

    def rot(ref, sl):
        x = ref[sl, :].astype(F32)
        if not use_rot:
            return x
        return x * cos_ref[sl, :] + pltpu.roll(x, HEAD_DIM // 2, 1) * sin_ref[sl, :]

    def local(s, j, carry):
        sl = chunk(s, j)
        k = rot(k_ref, sl)
        kr_scr[sl, :] = k.astype(BF16)
        k_t = k.T
        v = v_ref[sl, :]
        x_scr[0, j] = _dot((k_t * kd_f).astype(BF16), v)
        x_scr[1, j] = _dot((k_t * kd_b).astype(BF16), v)
        return carry

    zero = jnp.zeros((CHUNK, CHUNK), F32)

    def states(jj, carry):
        s_f, s_b = carry
        jf, jb = jj, nc - 1 - jj
        sf_scr[jf] = s_f.astype(BF16)
        sb_scr[jb] = s_b.astype(BF16)
        return s_f * cd_f + x_scr[0, jf], s_b * cd_b + x_scr[1, jb]

    def outputs(s, j, carry):
        sl = chunk(s, j)
        q = rot(q_ref, sl)
        att = _dot_nt(q.astype(BF16), kr_scr[sl, :]) * dmat
        lhs = jnp.concatenate([att.astype(BF16), (q * qd_f).astype(BF16), (q * qd_b).astype(BF16)], axis=1)
        rhs = jnp.concatenate([v_ref[sl, :], sf_scr[j], sb_scr[j]], axis=0)
        o = _dot(lhs, rhs)
        y = jax.nn.silu(g_ref[sl, :].astype(F32)) * _group_norm(o, gg_ref[...], gb_ref[...])
        y_ref[sl, :] = y.astype(y_ref.dtype)
        return carry

    def sequence(s, carry):
        lax.fori_loop(0, nc, functools.partial(local, s), 0, unroll=min(4, nc))
        init = (s0_ref[s, 0], s0_ref[s, 1]) if has_init else (zero, zero)
        s_f, s_b = lax.fori_loop(0, nc, states, init, unroll=2)
        if want_final:
            sfin_ref[s, 0] = s_f
            sfin_ref[s, 1] = s_b
        lax.fori_loop(0, nc, functools.partial(outputs, s), 0, unroll=min(8, nc))
        return carry

    if n_sub == 1:
        sequence(0, 0)
    else:
        lax.fori_loop(0, n_sub, sequence, 0)


def _retention(proj, dec, cos_t, sin_t, s0, gn_g, gn_b, *, n_seq, seq_len, n_sub, row0, use_rot, has_init,
               want_final):
    assert n_seq % n_sub == 0 and not (use_rot and n_sub > 1)
    nc = seq_len // CHUNK
    rows = n_sub * seq_len
    blk0 = row0 // rows
    kern = functools.partial(_ret_kernel, nc=nc, n_sub=n_sub, use_rot=use_rot, has_init=has_init, want_final=want_final)

    def tok(cb):
        return pl.BlockSpec((rows, HEAD_DIM), lambda s, h, cb=cb: (blk0 + s, cb + h))

    rot_spec = pl.BlockSpec((seq_len, HEAD_DIM), lambda s, h: (0, 0))
    st_spec = pl.BlockSpec((n_sub, 2, None, HEAD_DIM, HEAD_DIM), lambda s, h: (s, 0, h, 0, 0))
    gn_spec = pl.BlockSpec((1, HEAD_DIM), lambda s, h: (0, h))
    out_specs = [pl.BlockSpec((rows, HEAD_DIM), lambda s, h: (s, h))]
    out_shape = [jax.ShapeDtypeStruct((n_seq * seq_len, WIDTH), BF16)]
    if want_final:
        out_specs.append(st_spec)
        out_shape.append(jax.ShapeDtypeStruct((n_seq, 2, HEADS, HEAD_DIM, HEAD_DIM), F32))
    return pl.pallas_call(
        kern,
        grid=(n_seq // n_sub, HEADS),
        in_specs=[pl.BlockSpec(memory_space=pltpu.SMEM), tok(0), tok(4), tok(8), tok(12),
                  rot_spec, rot_spec, st_spec, gn_spec, gn_spec],
        out_specs=out_specs,
        out_shape=out_shape,
        scratch_shapes=[pltpu.VMEM((nc, HEAD_DIM, HEAD_DIM), BF16), pltpu.VMEM((nc, HEAD_DIM, HEAD_DIM), BF16),
                        pltpu.VMEM((2, nc, HEAD_DIM, HEAD_DIM), F32), pltpu.VMEM((rows, HEAD_DIM), BF16)],
        compiler_params=_params(("parallel", "parallel"), 48),
        name="retention",
    )(dec, proj, proj, proj, proj, cos_t, sin_t, s0, gn_g, gn_b)


def _gate_prep_kernel(g_ref, col_ref, row_ref):
    lane = _iota((CHUNK, LANE), 1)
    tri = jnp.where(_iota((CHUNK, CHUNK), 0) >= _iota((CHUNK, CHUNK), 1), 1.0, 0.0)
    for c in range(g_ref.shape[0] // CHUNK):
        sl = slice(c * CHUNK, (c + 1) * CHUNK)
        g = g_ref[sl, :]
        lf = jnp.where(lane < 4 * HEADS, jnp.minimum(g, 0.0) - jnp.log1p(jnp.exp(-jnp.abs(g))), 0.0)
        cs = _dot_hi(tri, lf)
        tot = cs[CHUNK - 1:CHUNK, :]
        bc = jnp.where(lane < 2 * HEADS, cs, tot - cs + lf)
        rest = jnp.where(lane < 2 * HEADS, tot - cs, cs - lf)
        pack = g + pltpu.roll(bc, GATE_BC, 1) + pltpu.roll(rest, GATE_REST, 1)
        col_ref[sl, :] = pack
        row_ref[sl, :] = pack.T


def _gate_prep(gates, cfg):
    t_tok = gates.shape[0]
    tm = cfg["tm_in"]
    spec = pl.BlockSpec((tm, LANE), lambda i: (i, 0))
    return pl.pallas_call(
        _gate_prep_kernel,
        grid=(t_tok // tm,),
        in_specs=[spec],
        out_specs=[spec, spec],
        out_shape=[jax.ShapeDtypeStruct((t_tok, LANE), F32)] * 2,
        compiler_params=_params(("parallel",), 32),
        name="gate_prep",
    )(gates)


def _mlstm_kernel(q_ref, k_ref, v_ref, o_ref, col_ref, row_ref, c0_ref, n0_ref, m0_ref, gg_ref, gb_ref, y_ref, *rest,
                  nc, n_sub, has_init, want_final):
    if want_final:
        cfin_ref, nfin_ref, mfin_ref, c_scr, n_scr, m_scr, x_scr, nl_scr, ml_scr, bl_scr = rest
    else:
        c_scr, n_scr, m_scr, x_scr, nl_scr, ml_scr, bl_scr = rest
    src = _iota((CHUNK, CHUNK), 0)
    dst = _iota((CHUNK, CHUNK), 1)
    scale = HEAD_DIM ** -0.5

    def chunk(s, j):
        return pl.ds(pl.multiple_of((s * nc + j) * CHUNK, CHUNK), CHUNK)

    def gate_idx(d, h):
        return d * 2 * HEADS + h, GATE_BC + d * 2 * HEADS + HEADS + h, GATE_REST + d * 2 * HEADS + HEADS + h

    def head_body(h):
        def init(s, d):
            if has_init:
                return c0_ref[s, d], n0_ref[s, d], m0_ref[s, d]
            return (jnp.zeros((HEAD_DIM, HEAD_DIM), F32), jnp.zeros((1, HEAD_DIM), F32),
                    jnp.zeros((1, HEAD_DIM), F32))

        def local(s, j, carry):
            sl = chunk(s, j)
            k = k_ref[sl, :]
            v_t = v_ref[sl, :].astype(F32).T
            rp = row_ref[sl, :]
            for d in range(2):
                ii, bi, ri = gate_idx(d, h)
                bc_row = rp[bi:bi + 1, :]
                b_last = bc_row[:, CHUNK - 1:CHUNK] if d == 0 else bc_row[:, 0:1]
                log_k = rp[ri:ri + 1, :] + rp[ii:ii + 1, :]
                m_loc = jnp.max(log_k, axis=1, keepdims=True)
                kw = jnp.exp(log_k - m_loc)
                x_scr[d, j] = _dot((v_t * kw).astype(BF16), k) * scale
                kw_hi = kw.astype(BF16).astype(F32)
                kw2 = jnp.concatenate([jnp.broadcast_to(kw_hi, (8, CHUNK)), jnp.broadcast_to(kw - kw_hi, (8, CHUNK))], 0)
                nl = _dot(kw2.astype(BF16), k)
                nl_scr[d, j] = (nl[0:1, :] + nl[8:9, :]) * scale
                ml_scr[d, j] = jnp.broadcast_to(m_loc, (1, HEAD_DIM))
                bl_scr[d, j] = jnp.broadcast_to(b_last, (1, HEAD_DIM))
            return carry

        def states(jj, carry):
            out = []
            for d, j in ((0, jj), (1, nc - 1 - jj)):
                cmat, nvec, m = carry[d]
                c_scr[d, j] = cmat.astype(BF16)
                n_scr[d, j] = nvec
                m_scr[d, j] = m
                m_new = jnp.maximum(bl_scr[d, j] + m, ml_scr[d, j])
                keep = jnp.exp(bl_scr[d, j] + m - m_new)
                add = jnp.exp(ml_scr[d, j] - m_new)
                out.append((keep * cmat + add * x_scr[d, j], keep * nvec + add * nl_scr[d, j], m_new))
            return tuple(out)

        def outputs(s, j, carry):
            sl = chunk(s, j)
            q = q_ref[sl, :]
            v = v_ref[sl, :]
            s_t = _dot_nt(k_ref[sl, :], q) * scale
            cp = col_ref[sl, :]
            rp = row_ref[sl, :]
            h_t = None
            for d in range(2):
                ii, bi, _ = gate_idx(d, h)
                causal = (src <= dst) if d == 0 else (src >= dst)
                bc_row = rp[bi:bi + 1, :]
                log_d = jnp.where(causal, bc_row + (cp[:, ii:ii + 1] - cp[:, bi:bi + 1]), -jnp.inf)
                log_prev = bc_row + m_scr[d, j]
                m_t = jnp.maximum(log_prev, jnp.max(log_d, axis=0, keepdims=True))
                w = s_t * jnp.exp(log_d - m_t)
                w_prev = jnp.exp(log_prev - m_t)
                qn = _dot_nt(jnp.broadcast_to(n_scr[d, j], (16, HEAD_DIM)).astype(BF16), q)[0:1, :]
                den = jnp.sum(w, axis=0, keepdims=True) + w_prev * qn
                inv = 1.0 / jnp.maximum(jnp.abs(den), jnp.exp(-m_t))
                num = _dot_tn(v, w.astype(BF16)) + _dot_nt(c_scr[d, j], q) * w_prev
                h_t = num * inv if h_t is None else h_t + num * inv
            y = jax.nn.sigmoid(o_ref[sl, :].astype(F32).T) * h_t
            mu = jnp.mean(y, axis=0, keepdims=True)
            var = jnp.mean(jnp.square(y - mu), axis=0, keepdims=True)
            y = (y - mu) * lax.rsqrt(var + GN_EPS) * gg_ref[...] + gb_ref[...]
            y_ref[sl, :] = y.T.astype(y_ref.dtype)
            return carry

        def sequence(s, carry):
            lax.fori_loop(0, nc, functools.partial(local, s), 0, unroll=min(4, nc))
            fin = lax.fori_loop(0, nc, states, (init(s, 0), init(s, 1)), unroll=2)
            if want_final:
                for d in range(2):
                    cfin_ref[s, d] = fin[d][0]
                    nfin_ref[s, d] = fin[d][1]
                    mfin_ref[s, d] = fin[d][2]
            lax.fori_loop(0, nc, functools.partial(outputs, s), 0, unroll=min(4, nc))
            return carry

        if n_sub == 1:
            sequence(0, 0)
        else:
            lax.fori_loop(0, n_sub, sequence, 0)

    hh = pl.program_id(1)
    for h in range(HEADS):
        pl.when(hh == h)(functools.partial(head_body, h))


def _mlstm(proj, gcol, grow, c0, n0, m0, gn_g, gn_b, *, n_seq, seq_len, n_sub, row0, has_init, want_final):
    assert n_seq % n_sub == 0
    nc = seq_len // CHUNK
    rows = n_sub * seq_len
    blk0 = row0 // rows
    kern = functools.partial(_mlstm_kernel, nc=nc, n_sub=n_sub, has_init=has_init, want_final=want_final)

    def tok(cb):
        return pl.BlockSpec((rows, HEAD_DIM), lambda s, h, cb=cb: (blk0 + s, cb + h))

    gate_spec = pl.BlockSpec((rows, LANE), lambda s, h: (blk0 + s, 0))
    c_spec = pl.BlockSpec((n_sub, 2, None, HEAD_DIM, HEAD_DIM), lambda s, h: (s, 0, h, 0, 0))
    v_spec = pl.BlockSpec((n_sub, 2, None, 1, HEAD_DIM), lambda s, h: (s, 0, h, 0, 0))
    gn_spec = pl.BlockSpec((HEAD_DIM, LANE), lambda s, h: (h, 0))
    out_specs = [pl.BlockSpec((rows, HEAD_DIM), lambda s, h: (s, h))]
    out_shape = [jax.ShapeDtypeStruct((n_seq * seq_len, WIDTH), BF16)]
    if want_final:
        out_specs += [c_spec, v_spec, v_spec]
        out_shape += [jax.ShapeDtypeStruct((n_seq, 2, HEADS, HEAD_DIM, HEAD_DIM), F32),
                      jax.ShapeDtypeStruct((n_seq, 2, HEADS, 1, HEAD_DIM), F32),
                      jax.ShapeDtypeStruct((n_seq, 2, HEADS, 1, HEAD_DIM), F32)]
    return pl.pallas_call(
        kern,
        grid=(n_seq // n_sub, HEADS),
        in_specs=[tok(16), tok(20), tok(24), tok(28), gate_spec, gate_spec,
                  c_spec, v_spec, v_spec, gn_spec, gn_spec],
        out_specs=out_specs,
        out_shape=out_shape,
        scratch_shapes=[pltpu.VMEM((2, nc, HEAD_DIM, HEAD_DIM), BF16)] + [pltpu.VMEM((2, nc, 1, HEAD_DIM), F32)] * 2
        + [pltpu.VMEM((2, nc, HEAD_DIM, HEAD_DIM), F32)] + [pltpu.VMEM((2, nc, 1, HEAD_DIM), F32)] * 3,
        compiler_params=_params(("parallel", "parallel"), 48),
        name="mlstm",
    )(proj, proj, proj, proj, gcol, grow, c0, n0, m0, gn_g, gn_b)


def _layer_norm(x, g, b):
    mu = jnp.mean(x, axis=-1, keepdims=True)
    var = jnp.mean(jnp.square(x - mu), axis=-1, keepdims=True)
    return (x - mu) * lax.rsqrt(var + LN_EPS) * g + b


def _top2_gates(h, whi_ref, wlo_ref, b_ref):
    lane = _iota((h.shape[0], LANE), 1)
    h_hi = h.astype(BF16)
    h_lo = (h - h_hi.astype(F32)).astype(BF16)
    logits = _dot(h_hi, whi_ref[...]) + _dot(h_hi, wlo_ref[...]) + _dot(h_lo, whi_ref[...]) + b_ref[...]
    logits = jnp.where(lane < N_EXPERTS, logits, NEG_BIG)
    m1 = jnp.max(logits, axis=1, keepdims=True)
    i1 = jnp.min(jnp.where(logits == m1, lane, LANE), axis=1, keepdims=True)
    rest = jnp.where(lane == i1, NEG_BIG, logits)
    m2 = jnp.max(rest, axis=1, keepdims=True)
    i2 = jnp.min(jnp.where(rest == m2, lane, LANE), axis=1, keepdims=True)
    e2 = jnp.exp(m2 - m1)
    den = 1.0 + e2
    return jnp.where(lane == i1, 1.0 / den, 0.0) + jnp.where(lane == i2, e2 / den, 0.0)


def _merge_kernel(*refs, n_ctx_tiles, with_router, split):
    if split:
        xc_ref, xs_ref = refs[:2]
    (x_ref, mod_ref, z_ref, ybc_ref, ybs_ref, ycc_ref, ycs_ref, wm_ref, bm_ref, wglu_ref, bglu_ref,
     wa_ref, wb_ref, wc_ref, wo_ref, lg_ref, lb_ref, *rest) = refs[1 if split else 0:]
    if with_router:
        rwh_ref, rwl_ref, rb_ref, o_ref, g_ref = rest
    else:
        o_ref, = rest
    is_ctx = pl.program_id(0) < n_ctx_tiles
    x = jnp.where(is_ctx, xc_ref[...], xs_ref[...]) if split else x_ref[...]
    h = (x * (1.0 + mod_ref[1:2, :]) + mod_ref[0:1, :]).astype(BF16)
    z = z_ref[...]
    ya = (z.astype(F32) * jax.nn.sigmoid(_dot(z, wglu_ref[...]) + bglu_ref[...])).astype(BF16)
    yb = jnp.where(is_ctx, ybc_ref[...], ybs_ref[...])
    yc = jnp.where(is_ctx, ycc_ref[...], ycs_ref[...])
    merged = None
    for j, (y, w_ref) in enumerate(((ya, wa_ref), (yb, wb_ref), (yc, wc_ref))):
        gate = jax.nn.sigmoid(_dot(h, wm_ref[:, j * D_MODEL:(j + 1) * D_MODEL]) + bm_ref[:, j * D_MODEL:(j + 1) * D_MODEL])
        term = gate * _dot(y, w_ref[...])
        merged = term if merged is None else merged + term
    mix = _dot(merged.astype(BF16), wo_ref[...])
    x1 = _layer_norm(ALPHA * x + mod_ref[2:3, :] * mix, lg_ref[...], lb_ref[...])
    o_ref[...] = x1
    if with_router:
        g_ref[...] = _top2_gates(x1 * (1.0 + mod_ref[4:5, :]) + mod_ref[3:4, :], rwh_ref, rwl_ref, rb_ref)


def _merge(x, mod_l, z, yb_c, yb_s, yc_c, yc_s, wm, bm, wglu, bglu, wa, wb, wc, wo, lg, lb, router, cfg):
    split = isinstance(x, tuple)
    xs = list(x) if split else [x]
    t_tok = sum(a.shape[0] for a in xs)
    tm = cfg["tm_merge"]
    n_ctx_tiles = cfg["n_ctx_tok"] // tm
    cond = functools.partial(_cond_row, tm=tm, n_ctx_tok=cfg["n_ctx_tok"], lat_len=cfg["lat_len"])

    def full(shape):
        return pl.BlockSpec(shape, lambda i: (0,) * len(shape))

    def tok(w):
        return pl.BlockSpec((tm, w), lambda i: (i, 0))

    ctx_spec, lat_spec = _group_specs(tm, WIDTH, n_ctx_tiles)
    x_specs = list(_group_specs(tm, D_MODEL, n_ctx_tiles)) if split else [tok(D_MODEL)]
    in_specs = x_specs + [pl.BlockSpec((None, 6, D_MODEL), lambda i: (cond(i), 0, 0)),
                tok(WIDTH), ctx_spec, lat_spec, ctx_spec, lat_spec,
                full((D_MODEL, 3 * D_MODEL)), full((1, 3 * D_MODEL)), full((WIDTH, WIDTH)), full((1, WIDTH)),
                full((WIDTH, D_MODEL)), full((WIDTH, D_MODEL)), full((WIDTH, D_MODEL)),
                full((D_MODEL, D_MODEL)), full((1, D_MODEL)), full((1, D_MODEL))]
    args = xs + [mod_l, z, yb_c, yb_s, yc_c, yc_s, wm, bm, wglu, bglu, wa, wb, wc, wo, lg, lb]
    out_specs = [tok(D_MODEL)]
    out_shape = [jax.ShapeDtypeStruct((t_tok, D_MODEL), F32)]
    if router is not None:
        in_specs += [full((D_MODEL, LANE)), full((D_MODEL, LANE)), full((1, LANE))]
        args += list(router)
        out_specs.append(tok(LANE))
        out_shape.append(jax.ShapeDtypeStruct((t_tok, LANE), F32))
    return pl.pallas_call(
        functools.partial(_merge_kernel, n_ctx_tiles=n_ctx_tiles, with_router=router is not None, split=split),
        grid=(t_tok // tm,),
        in_specs=in_specs,
        out_specs=out_specs,
        out_shape=out_shape,
        compiler_params=_params(("parallel",), 56),
        name="merge",
    )(*args)


def _ffn_kernel(x_ref, mod_ref, w1_ref, w3_ref, w2_ref, lg_ref, lb_ref, o_ref, h_scr, acc_scr):
    f = pl.program_id(1)

    @pl.when(f == 0)
    def _():
        h_scr[...] = (x_ref[...] * (1.0 + mod_ref[4:5, :]) + mod_ref[3:4, :]).astype(BF16)
        acc_scr[...] = jnp.zeros_like(acc_scr)

    h = h_scr[...]
    act = (jax.nn.silu(_dot(h, w1_ref[...])) * _dot(h, w3_ref[...])).astype(BF16)
    acc_scr[...] += _dot(act, w2_ref[...])

    @pl.when(f == pl.num_programs(1) - 1)
    def _():
        o_ref[...] = _layer_norm(ALPHA * x_ref[...] + mod_ref[5:6, :] * acc_scr[...], lg_ref[...], lb_ref[...])


def _dense_ffn(x, mod_l, w1, w3, w2, lg, lb, cfg):
    t_tok = x.shape[0]
    tm, tf = cfg["tm_ffn"], D_FF
    cond = functools.partial(_cond_row, tm=tm, n_ctx_tok=cfg["n_ctx_tok"], lat_len=cfg["lat_len"])
    return pl.pallas_call(
        _ffn_kernel,
        grid=(t_tok // tm, D_FF // tf),
        in_specs=[pl.BlockSpec((tm, D_MODEL), lambda i, f: (i, 0)),
                  pl.BlockSpec((None, 6, D_MODEL), lambda i, f: (cond(i), 0, 0)),
                  pl.BlockSpec((D_MODEL, tf), lambda i, f: (0, f)),
                  pl.BlockSpec((D_MODEL, tf), lambda i, f: (0, f)),
                  pl.BlockSpec((tf, D_MODEL), lambda i, f: (f, 0)),
                  pl.BlockSpec((1, D_MODEL), lambda i, f: (0, 0)),
                  pl.BlockSpec((1, D_MODEL), lambda i, f: (0, 0))],
        out_specs=pl.BlockSpec((tm, D_MODEL), lambda i, f: (i, 0)),
        out_shape=jax.ShapeDtypeStruct((t_tok, D_MODEL), F32),
        scratch_shapes=[pltpu.VMEM((tm, D_MODEL), BF16), pltpu.VMEM((tm, D_MODEL), F32)],
        compiler_params=_params(("parallel", "arbitrary"), 56),
        name="dense_ffn",
    )(x, mod_l, w1, w3, w2, lg, lb)


def _moe_kernel(x_ref, mod_ref, gate_ref, w1_ref, w3_ref, w2_ref, lg_ref, lb_ref, oc_ref, os_ref,
                h_scr, acc_scr, hc_scr, ob_scr, sp_scr, gt_scr, cnt_smem, *, n_ctx_tiles):
    e = pl.program_id(1)
    f = pl.program_id(2)
    last_f = pl.num_programs(2) - 1
    tm = x_ref.shape[0]

    @pl.when((e == 0) & (f == 0))
    def _():
        h_scr[...] = (x_ref[...] * (1.0 + mod_ref[4:5, :]) + mod_ref[3:4, :]).astype(BF16)
        acc_scr[...] = jnp.zeros_like(acc_scr)
        g = gate_ref[...]
        sel = g > 0.0
        ones = jnp.where(sel, 1.0, 0.0)
        before = jnp.where(_iota((tm, tm), 0) > _iota((tm, tm), 1), 1.0, 0.0).astype(BF16)
        pos = _dot(before, ones.astype(BF16))
        spt = jnp.where(sel, pos, -1.0).T
        gt = g.T
        cnt = jnp.sum(ones, axis=0, keepdims=True)
        for ee in range(N_EXPERTS):
            sp_scr[ee] = spt[ee:ee + 1, :]
            gt_scr[ee] = gt[ee:ee + 1, :]
            cnt_smem[ee] = cnt[0, ee].astype(jnp.int32)

    nb = (cnt_smem[e] + (MOE_BLK - 1)) // MOE_BLK
    row = _iota((MOE_BLK, tm), 0)

    def onehot(b):
        return sp_scr[e] == (row + b * MOE_BLK).astype(F32)

    @pl.when(f == 0)
    def _():
        def gather(b, carry):
            p = jnp.where(onehot(b), 1.0, 0.0).astype(BF16)
            hc_scr[b] = _dot(p, h_scr[...]).astype(BF16)
            ob_scr[b] = jnp.zeros((MOE_BLK, D_MODEL), F32)
            return carry
        lax.fori_loop(0, nb, gather, 0)

        @pl.when(nb % 2 == 1)
        def _():
            ob_scr[nb] = jnp.zeros((MOE_BLK, D_MODEL), F32)

    def ffn(hc):
        act = (jax.nn.silu(_dot(hc, w1_ref[...])) * _dot(hc, w3_ref[...])).astype(BF16)
        return _dot(act, w2_ref[...])

    def ffn_pair(p, carry):
        two = pl.ds(2 * p, 2)
        ob_scr[two] += ffn(hc_scr[two].reshape(2 * MOE_BLK, D_MODEL)).reshape(2, MOE_BLK, D_MODEL)
        return carry
    lax.fori_loop(0, nb // 2, ffn_pair, 0)

    @pl.when(nb % 2 == 1)
    def _():
        ob_scr[nb - 1] += ffn(hc_scr[nb - 1])

    @pl.when(f == last_f)
    def _():
        row2 = _iota((2 * MOE_BLK, tm), 0)

        def scatter(p, carry):
            m = sp_scr[e] == (row2 + p * (2 * MOE_BLK)).astype(F32)
            gc = jnp.sum(jnp.where(m, gt_scr[e], 0.0), axis=1, keepdims=True)
            og = (ob_scr[pl.ds(2 * p, 2)].reshape(2 * MOE_BLK, D_MODEL) * gc).astype(BF16)
            acc_scr[...] += _dot_tn(jnp.where(m, 1.0, 0.0).astype(BF16), og)
            return carry
        lax.fori_loop(0, (nb + 1) // 2, scatter, 0)

    last = (e == pl.num_programs(1) - 1) & (f == last_f)
    is_ctx = pl.program_id(0) < n_ctx_tiles
    for o_ref, mine in ((oc_ref, is_ctx), (os_ref, jnp.logical_not(is_ctx))):
        @pl.when(last & mine)
        def _():
            o_ref[...] = _layer_norm(ALPHA * x_ref[...] + mod_ref[5:6, :] * acc_scr[...], lg_ref[...], lb_ref[...])


def _moe_ffn(x, mod_l, gates, w1, w3, w2, layer, lg, lb, cfg):
    t_tok = x.shape[0]
    tm, tf = cfg["tm_moe"], cfg["tf"]
    n_ctx_tiles = cfg["n_ctx_tok"] // tm
    cond = functools.partial(_cond_row, tm=tm, n_ctx_tok=cfg["n_ctx_tok"], lat_len=cfg["lat_len"])
    return pl.pallas_call(
        functools.partial(_moe_kernel, n_ctx_tiles=n_ctx_tiles),
        grid=(t_tok // tm, N_EXPERTS, D_FF // tf),
        in_specs=[pl.BlockSpec((tm, D_MODEL), lambda i, e, f: (i, 0), pipeline_mode=pl.Buffered(1)),
                  pl.BlockSpec((None, 6, D_MODEL), lambda i, e, f: (cond(i), 0, 0)),
                  pl.BlockSpec((tm, LANE), lambda i, e, f: (i, 0), pipeline_mode=pl.Buffered(1)),
                  pl.BlockSpec((None, None, D_MODEL, tf), lambda i, e, f: (layer, e, 0, f)),
                  pl.BlockSpec((None, None, D_MODEL, tf), lambda i, e, f: (layer, e, 0, f)),
                  pl.BlockSpec((None, None, tf, D_MODEL), lambda i, e, f: (layer, e, f, 0)),
                  pl.BlockSpec((1, D_MODEL), lambda i, e, f: (0, 0)),
                  pl.BlockSpec((1, D_MODEL), lambda i, e, f: (0, 0))],
        out_specs=list(_group_specs(tm, D_MODEL, n_ctx_tiles)),
        out_shape=[jax.ShapeDtypeStruct((cfg["n_ctx_tok"], D_MODEL), F32),
                   jax.ShapeDtypeStruct((t_tok - cfg["n_ctx_tok"], D_MODEL), F32)],
        scratch_shapes=[pltpu.VMEM((tm, D_MODEL), BF16), pltpu.VMEM((tm, D_MODEL), F32),
                        pltpu.VMEM((tm // MOE_BLK, MOE_BLK, D_MODEL), BF16),
                        pltpu.VMEM((tm // MOE_BLK, MOE_BLK, D_MODEL), F32),
                        pltpu.VMEM((N_EXPERTS, 1, tm), F32), pltpu.VMEM((N_EXPERTS, 1, tm), F32),
                        pltpu.SMEM((N_EXPERTS,), jnp.int32)],
        compiler_params=_params(("arbitrary", "arbitrary", "arbitrary"), 56),
        name="moe_ffn",
    )(x, mod_l, gates, w1, w3, w2, lg, lb)


def _rotary_tables(n_tok):
    rows = n_tok // GRID_W
    r = jnp.repeat(jnp.arange(rows, dtype=F32), GRID_W)
    col = jnp.tile(jnp.arange(GRID_W, dtype=F32), rows)
    n_freq = HEAD_DIM // 4
    inv = ROPE_BASE ** (-jnp.arange(n_freq, dtype=F32) / n_freq)
    ang = jnp.concatenate([r[:, None] * inv, col[:, None] * inv], -1)
    cos, sin = jnp.cos(ang), jnp.sin(ang)
    return jnp.concatenate([cos, cos], -1), jnp.concatenate([-sin, sin], -1)


def _row2(v):
    return v.reshape(1, -1).astype(F32)


def kernel(x_prompt, x_sample, cache_s5_re, cache_s5_im, cache_ret, cache_ml_c, cache_ml_n, cache_ml_m, c, c_ctx, ada_w, ada_b, w_in, b_in, s5_lam_re, s5_lam_im, s5_log_step, s5_b_re, s5_b_im, s5_c_re, s5_c_im, s5_d, s5_glu_w, s5_glu_b, ret_decay, ret_gn_g, ret_gn_b, ml_gn_g, ml_gn_b, w_a, w_b, w_c, w_o, ln1_g, ln1_b, ln2_g, ln2_b, ffn_w1, ffn_w3, ffn_w2, moe_router, moe_router_b, moe_w1, moe_w3, moe_w2):
    n_ctx_seq, ctx_len, _ = x_prompt.shape
    n_lat_seq, lat_len, _ = x_sample.shape
    n_ctx_tok = n_ctx_seq * ctx_len
    n_lat_tok = n_lat_seq * lat_len
    t_tok = n_ctx_tok + n_lat_tok
    assert n_lat_seq + 1 <= N_COND and n_ctx_seq % 8 == 0 and n_lat_seq % 8 == 0
    assert ctx_len % CHUNK == 0 and lat_len % CHUNK == 0 and n_ctx_tok % lat_len == 0
    cfg = dict(n_ctx_seq=n_ctx_seq, ctx_len=ctx_len, n_lat_seq=n_lat_seq, lat_len=lat_len, n_ctx_tok=n_ctx_tok,
               tm_in=min(2048, lat_len), tm_merge=min(512, lat_len), tm_ffn=min(256, lat_len), tm_moe=min(1024, lat_len), tf=1408,
               s5_rows=min(256, t_tok // SUB // 8))

    x = (x_prompt.reshape(n_ctx_tok, D_MODEL), x_sample.reshape(n_lat_tok, D_MODEL))
    cond = jnp.zeros((N_COND, D_MODEL), F32).at[0].set(c_ctx).at[1:1 + n_lat_seq].set(c)
    mod = _modulation(cond, ada_w, ada_b).reshape(DEPTH, N_COND, 6, D_MODEL)
    cos_t, sin_t = _rotary_tables(lat_len)

    n_main = S5_WIDTH + 8 * WIDTH
    gate_off = n_main
    merge_off = gate_off + 4 * HEADS
    s5_fac = jax.vmap(_s5_factors)(s5_lam_re, s5_lam_im, s5_log_step, s5_b_re, s5_b_im, s5_c_re, s5_c_im)

    st_s5, st_ret, st_c, st_n, st_m = [], [], [], [], []
    zero_ret = jnp.zeros((n_ctx_seq, 2, HEADS, HEAD_DIM, HEAD_DIM), F32)
    zero_vec = jnp.zeros((n_ctx_seq, 2, HEADS, 1, HEAD_DIM), F32)
    for l in range(DEPTH):
        mod_l = mod[l]
        w_main = w_in[l][:, :n_main].astype(BF16)
        b_main = _row2(b_in[l][:n_main])
        w_gate = jnp.zeros((D_MODEL, LANE), F32).at[:, :4 * HEADS].set(w_in[l][:, gate_off:merge_off]).astype(BF16)
        b_gate = jnp.zeros((1, LANE), F32).at[0, :4 * HEADS].set(b_in[l][gate_off:merge_off])
        proj, u4, gates = _inproj(x, mod_l, w_main, b_main, w_gate, b_gate, cfg)

        s5_ain, s5_klag, s5_bout, s5_a = (m[l] for m in s5_fac)
        loc = _s5_state_in(u4, s5_ain, cfg)
        x0 = jnp.stack([cache_s5_re[:, l], cache_s5_im[:, l]], 0)
        x0 = x0.reshape(2, n_lat_seq, 2, S5_Q, S5_HALF).transpose(3, 2, 0, 1, 4).astype(F32)
        xprev, s5_fin = _s5_scan(loc, s5_a, x0, cfg)
        d4 = jnp.broadcast_to(s5_d[l].astype(F32).reshape(S5_Q, S5_P, 1, S5_PB), (S5_Q, S5_P, SUB, S5_PB))
        d4 = d4.reshape(S5_Q, 1, S5_FLAT)
        z = _s5_output(u4, xprev, s5_klag, s5_bout, d4, cfg)
        st_s5.append(s5_fin)

        gg, gb = _row2(ret_gn_g[l]), _row2(ret_gn_b[l])
        dec = ret_decay[l].astype(F32)
        yb_c, ret_fin = _retention(proj, dec, cos_t, sin_t, zero_ret, gg, gb, n_seq=n_ctx_seq, seq_len=ctx_len,
                                   n_sub=8, row0=0, use_rot=False, has_init=False, want_final=True)
        yb_s, = _retention(proj, dec, cos_t, sin_t, cache_ret[:, l].astype(F32), gg, gb, n_seq=n_lat_seq,
                           seq_len=lat_len, n_sub=1, row0=n_ctx_tok, use_rot=True, has_init=True, want_final=False)
        st_ret.append(ret_fin)

        gg = jnp.broadcast_to(ml_gn_g[l].astype(F32)[:, None], (WIDTH, LANE))
        gb = jnp.broadcast_to(ml_gn_b[l].astype(F32)[:, None], (WIDTH, LANE))
        gcol, grow = _gate_prep(gates, cfg)
        yc_c, c_fin, n_fin, m_fin = _mlstm(proj, gcol, grow, zero_ret, zero_vec, zero_vec, gg, gb, n_seq=n_ctx_seq,
                                           seq_len=ctx_len, n_sub=8, row0=0, has_init=False, want_final=True)
        n0 = cache_ml_n[:, l].astype(F32)[:, :, :, None, :]
        m0 = jnp.broadcast_to(cache_ml_m[:, l].astype(F32)[:, :, :, None, None], n0.shape)
        yc_s, = _mlstm(proj, gcol, grow, cache_ml_c[:, l].astype(F32), n0, m0, gg, gb, n_seq=n_lat_seq,
                       seq_len=lat_len, n_sub=1, row0=n_ctx_tok, has_init=True, want_final=False)
        st_c.append(c_fin)
        st_n.append(n_fin[:, :, :, 0, :])
        st_m.append(m_fin[:, :, :, 0, 0])

        j = l // 2
        router = None
        if l % 2 == 1:
            rw = jnp.zeros((D_MODEL, LANE), F32).at[:, :N_EXPERTS].set(moe_router[j])
            rw_hi = rw.astype(BF16)
            router = (rw_hi, (rw - rw_hi.astype(F32)).astype(BF16),
                      jnp.zeros((1, LANE), F32).at[0, :N_EXPERTS].set(moe_router_b[j]))
        merged = _merge(x, mod_l, z, yb_c, yb_s, yc_c, yc_s, w_in[l][:, merge_off:].astype(BF16),
                        _row2(b_in[l][merge_off:]), s5_glu_w[l].astype(BF16), _row2(s5_glu_b[l]), w_a[l].astype(BF16),
                        w_b[l].astype(BF16), w_c[l].astype(BF16), w_o[l].astype(BF16), _row2(ln1_g[l]),
                        _row2(ln1_b[l]), router, cfg)

        if router is None:
            x = _dense_ffn(merged[0], mod_l, ffn_w1[j].astype(BF16), ffn_w3[j].astype(BF16), ffn_w2[j].astype(BF16),
                           _row2(ln2_g[l]), _row2(ln2_b[l]), cfg)
        else:
            x = tuple(_moe_ffn(merged[0], mod_l, merged[1], moe_w1.astype(BF16), moe_w3.astype(BF16),
                               moe_w2.astype(BF16), j, _row2(ln2_g[l]), _row2(ln2_b[l]), cfg))

    y_p = x[0].reshape(n_ctx_seq, ctx_len, D_MODEL)
    y_s = x[1].reshape(n_lat_seq, lat_len, D_MODEL)
    s5 = jnp.stack(st_s5, 0)
    s5 = s5.reshape(DEPTH, S5_Q, 2, 2, n_ctx_seq, S5_QG, S5_STATE).transpose(3, 4, 0, 2, 1, 5, 6)
    s5 = s5.reshape(2, n_ctx_seq, DEPTH, 2, S5_GROUPS, S5_STATE)
    return (y_p, y_s, s5[0], s5[1], jnp.stack(st_ret, 1), jnp.stack(st_c, 1), jnp.stack(st_n, 1),
            jnp.stack(st_m, 1))
```

```python
import functools

import jax
import jax.numpy as jnp
from jax import lax
from jax.experimental import pallas as pl
from jax.experimental.pallas import tpu as pltpu

F32 = jnp.float32
BF16 = jnp.bfloat16

D_MODEL = 1024
DEPTH = 4
GRID_W = 64
CHUNK = 128
S5_WIDTH = 512
S5_GROUP = 16
S5_GROUPS = 32
S5_STATE = 64
HEADS = 4
HEAD_DIM = 128
WIDTH = 512
ROPE_BASE = 10000.0
D_FF = 2816
N_EXPERTS = 8
ALPHA = (2.0 * DEPTH) ** 0.25
LN_EPS = 1e-5
GN_EPS = 1e-5
N_COND = 16
SUB = 16
N_LAG = 2 * SUB - 1
LANE = 128
S5_Q = S5_WIDTH // LANE
S5_QG = LANE // S5_GROUP
S5_FLAT = SUB * LANE
S5_HALF = S5_QG * S5_STATE
S5_ST = 4 * S5_HALF
S5_PG = 2
S5_P = S5_QG // S5_PG
S5_PB = S5_PG * S5_GROUP
S5_PFLAT = SUB * S5_PB
S5_PHALF = S5_PG * S5_STATE
NEG_BIG = -1e30
MOE_BLK = 128
GATE_BC = 16
GATE_REST = 32


def _dot(a, b):
    return jnp.dot(a, b, preferred_element_type=F32)


def _dot_hi(a, b):
    return jnp.dot(a, b, preferred_element_type=F32, precision=lax.Precision.HIGHEST)


def _dot_nt(a, b):
    return lax.dot_general(a, b, (((1,), (1,)), ((), ())), preferred_element_type=F32)


def _dot_tn(a, b):
    return lax.dot_general(a, b, (((0,), (0,)), ((), ())), preferred_element_type=F32)


def _params(sem, vmem_mb):
    return pltpu.CompilerParams(dimension_semantics=sem, vmem_limit_bytes=vmem_mb << 20)


def _cond_row(tile, tm, n_ctx_tok, lat_len):
    start = tile * tm
    return jnp.where(start < n_ctx_tok, 0, 1 + (start - n_ctx_tok) // lat_len)


def _iota(shape, axis):
    return lax.broadcasted_iota(jnp.int32, shape, axis)


def _group_specs(tm, width, n_ctx_tiles, pipeline_mode=None):
    return (pl.BlockSpec((tm, width), lambda i, *_: (jnp.minimum(i, n_ctx_tiles - 1), 0), pipeline_mode=pipeline_mode),
            pl.BlockSpec((tm, width), lambda i, *_: (jnp.maximum(i - n_ctx_tiles, 0), 0), pipeline_mode=pipeline_mode))


def _mod_kernel(c_ref, w_ref, b_ref, o_ref):
    o_ref[...] = _dot_hi(jax.nn.silu(c_ref[...]), w_ref[...]) + b_ref[...]


def _modulation(cond, ada_w, ada_b):
    tn = 1536
    n = ada_w.shape[-1]
    return pl.pallas_call(
        _mod_kernel,
        grid=(DEPTH, n // tn),
        in_specs=[pl.BlockSpec((N_COND, D_MODEL), lambda l, j: (0, 0)),
                  pl.BlockSpec((None, D_MODEL, tn), lambda l, j: (l, 0, j)),
                  pl.BlockSpec((None, 1, tn), lambda l, j: (l, 0, j))],
        out_specs=pl.BlockSpec((None, N_COND, tn), lambda l, j: (l, 0, j)),
        out_shape=jax.ShapeDtypeStruct((DEPTH, N_COND, n), F32),
        compiler_params=_params(("parallel", "parallel"), 40),
        name="modulation",
    )(cond, ada_w, ada_b.reshape(DEPTH, 1, n))


def _inproj_kernel(*refs, n_ctx_tiles, split):
    if split:
        xc_ref, xs_ref = refs[:2]
    x_ref, mod_ref, w_ref, b_ref, wg_ref, bg_ref, o_ref, u4_ref, g_ref, h_scr, u_scr = refs[1 if split else 0:]
    j = pl.program_id(1)
    tm = x_ref.shape[0]

    @pl.when(j == 0)
    def _():
        x = jnp.where(pl.program_id(0) < n_ctx_tiles, xc_ref[...], xs_ref[...]) if split else x_ref[...]
        h = (x * (1.0 + mod_ref[1:2, :]) + mod_ref[0:1, :]).astype(BF16)
        h_scr[...] = h
        g_ref[...] = _dot(h, wg_ref[...]) + bg_ref[...]
        u = _dot(h, w_ref[...]) + b_ref[...]
        for q in range(S5_Q):
            u_scr[q] = u[:, q * LANE:(q + 1) * LANE]
            rows = [u_scr[q, pl.ds(t, tm // SUB, stride=SUB), :].astype(BF16) for t in range(SUB)]
            for p in range(S5_P):
                u4_ref[q, :, p * S5_PFLAT:(p + 1) * S5_PFLAT] = jnp.concatenate(
                    [r[:, p * S5_PB:(p + 1) * S5_PB] for r in rows], axis=1)

    @pl.when(j > 0)
    def _():
        o_ref[...] = (_dot(h_scr[...], w_ref[...]) + b_ref[...]).astype(o_ref.dtype)


def _inproj(x, mod_l, w, b, wg, bg, cfg):
    split = isinstance(x, tuple)
    xs = list(x) if split else [x]
    t_tok = sum(a.shape[0] for a in xs)
    tm, tn = cfg["tm_in"], 512
    nj = w.shape[1] // tn
    n_ctx_tiles = cfg["n_ctx_tok"] // tm
    cond = functools.partial(_cond_row, tm=tm, n_ctx_tok=cfg["n_ctx_tok"], lat_len=cfg["lat_len"])
    x_specs = (list(_group_specs(tm, D_MODEL, n_ctx_tiles, pl.Buffered(1))) if split
               else [pl.BlockSpec((tm, D_MODEL), lambda i, j: (i, 0))])
    return pl.pallas_call(
        functools.partial(_inproj_kernel, n_ctx_tiles=n_ctx_tiles, split=split),
        grid=(t_tok // tm, nj),
        in_specs=x_specs + [
                  pl.BlockSpec((None, 6, D_MODEL), lambda i, j: (cond(i), 0, 0)),
                  pl.BlockSpec((D_MODEL, tn), lambda i, j: (0, j)),
                  pl.BlockSpec((1, tn), lambda i, j: (0, j)),
                  pl.BlockSpec((D_MODEL, LANE), lambda i, j: (0, 0)),
                  pl.BlockSpec((1, LANE), lambda i, j: (0, 0))],
        out_specs=[pl.BlockSpec((tm, tn), lambda i, j: (i, jnp.maximum(j - 1, 0))),
                   pl.BlockSpec((S5_Q, tm // SUB, S5_FLAT), lambda i, j: (0, i, 0)),
                   pl.BlockSpec((tm, LANE), lambda i, j: (i, 0))],
        out_shape=[jax.ShapeDtypeStruct((t_tok, (nj - 1) * tn), BF16),
                   jax.ShapeDtypeStruct((S5_Q, t_tok // SUB, S5_FLAT), BF16),
                   jax.ShapeDtypeStruct((t_tok, LANE), F32)],
        scratch_shapes=[pltpu.VMEM((tm, D_MODEL), BF16), pltpu.VMEM((S5_Q, tm, LANE), F32)],
        compiler_params=_params(("parallel", "arbitrary"), 48),
        name="inproj",
    )(*xs, mod_l, w, b, wg, bg)


def _s5_factors(lam_re, lam_im, log_step, b_re, b_im, c_re, c_im):
    lam = lax.complex(lam_re.astype(F32), lam_im.astype(F32))
    lam_dt = lam * jnp.exp(log_step.astype(F32))[..., None]
    lam_bar = jnp.exp(lam_dt)
    bbar = ((lam_bar - 1.0) / lam)[..., None] * lax.complex(b_re.astype(F32), b_im.astype(F32))
    cmat = lax.complex(c_re.astype(F32), c_im.astype(F32))
    ks = jnp.arange(SUB + 1, dtype=F32)
    pw = jnp.exp(lam_dt[None] * ks[:, None, None, None])
    kern = jnp.einsum('dgcp,tdgp,dgpe->dgtce', cmat, pw[:SUB], bbar).real
    pad = jnp.zeros_like(kern[0][:, :SUB - 1])
    ktab = jnp.concatenate([pad, kern[0]], 1) + jnp.concatenate([kern[1][:, ::-1], pad], 1)
    k_lag = ktab.reshape(S5_Q, S5_QG, N_LAG, S5_GROUP, S5_GROUP).transpose(0, 2, 3, 1, 4)
    k_lag = k_lag.reshape(S5_Q, N_LAG, S5_GROUP, S5_P, S5_PB).transpose(0, 3, 1, 2, 4)
    pw_in = jnp.stack([pw[:SUB][::-1, 0], pw[:SUB][:, 1]], 0)
    wb = pw_in[..., None] * bbar[:, None]
    wb = jnp.stack([wb.real, wb.imag], 1).reshape(2, 2, SUB, S5_Q, S5_QG, S5_STATE, S5_GROUP)
    a_in = wb.transpose(3, 2, 0, 1, 5, 4, 6).reshape(S5_Q, SUB, 4, S5_STATE, S5_P, S5_PB).transpose(0, 4, 1, 2, 3, 5)
    pw_out = jnp.stack([pw[1:, 0], pw[1:][::-1, 1]], 0)
    ce = cmat[:, None] * pw_out[:, :, :, None, :]
    ce = jnp.stack([ce.real, -ce.imag], 1).reshape(2, 2, SUB, S5_Q, S5_QG, S5_GROUP, S5_STATE)
    b_out = ce.transpose(3, 0, 1, 2, 5, 4, 6).reshape(S5_Q, 4, SUB, S5_GROUP, S5_P, S5_PHALF).transpose(0, 4, 1, 2, 3, 5)
    a = pw[SUB]
    a = jnp.stack([a.real, a.imag], 1).reshape(2, 2, S5_Q, 1, S5_HALF).transpose(2, 0, 1, 3, 4)
    return a_in.astype(BF16), k_lag.astype(BF16), b_out.astype(BF16), a


def _expand(src_t, n_rep, row_shift, col_shift):
    k, r = src_t.shape
    rep = jnp.where(_iota((k, n_rep * k), 0) == (_iota((k, n_rep * k), 1) & (k - 1)), 1.0, 0.0).astype(BF16)
    same = (_iota((r, n_rep * k), 0) >> row_shift) == (_iota((r, n_rep * k), 1) >> col_shift)
    return jnp.where(same, _dot_tn(src_t, rep), 0.0).astype(BF16)


def _s5a_kernel(u_ref, a_ref, o_ref, w_scr):
    @pl.when(pl.program_id(1) == 0)
    def _():
        for p in range(S5_P):
            for k in range(4):
                w_scr[p, :, k * S5_PHALF:(k + 1) * S5_PHALF] = jnp.concatenate(
                    [_expand(a_ref[p, t, k], S5_PG, 4, 6) for t in range(SUB)], axis=0)

    for p in range(S5_P):
        res = _dot(u_ref[:, p * S5_PFLAT:(p + 1) * S5_PFLAT], w_scr[p])
        for d in range(2):
            for r in range(2):
                k = 2 * d + r
                o_ref[d, r, :, p * S5_PHALF:(p + 1) * S5_PHALF] = res[:, k * S5_PHALF:(k + 1) * S5_PHALF]


def _s5_state_in(u4, a_in, cfg):
    rows = u4.shape[1]
    rt = cfg["s5_rows"]
    return pl.pallas_call(
        _s5a_kernel,
        grid=(S5_Q, rows // rt),
        in_specs=[pl.BlockSpec((None, rt, S5_FLAT), lambda q, i: (q, i, 0)),
                  pl.BlockSpec((None, S5_P, SUB, 4, S5_STATE, S5_PB), lambda q, i: (q, 0, 0, 0, 0, 0))],
        out_specs=pl.BlockSpec((None, 2, 2, rt, S5_HALF), lambda q, i: (q, 0, 0, i, 0)),
        out_shape=jax.ShapeDtypeStruct((S5_Q, 2, 2, rows, S5_HALF), F32),
        scratch_shapes=[pltpu.VMEM((S5_P, S5_PFLAT, 4 * S5_PHALF), BF16)],
        compiler_params=_params(("parallel", "arbitrary"), 48),
        name="s5_state_in",
    )(u4, a_in)


def _s5b_kernel(loc_ref, a_ref, x0_ref, xp_ref, fin_ref, *, n_ctx_seq, ctx_sub, n_lat_seq, lat_sub):
    d = pl.program_id(1)
    ar = jnp.broadcast_to(a_ref[0], (8, LANE))
    ai = jnp.broadcast_to(a_ref[1], (8, LANE))

    def run(base, nsub, xr0, xi0):
        def body(jj, carry):
            xr, xi = carry
            j = jnp.where(d == 0, jj, nsub - 1 - jj)
            idx = pl.ds(base + j, 8, stride=nsub)
            xp_ref[0, idx, :] = xr
            xp_ref[1, idx, :] = xi
            lr = loc_ref[0, idx, :]
            li = loc_ref[1, idx, :]
            return ar * xr - ai * xi + lr, ar * xi + ai * xr + li
        return lax.fori_loop(0, nsub, body, (xr0, xi0))

    zero = jnp.zeros((8, LANE), F32)
    for bg in range(n_ctx_seq // 8):
        xr, xi = run(bg * 8 * ctx_sub, ctx_sub, zero, zero)
        fin_ref[0, bg * 8:(bg + 1) * 8, :] = xr
        fin_ref[1, bg * 8:(bg + 1) * 8, :] = xi
    for bg in range(n_lat_seq // 8):
        run(n_ctx_seq * ctx_sub + bg * 8 * lat_sub, lat_sub, x0_ref[0, bg * 8:(bg + 1) * 8, :],
            x0_ref[1, bg * 8:(bg + 1) * 8, :])


def _s5_scan(loc, a, x0, cfg):
    rows = loc.shape[3]
    n_ctx_seq, n_lat_seq = cfg["n_ctx_seq"], cfg["n_lat_seq"]
    kern = functools.partial(_s5b_kernel, n_ctx_seq=n_ctx_seq, ctx_sub=cfg["ctx_len"] // SUB,
                             n_lat_seq=n_lat_seq, lat_sub=cfg["lat_len"] // SUB)
    nlb = S5_HALF // LANE
    return pl.pallas_call(
        kern,
        grid=(S5_Q, 2, nlb),
        in_specs=[pl.BlockSpec((None, None, 2, rows, LANE), lambda q, d, b: (q, d, 0, 0, b)),
                  pl.BlockSpec((None, None, 2, 1, LANE), lambda q, d, b: (q, d, 0, 0, b)),
                  pl.BlockSpec((None, None, 2, n_lat_seq, LANE), lambda q, d, b: (q, d, 0, 0, b))],
        out_specs=[pl.BlockSpec((None, None, 2, rows, LANE), lambda q, d, b: (q, d, 0, 0, b)),
                   pl.BlockSpec((None, None, 2, n_ctx_seq, LANE), lambda q, d, b: (q, d, 0, 0, b))],
        out_shape=[jax.ShapeDtypeStruct(loc.shape, F32),
                   jax.ShapeDtypeStruct((S5_Q, 2, 2, n_ctx_seq, S5_HALF), F32)],
        compiler_params=_params(("parallel", "parallel", "parallel"), 48),
        name="s5_scan",
    )(loc, a, x0)


def _s5c_kernel(u_ref, xp_ref, k_ref, b_ref, d_ref, z_ref, m_scr, wo_scr, z_scr):
    rt = u_ref.shape[0]

    @pl.when(pl.program_id(1) == 0)
    def _():
        for p in range(S5_P):
            lags = jnp.concatenate([_expand(k_ref[p, l], S5_PG, 4, 4) for l in range(N_LAG)]
                                   + [jnp.zeros((S5_PB, S5_PB), BF16)], axis=1)
            for t in range(SUB):
                lo = (SUB - 1 - t) * S5_PB
                m_scr[p, t * S5_PB:(t + 1) * S5_PB, :] = lags[:, lo:lo + S5_PFLAT]
            for k in range(4):
                wo_scr[p, k * S5_PHALF:(k + 1) * S5_PHALF, :] = jnp.concatenate(
                    [_expand(b_ref[p, k, t], S5_PG, 6, 4) for t in range(SUB)], axis=1)

    z = []
    for p in range(S5_P):
        u = u_ref[:, p * S5_PFLAT:(p + 1) * S5_PFLAT]
        xcat = jnp.concatenate([xp_ref[d, r, :, p * S5_PHALF:(p + 1) * S5_PHALF] for d in range(2) for r in range(2)],
                               axis=1).astype(BF16)
        y = _dot(u, m_scr[p]) + _dot(xcat, wo_scr[p])
        z.append(jax.nn.gelu(d_ref[:, p * S5_PFLAT:(p + 1) * S5_PFLAT] * u.astype(F32) + y))
    for t in range(SUB):
        z_scr[pl.ds(t, rt, stride=SUB), :] = jnp.concatenate([zp[:, t * S5_PB:(t + 1) * S5_PB] for zp in z], axis=1)
    z_ref[...] = z_scr[...].astype(z_ref.dtype)


def _s5_output(u4, xprev, k_lag, b_out, d4, cfg):
    rows = u4.shape[1]
    rt = cfg["s5_rows"]
    return pl.pallas_call(
        _s5c_kernel,
        grid=(S5_Q, rows // rt),
        in_specs=[pl.BlockSpec((None, rt, S5_FLAT), lambda q, i: (q, i, 0)),
                  pl.BlockSpec((None, 2, 2, rt, S5_HALF), lambda q, i: (q, 0, 0, i, 0)),
                  pl.BlockSpec((None, S5_P, N_LAG, S5_GROUP, S5_PB), lambda q, i: (q, 0, 0, 0, 0)),
                  pl.BlockSpec((None, S5_P, 4, SUB, S5_GROUP, S5_PHALF), lambda q, i: (q, 0, 0, 0, 0, 0)),
                  pl.BlockSpec((None, 1, S5_FLAT), lambda q, i: (q, 0, 0))],
        out_specs=pl.BlockSpec((rt * SUB, LANE), lambda q, i: (i, q)),
        out_shape=jax.ShapeDtypeStruct((rows * SUB, S5_WIDTH), BF16),
        scratch_shapes=[pltpu.VMEM((S5_P, S5_PFLAT, S5_PFLAT), BF16), pltpu.VMEM((S5_P, 4 * S5_PHALF, S5_PFLAT), BF16),
                        pltpu.VMEM((rt * SUB, LANE), F32)],
        compiler_params=_params(("parallel", "arbitrary"), 56),
        name="s5_output",
    )(u4, xprev, k_lag, b_out, d4)


def _group_norm(o, g, b):
    mu = jnp.mean(o, axis=-1, keepdims=True)
    var = jnp.mean(jnp.square(o - mu), axis=-1, keepdims=True)
    return (o - mu) * lax.rsqrt(var + GN_EPS) * g + b


def _ret_kernel(dec_ref, q_ref, k_ref, v_ref, g_ref, cos_ref, sin_ref, s0_ref, gg_ref, gb_ref, y_ref, *rest,
                nc, n_sub, use_rot, has_init, want_final):
    if want_final:
        sfin_ref, sf_scr, sb_scr, x_scr, kr_scr = rest
    else:
        sf_scr, sb_scr, x_scr, kr_scr = rest
    h = pl.program_id(1)
    row = _iota((CHUNK, CHUNK), 0).astype(F32)
    col = _iota((CHUNK, CHUNK), 1).astype(F32)
    lg_f = -jnp.exp(jnp.full((CHUNK, CHUNK), dec_ref[0, h], F32))
    lg_b = -jnp.exp(jnp.full((CHUNK, CHUNK), dec_ref[1, h], F32))
    lag = row - col
    scale = HEAD_DIM ** -0.5
    dmat = (jnp.where(lag >= 0, jnp.exp(lg_f * jnp.maximum(lag, 0.0)), 0.0)
            + jnp.where(lag <= 0, jnp.exp(lg_b * jnp.maximum(-lag, 0.0)), 0.0)) * scale
    qd_f = jnp.exp(lg_f * (row + 1.0))
    qd_b = jnp.exp(lg_b * (CHUNK - row))
    kd_f = jnp.exp(lg_f * (CHUNK - 1.0 - col)) * scale
    kd_b = jnp.exp(lg_b * col) * scale
    cd_f = jnp.exp(lg_f * CHUNK)
    cd_b = jnp.exp(lg_b * CHUNK)

    def chunk(s, j):
        return pl.ds(pl.multiple_of((s * nc + j) * CHUNK, CHUNK), CHUNK)

    def rot(ref, sl):
        x = ref[sl, :].astype(F32)
        if not use_rot:
            return x
        return x * cos_ref[sl, :] + pltpu.roll(x, HEAD_DIM // 2, 1) * sin_ref[sl, :]

    def local(s, j, carry):
        sl = chunk(s, j)
        k = rot(k_ref, sl)
        kr_scr[sl, :] = k.astype(BF16)
        k_t = k.T
        v = v_ref[sl, :]
        x_scr[0, j] = _dot((k_t * kd_f).astype(BF16), v)
        x_scr[1, j] = _dot((k_t * kd_b).astype(BF16), v)
        return carry

    zero = jnp.zeros((CHUNK, CHUNK), F32)

    def states(jj, carry):
        s_f, s_b = carry
        jf, jb = jj, nc - 1 - jj
        sf_scr[jf] = s_f.astype(BF16)
        sb_scr[jb] = s_b.astype(BF16)
        return s_f * cd_f + x_scr[0, jf], s_b * cd_b + x_scr[1, jb]

    def outputs(s, j, carry):
        sl = chunk(s, j)
        q = rot(q_ref, sl)
        att = _dot_nt(q.astype(BF16), kr_scr[sl, :]) * dmat
        lhs = jnp.concatenate([att.astype(BF16), (q * qd_f).astype(BF16), (q * qd_b).astype(BF16)], axis=1)
        rhs = jnp.concatenate([v_ref[sl, :], sf_scr[j], sb_scr[j]], axis=0)
        o = _dot(lhs, rhs)
        y = jax.nn.silu(g_ref[sl, :].astype(F32)) * _group_norm(o, gg_ref[...], gb_ref[...])
        y_ref[sl, :] = y.astype(y_ref.dtype)
        return carry

    def sequence(s, carry):
        lax.fori_loop(0, nc, functools.partial(local, s), 0, unroll=min(4, nc))
        init = (s0_ref[s, 0], s0_ref[s, 1]) if has_init else (zero, zero)
        s_f, s_b = lax.fori_loop(0, nc, states, init, unroll=2)
        if want_final:
            sfin_ref[s, 0] = s_f
            sfin_ref[s, 1] = s_b
        lax.fori_loop(0, nc, functools.partial(outputs, s), 0, unroll=min(8, nc))
        return carry

    if n_sub == 1:
        sequence(0, 0)
    else:
        lax.fori_loop(0, n_sub, sequence, 0)


def _retention(proj, dec, cos_t, sin_t, s0, gn_g, gn_b, *, n_seq, seq_len, n_sub, row0, use_rot, has_init,
               want_final):
    assert n_seq % n_sub == 0 and not (use_rot and n_sub > 1)
    nc = seq_len // CHUNK
    rows = n_sub * seq_len
    blk0 = row0 // rows
    kern = functools.partial(_ret_kernel, nc=nc, n_sub=n_sub, use_rot=use_rot, has_init=has_init, want_final=want_final)

    def tok(cb):
        return pl.BlockSpec((rows, HEAD_DIM), lambda s, h, cb=cb: (blk0 + s, cb + h))

    rot_spec = pl.BlockSpec((seq_len, HEAD_DIM), lambda s, h: (0, 0))
    st_spec = pl.BlockSpec((n_sub, 2, None, HEAD_DIM, HEAD_DIM), lambda s, h: (s, 0, h, 0, 0))
    gn_spec = pl.BlockSpec((1, HEAD_DIM), lambda s, h: (0, h))
    out_specs = [pl.BlockSpec((rows, HEAD_DIM), lambda s, h: (s, h))]
    out_shape = [jax.ShapeDtypeStruct((n_seq * seq_len, WIDTH), BF16)]
    if want_final:
        out_specs.append(st_spec)
        out_shape.append(jax.ShapeDtypeStruct((n_seq, 2, HEADS, HEAD_DIM, HEAD_DIM), F32))
    return pl.pallas_call(
        kern,
        grid=(n_seq // n_sub, HEADS),
        in_specs=[pl.BlockSpec(memory_space=pltpu.SMEM), tok(0), tok(4), tok(8), tok(12),
                  rot_spec, rot_spec, st_spec, gn_spec, gn_spec],
        out_specs=out_specs,
        out_shape=out_shape,
        scratch_shapes=[pltpu.VMEM((nc, HEAD_DIM, HEAD_DIM), BF16), pltpu.VMEM((nc, HEAD_DIM, HEAD_DIM), BF16),
                        pltpu.VMEM((2, nc, HEAD_DIM, HEAD_DIM), F32), pltpu.VMEM((rows, HEAD_DIM), BF16)],
        compiler_params=_params(("parallel", "parallel"), 48),
        name="retention",
    )(dec, proj, proj, proj, proj, cos_t, sin_t, s0, gn_g, gn_b)


def _gate_prep_kernel(g_ref, col_ref, row_ref):
    lane = _iota((CHUNK, LANE), 1)
    tri = jnp.where(_iota((CHUNK, CHUNK), 0) >= _iota((CHUNK, CHUNK), 1), 1.0, 0.0)
    for c in range(g_ref.shape[0] // CHUNK):
        sl = slice(c * CHUNK, (c + 1) * CHUNK)
        g = g_ref[sl, :]
        lf = jnp.where(lane < 4 * HEADS, jnp.minimum(g, 0.0) - jnp.log1p(jnp.exp(-jnp.abs(g))), 0.0)
        cs = _dot_hi(tri, lf)
        tot = cs[CHUNK - 1:CHUNK, :]
        bc = jnp.where(lane < 2 * HEADS, cs, tot - cs + lf)
        rest = jnp.where(lane < 2 * HEADS, tot - cs, cs - lf)
        pack = g + pltpu.roll(bc, GATE_BC, 1) + pltpu.roll(rest, GATE_REST, 1)
        col_ref[sl, :] = pack
        row_ref[sl, :] = pack.T


def _gate_prep(gates, cfg):
    t_tok = gates.shape[0]
    tm = cfg["tm_in"]
    spec = pl.BlockSpec((tm, LANE), lambda i: (i, 0))
    return pl.pallas_call(
        _gate_prep_kernel,
        grid=(t_tok // tm,),
        in_specs=[spec],
        out_specs=[spec, spec],
        out_shape=[jax.ShapeDtypeStruct((t_tok, LANE), F32)] * 2,
        compiler_params=_params(("parallel",), 32),
        name="gate_prep",
    )(gates)


def _mlstm_kernel(q_ref, k_ref, v_ref, o_ref, col_ref, row_ref, c0_ref, n0_ref, m0_ref, gg_ref, gb_ref, y_ref, *rest,
                  nc, n_sub, has_init, want_final):
    if want_final:
        cfin_ref, nfin_ref, mfin_ref, c_scr, n_scr, m_scr, x_scr, nl_scr, ml_scr, bl_scr = rest
    else:
        c_scr, n_scr, m_scr, x_scr, nl_scr, ml_scr, bl_scr = rest
    src = _iota((CHUNK, CHUNK), 0)
    dst = _iota((CHUNK, CHUNK), 1)
    scale = HEAD_DIM ** -0.5

    def chunk(s, j):
        return pl.ds(pl.multiple_of((s * nc + j) * CHUNK, CHUNK), CHUNK)

    def gate_idx(d, h):
        return d * 2 * HEADS + h, GATE_BC + d * 2 * HEADS + HEADS + h, GATE_REST + d * 2 * HEADS + HEADS + h

    def head_body(h):
        def init(s, d):
            if has_init:
                return c0_ref[s, d], n0_ref[s, d], m0_ref[s, d]
            return (jnp.zeros((HEAD_DIM, HEAD_DIM), F32), jnp.zeros((1, HEAD_DIM), F32),
                    jnp.zeros((1, HEAD_DIM), F32))

        def local(s, j, carry):
            sl = chunk(s, j)
            k = k_ref[sl, :]
            v_t = v_ref[sl, :].astype(F32).T
            rp = row_ref[sl, :]
            for d in range(2):
                ii, bi, ri = gate_idx(d, h)
                bc_row = rp[bi:bi + 1, :]
                b_last = bc_row[:, CHUNK - 1:CHUNK] if d == 0 else bc_row[:, 0:1]
                log_k = rp[ri:ri + 1, :] + rp[ii:ii + 1, :]
                m_loc = jnp.max(log_k, axis=1, keepdims=True)
                kw = jnp.exp(log_k - m_loc)
                x_scr[d, j] = _dot((v_t * kw).astype(BF16), k) * scale
                kw_hi = kw.astype(BF16).astype(F32)
                kw2 = jnp.concatenate([jnp.broadcast_to(kw_hi, (8, CHUNK)), jnp.broadcast_to(kw - kw_hi, (8, CHUNK))], 0)
                nl = _dot(kw2.astype(BF16), k)
                nl_scr[d, j] = (nl[0:1, :] + nl[8:9, :]) * scale
                ml_scr[d, j] = jnp.broadcast_to(m_loc, (1, HEAD_DIM))
                bl_scr[d, j] = jnp.broadcast_to(b_last, (1, HEAD_DIM))
            return carry

        def states(jj, carry):
            out = []
            for d, j in ((0, jj), (1, nc - 1 - jj)):
                cmat, nvec, m = carry[d]
                c_scr[d, j] = cmat.astype(BF16)
                n_scr[d, j] = nvec
                m_scr[d, j] = m
                m_new = jnp.maximum(bl_scr[d, j] + m, ml_scr[d, j])
                keep = jnp.exp(bl_scr[d, j] + m - m_new)
                add = jnp.exp(ml_scr[d, j] - m_new)
                out.append((keep * cmat + add * x_scr[d, j], keep * nvec + add * nl_scr[d, j], m_new))
            return tuple(out)

        def outputs(s, j, carry):
            sl = chunk(s, j)
            q = q_ref[sl, :]
            v = v_ref[sl, :]
            s_t = _dot_nt(k_ref[sl, :], q) * scale
            cp = col_ref[sl, :]
            rp = row_ref[sl, :]
            h_t = None
            for d in range(2):
                ii, bi, _ = gate_idx(d, h)
                causal = (src <= dst) if d == 0 else (src >= dst)
                bc_row = rp[bi:bi + 1, :]
                log_d = jnp.where(causal, bc_row + (cp[:, ii:ii + 1] - cp[:, bi:bi + 1]), -jnp.inf)
                log_prev = bc_row + m_scr[d, j]
                m_t = jnp.maximum(log_prev, jnp.max(log_d, axis=0, keepdims=True))
                w = s_t * jnp.exp(log_d - m_t)
                w_prev = jnp.exp(log_prev - m_t)
                qn = _dot_nt(jnp.broadcast_to(n_scr[d, j], (16, HEAD_DIM)).astype(BF16), q)[0:1, :]
                den = jnp.sum(w, axis=0, keepdims=True) + w_prev * qn
                inv = 1.0 / jnp.maximum(jnp.abs(den), jnp.exp(-m_t))
                num = _dot_tn(v, w.astype(BF16)) + _dot_nt(c_scr[d, j], q) * w_prev
                h_t = num * inv if h_t is None else h_t + num * inv
            y = jax.nn.sigmoid(o_ref[sl, :].astype(F32).T) * h_t
            mu = jnp.mean(y, axis=0, keepdims=True)
            var = jnp.mean(jnp.square(y - mu), axis=0, keepdims=True)
            y = (y - mu) * lax.rsqrt(var + GN_EPS) * gg_ref[...] + gb_ref[...]
            y_ref[sl, :] = y.T.astype(y_ref.dtype)
            return carry

        def sequence(s, carry):
            lax.fori_loop(0, nc, functools.partial(local, s), 0, unroll=min(4, nc))
            fin = lax.fori_loop(0, nc, states, (init(s, 0), init(s, 1)), unroll=2)
            if want_final:
                for d in range(2):
                    cfin_ref[s, d] = fin[d][0]
                    nfin_ref[s, d] = fin[d][1]
                    mfin_ref[s, d] = fin[d][2]
            lax.fori_loop(0, nc, functools.partial(outputs, s), 0, unroll=min(4, nc))
            return carry

        if n_sub == 1:
            sequence(0, 0)
        else:
            lax.fori_loop(0, n_sub, sequence, 0)

    hh = pl.program_id(1)
    for h in range(HEADS):
        pl.when(hh == h)(functools.partial(head_body, h))


def _mlstm(proj, gcol, grow, c0, n0, m0, gn_g, gn_b, *, n_seq, seq_len, n_sub, row0, has_init, want_final):
    assert n_seq % n_sub == 0
    nc = seq_len // CHUNK
    rows = n_sub * seq_len
    blk0 = row0 // rows
    kern = functools.partial(_mlstm_kernel, nc=nc, n_sub=n_sub, has_init=has_init, want_final=want_final)

    def tok(cb):
        return pl.BlockSpec((rows, HEAD_DIM), lambda s, h, cb=cb: (blk0 + s, cb + h))

    gate_spec = pl.BlockSpec((rows, LANE), lambda s, h: (blk0 + s, 0))
    c_spec = pl.BlockSpec((n_sub, 2, None, HEAD_DIM, HEAD_DIM), lambda s, h: (s, 0, h, 0, 0))
    v_spec = pl.BlockSpec((n_sub, 2, None, 1, HEAD_DIM), lambda s, h: (s, 0, h, 0, 0))
    gn_spec = pl.BlockSpec((HEAD_DIM, LANE), lambda s, h: (h, 0))
    out_specs = [pl.BlockSpec((rows, HEAD_DIM), lambda s, h: (s, h))]
    out_shape = [jax.ShapeDtypeStruct((n_seq * seq_len, WIDTH), BF16)]
    if want_final:
        out_specs += [c_spec, v_spec, v_spec]
        out_shape += [jax.ShapeDtypeStruct((n_seq, 2, HEADS, HEAD_DIM, HEAD_DIM), F32),
                      jax.ShapeDtypeStruct((n_seq, 2, HEADS, 1, HEAD_DIM), F32),
                      jax.ShapeDtypeStruct((n_seq, 2, HEADS, 1, HEAD_DIM), F32)]
    return pl.pallas_call(
        kern,
        grid=(n_seq // n_sub, HEADS),
        in_specs=[tok(16), tok(20), tok(24), tok(28), gate_spec, gate_spec,
                  c_spec, v_spec, v_spec, gn_spec, gn_spec],
        out_specs=out_specs,
        out_shape=out_shape,
        scratch_shapes=[pltpu.VMEM((2, nc, HEAD_DIM, HEAD_DIM), BF16)] + [pltpu.VMEM((2, nc, 1, HEAD_DIM), F32)] * 2
        + [pltpu.VMEM((2, nc, HEAD_DIM, HEAD_DIM), F32)] + [pltpu.VMEM((2, nc, 1, HEAD_DIM), F32)] * 3,
        compiler_params=_params(("parallel", "parallel"), 48),
        name="mlstm",
    )(proj, proj, proj, proj, gcol, grow, c0, n0, m0, gn_g, gn_b)


def _layer_norm(x, g, b):
    mu = jnp.mean(x, axis=-1, keepdims=True)
    var = jnp.mean(jnp.square(x - mu), axis=-1, keepdims=True)
    return (x - mu) * lax.rsqrt(var + LN_EPS) * g + b


def _top2_gates(h, whi_ref, wlo_ref, b_ref):
    lane = _iota((h.shape[0], LANE), 1)
    h_hi = h.astype(BF16)
    h_lo = (h - h_hi.astype(F32)).astype(BF16)
    logits = _dot(h_hi, whi_ref[...]) + _dot(h_hi, wlo_ref[...]) + _dot(h_lo, whi_ref[...]) + b_ref[...]
    logits = jnp.where(lane < N_EXPERTS, logits, NEG_BIG)
    m1 = jnp.max(logits, axis=1, keepdims=True)
    i1 = jnp.min(jnp.where(logits == m1, lane, LANE), axis=1, keepdims=True)
    rest = jnp.where(lane == i1, NEG_BIG, logits)
    m2 = jnp.max(rest, axis=1, keepdims=True)
    i2 = jnp.min(jnp.where(rest == m2, lane, LANE), axis=1, keepdims=True)
    e2 = jnp.exp(m2 - m1)
    den = 1.0 + e2
    return jnp.where(lane == i1, 1.0 / den, 0.0) + jnp.where(lane == i2, e2 / den, 0.0)


def _merge_kernel(*refs, n_ctx_tiles, with_router, split):
    if split:
        xc_ref, xs_ref = refs[:2]
    (x_ref, mod_ref, z_ref, ybc_ref, ybs_ref, ycc_ref, ycs_ref, wm_ref, bm_ref, wglu_ref, bglu_ref,
     wa_ref, wb_ref, wc_ref, wo_ref, lg_ref, lb_ref, *rest) = refs[1 if split else 0:]
    if with_router:
        rwh_ref, rwl_ref, rb_ref, o_ref, g_ref = rest
    else:
        o_ref, = rest
    is_ctx = pl.program_id(0) < n_ctx_tiles
    x = jnp.where(is_ctx, xc_ref[...], xs_ref[...]) if split else x_ref[...]
    h = (x * (1.0 + mod_ref[1:2, :]) + mod_ref[0:1, :]).astype(BF16)
    z = z_ref[...]
    ya = (z.astype(F32) * jax.nn.sigmoid(_dot(z, wglu_ref[...]) + bglu_ref[...])).astype(BF16)
    yb = jnp.where(is_ctx, ybc_ref[...], ybs_ref[...])
    yc = jnp.where(is_ctx, ycc_ref[...], ycs_ref[...])
    merged = None
    for j, (y, w_ref) in enumerate(((ya, wa_ref), (yb, wb_ref), (yc, wc_ref))):
        gate = jax.nn.sigmoid(_dot(h, wm_ref[:, j * D_MODEL:(j + 1) * D_MODEL]) + bm_ref[:, j * D_MODEL:(j + 1) * D_MODEL])
        term = gate * _dot(y, w_ref[...])
        merged = term if merged is None else merged + term
    mix = _dot(merged.astype(BF16), wo_ref[...])
    x1 = _layer_norm(ALPHA * x + mod_ref[2:3, :] * mix, lg_ref[...], lb_ref[...])
    o_ref[...] = x1
    if with_router:
        g_ref[...] = _top2_gates(x1 * (1.0 + mod_ref[4:5, :]) + mod_ref[3:4, :], rwh_ref, rwl_ref, rb_ref)


def _merge(x, mod_l, z, yb_c, yb_s, yc_c, yc_s, wm, bm, wglu, bglu, wa, wb, wc, wo, lg, lb, router, cfg):
    split = isinstance(x, tuple)
    xs = list(x) if split else [x]
    t_tok = sum(a.shape[0] for a in xs)
    tm = cfg["tm_merge"]
    n_ctx_tiles = cfg["n_ctx_tok"] // tm
    cond = functools.partial(_cond_row, tm=tm, n_ctx_tok=cfg["n_ctx_tok"], lat_len=cfg["lat_len"])

    def full(shape):
        return pl.BlockSpec(shape, lambda i: (0,) * len(shape))

    def tok(w):
        return pl.BlockSpec((tm, w), lambda i: (i, 0))

    ctx_spec, lat_spec = _group_specs(tm, WIDTH, n_ctx_tiles)
    x_specs = list(_group_specs(tm, D_MODEL, n_ctx_tiles)) if split else [tok(D_MODEL)]
    in_specs = x_specs + [pl.BlockSpec((None, 6, D_MODEL), lambda i: (cond(i), 0, 0)),
                tok(WIDTH), ctx_spec, lat_spec, ctx_spec, lat_spec,
                full((D_MODEL, 3 * D_MODEL)), full((1, 3 * D_MODEL)), full((WIDTH, WIDTH)), full((1, WIDTH)),
                full((WIDTH, D_MODEL)), full((WIDTH, D_MODEL)), full((WIDTH, D_MODEL)),
                full((D_MODEL, D_MODEL)), full((1, D_MODEL)), full((1, D_MODEL))]
    args = xs + [mod_l, z, yb_c, yb_s, yc_c, yc_s, wm, bm, wglu, bglu, wa, wb, wc, wo, lg, lb]
    out_specs = [tok(D_MODEL)]
    out_shape = [jax.ShapeDtypeStruct((t_tok, D_MODEL), F32)]
    if router is not None:
        in_specs += [full((D_MODEL, LANE)), full((D_MODEL, LANE)), full((1, LANE))]
        args += list(router)
        out_specs.append(tok(LANE))
        out_shape.append(jax.ShapeDtypeStruct((t_tok, LANE), F32))
    return pl.pallas_call(
        functools.partial(_merge_kernel, n_ctx_tiles=n_ctx_tiles, with_router=router is not None, split=split),
        grid=(t_tok // tm,),
        in_specs=in_specs,
        out_specs=out_specs,
        out_shape=out_shape,
        compiler_params=_params(("parallel",), 56),
        name="merge",
    )(*args)


def _ffn_kernel(x_ref, mod_ref, w1_ref, w3_ref, w2_ref, lg_ref, lb_ref, o_ref, h_scr, acc_scr):
    f = pl.program_id(1)

    @pl.when(f == 0)
    def _():
        h_scr[...] = (x_ref[...] * (1.0 + mod_ref[4:5, :]) + mod_ref[3:4, :]).astype(BF16)
        acc_scr[...] = jnp.zeros_like(acc_scr)

    h = h_scr[...]
    act = (jax.nn.silu(_dot(h, w1_ref[...])) * _dot(h, w3_ref[...])).astype(BF16)
    acc_scr[...] += _dot(act, w2_ref[...])

    @pl.when(f == pl.num_programs(1) - 1)
    def _():
        o_ref[...] = _layer_norm(ALPHA * x_ref[...] + mod_ref[5:6, :] * acc_scr[...], lg_ref[...], lb_ref[...])


def _dense_ffn(x, mod_l, w1, w3, w2, lg, lb, cfg):
    t_tok = x.shape[0]
    tm, tf = cfg["tm_ffn"], D_FF
    cond = functools.partial(_cond_row, tm=tm, n_ctx_tok=cfg["n_ctx_tok"], lat_len=cfg["lat_len"])
    return pl.pallas_call(
        _ffn_kernel,
        grid=(t_tok // tm, D_FF // tf),
        in_specs=[pl.BlockSpec((tm, D_MODEL), lambda i, f: (i, 0)),
                  pl.BlockSpec((None, 6, D_MODEL), lambda i, f: (cond(i), 0, 0)),
                  pl.BlockSpec((D_MODEL, tf), lambda i, f: (0, f)),
                  pl.BlockSpec((D_MODEL, tf), lambda i, f: (0, f)),
                  pl.BlockSpec((tf, D_MODEL), lambda i, f: (f, 0)),
                  pl.BlockSpec((1, D_MODEL), lambda i, f: (0, 0)),
                  pl.BlockSpec((1, D_MODEL), lambda i, f: (0, 0))],
        out_specs=pl.BlockSpec((tm, D_MODEL), lambda i, f: (i, 0)),
        out_shape=jax.ShapeDtypeStruct((t_tok, D_MODEL), F32),
        scratch_shapes=[pltpu.VMEM((tm, D_MODEL), BF16), pltpu.VMEM((tm, D_MODEL), F32)],
        compiler_params=_params(("parallel", "arbitrary"), 56),
        name="dense_ffn",
    )(x, mod_l, w1, w3, w2, lg, lb)


def _moe_kernel(x_ref, mod_ref, gate_ref, w1_ref, w3_ref, w2_ref, lg_ref, lb_ref, oc_ref, os_ref,
                h_scr, acc_scr, hc_scr, ob_scr, sp_scr, gt_scr, cnt_smem, *, n_ctx_tiles):
    e = pl.program_id(1)
    f = pl.program_id(2)
    last_f = pl.num_programs(2) - 1
    tm = x_ref.shape[0]

    @pl.when((e == 0) & (f == 0))
    def _():
        h_scr[...] = (x_ref[...] * (1.0 + mod_ref[4:5, :]) + mod_ref[3:4, :]).astype(BF16)
        acc_scr[...] = jnp.zeros_like(acc_scr)
        g = gate_ref[...]
        sel = g > 0.0
        ones = jnp.where(sel, 1.0, 0.0)
        before = jnp.where(_iota((tm, tm), 0) > _iota((tm, tm), 1), 1.0, 0.0).astype(BF16)
        pos = _dot(before, ones.astype(BF16))
        spt = jnp.where(sel, pos, -1.0).T
        gt = g.T
        cnt = jnp.sum(ones, axis=0, keepdims=True)
        for ee in range(N_EXPERTS):
            sp_scr[ee] = spt[ee:ee + 1, :]
            gt_scr[ee] = gt[ee:ee + 1, :]
            cnt_smem[ee] = cnt[0, ee].astype(jnp.int32)

    nb = (cnt_smem[e] + (MOE_BLK - 1)) // MOE_BLK
    row = _iota((MOE_BLK, tm), 0)

    def onehot(b):
        return sp_scr[e] == (row + b * MOE_BLK).astype(F32)

    @pl.when(f == 0)
    def _():
        def gather(b, carry):
            p = jnp.where(onehot(b), 1.0, 0.0).astype(BF16)
            hc_scr[b] = _dot(p, h_scr[...]).astype(BF16)
            ob_scr[b] = jnp.zeros((MOE_BLK, D_MODEL), F32)
            return carry
        lax.fori_loop(0, nb, gather, 0)

        @pl.when(nb % 2 == 1)
        def _():
            ob_scr[nb] = jnp.zeros((MOE_BLK, D_MODEL), F32)

    def ffn(hc):
        act = (jax.nn.silu(_dot(hc, w1_ref[...])) * _dot(hc, w3_ref[...])).astype(BF16)
        return _dot(act, w2_ref[...])

    def ffn_pair(p, carry):
        two = pl.ds(2 * p, 2)
        ob_scr[two] += ffn(hc_scr[two].reshape(2 * MOE_BLK, D_MODEL)).reshape(2, MOE_BLK, D_MODEL)
        return carry
    lax.fori_loop(0, nb // 2, ffn_pair, 0)

    @pl.when(nb % 2 == 1)
    def _():
        ob_scr[nb - 1] += ffn(hc_scr[nb - 1])

    @pl.when(f == last_f)
    def _():
        row2 = _iota((2 * MOE_BLK, tm), 0)

        def scatter(p, carry):
            m = sp_scr[e] == (row2 + p * (2 * MOE_BLK)).astype(F32)
            gc = jnp.sum(jnp.where(m, gt_scr[e], 0.0), axis=1, keepdims=True)
            og = (ob_scr[pl.ds(2 * p, 2)].reshape(2 * MOE_BLK, D_MODEL) * gc).astype(BF16)
            acc_scr[...] += _dot_tn(jnp.where(m, 1.0, 0.0).astype(BF16), og)
            return carry
        lax.fori_loop(0, (nb + 1) // 2, scatter, 0)

    last = (e == pl.num_programs(1) - 1) & (f == last_f)
    is_ctx = pl.program_id(0) < n_ctx_tiles
    for o_ref, mine in ((oc_ref, is_ctx), (os_ref, jnp.logical_not(is_ctx))):
        @pl.when(last & mine)
        def _():
            o_ref[...] = _layer_norm(ALPHA * x_ref[...] + mod_ref[5:6, :] * acc_scr[...], lg_ref[...], lb_ref[...])


def _moe_ffn(x, mod_l, gates, w1, w3, w2, layer, lg, lb, cfg):
    t_tok = x.shape[0]
    tm, tf = cfg["tm_moe"], cfg["tf"]
    n_ctx_tiles = cfg["n_ctx_tok"] // tm
    cond = functools.partial(_cond_row, tm=tm, n_ctx_tok=cfg["n_ctx_tok"], lat_len=cfg["lat_len"])
    return pl.pallas_call(
        functools.partial(_moe_kernel, n_ctx_tiles=n_ctx_tiles),
        grid=(t_tok // tm, N_EXPERTS, D_FF // tf),
        in_specs=[pl.BlockSpec((tm, D_MODEL), lambda i, e, f: (i, 0), pipeline_mode=pl.Buffered(1)),
                  pl.BlockSpec((None, 6, D_MODEL), lambda i, e, f: (cond(i), 0, 0)),
                  pl.BlockSpec((tm, LANE), lambda i, e, f: (i, 0), pipeline_mode=pl.Buffered(1)),
                  pl.BlockSpec((None, None, D_MODEL, tf), lambda i, e, f: (layer, e, 0, f)),
                  pl.BlockSpec((None, None, D_MODEL, tf), lambda i, e, f: (layer, e, 0, f)),
                  pl.BlockSpec((None, None, tf, D_MODEL), lambda i, e, f: (layer, e, f, 0)),
                  pl.BlockSpec((1, D_MODEL), lambda i, e, f: (0, 0)),
                  pl.BlockSpec((1, D_MODEL), lambda i, e, f: (0, 0))],
        out_specs=list(_group_specs(tm, D_MODEL, n_ctx_tiles)),
        out_shape=[jax.ShapeDtypeStruct((cfg["n_ctx_tok"], D_MODEL), F32),
                   jax.ShapeDtypeStruct((t_tok - cfg["n_ctx_tok"], D_MODEL), F32)],
        scratch_shapes=[pltpu.VMEM((tm, D_MODEL), BF16), pltpu.VMEM((tm, D_MODEL), F32),
                        pltpu.VMEM((tm // MOE_BLK, MOE_BLK, D_MODEL), BF16),
                        pltpu.VMEM((tm // MOE_BLK, MOE_BLK, D_MODEL), F32),
                        pltpu.VMEM((N_EXPERTS, 1, tm), F32), pltpu.VMEM((N_EXPERTS, 1, tm), F32),
                        pltpu.SMEM((N_EXPERTS,), jnp.int32)],
        compiler_params=_params(("arbitrary", "arbitrary", "arbitrary"), 56),
        name="moe_ffn",
    )(x, mod_l, gates, w1, w3, w2, lg, lb)


def _rotary_tables(n_tok):
    rows = n_tok // GRID_W
    r = jnp.repeat(jnp.arange(rows, dtype=F32), GRID_W)
    col = jnp.tile(jnp.arange(GRID_W, dtype=F32), rows)
    n_freq = HEAD_DIM // 4
    inv = ROPE_BASE ** (-jnp.arange(n_freq, dtype=F32) / n_freq)
    ang = jnp.concatenate([r[:, None] * inv, col[:, None] * inv], -1)
    cos, sin = jnp.cos(ang), jnp.sin(ang)
    return jnp.concatenate([cos, cos], -1), jnp.concatenate([-sin, sin], -1)


def _row2(v):
    return v.reshape(1, -1).astype(F32)


def kernel(x_prompt, x_sample, cache_s5_re, cache_s5_im, cache_ret, cache_ml_c, cache_ml_n, cache_ml_m, c, c_ctx, ada_w, ada_b, w_in, b_in, s5_lam_re, s5_lam_im, s5_log_step, s5_b_re, s5_b_im, s5_c_re, s5_c_im, s5_d, s5_glu_w, s5_glu_b, ret_decay, ret_gn_g, ret_gn_b, ml_gn_g, ml_gn_b, w_a, w_b, w_c, w_o, ln1_g, ln1_b, ln2_g, ln2_b, ffn_w1, ffn_w3, ffn_w2, moe_router, moe_router_b, moe_w1, moe_w3, moe_w2):
    n_ctx_seq, ctx_len, _ = x_prompt.shape
    n_lat_seq, lat_len, _ = x_sample.shape
    n_ctx_tok = n_ctx_seq * ctx_len
    n_lat_tok = n_lat_seq * lat_len
    t_tok = n_ctx_tok + n_lat_tok
    assert n_lat_seq + 1 <= N_COND and n_ctx_seq % 8 == 0 and n_lat_seq % 8 == 0
    assert ctx_len % CHUNK == 0 and lat_len % CHUNK == 0 and n_ctx_tok % lat_len == 0
    cfg = dict(n_ctx_seq=n_ctx_seq, ctx_len=ctx_len, n_lat_seq=n_lat_seq, lat_len=lat_len, n_ctx_tok=n_ctx_tok,
               tm_in=min(2048, lat_len), tm_merge=min(512, lat_len), tm_ffn=min(256, lat_len), tm_moe=min(1024, lat_len), tf=1408,
               s5_rows=min(256, t_tok // SUB // 8))

    x = (x_prompt.reshape(n_ctx_tok, D_MODEL), x_sample.reshape(n_lat_tok, D_MODEL))
    cond = jnp.zeros((N_COND, D_MODEL), F32).at[0].set(c_ctx).at[1:1 + n_lat_seq].set(c)
    mod = _modulation(cond, ada_w, ada_b).reshape(DEPTH, N_COND, 6, D_MODEL)
    cos_t, sin_t = _rotary_tables(lat_len)

    n_main = S5_WIDTH + 8 * WIDTH
    gate_off = n_main
    merge_off = gate_off + 4 * HEADS
    s5_fac = jax.vmap(_s5_factors)(s5_lam_re, s5_lam_im, s5_log_step, s5_b_re, s5_b_im, s5_c_re, s5_c_im)

    st_s5, st_ret, st_c, st_n, st_m = [], [], [], [], []
    zero_ret = jnp.zeros((n_ctx_seq, 2, HEADS, HEAD_DIM, HEAD_DIM), F32)
    zero_vec = jnp.zeros((n_ctx_seq, 2, HEADS, 1, HEAD_DIM), F32)
    for l in range(DEPTH):
        mod_l = mod[l]
        w_main = w_in[l][:, :n_main].astype(BF16)
        b_main = _row2(b_in[l][:n_main])
        w_gate = jnp.zeros((D_MODEL, LANE), F32).at[:, :4 * HEADS].set(w_in[l][:, gate_off:merge_off]).astype(BF16)
        b_gate = jnp.zeros((1, LANE), F32).at[0, :4 * HEADS].set(b_in[l][gate_off:merge_off])
        proj, u4, gates = _inproj(x, mod_l, w_main, b_main, w_gate, b_gate, cfg)

        s5_ain, s5_klag, s5_bout, s5_a = (m[l] for m in s5_fac)
        loc = _s5_state_in(u4, s5_ain, cfg)
        x0 = jnp.stack([cache_s5_re[:, l], cache_s5_im[:, l]], 0)
        x0 = x0.reshape(2, n_lat_seq, 2, S5_Q, S5_HALF).transpose(3, 2, 0, 1, 4).astype(F32)
        xprev, s5_fin = _s5_scan(loc, s5_a, x0, cfg)
        d4 = jnp.broadcast_to(s5_d[l].astype(F32).reshape(S5_Q, S5_P, 1, S5_PB), (S5_Q, S5_P, SUB, S5_PB))
        d4 = d4.reshape(S5_Q, 1, S5_FLAT)
        z = _s5_output(u4, xprev, s5_klag, s5_bout, d4, cfg)
        st_s5.append(s5_fin)

        gg, gb = _row2(ret_gn_g[l]), _row2(ret_gn_b[l])
        dec = ret_decay[l].astype(F32)
        yb_c, ret_fin = _retention(proj, dec, cos_t, sin_t, zero_ret, gg, gb, n_seq=n_ctx_seq, seq_len=ctx_len,
                                   n_sub=8, row0=0, use_rot=False, has_init=False, want_final=True)
        yb_s, = _retention(proj, dec, cos_t, sin_t, cache_ret[:, l].astype(F32), gg, gb, n_seq=n_lat_seq,
                           seq_len=lat_len, n_sub=1, row0=n_ctx_tok, use_rot=True, has_init=True, want_final=False)
        st_ret.append(ret_fin)

        gg = jnp.broadcast_to(ml_gn_g[l].astype(F32)[:, None], (WIDTH, LANE))
        gb = jnp.broadcast_to(ml_gn_b[l].astype(F32)[:, None], (WIDTH, LANE))
        gcol, grow = _gate_prep(gates, cfg)
        yc_c, c_fin, n_fin, m_fin = _mlstm(proj, gcol, grow, zero_ret, zero_vec, zero_vec, gg, gb, n_seq=n_ctx_seq,
                                           seq_len=ctx_len, n_sub=8, row0=0, has_init=False, want_final=True)
        n0 = cache_ml_n[:, l].astype(F32)[:, :, :, None, :]
        m0 = jnp.broadcast_to(cache_ml_m[:, l].astype(F32)[:, :, :, None, None], n0.shape)
        yc_s, = _mlstm(proj, gcol, grow, cache_ml_c[:, l].astype(F32), n0, m0, gg, gb, n_seq=n_lat_seq,
                       seq_len=lat_len, n_sub=1, row0=n_ctx_tok, has_init=True, want_final=False)
        st_c.append(c_fin)
        st_n.append(n_fin[:, :, :, 0, :])
        st_m.append(m_fin[:, :, :, 0, 0])

        j = l // 2
        router = None
        if l % 2 == 1:
            rw = jnp.zeros((D_MODEL, LANE), F32).at[:, :N_EXPERTS].set(moe_router[j])
            rw_hi = rw.astype(BF16)
            router = (rw_hi, (rw - rw_hi.astype(F32)).astype(BF16),
                      jnp.zeros((1, LANE), F32).at[0, :N_EXPERTS].set(moe_router_b[j]))
        merged = _merge(x, mod_l, z, yb_c, yb_s, yc_c, yc_s, w_in[l][:, merge_off:].astype(BF16),
                        _row2(b_in[l][merge_off:]), s5_glu_w[l].astype(BF16), _row2(s5_glu_b[l]), w_a[l].astype(BF16),
                        w_b[l].astype(BF16), w_c[l].astype(BF16), w_o[l].astype(BF16), _row2(ln1_g[l]),
                        _row2(ln1_b[l]), router, cfg)

        if router is None:
            x = _dense_ffn(merged[0], mod_l, ffn_w1[j].astype(BF16), ffn_w3[j].astype(BF16), ffn_w2[j].astype(BF16),
                           _row2(ln2_g[l]), _row2(ln2_b[l]), cfg)
        else:
            x = tuple(_moe_ffn(merged[0], mod_l, merged[1], moe_w1.astype(BF16), moe_w3.astype(BF16),
                               moe_w2.astype(BF16), j, _row2(ln2_g[l]), _row2(ln2_b[l]), cfg))

    y_p = x[0].reshape(n_ctx_seq, ctx_len, D_MODEL)
    y_s = x[1].reshape(n_lat_seq, lat_len, D_MODEL)
    s5 = jnp.stack(st_s5, 0)
    s5 = s5.reshape(DEPTH, S5_Q, 2, 2, n_ctx_seq, S5_QG, S5_STATE).transpose(3, 4, 0, 2, 1, 5, 6)
    s5 = s5.reshape(2, n_ctx_seq, DEPTH, 2, S5_GROUPS, S5_STATE)
    return (y_p, y_s, s5[0], s5[1], jnp.stack(st_ret, 1), jnp.stack(st_c, 1), jnp.stack(st_n, 1),
            jnp.stack(st_m, 1))
```

```python
import functools

import jax
import jax.numpy as jnp
from jax import lax
from jax.experimental import pallas as pl
from jax.experimental.pallas import tpu as pltpu

F32 = jnp.float32
BF16 = jnp.bfloat16

D_MODEL = 1024
DEPTH = 4
GRID_W = 64
CHUNK = 128
S5_WIDTH = 512
S5_GROUP = 16
S5_GROUPS = 32
S5_STATE = 64
HEADS = 4
HEAD_DIM = 128
WIDTH = 512
ROPE_BASE = 10000.0
D_FF = 2816
N_EXPERTS = 8
ALPHA = (2.0 * DEPTH) ** 0.25
LN_EPS = 1e-5
GN_EPS = 1e-5
N_COND = 16
SUB = 16
N_LAG = 2 * SUB - 1
LANE = 128
S5_Q = S5_WIDTH // LANE
S5_QG = LANE // S5_GROUP
S5_FLAT = SUB * LANE
S5_HALF = S5_QG * S5_STATE
S5_ST = 4 * S5_HALF
S5_PG = 2
S5_P = S5_QG // S5_PG
S5_PB = S5_PG * S5_GROUP
S5_PFLAT = SUB * S5_PB
S5_PHALF = S5_PG * S5_STATE
NEG_BIG = -1e30
MOE_BLK = 128
GATE_BC = 16
GATE_REST = 32


def _dot(a, b):
    return jnp.dot(a, b, preferred_element_type=F32)


def _dot_hi(a, b):
    return jnp.dot(a, b, preferred_element_type=F32, precision=lax.Precision.HIGHEST)


def _dot_nt(a, b):
    return lax.dot_general(a, b, (((1,), (1,)), ((), ())), preferred_element_type=F32)


def _dot_tn(a, b):
    return lax.dot_general(a, b, (((0,), (0,)), ((), ())), preferred_element_type=F32)


def _params(sem, vmem_mb):
    return pltpu.CompilerParams(dimension_semantics=sem, vmem_limit_bytes=vmem_mb << 20)


def _cond_row(tile, tm, n_ctx_tok, lat_len):
    start = tile * tm
    return jnp.where(start < n_ctx_tok, 0, 1 + (start - n_ctx_tok) // lat_len)


def _iota(shape, axis):
    return lax.broadcasted_iota(jnp.int32, shape, axis)


def _group_specs(tm, width, n_ctx_tiles, pipeline_mode=None):
    return (pl.BlockSpec((tm, width), lambda i, *_: (jnp.minimum(i, n_ctx_tiles - 1), 0), pipeline_mode=pipeline_mode),
            pl.BlockSpec((tm, width), lambda i, *_: (jnp.maximum(i - n_ctx_tiles, 0), 0), pipeline_mode=pipeline_mode))


def _mod_kernel(c_ref, w_ref, b_ref, o_ref):
    o_ref[...] = _dot_hi(jax.nn.silu(c_ref[...]), w_ref[...]) + b_ref[...]


def _modulation(cond, ada_w, ada_b):
    tn = 1536
    n = ada_w.shape[-1]
    return pl.pallas_call(
        _mod_kernel,
        grid=(DEPTH, n // tn),
        in_specs=[pl.BlockSpec((N_COND, D_MODEL), lambda l, j: (0, 0)),
                  pl.BlockSpec((None, D_MODEL, tn), lambda l, j: (l, 0, j)),
                  pl.BlockSpec((None, 1, tn), lambda l, j: (l, 0, j))],
        out_specs=pl.BlockSpec((None, N_COND, tn), lambda l, j: (l, 0, j)),
        out_shape=jax.ShapeDtypeStruct((DEPTH, N_COND, n), F32),
        compiler_params=_params(("parallel", "parallel"), 40),
        name="modulation",
    )(cond, ada_w, ada_b.reshape(DEPTH, 1, n))


def _inproj_kernel(*refs, n_ctx_tiles, split):
    if split:
        xc_ref, xs_ref = refs[:2]
    x_ref, mod_ref, w_ref, b_ref, wg_ref, bg_ref, o_ref, u4_ref, g_ref, h_scr, u_scr = refs[1 if split else 0:]
    j = pl.program_id(1)
    tm = x_ref.shape[0]

    @pl.when(j == 0)
    def _():
        x = jnp.where(pl.program_id(0) < n_ctx_tiles, xc_ref[...], xs_ref[...]) if split else x_ref[...]
        h = (x * (1.0 + mod_ref[1:2, :]) + mod_ref[0:1, :]).astype(BF16)
        h_scr[...] = h
        g_ref[...] = _dot(h, wg_ref[...]) + bg_ref[...]
        u = _dot(h, w_ref[...]) + b_ref[...]
        for q in range(S5_Q):
            u_scr[q] = u[:, q * LANE:(q + 1) * LANE]
            rows = [u_scr[q, pl.ds(t, tm // SUB, stride=SUB), :].astype(BF16) for t in range(SUB)]
            for p in range(S5_P):
                u4_ref[q, :, p * S5_PFLAT:(p + 1) * S5_PFLAT] = jnp.concatenate(
                    [r[:, p * S5_PB:(p + 1) * S5_PB] for r in rows], axis=1)

    @pl.when(j > 0)
    def _():
        o_ref[...] = (_dot(h_scr[...], w_ref[...]) + b_ref[...]).astype(o_ref.dtype)


def _inproj(x, mod_l, w, b, wg, bg, cfg):
    split = isinstance(x, tuple)
    xs = list(x) if split else [x]
    t_tok = sum(a.shape[0] for a in xs)
    tm, tn = cfg["tm_in"], 512
    nj = w.shape[1] // tn
    n_ctx_tiles = cfg["n_ctx_tok"] // tm
    cond = functools.partial(_cond_row, tm=tm, n_ctx_tok=cfg["n_ctx_tok"], lat_len=cfg["lat_len"])
    x_specs = (list(_group_specs(tm, D_MODEL, n_ctx_tiles, pl.Buffered(1))) if split
               else [pl.BlockSpec((tm, D_MODEL), lambda i, j: (i, 0))])
    return pl.pallas_call(
        functools.partial(_inproj_kernel, n_ctx_tiles=n_ctx_tiles, split=split),
        grid=(t_tok // tm, nj),
        in_specs=x_specs + [
                  pl.BlockSpec((None, 6, D_MODEL), lambda i, j: (cond(i), 0, 0)),
                  pl.BlockSpec((D_MODEL, tn), lambda i, j: (0, j)),
                  pl.BlockSpec((1, tn), lambda i, j: (0, j)),
                  pl.BlockSpec((D_MODEL, LANE), lambda i, j: (0, 0)),
                  pl.BlockSpec((1, LANE), lambda i, j: (0, 0))],
        out_specs=[pl.BlockSpec((tm, tn), lambda i, j: (i, jnp.maximum(j - 1, 0))),
                   pl.BlockSpec((S5_Q, tm // SUB, S5_FLAT), lambda i, j: (0, i, 0)),
                   pl.BlockSpec((tm, LANE), lambda i, j: (i, 0))],
        out_shape=[jax.ShapeDtypeStruct((t_tok, (nj - 1) * tn), BF16),
                   jax.ShapeDtypeStruct((S5_Q, t_tok // SUB, S5_FLAT), BF16),
                   jax.ShapeDtypeStruct((t_tok, LANE), F32)],
        scratch_shapes=[pltpu.VMEM((tm, D_MODEL), BF16), pltpu.VMEM((S5_Q, tm, LANE), F32)],
        compiler_params=_params(("parallel", "arbitrary"), 48),
        name="inproj",
    )(*xs, mod_l, w, b, wg, bg)


def _s5_factors(lam_re, lam_im, log_step, b_re, b_im, c_re, c_im):
    lam = lax.complex(lam_re.astype(F32), lam_im.astype(F32))
    lam_dt = lam * jnp.exp(log_step.astype(F32))[..., None]
    lam_bar = jnp.exp(lam_dt)
    bbar = ((lam_bar - 1.0) / lam)[..., None] * lax.complex(b_re.astype(F32), b_im.astype(F32))
    cmat = lax.complex(c_re.astype(F32), c_im.astype(F32))
    ks = jnp.arange(SUB + 1, dtype=F32)
    pw = jnp.exp(lam_dt[None] * ks[:, None, None, None])
    kern = jnp.einsum('dgcp,tdgp,dgpe->dgtce', cmat, pw[:SUB], bbar).real
    pad = jnp.zeros_like(kern[0][:, :SUB - 1])
    ktab = jnp.concatenate([pad, kern[0]], 1) + jnp.concatenate([kern[1][:, ::-1], pad], 1)
    k_lag = ktab.reshape(S5_Q, S5_QG, N_LAG, S5_GROUP, S5_GROUP).transpose(0, 2, 3, 1, 4)
    k_lag = k_lag.reshape(S5_Q, N_LAG, S5_GROUP, S5_P, S5_PB).transpose(0, 3, 1, 2, 4)
    pw_in = jnp.stack([pw[:SUB][::-1, 0], pw[:SUB][:, 1]], 0)
    wb = pw_in[..., None] * bbar[:, None]
    wb = jnp.stack([wb.real, wb.imag], 1).reshape(2, 2, SUB, S5_Q, S5_QG, S5_STATE, S5_GROUP)
    a_in = wb.transpose(3, 2, 0, 1, 5, 4, 6).reshape(S5_Q, SUB, 4, S5_STATE, S5_P, S5_PB).transpose(0, 4, 1, 2, 3, 5)
    pw_out = jnp.stack([pw[1:, 0], pw[1:][::-1, 1]], 0)
    ce = cmat[:, None] * pw_out[:, :, :, None, :]
    ce = jnp.stack([ce.real, -ce.imag], 1).reshape(2, 2, SUB, S5_Q, S5_QG, S5_GROUP, S5_STATE)
    b_out = ce.transpose(3, 0, 1, 2, 5, 4, 6).reshape(S5_Q, 4, SUB, S5_GROUP, S5_P, S5_PHALF).transpose(0, 4, 1, 2, 3, 5)
    a = pw[SUB]
    a = jnp.stack([a.real, a.imag], 1).reshape(2, 2, S5_Q, 1, S5_HALF).transpose(2, 0, 1, 3, 4)
    return a_in.astype(BF16), k_lag.astype(BF16), b_out.astype(BF16), a


def _expand(src_t, n_rep, row_shift, col_shift):
    k, r = src_t.shape
    rep = jnp.where(_iota((k, n_rep * k), 0) == (_iota((k, n_rep * k), 1) & (k - 1)), 1.0, 0.0).astype(BF16)
    same = (_iota((r, n_rep * k), 0) >> row_shift) == (_iota((r, n_rep * k), 1) >> col_shift)
    return jnp.where(same, _dot_tn(src_t, rep), 0.0).astype(BF16)


def _s5a_kernel(u_ref, a_ref, o_ref, w_scr):
    @pl.when(pl.program_id(1) == 0)
    def _():
        for p in range(S5_P):
            for k in range(4):
                w_scr[p, :, k * S5_PHALF:(k + 1) * S5_PHALF] = jnp.concatenate(
                    [_expand(a_ref[p, t, k], S5_PG, 4, 6) for t in range(SUB)], axis=0)

    for p in range(S5_P):
        res = _dot(u_ref[:, p * S5_PFLAT:(p + 1) * S5_PFLAT], w_scr[p])
        for d in range(2):
            for r in range(2):
                k = 2 * d + r
                o_ref[d, r, :, p * S5_PHALF:(p + 1) * S5_PHALF] = res[:, k * S5_PHALF:(k + 1) * S5_PHALF]


def _s5_state_in(u4, a_in, cfg):
    rows = u4.shape[1]
    rt = cfg["s5_rows"]
    return pl.pallas_call(
        _s5a_kernel,
        grid=(S5_Q, rows // rt),
        in_specs=[pl.BlockSpec((None, rt, S5_FLAT), lambda q, i: (q, i, 0)),
                  pl.BlockSpec((None, S5_P, SUB, 4, S5_STATE, S5_PB), lambda q, i: (q, 0, 0, 0, 0, 0))],
        out_specs=pl.BlockSpec((None, 2, 2, rt, S5_HALF), lambda q, i: (q, 0, 0, i, 0)),
        out_shape=jax.ShapeDtypeStruct((S5_Q, 2, 2, rows, S5_HALF), F32),
        scratch_shapes=[pltpu.VMEM((S5_P, S5_PFLAT, 4 * S5_PHALF), BF16)],
        compiler_params=_params(("parallel", "arbitrary"), 48),
        name="s5_state_in",
    )(u4, a_in)


def _s5b_kernel(loc_ref, a_ref, x0_ref, xp_ref, fin_ref, *, n_ctx_seq, ctx_sub, n_lat_seq, lat_sub):
    d = pl.program_id(1)
    ar = jnp.broadcast_to(a_ref[0], (8, LANE))
    ai = jnp.broadcast_to(a_ref[1], (8, LANE))

    def run(base, nsub, xr0, xi0):
        def body(jj, carry):
            xr, xi = carry
            j = jnp.where(d == 0, jj, nsub - 1 - jj)
            idx = pl.ds(base + j, 8, stride=nsub)
            xp_ref[0, idx, :] = xr
            xp_ref[1, idx, :] = xi
            lr = loc_ref[0, idx, :]
            li = loc_ref[1, idx, :]
            return ar * xr - ai * xi + lr, ar * xi + ai * xr + li
        return lax.fori_loop(0, nsub, body, (xr0, xi0))

    zero = jnp.zeros((8, LANE), F32)
    for bg in range(n_ctx_seq // 8):
        xr, xi = run(bg * 8 * ctx_sub, ctx_sub, zero, zero)
        fin_ref[0, bg * 8:(bg + 1) * 8, :] = xr
        fin_ref[1, bg * 8:(bg + 1) * 8, :] = xi
    for bg in range(n_lat_seq // 8):
        run(n_ctx_seq * ctx_sub + bg * 8 * lat_sub, lat_sub, x0_ref[0, bg * 8:(bg + 1) * 8, :],
            x0_ref[1, bg * 8:(bg + 1) * 8, :])


def _s5_scan(loc, a, x0, cfg):
    rows = loc.shape[3]
    n_ctx_seq, n_lat_seq = cfg["n_ctx_seq"], cfg["n_lat_seq"]
    kern = functools.partial(_s5b_kernel, n_ctx_seq=n_ctx_seq, ctx_sub=cfg["ctx_len"] // SUB,
                             n_lat_seq=n_lat_seq, lat_sub=cfg["lat_len"] // SUB)
    nlb = S5_HALF // LANE
    return pl.pallas_call(
        kern,
        grid=(S5_Q, 2, nlb),
        in_specs=[pl.BlockSpec((None, None, 2, rows, LANE), lambda q, d, b: (q, d, 0, 0, b)),
                  pl.BlockSpec((None, None, 2, 1, LANE), lambda q, d, b: (q, d, 0, 0, b)),
                  pl.BlockSpec((None, None, 2, n_lat_seq, LANE), lambda q, d, b: (q, d, 0, 0, b))],
        out_specs=[pl.BlockSpec((None, None, 2, rows, LANE), lambda q, d, b: (q, d, 0, 0, b)),
                   pl.BlockSpec((None, None, 2, n_ctx_seq, LANE), lambda q, d, b: (q, d, 0, 0, b))],
        out_shape=[jax.ShapeDtypeStruct(loc.shape, F32),
                   jax.ShapeDtypeStruct((S5_Q, 2, 2, n_ctx_seq, S5_HALF), F32)],
        compiler_params=_params(("parallel", "parallel", "parallel"), 48),
        name="s5_scan",
    )(loc, a, x0)


def _s5c_kernel(u_ref, xp_ref, k_ref, b_ref, d_ref, z_ref, m_scr, wo_scr, z_scr):
    rt = u_ref.shape[0]

    @pl.when(pl.program_id(1) == 0)
    def _():
        for p in range(S5_P):
            lags = jnp.concatenate([_expand(k_ref[p, l], S5_PG, 4, 4) for l in range(N_LAG)]
                                   + [jnp.zeros((S5_PB, S5_PB), BF16)], axis=1)
            for t in range(SUB):
                lo = (SUB - 1 - t) * S5_PB
                m_scr[p, t * S5_PB:(t + 1) * S5_PB, :] = lags[:, lo:lo + S5_PFLAT]
            for k in range(4):
                wo_scr[p, k * S5_PHALF:(k + 1) * S5_PHALF, :] = jnp.concatenate(
                    [_expand(b_ref[p, k, t], S5_PG, 6, 4) for t in range(SUB)], axis=1)

    z = []
    for p in range(S5_P):
        u = u_ref[:, p * S5_PFLAT:(p + 1) * S5_PFLAT]
        xcat = jnp.concatenate([xp_ref[d, r, :, p * S5_PHALF:(p + 1) * S5_PHALF] for d in range(2) for r in range(2)],
                               axis=1).astype(BF16)
        y = _dot(u, m_scr[p]) + _dot(xcat, wo_scr[p])
        z.append(jax.nn.gelu(d_ref[:, p * S5_PFLAT:(p + 1) * S5_PFLAT] * u.astype(F32) + y))
    for t in range(SUB):
        z_scr[pl.ds(t, rt, stride=SUB), :] = jnp.concatenate([zp[:, t * S5_PB:(t + 1) * S5_PB] for zp in z], axis=1)
    z_ref[...] = z_scr[...].astype(z_ref.dtype)


def _s5_output(u4, xprev, k_lag, b_out, d4, cfg):
    rows = u4.shape[1]
    rt = cfg["s5_rows"]
    return pl.pallas_call(
        _s5c_kernel,
        grid=(S5_Q, rows // rt),
        in_specs=[pl.BlockSpec((None, rt, S5_FLAT), lambda q, i: (q, i, 0)),
                  pl.BlockSpec((None, 2, 2, rt, S5_HALF), lambda q, i: (q, 0, 0, i, 0)),
                  pl.BlockSpec((None, S5_P, N_LAG, S5_GROUP, S5_PB), lambda q, i: (q, 0, 0, 0, 0)),
                  pl.BlockSpec((None, S5_P, 4, SUB, S5_GROUP, S5_PHALF), lambda q, i: (q, 0, 0, 0, 0, 0)),
                  pl.BlockSpec((None, 1, S5_FLAT), lambda q, i: (q, 0, 0))],
        out_specs=pl.BlockSpec((rt * SUB, LANE), lambda q, i: (i, q)),
        out_shape=jax.ShapeDtypeStruct((rows * SUB, S5_WIDTH), BF16),
        scratch_shapes=[pltpu.VMEM((S5_P, S5_PFLAT, S5_PFLAT), BF16), pltpu.VMEM((S5_P, 4 * S5_PHALF, S5_PFLAT), BF16),
                        pltpu.VMEM((rt * SUB, LANE), F32)],
        compiler_params=_params(("parallel", "arbitrary"), 56),
        name="s5_output",
    )(u4, xprev, k_lag, b_out, d4)


def _group_norm(o, g, b):
    mu = jnp.mean(o, axis=-1, keepdims=True)
    var = jnp.mean(jnp.square(o - mu), axis=-1, keepdims=True)
    return (o - mu) * lax.rsqrt(var + GN_EPS) * g + b


def _ret_kernel(dec_ref, q_ref, k_ref, v_ref, g_ref, cos_ref, sin_ref, s0_ref, gg_ref, gb_ref, y_ref, *rest,
                nc, n_sub, use_rot, has_init, want_final):
    if want_final:
        sfin_ref, sf_scr, sb_scr, x_scr, kr_scr = rest
    else:
        sf_scr, sb_scr, x_scr, kr_scr = rest
    h = pl.program_id(1)
    row = _iota((CHUNK, CHUNK), 0).astype(F32)
    col = _iota((CHUNK, CHUNK), 1).astype(F32)
    lg_f = -jnp.exp(jnp.full((CHUNK, CHUNK), dec_ref[0, h], F32))
    lg_b = -jnp.exp(jnp.full((CHUNK, CHUNK), dec_ref[1, h], F32))
    lag = row - col
    scale = HEAD_DIM ** -0.5
    dmat = (jnp.where(lag >= 0, jnp.exp(lg_f * jnp.maximum(lag, 0.0)), 0.0)
            + jnp.where(lag <= 0, jnp.exp(lg_b * jnp.maximum(-lag, 0.0)), 0.0)) * scale
    qd_f = jnp.exp(lg_f * (row + 1.0))
    qd_b = jnp.exp(lg_b * (CHUNK - row))
    kd_f = jnp.exp(lg_f * (CHUNK - 1.0 - col)) * scale
    kd_b = jnp.exp(lg_b * col) * scale
    cd_f = jnp.exp(lg_f * CHUNK)
    cd_b = jnp.exp(lg_b * CHUNK)

    def chunk(s, j):
        return pl.ds(pl.multiple_of((s * nc + j) * CHUNK, CHUNK), CHUNK)

    def rot(ref, sl):
        x = ref[sl, :].astype(F32)
        if not use_rot:
            return x
        return x * cos_ref[sl, :] + pltpu.roll(x, HEAD_DIM // 2, 1) * sin_ref[sl, :]

    def local(s, j, carry):
        sl = chunk(s, j)
        k = rot(k_ref, sl)
        kr_scr[sl, :] = k.astype(BF16)
        k_t = k.T
        v = v_ref[sl, :]
        x_scr[0, j] = _dot((k_t * kd_f).astype(BF16), v)
        x_scr[1, j] = _dot((k_t * kd_b).astype(BF16), v)
        return carry

    zero = jnp.zeros((CHUNK, CHUNK), F32)

    def states(jj, carry):
        s_f, s_b = carry
        jf, jb = jj, nc - 1 - jj
        sf_scr[jf] = s_f.astype(BF16)
        sb_scr[jb] = s_b.astype(BF16)
        return s_f * cd_f + x_scr[0, jf], s_b * cd_b + x_scr[1, jb]

    def outputs(s, j, carry):
        sl = chunk(s, j)
        q = rot(q_ref, sl)
        att = _dot_nt(q.astype(BF16), kr_scr[sl, :]) * dmat
        lhs = jnp.concatenate([att.astype(BF16), (q * qd_f).astype(BF16), (q * qd_b).astype(BF16)], axis=1)
        rhs = jnp.concatenate([v_ref[sl, :], sf_scr[j], sb_scr[j]], axis=0)
        o = _dot(lhs, rhs)
        y = jax.nn.silu(g_ref[sl, :].astype(F32)) * _group_norm(o, gg_ref[...], gb_ref[...])
        y_ref[sl, :] = y.astype(y_ref.dtype)
        return carry

    def sequence(s, carry):
        lax.fori_loop(0, nc, functools.partial(local, s), 0, unroll=min(8, nc))
        init = (s0_ref[s, 0], s0_ref[s, 1]) if has_init else (zero, zero)
        s_f, s_b = lax.fori_loop(0, nc, states, init, unroll=2)
        if want_final:
            sfin_ref[s, 0] = s_f
            sfin_ref[s, 1] = s_b
        lax.fori_loop(0, nc, functools.partial(outputs, s), 0, unroll=min(8, nc))
        return carry

    if n_sub == 1:
        sequence(0, 0)
    else:
        lax.fori_loop(0, n_sub, sequence, 0)


def _retention(proj, dec, cos_t, sin_t, s0, gn_g, gn_b, *, n_seq, seq_len, n_sub, row0, use_rot, has_init,
               want_final):
    assert n_seq % n_sub == 0 and not (use_rot and n_sub > 1)
    nc = seq_len // CHUNK
    rows = n_sub * seq_len
    blk0 = row0 // rows
    kern = functools.partial(_ret_kernel, nc=nc, n_sub=n_sub, use_rot=use_rot, has_init=has_init, want_final=want_final)

    def tok(cb):
        return pl.BlockSpec((rows, HEAD_DIM), lambda s, h, cb=cb: (blk0 + s, cb + h))

    rot_spec = pl.BlockSpec((seq_len, HEAD_DIM), lambda s, h: (0, 0))
    st_spec = pl.BlockSpec((n_sub, 2, None, HEAD_DIM, HEAD_DIM), lambda s, h: (s, 0, h, 0, 0))
    gn_spec = pl.BlockSpec((1, HEAD_DIM), lambda s, h: (0, h))
    out_specs = [pl.BlockSpec((rows, HEAD_DIM), lambda s, h: (s, h))]
    out_shape = [jax.ShapeDtypeStruct((n_seq * seq_len, WIDTH), BF16)]
    if want_final:
        out_specs.append(st_spec)
        out_shape.append(jax.ShapeDtypeStruct((n_seq, 2, HEADS, HEAD_DIM, HEAD_DIM), F32))
    return pl.pallas_call(
        kern,
        grid=(n_seq // n_sub, HEADS),
        in_specs=[pl.BlockSpec(memory_space=pltpu.SMEM), tok(0), tok(4), tok(8), tok(12),
                  rot_spec, rot_spec, st_spec, gn_spec, gn_spec],
        out_specs=out_specs,
        out_shape=out_shape,
        scratch_shapes=[pltpu.VMEM((nc, HEAD_DIM, HEAD_DIM), BF16), pltpu.VMEM((nc, HEAD_DIM, HEAD_DIM), BF16),
                        pltpu.VMEM((2, nc, HEAD_DIM, HEAD_DIM), F32), pltpu.VMEM((rows, HEAD_DIM), BF16)],
        compiler_params=_params(("parallel", "parallel"), 48),
        name="retention",
    )(dec, proj, proj, proj, proj, cos_t, sin_t, s0, gn_g, gn_b)


def _gate_prep_kernel(g_ref, col_ref, row_ref):
    lane = _iota((CHUNK, LANE), 1)
    tri = jnp.where(_iota((CHUNK, CHUNK), 0) >= _iota((CHUNK, CHUNK), 1), 1.0, 0.0)
    for c in range(g_ref.shape[0] // CHUNK):
        sl = slice(c * CHUNK, (c + 1) * CHUNK)
        g = g_ref[sl, :]
        lf = jnp.where(lane < 4 * HEADS, jnp.minimum(g, 0.0) - jnp.log1p(jnp.exp(-jnp.abs(g))), 0.0)
        cs = _dot_hi(tri, lf)
        tot = cs[CHUNK - 1:CHUNK, :]
        bc = jnp.where(lane < 2 * HEADS, cs, tot - cs + lf)
        rest = jnp.where(lane < 2 * HEADS, tot - cs, cs - lf)
        pack = g + pltpu.roll(bc, GATE_BC, 1) + pltpu.roll(rest, GATE_REST, 1)
        col_ref[sl, :] = pack
        row_ref[sl, :] = pack.T


def _gate_prep(gates, cfg):
    t_tok = gates.shape[0]
    tm = cfg["tm_in"]
    spec = pl.BlockSpec((tm, LANE), lambda i: (i, 0))
    return pl.pallas_call(
        _gate_prep_kernel,
        grid=(t_tok // tm,),
        in_specs=[spec],
        out_specs=[spec, spec],
        out_shape=[jax.ShapeDtypeStruct((t_tok, LANE), F32)] * 2,
        compiler_params=_params(("parallel",), 32),
        name="gate_prep",
    )(gates)


def _mlstm_kernel(q_ref, k_ref, v_ref, o_ref, col_ref, row_ref, c0_ref, n0_ref, m0_ref, gg_ref, gb_ref, y_ref, *rest,
                  nc, n_sub, has_init, want_final):
    if want_final:
        cfin_ref, nfin_ref, mfin_ref, c_scr, n_scr, m_scr, x_scr, nl_scr, ml_scr, bl_scr = rest
    else:
        c_scr, n_scr, m_scr, x_scr, nl_scr, ml_scr, bl_scr = rest
    src = _iota((CHUNK, CHUNK), 0)
    dst = _iota((CHUNK, CHUNK), 1)
    scale = HEAD_DIM ** -0.5

    def chunk(s, j):
        return pl.ds(pl.multiple_of((s * nc + j) * CHUNK, CHUNK), CHUNK)

    def gate_idx(d, h):
        return d * 2 * HEADS + h, GATE_BC + d * 2 * HEADS + HEADS + h, GATE_REST + d * 2 * HEADS + HEADS + h

    def head_body(h):
        def init(s, d):
            if has_init:
                return c0_ref[s, d], n0_ref[s, d], m0_ref[s, d]
            return (jnp.zeros((HEAD_DIM, HEAD_DIM), F32), jnp.zeros((1, HEAD_DIM), F32),
                    jnp.zeros((1, HEAD_DIM), F32))

        def local(s, j, carry):
            sl = chunk(s, j)
            k = k_ref[sl, :]
            v_t = v_ref[sl, :].astype(F32).T
            rp = row_ref[sl, :]
            for d in range(2):
                ii, bi, ri = gate_idx(d, h)
                bc_row = rp[bi:bi + 1, :]
                b_last = bc_row[:, CHUNK - 1:CHUNK] if d == 0 else bc_row[:, 0:1]
                log_k = rp[ri:ri + 1, :] + rp[ii:ii + 1, :]
                m_loc = jnp.max(log_k, axis=1, keepdims=True)
                kw = jnp.exp(log_k - m_loc)
                x_scr[d, j] = _dot((v_t * kw).astype(BF16), k) * scale
                kw_hi = kw.astype(BF16).astype(F32)
                kw2 = jnp.concatenate([jnp.broadcast_to(kw_hi, (8, CHUNK)), jnp.broadcast_to(kw - kw_hi, (8, CHUNK))], 0)
                nl = _dot(kw2.astype(BF16), k)
                nl_scr[d, j] = (nl[0:1, :] + nl[8:9, :]) * scale
                ml_scr[d, j] = jnp.broadcast_to(m_loc, (1, HEAD_DIM))
                bl_scr[d, j] = jnp.broadcast_to(b_last, (1, HEAD_DIM))
            return carry

        def states(jj, carry):
            out = []
            for d, j in ((0, jj), (1, nc - 1 - jj)):
                cmat, nvec, m = carry[d]
                c_scr[d, j] = cmat.astype(BF16)
                n_scr[d, j] = nvec
                m_scr[d, j] = m
                m_new = jnp.maximum(bl_scr[d, j] + m, ml_scr[d, j])
                keep = jnp.exp(bl_scr[d, j] + m - m_new)
                add = jnp.exp(ml_scr[d, j] - m_new)
                out.append((keep * cmat + add * x_scr[d, j], keep * nvec + add * nl_scr[d, j], m_new))
            return tuple(out)

        def outputs(s, j, carry):
            sl = chunk(s, j)
            q = q_ref[sl, :]
            v = v_ref[sl, :]
            s_t = _dot_nt(k_ref[sl, :], q) * scale
            cp = col_ref[sl, :]
            rp = row_ref[sl, :]
            h_t = None
            for d in range(2):
                ii, bi, _ = gate_idx(d, h)
                causal = (src <= dst) if d == 0 else (src >= dst)
                bc_row = rp[bi:bi + 1, :]
                log_d = jnp.where(causal, bc_row + (cp[:, ii:ii + 1] - cp[:, bi:bi + 1]), -jnp.inf)
                log_prev = bc_row + m_scr[d, j]
                m_t = jnp.maximum(log_prev, jnp.max(log_d, axis=0, keepdims=True))
                w = s_t * jnp.exp(log_d - m_t)
                w_prev = jnp.exp(log_prev - m_t)
                qn = _dot_nt(jnp.broadcast_to(n_scr[d, j], (16, HEAD_DIM)).astype(BF16), q)[0:1, :]
                den = jnp.sum(w, axis=0, keepdims=True) + w_prev * qn
                inv = 1.0 / jnp.maximum(jnp.abs(den), jnp.exp(-m_t))
                num = _dot_tn(v, w.astype(BF16)) + _dot_nt(c_scr[d, j], q) * w_prev
                h_t = num * inv if h_t is None else h_t + num * inv
            y = jax.nn.sigmoid(o_ref[sl, :].astype(F32).T) * h_t
            mu = jnp.mean(y, axis=0, keepdims=True)
            var = jnp.mean(jnp.square(y - mu), axis=0, keepdims=True)
            y = (y - mu) * lax.rsqrt(var + GN_EPS) * gg_ref[...] + gb_ref[...]
            y_ref[sl, :] = y.T.astype(y_ref.dtype)
            return carry

        def sequence(s, carry):
            lax.fori_loop(0, nc, functools.partial(local, s), 0, unroll=min(8, nc))
            fin = lax.fori_loop(0, nc, states, (init(s, 0), init(s, 1)), unroll=2)
            if want_final:
                for d in range(2):
                    cfin_ref[s, d] = fin[d][0]
                    nfin_ref[s, d] = fin[d][1]
                    mfin_ref[s, d] = fin[d][2]
            lax.fori_loop(0, nc, functools.partial(outputs, s), 0, unroll=min(4, nc))
            return carry

        if n_sub == 1:
            sequence(0, 0)
        else:
            lax.fori_loop(0, n_sub, sequence, 0)

    hh = pl.program_id(1)
    for h in range(HEADS):
        pl.when(hh == h)(functools.partial(head_body, h))


def _mlstm(proj, gcol, grow, c0, n0, m0, gn_g, gn_b, *, n_seq, seq_len, n_sub, row0, has_init, want_final):
    assert n_seq % n_sub == 0
    nc = seq_len // CHUNK
    rows = n_sub * seq_len
    blk0 = row0 // rows
    kern = functools.partial(_mlstm_kernel, nc=nc, n_sub=n_sub, has_init=has_init, want_final=want_final)

    def tok(cb):
        return pl.BlockSpec((rows, HEAD_DIM), lambda s, h, cb=cb: (blk0 + s, cb + h))

    gate_spec = pl.BlockSpec((rows, LANE), lambda s, h: (blk0 + s, 0))
    c_spec = pl.BlockSpec((n_sub, 2, None, HEAD_DIM, HEAD_DIM), lambda s, h: (s, 0, h, 0, 0))
    v_spec = pl.BlockSpec((n_sub, 2, None, 1, HEAD_DIM), lambda s, h: (s, 0, h, 0, 0))
    gn_spec = pl.BlockSpec((HEAD_DIM, LANE), lambda s, h: (h, 0))
    out_specs = [pl.BlockSpec((rows, HEAD_DIM), lambda s, h: (s, h))]
    out_shape = [jax.ShapeDtypeStruct((n_seq * seq_len, WIDTH), BF16)]
    if want_final:
        out_specs += [c_spec, v_spec, v_spec]
        out_shape += [jax.ShapeDtypeStruct((n_seq, 2, HEADS, HEAD_DIM, HEAD_DIM), F32),
                      jax.ShapeDtypeStruct((n_seq, 2, HEADS, 1, HEAD_DIM), F32),
                      jax.ShapeDtypeStruct((n_seq, 2, HEADS, 1, HEAD_DIM), F32)]
    return pl.pallas_call(
        kern,
        grid=(n_seq // n_sub, HEADS),
        in_specs=[tok(16), tok(20), tok(24), tok(28), gate_spec, gate_spec,
                  c_spec, v_spec, v_spec, gn_spec, gn_spec],
        out_specs=out_specs,
        out_shape=out_shape,
        scratch_shapes=[pltpu.VMEM((2, nc, HEAD_DIM, HEAD_DIM), BF16)] + [pltpu.VMEM((2, nc, 1, HEAD_DIM), F32)] * 2
        + [pltpu.VMEM((2, nc, HEAD_DIM, HEAD_DIM), F32)] + [pltpu.VMEM((2, nc, 1, HEAD_DIM), F32)] * 3,
        compiler_params=_params(("parallel", "parallel"), 48),
        name="mlstm",
    )(proj, proj, proj, proj, gcol, grow, c0, n0, m0, gn_g, gn_b)


def _layer_norm(x, g, b):
    mu = jnp.mean(x, axis=-1, keepdims=True)
    var = jnp.mean(jnp.square(x - mu), axis=-1, keepdims=True)
    return (x - mu) * lax.rsqrt(var + LN_EPS) * g + b


def _top2_gates(h, whi_ref, wlo_ref, b_ref):
    lane = _iota((h.shape[0], LANE), 1)
    h_hi = h.astype(BF16)
    h_lo = (h - h_hi.astype(F32)).astype(BF16)
    logits = _dot(h_hi, whi_ref[...]) + _dot(h_hi, wlo_ref[...]) + _dot(h_lo, whi_ref[...]) + b_ref[...]
    logits = jnp.where(lane < N_EXPERTS, logits, NEG_BIG)
    m1 = jnp.max(logits, axis=1, keepdims=True)
    i1 = jnp.min(jnp.where(logits == m1, lane, LANE), axis=1, keepdims=True)
    rest = jnp.where(lane == i1, NEG_BIG, logits)
    m2 = jnp.max(rest, axis=1, keepdims=True)
    i2 = jnp.min(jnp.where(rest == m2, lane, LANE), axis=1, keepdims=True)
    e2 = jnp.exp(m2 - m1)
    den = 1.0 + e2
    return jnp.where(lane == i1, 1.0 / den, 0.0) + jnp.where(lane == i2, e2 / den, 0.0)


def _merge_kernel(*refs, n_ctx_tiles, with_router, split):
    if split:
        xc_ref, xs_ref = refs[:2]
    (x_ref, mod_ref, z_ref, ybc_ref, ybs_ref, ycc_ref, ycs_ref, wm_ref, bm_ref, wglu_ref, bglu_ref,
     wa_ref, wb_ref, wc_ref, wo_ref, lg_ref, lb_ref, *rest) = refs[1 if split else 0:]
    if with_router:
        rwh_ref, rwl_ref, rb_ref, o_ref, g_ref = rest
    else:
        o_ref, = rest
    is_ctx = pl.program_id(0) < n_ctx_tiles
    x = jnp.where(is_ctx, xc_ref[...], xs_ref[...]) if split else x_ref[...]
    h = (x * (1.0 + mod_ref[1:2, :]) + mod_ref[0:1, :]).astype(BF16)
    z = z_ref[...]
    ya = (z.astype(F32) * jax.nn.sigmoid(_dot(z, wglu_ref[...]) + bglu_ref[...])).astype(BF16)
    yb = jnp.where(is_ctx, ybc_ref[...], ybs_ref[...])
    yc = jnp.where(is_ctx, ycc_ref[...], ycs_ref[...])
    merged = None
    for j, (y, w_ref) in enumerate(((ya, wa_ref), (yb, wb_ref), (yc, wc_ref))):
        gate = jax.nn.sigmoid(_dot(h, wm_ref[:, j * D_MODEL:(j + 1) * D_MODEL]) + bm_ref[:, j * D_MODEL:(j + 1) * D_MODEL])
        term = gate * _dot(y, w_ref[...])
        merged = term if merged is None else merged + term
    mix = _dot(merged.astype(BF16), wo_ref[...])
    x1 = _layer_norm(ALPHA * x + mod_ref[2:3, :] * mix, lg_ref[...], lb_ref[...])
    o_ref[...] = x1
    if with_router:
        g_ref[...] = _top2_gates(x1 * (1.0 + mod_ref[4:5, :]) + mod_ref[3:4, :], rwh_ref, rwl_ref, rb_ref)


def _merge(x, mod_l, z, yb_c, yb_s, yc_c, yc_s, wm, bm, wglu, bglu, wa, wb, wc, wo, lg, lb, router, cfg):
    split = isinstance(x, tuple)
    xs = list(x) if split else [x]
    t_tok = sum(a.shape[0] for a in xs)
    tm = cfg["tm_merge"]
    n_ctx_tiles = cfg["n_ctx_tok"] // tm
    cond = functools.partial(_cond_row, tm=tm, n_ctx_tok=cfg["n_ctx_tok"], lat_len=cfg["lat_len"])

    def full(shape):
        return pl.BlockSpec(shape, lambda i: (0,) * len(shape))

    def tok(w):
        return pl.BlockSpec((tm, w), lambda i: (i, 0))

    ctx_spec, lat_spec = _group_specs(tm, WIDTH, n_ctx_tiles)
    x_specs = list(_group_specs(tm, D_MODEL, n_ctx_tiles)) if split else [tok(D_MODEL)]
    in_specs = x_specs + [pl.BlockSpec((None, 6, D_MODEL), lambda i: (cond(i), 0, 0)),
                tok(WIDTH), ctx_spec, lat_spec, ctx_spec, lat_spec,
                full((D_MODEL, 3 * D_MODEL)), full((1, 3 * D_MODEL)), full((WIDTH, WIDTH)), full((1, WIDTH)),
                full((WIDTH, D_MODEL)), full((WIDTH, D_MODEL)), full((WIDTH, D_MODEL)),
                full((D_MODEL, D_MODEL)), full((1, D_MODEL)), full((1, D_MODEL))]
    args = xs + [mod_l, z, yb_c, yb_s, yc_c, yc_s, wm, bm, wglu, bglu, wa, wb, wc, wo, lg, lb]
    out_specs = [tok(D_MODEL)]
    out_shape = [jax.ShapeDtypeStruct((t_tok, D_MODEL), F32)]
    if router is not None:
        in_specs += [full((D_MODEL, LANE)), full((D_MODEL, LANE)), full((1, LANE))]
        args += list(router)
        out_specs.append(tok(LANE))
        out_shape.append(jax.ShapeDtypeStruct((t_tok, LANE), F32))
    return pl.pallas_call(
        functools.partial(_merge_kernel, n_ctx_tiles=n_ctx_tiles, with_router=router is not None, split=split),
        grid=(t_tok // tm,),
        in_specs=in_specs,
        out_specs=out_specs,
        out_shape=out_shape,
        compiler_params=_params(("parallel",), 56),
        name="merge",
    )(*args)


def _ffn_kernel(x_ref, mod_ref, w1_ref, w3_ref, w2_ref, lg_ref, lb_ref, o_ref, h_scr, acc_scr):
    f = pl.program_id(1)

    @pl.when(f == 0)
    def _():
        h_scr[...] = (x_ref[...] * (1.0 + mod_ref[4:5, :]) + mod_ref[3:4, :]).astype(BF16)
        acc_scr[...] = jnp.zeros_like(acc_scr)

    h = h_scr[...]
    act = (jax.nn.silu(_dot(h, w1_ref[...])) * _dot(h, w3_ref[...])).astype(BF16)
    acc_scr[...] += _dot(act, w2_ref[...])

    @pl.when(f == pl.num_programs(1) - 1)
    def _():
        o_ref[...] = _layer_norm(ALPHA * x_ref[...] + mod_ref[5:6, :] * acc_scr[...], lg_ref[...], lb_ref[...])


def _dense_ffn(x, mod_l, w1, w3, w2, lg, lb, cfg):
    t_tok = x.shape[0]
    tm, tf = cfg["tm_ffn"], D_FF
    cond = functools.partial(_cond_row, tm=tm, n_ctx_tok=cfg["n_ctx_tok"], lat_len=cfg["lat_len"])
    return pl.pallas_call(
        _ffn_kernel,
        grid=(t_tok // tm, D_FF // tf),
        in_specs=[pl.BlockSpec((tm, D_MODEL), lambda i, f: (i, 0)),
                  pl.BlockSpec((None, 6, D_MODEL), lambda i, f: (cond(i), 0, 0)),
                  pl.BlockSpec((D_MODEL, tf), lambda i, f: (0, f)),
                  pl.BlockSpec((D_MODEL, tf), lambda i, f: (0, f)),
                  pl.BlockSpec((tf, D_MODEL), lambda i, f: (f, 0)),
                  pl.BlockSpec((1, D_MODEL), lambda i, f: (0, 0)),
                  pl.BlockSpec((1, D_MODEL), lambda i, f: (0, 0))],
        out_specs=pl.BlockSpec((tm, D_MODEL), lambda i, f: (i, 0)),
        out_shape=jax.ShapeDtypeStruct((t_tok, D_MODEL), F32),
        scratch_shapes=[pltpu.VMEM((tm, D_MODEL), BF16), pltpu.VMEM((tm, D_MODEL), F32)],
        compiler_params=_params(("parallel", "arbitrary"), 56),
        name="dense_ffn",
    )(x, mod_l, w1, w3, w2, lg, lb)


def _moe_kernel(x_ref, mod_ref, gate_ref, w1_ref, w3_ref, w2_ref, lg_ref, lb_ref, oc_ref, os_ref,
                h_scr, acc_scr, hc_scr, ob_scr, sp_scr, gt_scr, cnt_smem, *, n_ctx_tiles):
    e = pl.program_id(1)
    f = pl.program_id(2)
    last_f = pl.num_programs(2) - 1
    tm = x_ref.shape[0]

    @pl.when((e == 0) & (f == 0))
    def _():
        h_scr[...] = (x_ref[...] * (1.0 + mod_ref[4:5, :]) + mod_ref[3:4, :]).astype(BF16)
        acc_scr[...] = jnp.zeros_like(acc_scr)
        g = gate_ref[...]
        sel = g > 0.0
        ones = jnp.where(sel, 1.0, 0.0)
        before = jnp.where(_iota((tm, tm), 0) > _iota((tm, tm), 1), 1.0, 0.0).astype(BF16)
        pos = _dot(before, ones.astype(BF16))
        spt = jnp.where(sel, pos, -1.0).T
        gt = g.T
        cnt = jnp.sum(ones, axis=0, keepdims=True)
        for ee in range(N_EXPERTS):
            sp_scr[ee] = spt[ee:ee + 1, :]
            gt_scr[ee] = gt[ee:ee + 1, :]
            cnt_smem[ee] = cnt[0, ee].astype(jnp.int32)

    nb = (cnt_smem[e] + (MOE_BLK - 1)) // MOE_BLK
    row = _iota((MOE_BLK, tm), 0)

    def onehot(b):
        return sp_scr[e] == (row + b * MOE_BLK).astype(F32)

    @pl.when(f == 0)
    def _():
        def gather(b, carry):
            p = jnp.where(onehot(b), 1.0, 0.0).astype(BF16)
            hc_scr[b] = _dot(p, h_scr[...]).astype(BF16)
            ob_scr[b] = jnp.zeros((MOE_BLK, D_MODEL), F32)
            return carry
        lax.fori_loop(0, nb, gather, 0)

        @pl.when(nb % 2 == 1)
        def _():
            ob_scr[nb] = jnp.zeros((MOE_BLK, D_MODEL), F32)

    def ffn(hc):
        act = (jax.nn.silu(_dot(hc, w1_ref[...])) * _dot(hc, w3_ref[...])).astype(BF16)
        return _dot(act, w2_ref[...])

    def ffn_pair(p, carry):
        two = pl.ds(2 * p, 2)
        ob_scr[two] += ffn(hc_scr[two].reshape(2 * MOE_BLK, D_MODEL)).reshape(2, MOE_BLK, D_MODEL)
        return carry
    lax.fori_loop(0, nb // 2, ffn_pair, 0)

    @pl.when(nb % 2 == 1)
    def _():
        ob_scr[nb - 1] += ffn(hc_scr[nb - 1])

    @pl.when(f == last_f)
    def _():
        row2 = _iota((2 * MOE_BLK, tm), 0)

        def scatter(p, carry):
            m = sp_scr[e] == (row2 + p * (2 * MOE_BLK)).astype(F32)
            gc = jnp.sum(jnp.where(m, gt_scr[e], 0.0), axis=1, keepdims=True)
            og = (ob_scr[pl.ds(2 * p, 2)].reshape(2 * MOE_BLK, D_MODEL) * gc).astype(BF16)
            acc_scr[...] += _dot_tn(jnp.where(m, 1.0, 0.0).astype(BF16), og)
            return carry
        lax.fori_loop(0, (nb + 1) // 2, scatter, 0)

    last = (e == pl.num_programs(1) - 1) & (f == last_f)
    is_ctx = pl.program_id(0) < n_ctx_tiles
    for o_ref, mine in ((oc_ref, is_ctx), (os_ref, jnp.logical_not(is_ctx))):
        @pl.when(last & mine)
        def _():
            o_ref[...] = _layer_norm(ALPHA * x_ref[...] + mod_ref[5:6, :] * acc_scr[...], lg_ref[...], lb_ref[...])


def _moe_ffn(x, mod_l, gates, w1, w3, w2, layer, lg, lb, cfg):
    t_tok = x.shape[0]
    tm, tf = cfg["tm_moe"], cfg["tf"]
    n_ctx_tiles = cfg["n_ctx_tok"] // tm
    cond = functools.partial(_cond_row, tm=tm, n_ctx_tok=cfg["n_ctx_tok"], lat_len=cfg["lat_len"])
    return pl.pallas_call(
        functools.partial(_moe_kernel, n_ctx_tiles=n_ctx_tiles),
        grid=(t_tok // tm, N_EXPERTS, D_FF // tf),
        in_specs=[pl.BlockSpec((tm, D_MODEL), lambda i, e, f: (i, 0), pipeline_mode=pl.Buffered(1)),
                  pl.BlockSpec((None, 6, D_MODEL), lambda i, e, f: (cond(i), 0, 0)),
                  pl.BlockSpec((tm, LANE), lambda i, e, f: (i, 0), pipeline_mode=pl.Buffered(1)),
                  pl.BlockSpec((None, None, D_MODEL, tf), lambda i, e, f: (layer, e, 0, f)),
                  pl.BlockSpec((None, None, D_MODEL, tf), lambda i, e, f: (layer, e, 0, f)),
                  pl.BlockSpec((None, None, tf, D_MODEL), lambda i, e, f: (layer, e, f, 0)),
                  pl.BlockSpec((1, D_MODEL), lambda i, e, f: (0, 0)),
                  pl.BlockSpec((1, D_MODEL), lambda i, e, f: (0, 0))],
        out_specs=list(_group_specs(tm, D_MODEL, n_ctx_tiles)),
        out_shape=[jax.ShapeDtypeStruct((cfg["n_ctx_tok"], D_MODEL), F32),
                   jax.ShapeDtypeStruct((t_tok - cfg["n_ctx_tok"], D_MODEL), F32)],
        scratch_shapes=[pltpu.VMEM((tm, D_MODEL), BF16), pltpu.VMEM((tm, D_MODEL), F32),
                        pltpu.VMEM((tm // MOE_BLK, MOE_BLK, D_MODEL), BF16),
                        pltpu.VMEM((tm // MOE_BLK, MOE_BLK, D_MODEL), F32),
                        pltpu.VMEM((N_EXPERTS, 1, tm), F32), pltpu.VMEM((N_EXPERTS, 1, tm), F32),
                        pltpu.SMEM((N_EXPERTS,), jnp.int32)],
        compiler_params=_params(("arbitrary", "arbitrary", "arbitrary"), 56),
        name="moe_ffn",
    )(x, mod_l, gates, w1, w3, w2, lg, lb)


def _rotary_tables(n_tok):
    rows = n_tok // GRID_W
    r = jnp.repeat(jnp.arange(rows, dtype=F32), GRID_W)
    col = jnp.tile(jnp.arange(GRID_W, dtype=F32), rows)
    n_freq = HEAD_DIM // 4
    inv = ROPE_BASE ** (-jnp.arange(n_freq, dtype=F32) / n_freq)
    ang = jnp.concatenate([r[:, None] * inv, col[:, None] * inv], -1)
    cos, sin = jnp.cos(ang), jnp.sin(ang)
    return jnp.concatenate([cos, cos], -1), jnp.concatenate([-sin, sin], -1)


def _row2(v):
    return v.reshape(1, -1).astype(F32)


def kernel(x_prompt, x_sample, cache_s5_re, cache_s5_im, cache_ret, cache_ml_c, cache_ml_n, cache_ml_m, c, c_ctx, ada_w, ada_b, w_in, b_in, s5_lam_re, s5_lam_im, s5_log_step, s5_b_re, s5_b_im, s5_c_re, s5_c_im, s5_d, s5_glu_w, s5_glu_b, ret_decay, ret_gn_g, ret_gn_b, ml_gn_g, ml_gn_b, w_a, w_b, w_c, w_o, ln1_g, ln1_b, ln2_g, ln2_b, ffn_w1, ffn_w3, ffn_w2, moe_router, moe_router_b, moe_w1, moe_w3, moe_w2):
    n_ctx_seq, ctx_len, _ = x_prompt.shape
    n_lat_seq, lat_len, _ = x_sample.shape
    n_ctx_tok = n_ctx_seq * ctx_len
    n_lat_tok = n_lat_seq * lat_len
    t_tok = n_ctx_tok + n_lat_tok
    assert n_lat_seq + 1 <= N_COND and n_ctx_seq % 8 == 0 and n_lat_seq % 8 == 0
    assert ctx_len % CHUNK == 0 and lat_len % CHUNK == 0 and n_ctx_tok % lat_len == 0
    cfg = dict(n_ctx_seq=n_ctx_seq, ctx_len=ctx_len, n_lat_seq=n_lat_seq, lat_len=lat_len, n_ctx_tok=n_ctx_tok,
               tm_in=min(2048, lat_len), tm_merge=min(512, lat_len), tm_ffn=min(256, lat_len), tm_moe=min(1024, lat_len), tf=1408,
               s5_rows=min(256, t_tok // SUB // 8))

    x = (x_prompt.reshape(n_ctx_tok, D_MODEL), x_sample.reshape(n_lat_tok, D_MODEL))
    cond = jnp.zeros((N_COND, D_MODEL), F32).at[0].set(c_ctx).at[1:1 + n_lat_seq].set(c)
    mod = _modulation(cond, ada_w, ada_b).reshape(DEPTH, N_COND, 6, D_MODEL)
    cos_t, sin_t = _rotary_tables(lat_len)

    n_main = S5_WIDTH + 8 * WIDTH
    gate_off = n_main
    merge_off = gate_off + 4 * HEADS
    s5_fac = jax.vmap(_s5_factors)(s5_lam_re, s5_lam_im, s5_log_step, s5_b_re, s5_b_im, s5_c_re, s5_c_im)

    st_s5, st_ret, st_c, st_n, st_m = [], [], [], [], []
    zero_ret = jnp.zeros((n_ctx_seq, 2, HEADS, HEAD_DIM, HEAD_DIM), F32)
    zero_vec = jnp.zeros((n_ctx_seq, 2, HEADS, 1, HEAD_DIM), F32)
    for l in range(DEPTH):
        mod_l = mod[l]
        w_main = w_in[l][:, :n_main].astype(BF16)
        b_main = _row2(b_in[l][:n_main])
        w_gate = jnp.zeros((D_MODEL, LANE), F32).at[:, :4 * HEADS].set(w_in[l][:, gate_off:merge_off]).astype(BF16)
        b_gate = jnp.zeros((1, LANE), F32).at[0, :4 * HEADS].set(b_in[l][gate_off:merge_off])
        proj, u4, gates = _inproj(x, mod_l, w_main, b_main, w_gate, b_gate, cfg)

        s5_ain, s5_klag, s5_bout, s5_a = (m[l] for m in s5_fac)
        loc = _s5_state_in(u4, s5_ain, cfg)
        x0 = jnp.stack([cache_s5_re[:, l], cache_s5_im[:, l]], 0)
        x0 = x0.reshape(2, n_lat_seq, 2, S5_Q, S5_HALF).transpose(3, 2, 0, 1, 4).astype(F32)
        xprev, s5_fin = _s5_scan(loc, s5_a, x0, cfg)
        d4 = jnp.broadcast_to(s5_d[l].astype(F32).reshape(S5_Q, S5_P, 1, S5_PB), (S5_Q, S5_P, SUB, S5_PB))
        d4 = d4.reshape(S5_Q, 1, S5_FLAT)
        z = _s5_output(u4, xprev, s5_klag, s5_bout, d4, cfg)
        st_s5.append(s5_fin)

        gg, gb = _row2(ret_gn_g[l]), _row2(ret_gn_b[l])
        dec = ret_decay[l].astype(F32)
        yb_c, ret_fin = _retention(proj, dec, cos_t, sin_t, zero_ret, gg, gb, n_seq=n_ctx_seq, seq_len=ctx_len,
                                   n_sub=8, row0=0, use_rot=False, has_init=False, want_final=True)
        yb_s, = _retention(proj, dec, cos_t, sin_t, cache_ret[:, l].astype(F32), gg, gb, n_seq=n_lat_seq,
                           seq_len=lat_len, n_sub=1, row0=n_ctx_tok, use_rot=True, has_init=True, want_final=False)
        st_ret.append(ret_fin)

        gg = jnp.broadcast_to(ml_gn_g[l].astype(F32)[:, None], (WIDTH, LANE))
        gb = jnp.broadcast_to(ml_gn_b[l].astype(F32)[:, None], (WIDTH, LANE))
        gcol, grow = _gate_prep(gates, cfg)
        yc_c, c_fin, n_fin, m_fin = _mlstm(proj, gcol, grow, zero_ret, zero_vec, zero_vec, gg, gb, n_seq=n_ctx_seq,
                                           seq_len=ctx_len, n_sub=8, row0=0, has_init=False, want_final=True)
        n0 = cache_ml_n[:, l].astype(F32)[:, :, :, None, :]
        m0 = jnp.broadcast_to(cache_ml_m[:, l].astype(F32)[:, :, :, None, None], n0.shape)
        yc_s, = _mlstm(proj, gcol, grow, cache_ml_c[:, l].astype(F32), n0, m0, gg, gb, n_seq=n_lat_seq,
                       seq_len=lat_len, n_sub=1, row0=n_ctx_tok, has_init=True, want_final=False)
        st_c.append(c_fin)
        st_n.append(n_fin[:, :, :, 0, :])
        st_m.append(m_fin[:, :, :, 0, 0])

        j = l // 2
        router = None
        if l % 2 == 1:
            rw = jnp.zeros((D_MODEL, LANE), F32).at[:, :N_EXPERTS].set(moe_router[j])
            rw_hi = rw.astype(BF16)
            router = (rw_hi, (rw - rw_hi.astype(F32)).astype(BF16),
                      jnp.zeros((1, LANE), F32).at[0, :N_EXPERTS].set(moe_router_b[j]))
        merged = _merge(x, mod_l, z, yb_c, yb_s, yc_c, yc_s, w_in[l][:, merge_off:].astype(BF16),
                        _row2(b_in[l][merge_off:]), s5_glu_w[l].astype(BF16), _row2(s5_glu_b[l]), w_a[l].astype(BF16),
                        w_b[l].astype(BF16), w_c[l].astype(BF16), w_o[l].astype(BF16), _row2(ln1_g[l]),
                        _row2(ln1_b[l]), router, cfg)

        if router is None:
            x = _dense_ffn(merged[0], mod_l, ffn_w1[j].astype(BF16), ffn_w3[j].astype(BF16), ffn_w2[j].astype(BF16),
                           _row2(ln2_g[l]), _row2(ln2_b[l]), cfg)
        else:
            x = tuple(_moe_ffn(merged[0], mod_l, merged[1], moe_w1.astype(BF16), moe_w3.astype(BF16),
                               moe_w2.astype(BF16), j, _row2(ln2_g[l]), _row2(ln2_b[l]), cfg))

    y_p = x[0].reshape(n_ctx_seq, ctx_len, D_MODEL)
    y_s = x[1].reshape(n_lat_seq, lat_len, D_MODEL)
    s5 = jnp.stack(st_s5, 0)
    s5 = s5.reshape(DEPTH, S5_Q, 2, 2, n_ctx_seq, S5_QG, S5_STATE).transpose(3, 4, 0, 2, 1, 5, 6)
    s5 = s5.reshape(2, n_ctx_seq, DEPTH, 2, S5_GROUPS, S5_STATE)
    return (y_p, y_s, s5[0], s5[1], jnp.stack(st_ret, 1), jnp.stack(st_c, 1), jnp.stack(st_n, 1),
            jnp.stack(st_m, 1))
```

```python
import functools

import jax
import jax.numpy as jnp
from jax import lax
from jax.experimental import pallas as pl
from jax.experimental.pallas import tpu as pltpu

F32 = jnp.float32
BF16 = jnp.bfloat16

D_MODEL = 1024
DEPTH = 4
GRID_W = 64
CHUNK = 128
S5_WIDTH = 512
S5_GROUP = 16
S5_GROUPS = 32
S5_STATE = 64
HEADS = 4
HEAD_DIM = 128
WIDTH = 512
ROPE_BASE = 10000.0
D_FF = 2816
N_EXPERTS = 8
ALPHA = (2.0 * DEPTH) ** 0.25
LN_EPS = 1e-5
GN_EPS = 1e-5
N_COND = 16
SUB = 16
N_LAG = 2 * SUB - 1
LANE = 128
S5_Q = S5_WIDTH // LANE
S5_QG = LANE // S5_GROUP
S5_FLAT = SUB * LANE
S5_HALF = S5_QG * S5_STATE
S5_ST = 4 * S5_HALF
S5_PG = 2
S5_P = S5_QG // S5_PG
S5_PB = S5_PG * S5_GROUP
S5_PFLAT = SUB * S5_PB
S5_PHALF = S5_PG * S5_STATE
NEG_BIG = -1e30
MOE_BLK = 128
GATE_BC = 16
GATE_REST = 32


def _dot(a, b):
    return jnp.dot(a, b, preferred_element_type=F32)


def _dot_hi(a, b):
    return jnp.dot(a, b, preferred_element_type=F32, precision=lax.Precision.HIGHEST)


def _dot_nt(a, b):
    return lax.dot_general(a, b, (((1,), (1,)), ((), ())), preferred_element_type=F32)


def _dot_tn(a, b):
    return lax.dot_general(a, b, (((0,), (0,)), ((), ())), preferred_element_type=F32)


def _params(sem, vmem_mb):
    return pltpu.CompilerParams(dimension_semantics=sem, vmem_limit_bytes=vmem_mb << 20)


def _cond_row(tile, tm, n_ctx_tok, lat_len):
    start = tile * tm
    return jnp.where(start < n_ctx_tok, 0, 1 + (start - n_ctx_tok) // lat_len)


def _iota(shape, axis):
    return lax.broadcasted_iota(jnp.int32, shape, axis)


def _group_specs(tm, width, n_ctx_tiles, pipeline_mode=None):
    return (pl.BlockSpec((tm, width), lambda i, *_: (jnp.minimum(i, n_ctx_tiles - 1), 0), pipeline_mode=pipeline_mode),
            pl.BlockSpec((tm, width), lambda i, *_: (jnp.maximum(i - n_ctx_tiles, 0), 0), pipeline_mode=pipeline_mode))


def _mod_kernel(c_ref, w_ref, b_ref, o_ref):
    o_ref[...] = _dot_hi(jax.nn.silu(c_ref[...]), w_ref[...]) + b_ref[...]


def _modulation(cond, ada_w, ada_b):
    tn = 1536
    n = ada_w.shape[-1]
    return pl.pallas_call(
        _mod_kernel,
        grid=(DEPTH, n // tn),
        in_specs=[pl.BlockSpec((N_COND, D_MODEL), lambda l, j: (0, 0)),
                  pl.BlockSpec((None, D_MODEL, tn), lambda l, j: (l, 0, j)),
                  pl.BlockSpec((None, 1, tn), lambda l, j: (l, 0, j))],
        out_specs=pl.BlockSpec((None, N_COND, tn), lambda l, j: (l, 0, j)),
        out_shape=jax.ShapeDtypeStruct((DEPTH, N_COND, n), F32),
        compiler_params=_params(("parallel", "parallel"), 40),
        name="modulation",
    )(cond, ada_w, ada_b.reshape(DEPTH, 1, n))


def _inproj_kernel(*refs, n_ctx_tiles, split):
    if split:
        xc_ref, xs_ref = refs[:2]
    x_ref, mod_ref, w_ref, b_ref, wg_ref, bg_ref, o_ref, u4_ref, g_ref, h_scr, u_scr = refs[1 if split else 0:]
    j = pl.program_id(1)
    tm = x_ref.shape[0]

    @pl.when(j == 0)
    def _():
        x = jnp.where(pl.program_id(0) < n_ctx_tiles, xc_ref[...], xs_ref[...]) if split else x_ref[...]
        h = (x * (1.0 + mod_ref[1:2, :]) + mod_ref[0:1, :]).astype(BF16)
        h_scr[...] = h
        g_ref[...] = _dot(h, wg_ref[...]) + bg_ref[...]
        u = _dot(h, w_ref[...]) + b_ref[...]
        for q in range(S5_Q):
            u_scr[q] = u[:, q * LANE:(q + 1) * LANE]
            rows = [u_scr[q, pl.ds(t, tm // SUB, stride=SUB), :].astype(BF16) for t in range(SUB)]
            for p in range(S5_P):
                u4_ref[q, :, p * S5_PFLAT:(p + 1) * S5_PFLAT] = jnp.concatenate(
                    [r[:, p * S5_PB:(p + 1) * S5_PB] for r in rows], axis=1)

    @pl.when(j > 0)
    def _():
        o_ref[...] = (_dot(h_scr[...], w_ref[...]) + b_ref[...]).astype(o_ref.dtype)


def _inproj(x, mod_l, w, b, wg, bg, cfg):
    split = isinstance(x, tuple)
    xs = list(x) if split else [x]
    t_tok = sum(a.shape[0] for a in xs)
    tm, tn = cfg["tm_in"], 512
    nj = w.shape[1] // tn
    n_ctx_tiles = cfg["n_ctx_tok"] // tm
    cond = functools.partial(_cond_row, tm=tm, n_ctx_tok=cfg["n_ctx_tok"], lat_len=cfg["lat_len"])
    x_specs = (list(_group_specs(tm, D_MODEL, n_ctx_tiles, pl.Buffered(1))) if split
               else [pl.BlockSpec((tm, D_MODEL), lambda i, j: (i, 0))])
    return pl.pallas_call(
        functools.partial(_inproj_kernel, n_ctx_tiles=n_ctx_tiles, split=split),
        grid=(t_tok // tm, nj),
        in_specs=x_specs + [
                  pl.BlockSpec((None, 6, D_MODEL), lambda i, j: (cond(i), 0, 0)),
                  pl.BlockSpec((D_MODEL, tn), lambda i, j: (0, j)),
                  pl.BlockSpec((1, tn), lambda i, j: (0, j)),
                  pl.BlockSpec((D_MODEL, LANE), lambda i, j: (0, 0)),
                  pl.BlockSpec((1, LANE), lambda i, j: (0, 0))],
        out_specs=[pl.BlockSpec((tm, tn), lambda i, j: (i, jnp.maximum(j - 1, 0))),
                   pl.BlockSpec((S5_Q, tm // SUB, S5_FLAT), lambda i, j: (0, i, 0)),
                   pl.BlockSpec((tm, LANE), lambda i, j: (i, 0))],
        out_shape=[jax.ShapeDtypeStruct((t_tok, (nj - 1) * tn), BF16),
                   jax.ShapeDtypeStruct((S5_Q, t_tok // SUB, S5_FLAT), BF16),
                   jax.ShapeDtypeStruct((t_tok, LANE), F32)],
        scratch_shapes=[pltpu.VMEM((tm, D_MODEL), BF16), pltpu.VMEM((S5_Q, tm, LANE), F32)],
        compiler_params=_params(("parallel", "arbitrary"), 48),
        name="inproj",
    )(*xs, mod_l, w, b, wg, bg)


def _s5_factors(lam_re, lam_im, log_step, b_re, b_im, c_re, c_im):
    lam = lax.complex(lam_re.astype(F32), lam_im.astype(F32))
    lam_dt = lam * jnp.exp(log_step.astype(F32))[..., None]
    lam_bar = jnp.exp(lam_dt)
    bbar = ((lam_bar - 1.0) / lam)[..., None] * lax.complex(b_re.astype(F32), b_im.astype(F32))
    cmat = lax.complex(c_re.astype(F32), c_im.astype(F32))
    ks = jnp.arange(SUB + 1, dtype=F32)
    pw = jnp.exp(lam_dt[None] * ks[:, None, None, None])
    kern = jnp.einsum('dgcp,tdgp,dgpe->dgtce', cmat, pw[:SUB], bbar).real
    pad = jnp.zeros_like(kern[0][:, :SUB - 1])
    ktab = jnp.concatenate([pad, kern[0]], 1) + jnp.concatenate([kern[1][:, ::-1], pad], 1)
    k_lag = ktab.reshape(S5_Q, S5_QG, N_LAG, S5_GROUP, S5_GROUP).transpose(0, 2, 3, 1, 4)
    k_lag = k_lag.reshape(S5_Q, N_LAG, S5_GROUP, S5_P, S5_PB).transpose(0, 3, 1, 2, 4)
    pw_in = jnp.stack([pw[:SUB][::-1, 0], pw[:SUB][:, 1]], 0)
    wb = pw_in[..., None] * bbar[:, None]
    wb = jnp.stack([wb.real, wb.imag], 1).reshape(2, 2, SUB, S5_Q, S5_QG, S5_STATE, S5_GROUP)
    a_in = wb.transpose(3, 2, 0, 1, 5, 4, 6).reshape(S5_Q, SUB, 4, S5_STATE, S5_P, S5_PB).transpose(0, 4, 1, 2, 3, 5)
    pw_out = jnp.stack([pw[1:, 0], pw[1:][::-1, 1]], 0)
    ce = cmat[:, None] * pw_out[:, :, :, None, :]
    ce = jnp.stack([ce.real, -ce.imag], 1).reshape(2, 2, SUB, S5_Q, S5_QG, S5_GROUP, S5_STATE)
    b_out = ce.transpose(3, 0, 1, 2, 5, 4, 6).reshape(S5_Q, 4, SUB, S5_GROUP, S5_P, S5_PHALF).transpose(0, 4, 1, 2, 3, 5)
    a = pw[SUB]
    a = jnp.stack([a.real, a.imag], 1).reshape(2, 2, S5_Q, 1, S5_HALF).transpose(2, 0, 1, 3, 4)
    return a_in.astype(BF16), k_lag.astype(BF16), b_out.astype(BF16), a


def _expand(src_t, n_rep, row_shift, col_shift):
    k, r = src_t.shape
    rep = jnp.where(_iota((k, n_rep * k), 0) == (_iota((k, n_rep * k), 1) & (k - 1)), 1.0, 0.0).astype(BF16)
    same = (_iota((r, n_rep * k), 0) >> row_shift) == (_iota((r, n_rep * k), 1) >> col_shift)
    return jnp.where(same, _dot_tn(src_t, rep), 0.0).astype(BF16)


def _s5a_kernel(u_ref, a_ref, o_ref, w_scr):
    @pl.when(pl.program_id(1) == 0)
    def _():
        for p in range(S5_P):
            for k in range(4):
                w_scr[p, :, k * S5_PHALF:(k + 1) * S5_PHALF] = jnp.concatenate(
                    [_expand(a_ref[p, t, k], S5_PG, 4, 6) for t in range(SUB)], axis=0)

    for p in range(S5_P):
        res = _dot(u_ref[:, p * S5_PFLAT:(p + 1) * S5_PFLAT], w_scr[p])
        for d in range(2):
            for r in range(2):
                k = 2 * d + r
                o_ref[d, r, :, p * S5_PHALF:(p + 1) * S5_PHALF] = res[:, k * S5_PHALF:(k + 1) * S5_PHALF]


def _s5_state_in(u4, a_in, cfg):
    rows = u4.shape[1]
    rt = cfg["s5_rows"]
    return pl.pallas_call(
        _s5a_kernel,
        grid=(S5_Q, rows // rt),
        in_specs=[pl.BlockSpec((None, rt, S5_FLAT), lambda q, i: (q, i, 0)),
                  pl.BlockSpec((None, S5_P, SUB, 4, S5_STATE, S5_PB), lambda q, i: (q, 0, 0, 0, 0, 0))],
        out_specs=pl.BlockSpec((None, 2, 2, rt, S5_HALF), lambda q, i: (q, 0, 0, i, 0)),
        out_shape=jax.ShapeDtypeStruct((S5_Q, 2, 2, rows, S5_HALF), F32),
        scratch_shapes=[pltpu.VMEM((S5_P, S5_PFLAT, 4 * S5_PHALF), BF16)],
        compiler_params=_params(("parallel", "arbitrary"), 48),
        name="s5_state_in",
    )(u4, a_in)


def _s5b_kernel(loc_ref, a_ref, x0_ref, xp_ref, fin_ref, *, n_ctx_seq, ctx_sub, n_lat_seq, lat_sub):
    d = pl.program_id(1)
    ar = jnp.broadcast_to(a_ref[0], (8, LANE))
    ai = jnp.broadcast_to(a_ref[1], (8, LANE))

    def run(base, nsub, xr0, xi0):
        def body(jj, carry):
            xr, xi = carry
            j = jnp.where(d == 0, jj, nsub - 1 - jj)
            idx = pl.ds(base + j, 8, stride=nsub)
            xp_ref[0, idx, :] = xr
            xp_ref[1, idx, :] = xi
            lr = loc_ref[0, idx, :]
            li = loc_ref[1, idx, :]
            return ar * xr - ai * xi + lr, ar * xi + ai * xr + li
        return lax.fori_loop(0, nsub, body, (xr0, xi0))

    zero = jnp.zeros((8, LANE), F32)
    for bg in range(n_ctx_seq // 8):
        xr, xi = run(bg * 8 * ctx_sub, ctx_sub, zero, zero)
        fin_ref[0, bg * 8:(bg + 1) * 8, :] = xr
        fin_ref[1, bg * 8:(bg + 1) * 8, :] = xi
    for bg in range(n_lat_seq // 8):
        run(n_ctx_seq * ctx_sub + bg * 8 * lat_sub, lat_sub, x0_ref[0, bg * 8:(bg + 1) * 8, :],
            x0_ref[1, bg * 8:(bg + 1) * 8, :])


def _s5_scan(loc, a, x0, cfg):
    rows = loc.shape[3]
    n_ctx_seq, n_lat_seq = cfg["n_ctx_seq"], cfg["n_lat_seq"]
    kern = functools.partial(_s5b_kernel, n_ctx_seq=n_ctx_seq, ctx_sub=cfg["ctx_len"] // SUB,
                             n_lat_seq=n_lat_seq, lat_sub=cfg["lat_len"] // SUB)
    nlb = S5_HALF // LANE
    return pl.pallas_call(
        kern,
        grid=(S5_Q, 2, nlb),
        in_specs=[pl.BlockSpec((None, None, 2, rows, LANE), lambda q, d, b: (q, d, 0, 0, b)),
                  pl.BlockSpec((None, None, 2, 1, LANE), lambda q, d, b: (q, d, 0, 0, b)),
                  pl.BlockSpec((None, None, 2, n_lat_seq, LANE), lambda q, d, b: (q, d, 0, 0, b))],
        out_specs=[pl.BlockSpec((None, None, 2, rows, LANE), lambda q, d, b: (q, d, 0, 0, b)),
                   pl.BlockSpec((None, None, 2, n_ctx_seq, LANE), lambda q, d, b: (q, d, 0, 0, b))],
        out_shape=[jax.ShapeDtypeStruct(loc.shape, F32),
                   jax.ShapeDtypeStruct((S5_Q, 2, 2, n_ctx_seq, S5_HALF), F32)],
        compiler_params=_params(("parallel", "parallel", "parallel"), 48),
        name="s5_scan",
    )(loc, a, x0)


def _s5c_kernel(u_ref, xp_ref, k_ref, b_ref, d_ref, z_ref, m_scr, wo_scr, z_scr):
    rt = u_ref.shape[0]

    @pl.when(pl.program_id(1) == 0)
    def _():
        for p in range(S5_P):
            lags = jnp.concatenate([_expand(k_ref[p, l], S5_PG, 4, 4) for l in range(N_LAG)]
                                   + [jnp.zeros((S5_PB, S5_PB), BF16)], axis=1)
            for t in range(SUB):
                lo = (SUB - 1 - t) * S5_PB
                m_scr[p, t * S5_PB:(t + 1) * S5_PB, :] = lags[:, lo:lo + S5_PFLAT]
            for k in range(4):
                wo_scr[p, k * S5_PHALF:(k + 1) * S5_PHALF, :] = jnp.concatenate(
                    [_expand(b_ref[p, k, t], S5_PG, 6, 4) for t in range(SUB)], axis=1)

    z = []
    for p in range(S5_P):
        u = u_ref[:, p * S5_PFLAT:(p + 1) * S5_PFLAT]
        xcat = jnp.concatenate([xp_ref[d, r, :, p * S5_PHALF:(p + 1) * S5_PHALF] for d in range(2) for r in range(2)],
                               axis=1).astype(BF16)
        y = _dot(u, m_scr[p]) + _dot(xcat, wo_scr[p])
        z.append(jax.nn.gelu(d_ref[:, p * S5_PFLAT:(p + 1) * S5_PFLAT] * u.astype(F32) + y))
    for t in range(SUB):
        z_scr[pl.ds(t, rt, stride=SUB), :] = jnp.concatenate([zp[:, t * S5_PB:(t + 1) * S5_PB] for zp in z], axis=1)
    z_ref[...] = z_scr[...].astype(z_ref.dtype)


def _s5_output(u4, xprev, k_lag, b_out, d4, cfg):
    rows = u4.shape[1]
    rt = cfg["s5_rows"]
    return pl.pallas_call(
        _s5c_kernel,
        grid=(S5_Q, rows // rt),
        in_specs=[pl.BlockSpec((None, rt, S5_FLAT), lambda q, i: (q, i, 0)),
                  pl.BlockSpec((None, 2, 2, rt, S5_HALF), lambda q, i: (q, 0, 0, i, 0)),
                  pl.BlockSpec((None, S5_P, N_LAG, S5_GROUP, S5_PB), lambda q, i: (q, 0, 0, 0, 0)),
                  pl.BlockSpec((None, S5_P, 4, SUB, S5_GROUP, S5_PHALF), lambda q, i: (q, 0, 0, 0, 0, 0)),
                  pl.BlockSpec((None, 1, S5_FLAT), lambda q, i: (q, 0, 0))],
        out_specs=pl.BlockSpec((rt * SUB, LANE), lambda q, i: (i, q)),
        out_shape=jax.ShapeDtypeStruct((rows * SUB, S5_WIDTH), BF16),
        scratch_shapes=[pltpu.VMEM((S5_P, S5_PFLAT, S5_PFLAT), BF16), pltpu.VMEM((S5_P, 4 * S5_PHALF, S5_PFLAT), BF16),
                        pltpu.VMEM((rt * SUB, LANE), F32)],
        compiler_params=_params(("parallel", "arbitrary"), 56),
        name="s5_output",
    )(u4, xprev, k_lag, b_out, d4)


def _group_norm(o, g, b):
    mu = jnp.mean(o, axis=-1, keepdims=True)
    var = jnp.mean(jnp.square(o - mu), axis=-1, keepdims=True)
    return (o - mu) * lax.rsqrt(var + GN_EPS) * g + b


def _ret_kernel(dec_ref, q_ref, k_ref, v_ref, g_ref, cos_ref, sin_ref, s0_ref, gg_ref, gb_ref, y_ref, *rest,
                nc, n_sub, use_rot, has_init, want_final):
    if want_final:
        sfin_ref, sf_scr, sb_scr, x_scr, kr_scr = rest
    else:
        sf_scr, sb_scr, x_scr, kr_scr = rest
    h = pl.program_id(1)
    row = _iota((CHUNK, CHUNK), 0).astype(F32)
    col = _iota((CHUNK, CHUNK), 1).astype(F32)
    lg_f = -jnp.exp(jnp.full((CHUNK, CHUNK), dec_ref[0, h], F32))
    lg_b = -jnp.exp(jnp.full((CHUNK, CHUNK), dec_ref[1, h], F32))
    lag = row - col
    scale = HEAD_DIM ** -0.5
    dmat = (jnp.where(lag >= 0, jnp.exp(lg_f * jnp.maximum(lag, 0.0)), 0.0)
            + jnp.where(lag <= 0, jnp.exp(lg_b * jnp.maximum(-lag, 0.0)), 0.0)) * scale
    qd_f = jnp.exp(lg_f * (row + 1.0))
    qd_b = jnp.exp(lg_b * (CHUNK - row))
    kd_f = jnp.exp(lg_f * (CHUNK - 1.0 - col)) * scale
    kd_b = jnp.exp(lg_b * col) * scale
    cd_f = jnp.exp(lg_f * CHUNK)
    cd_b = jnp.exp(lg_b * CHUNK)

    def chunk(s, j):
        return pl.ds(pl.multiple_of((s * nc + j) * CHUNK, CHUNK), CHUNK)

    def rot(ref, sl):
        x = ref[sl, :].astype(F32)
        if not use_rot:
            return x
        return x * cos_ref[sl, :] + pltpu.roll(x, HEAD_DIM // 2, 1) * sin_ref[sl, :]

    def local(s, j, carry):
        sl = chunk(s, j)
        k = rot(k_ref, sl)
        kr_scr[sl, :] = k.astype(BF16)
        k_t = k.T
        v = v_ref[sl, :]
        x_scr[0, j] = _dot((k_t * kd_f).astype(BF16), v)
        x_scr[1, j] = _dot((k_t * kd_b).astype(BF16), v)
        return carry

    zero = jnp.zeros((CHUNK, CHUNK), F32)

    def states(jj, carry):
        s_f, s_b = carry
        jf, jb = jj, nc - 1 - jj
        sf_scr[jf] = s_f.astype(BF16)
        sb_scr[jb] = s_b.astype(BF16)
        return s_f * cd_f + x_scr[0, jf], s_b * cd_b + x_scr[1, jb]

    def outputs(s, j, carry):
        sl = chunk(s, j)
        q = rot(q_ref, sl)
        att = _dot_nt(q.astype(BF16), kr_scr[sl, :]) * dmat
        lhs = jnp.concatenate([att.astype(BF16), (q * qd_f).astype(BF16), (q * qd_b).astype(BF16)], axis=1)
        rhs = jnp.concatenate([v_ref[sl, :], sf_scr[j], sb_scr[j]], axis=0)
        o = _dot(lhs, rhs)
        y = jax.nn.silu(g_ref[sl, :].astype(F32)) * _group_norm(o, gg_ref[...], gb_ref[...])
        y_ref[sl, :] = y.astype(y_ref.dtype)
        return carry

    def sequence(s, carry):
        lax.fori_loop(0, nc, functools.partial(local, s), 0, unroll=min(8, nc))
        init = (s0_ref[s, 0], s0_ref[s, 1]) if has_init else (zero, zero)
        s_f, s_b = lax.fori_loop(0, nc, states, init, unroll=2)
        if want_final:
            sfin_ref[s, 0] = s_f
            sfin_ref[s, 1] = s_b
        lax.fori_loop(0, nc, functools.partial(outputs, s), 0, unroll=min(16, nc))
        return carry

    if n_sub == 1:
        sequence(0, 0)
    else:
        lax.fori_loop(0, n_sub, sequence, 0)


def _retention(proj, dec, cos_t, sin_t, s0, gn_g, gn_b, *, n_seq, seq_len, n_sub, row0, use_rot, has_init,
               want_final):
    assert n_seq % n_sub == 0 and not (use_rot and n_sub > 1)
    nc = seq_len // CHUNK
    rows = n_sub * seq_len
    blk0 = row0 // rows
    kern = functools.partial(_ret_kernel, nc=nc, n_sub=n_sub, use_rot=use_rot, has_init=has_init, want_final=want_final)

    def tok(cb):
        return pl.BlockSpec((rows, HEAD_DIM), lambda s, h, cb=cb: (blk0 + s, cb + h))

    rot_spec = pl.BlockSpec((seq_len, HEAD_DIM), lambda s, h: (0, 0))
    st_spec = pl.BlockSpec((n_sub, 2, None, HEAD_DIM, HEAD_DIM), lambda s, h: (s, 0, h, 0, 0))
    gn_spec = pl.BlockSpec((1, HEAD_DIM), lambda s, h: (0, h))
    out_specs = [pl.BlockSpec((rows, HEAD_DIM), lambda s, h: (s, h))]
    out_shape = [jax.ShapeDtypeStruct((n_seq * seq_len, WIDTH), BF16)]
    if want_final:
        out_specs.append(st_spec)
        out_shape.append(jax.ShapeDtypeStruct((n_seq, 2, HEADS, HEAD_DIM, HEAD_DIM), F32))
    return pl.pallas_call(
        kern,
        grid=(n_seq // n_sub, HEADS),
        in_specs=[pl.BlockSpec(memory_space=pltpu.SMEM), tok(0), tok(4), tok(8), tok(12),
                  rot_spec, rot_spec, st_spec, gn_spec, gn_spec],
        out_specs=out_specs,
        out_shape=out_shape,
        scratch_shapes=[pltpu.VMEM((nc, HEAD_DIM, HEAD_DIM), BF16), pltpu.VMEM((nc, HEAD_DIM, HEAD_DIM), BF16),
                        pltpu.VMEM((2, nc, HEAD_DIM, HEAD_DIM), F32), pltpu.VMEM((rows, HEAD_DIM), BF16)],
        compiler_params=_params(("parallel", "parallel"), 48),
        name="retention",
    )(dec, proj, proj, proj, proj, cos_t, sin_t, s0, gn_g, gn_b)


def _gate_prep_kernel(g_ref, col_ref, row_ref):
    lane = _iota((CHUNK, LANE), 1)
    tri = jnp.where(_iota((CHUNK, CHUNK), 0) >= _iota((CHUNK, CHUNK), 1), 1.0, 0.0)
    for c in range(g_ref.shape[0] // CHUNK):
        sl = slice(c * CHUNK, (c + 1) * CHUNK)
        g = g_ref[sl, :]
        lf = jnp.where(lane < 4 * HEADS, jnp.minimum(g, 0.0) - jnp.log1p(jnp.exp(-jnp.abs(g))), 0.0)
        cs = _dot_hi(tri, lf)
        tot = cs[CHUNK - 1:CHUNK, :]
        bc = jnp.where(lane < 2 * HEADS, cs, tot - cs + lf)
        rest = jnp.where(lane < 2 * HEADS, tot - cs, cs - lf)
        pack = g + pltpu.roll(bc, GATE_BC, 1) + pltpu.roll(rest, GATE_REST, 1)
        col_ref[sl, :] = pack
        row_ref[sl, :] = pack.T


def _gate_prep(gates, cfg):
    t_tok = gates.shape[0]
    tm = cfg["tm_in"]
    spec = pl.BlockSpec((tm, LANE), lambda i: (i, 0))
    return pl.pallas_call(
        _gate_prep_kernel,
        grid=(t_tok // tm,),
        in_specs=[spec],
        out_specs=[spec, spec],
        out_shape=[jax.ShapeDtypeStruct((t_tok, LANE), F32)] * 2,
        compiler_params=_params(("parallel",), 32),
        name="gate_prep",
    )(gates)


def _mlstm_kernel(q_ref, k_ref, v_ref, o_ref, col_ref, row_ref, c0_ref, n0_ref, m0_ref, gg_ref, gb_ref, y_ref, *rest,
                  nc, n_sub, has_init, want_final):
    if want_final:
        cfin_ref, nfin_ref, mfin_ref, c_scr, n_scr, m_scr, x_scr, nl_scr, ml_scr, bl_scr = rest
    else:
        c_scr, n_scr, m_scr, x_scr, nl_scr, ml_scr, bl_scr = rest
    src = _iota((CHUNK, CHUNK), 0)
    dst = _iota((CHUNK, CHUNK), 1)
    scale = HEAD_DIM ** -0.5

    def chunk(s, j):
        return pl.ds(pl.multiple_of((s * nc + j) * CHUNK, CHUNK), CHUNK)

    def gate_idx(d, h):
        return d * 2 * HEADS + h, GATE_BC + d * 2 * HEADS + HEADS + h, GATE_REST + d * 2 * HEADS + HEADS + h

    def head_body(h):
        def init(s, d):
            if has_init:
                return c0_ref[s, d], n0_ref[s, d], m0_ref[s, d]
            return (jnp.zeros((HEAD_DIM, HEAD_DIM), F32), jnp.zeros((1, HEAD_DIM), F32),
                    jnp.zeros((1, HEAD_DIM), F32))

        def local(s, j, carry):
            sl = chunk(s, j)
            k = k_ref[sl, :]
            v_t = v_ref[sl, :].astype(F32).T
            rp = row_ref[sl, :]
            for d in range(2):
                ii, bi, ri = gate_idx(d, h)
                bc_row = rp[bi:bi + 1, :]
                b_last = bc_row[:, CHUNK - 1:CHUNK] if d == 0 else bc_row[:, 0:1]
                log_k = rp[ri:ri + 1, :] + rp[ii:ii + 1, :]
                m_loc = jnp.max(log_k, axis=1, keepdims=True)
                kw = jnp.exp(log_k - m_loc)
                x_scr[d, j] = _dot((v_t * kw).astype(BF16), k) * scale
                kw_hi = kw.astype(BF16).astype(F32)
                kw2 = jnp.concatenate([jnp.broadcast_to(kw_hi, (8, CHUNK)), jnp.broadcast_to(kw - kw_hi, (8, CHUNK))], 0)
                nl = _dot(kw2.astype(BF16), k)
                nl_scr[d, j] = (nl[0:1, :] + nl[8:9, :]) * scale
                ml_scr[d, j] = jnp.broadcast_to(m_loc, (1, HEAD_DIM))
                bl_scr[d, j] = jnp.broadcast_to(b_last, (1, HEAD_DIM))
            return carry

        def states(jj, carry):
            out = []
            for d, j in ((0, jj), (1, nc - 1 - jj)):
                cmat, nvec, m = carry[d]
                c_scr[d, j] = cmat.astype(BF16)
                n_scr[d, j] = nvec
                m_scr[d, j] = m
                m_new = jnp.maximum(bl_scr[d, j] + m, ml_scr[d, j])
                keep = jnp.exp(bl_scr[d, j] + m - m_new)
                add = jnp.exp(ml_scr[d, j] - m_new)
                out.append((keep * cmat + add * x_scr[d, j], keep * nvec + add * nl_scr[d, j], m_new))
            return tuple(out)

        def outputs(s, j, carry):
            sl = chunk(s, j)
            q = q_ref[sl, :]
            v = v_ref[sl, :]
            s_t = _dot_nt(k_ref[sl, :], q) * scale
            cp = col_ref[sl, :]
            rp = row_ref[sl, :]
            h_t = None
            for d in range(2):
                ii, bi, _ = gate_idx(d, h)
                causal = (src <= dst) if d == 0 else (src >= dst)
                bc_row = rp[bi:bi + 1, :]
                log_d = jnp.where(causal, bc_row + (cp[:, ii:ii + 1] - cp[:, bi:bi + 1]), -jnp.inf)
                log_prev = bc_row + m_scr[d, j]
                m_t = jnp.maximum(log_prev, jnp.max(log_d, axis=0, keepdims=True))
                w = s_t * jnp.exp(log_d - m_t)
                w_prev = jnp.exp(log_prev - m_t)
                qn = _dot_nt(jnp.broadcast_to(n_scr[d, j], (16, HEAD_DIM)).astype(BF16), q)[0:1, :]
                den = jnp.sum(w, axis=0, keepdims=True) + w_prev * qn
                inv = 1.0 / jnp.maximum(jnp.abs(den), jnp.exp(-m_t))
                num = _dot_tn(v, w.astype(BF16)) + _dot_nt(c_scr[d, j], q) * w_prev
                h_t = num * inv if h_t is None else h_t + num * inv
            y = jax.nn.sigmoid(o_ref[sl, :].astype(F32).T) * h_t
            mu = jnp.mean(y, axis=0, keepdims=True)
            var = jnp.mean(jnp.square(y - mu), axis=0, keepdims=True)
            y = (y - mu) * lax.rsqrt(var + GN_EPS) * gg_ref[...] + gb_ref[...]
            y_ref[sl, :] = y.T.astype(y_ref.dtype)
            return carry

        def sequence(s, carry):
            lax.fori_loop(0, nc, functools.partial(local, s), 0, unroll=min(8, nc))
            fin = lax.fori_loop(0, nc, states, (init(s, 0), init(s, 1)), unroll=2)
            if want_final:
                for d in range(2):
                    cfin_ref[s, d] = fin[d][0]
                    nfin_ref[s, d] = fin[d][1]
                    mfin_ref[s, d] = fin[d][2]
            lax.fori_loop(0, nc, functools.partial(outputs, s), 0, unroll=min(4, nc))
            return carry

        if n_sub == 1:
            sequence(0, 0)
        else:
            lax.fori_loop(0, n_sub, sequence, 0)

    hh = pl.program_id(1)
    for h in range(HEADS):
        pl.when(hh == h)(functools.partial(head_body, h))


def _mlstm(proj, gcol, grow, c0, n0, m0, gn_g, gn_b, *, n_seq, seq_len, n_sub, row0, has_init, want_final):
    assert n_seq % n_sub == 0
    nc = seq_len // CHUNK
    rows = n_sub * seq_len
    blk0 = row0 // rows
    kern = functools.partial(_mlstm_kernel, nc=nc, n_sub=n_sub, has_init=has_init, want_final=want_final)

    def tok(cb):
        return pl.BlockSpec((rows, HEAD_DIM), lambda s, h, cb=cb: (blk0 + s, cb + h))

    gate_spec = pl.BlockSpec((rows, LANE), lambda s, h: (blk0 + s, 0))
    c_spec = pl.BlockSpec((n_sub, 2, None, HEAD_DIM, HEAD_DIM), lambda s, h: (s, 0, h, 0, 0))
    v_spec = pl.BlockSpec((n_sub, 2, None, 1, HEAD_DIM), lambda s, h: (s, 0, h, 0, 0))
    gn_spec = pl.BlockSpec((HEAD_DIM, LANE), lambda s, h: (h, 0))
    out_specs = [pl.BlockSpec((rows, HEAD_DIM), lambda s, h: (s, h))]
    out_shape = [jax.ShapeDtypeStruct((n_seq * seq_len, WIDTH), BF16)]
    if want_final:
        out_specs += [c_spec, v_spec, v_spec]
        out_shape += [jax.ShapeDtypeStruct((n_seq, 2, HEADS, HEAD_DIM, HEAD_DIM), F32),
                      jax.ShapeDtypeStruct((n_seq, 2, HEADS, 1, HEAD_DIM), F32),
                      jax.ShapeDtypeStruct((n_seq, 2, HEADS, 1, HEAD_DIM), F32)]
    return pl.pallas_call(
        kern,
        grid=(n_seq // n_sub, HEADS),
        in_specs=[tok(16), tok(20), tok(24), tok(28), gate_spec, gate_spec,
                  c_spec, v_spec, v_spec, gn_spec, gn_spec],
        out_specs=out_specs,
        out_shape=out_shape,
        scratch_shapes=[pltpu.VMEM((2, nc, HEAD_DIM, HEAD_DIM), BF16)] + [pltpu.VMEM((2, nc, 1, HEAD_DIM), F32)] * 2
        + [pltpu.VMEM((2, nc, HEAD_DIM, HEAD_DIM), F32)] + [pltpu.VMEM((2, nc, 1, HEAD_DIM), F32)] * 3,
        compiler_params=_params(("parallel", "parallel"), 48),
        name="mlstm",
    )(proj, proj, proj, proj, gcol, grow, c0, n0, m0, gn_g, gn_b)


def _layer_norm(x, g, b):
    mu = jnp.mean(x, axis=-1, keepdims=True)
    var = jnp.mean(jnp.square(x - mu), axis=-1, keepdims=True)
    return (x - mu) * lax.rsqrt(var + LN_EPS) * g + b


def _top2_gates(h, whi_ref, wlo_ref, b_ref):
    lane = _iota((h.shape[0], LANE), 1)
    h_hi = h.astype(BF16)
    h_lo = (h - h_hi.astype(F32)).astype(BF16)
    logits = _dot(h_hi, whi_ref[...]) + _dot(h_hi, wlo_ref[...]) + _dot(h_lo, whi_ref[...]) + b_ref[...]
    logits = jnp.where(lane < N_EXPERTS, logits, NEG_BIG)
    m1 = jnp.max(logits, axis=1, keepdims=True)
    i1 = jnp.min(jnp.where(logits == m1, lane, LANE), axis=1, keepdims=True)
    rest = jnp.where(lane == i1, NEG_BIG, logits)
    m2 = jnp.max(rest, axis=1, keepdims=True)
    i2 = jnp.min(jnp.where(rest == m2, lane, LANE), axis=1, keepdims=True)
    e2 = jnp.exp(m2 - m1)
    den = 1.0 + e2
    return jnp.where(lane == i1, 1.0 / den, 0.0) + jnp.where(lane == i2, e2 / den, 0.0)


def _merge_kernel(*refs, n_ctx_tiles, with_router, split):
    if split:
        xc_ref, xs_ref = refs[:2]
    (x_ref, mod_ref, z_ref, ybc_ref, ybs_ref, ycc_ref, ycs_ref, wm_ref, bm_ref, wglu_ref, bglu_ref,
     wa_ref, wb_ref, wc_ref, wo_ref, lg_ref, lb_ref, *rest) = refs[1 if split else 0:]
    if with_router:
        rwh_ref, rwl_ref, rb_ref, o_ref, g_ref = rest
    else:
        o_ref, = rest
    is_ctx = pl.program_id(0) < n_ctx_tiles
    x = jnp.where(is_ctx, xc_ref[...], xs_ref[...]) if split else x_ref[...]
    h = (x * (1.0 + mod_ref[1:2, :]) + mod_ref[0:1, :]).astype(BF16)
    z = z_ref[...]
    ya = (z.astype(F32) * jax.nn.sigmoid(_dot(z, wglu_ref[...]) + bglu_ref[...])).astype(BF16)
    yb = jnp.where(is_ctx, ybc_ref[...], ybs_ref[...])
    yc = jnp.where(is_ctx, ycc_ref[...], ycs_ref[...])
    merged = None
    for j, (y, w_ref) in enumerate(((ya, wa_ref), (yb, wb_ref), (yc, wc_ref))):
        gate = jax.nn.sigmoid(_dot(h, wm_ref[:, j * D_MODEL:(j + 1) * D_MODEL]) + bm_ref[:, j * D_MODEL:(j + 1) * D_MODEL])
        term = gate * _dot(y, w_ref[...])
        merged = term if merged is None else merged + term
    mix = _dot(merged.astype(BF16), wo_ref[...])
    x1 = _layer_norm(ALPHA * x + mod_ref[2:3, :] * mix, lg_ref[...], lb_ref[...])
    o_ref[...] = x1
    if with_router:
        g_ref[...] = _top2_gates(x1 * (1.0 + mod_ref[4:5, :]) + mod_ref[3:4, :], rwh_ref, rwl_ref, rb_ref)


def _merge(x, mod_l, z, yb_c, yb_s, yc_c, yc_s, wm, bm, wglu, bglu, wa, wb, wc, wo, lg, lb, router, cfg):
    split = isinstance(x, tuple)
    xs = list(x) if split else [x]
    t_tok = sum(a.shape[0] for a in xs)
    tm = cfg["tm_merge"]
    n_ctx_tiles = cfg["n_ctx_tok"] // tm
    cond = functools.partial(_cond_row, tm=tm, n_ctx_tok=cfg["n_ctx_tok"], lat_len=cfg["lat_len"])

    def full(shape):
        return pl.BlockSpec(shape, lambda i: (0,) * len(shape))

    def tok(w):
        return pl.BlockSpec((tm, w), lambda i: (i, 0))

    ctx_spec, lat_spec = _group_specs(tm, WIDTH, n_ctx_tiles)
    x_specs = list(_group_specs(tm, D_MODEL, n_ctx_tiles)) if split else [tok(D_MODEL)]
    in_specs = x_specs + [pl.BlockSpec((None, 6, D_MODEL), lambda i: (cond(i), 0, 0)),
                tok(WIDTH), ctx_spec, lat_spec, ctx_spec, lat_spec,
                full((D_MODEL, 3 * D_MODEL)), full((1, 3 * D_MODEL)), full((WIDTH, WIDTH)), full((1, WIDTH)),
                full((WIDTH, D_MODEL)), full((WIDTH, D_MODEL)), full((WIDTH, D_MODEL)),
                full((D_MODEL, D_MODEL)), full((1, D_MODEL)), full((1, D_MODEL))]
    args = xs + [mod_l, z, yb_c, yb_s, yc_c, yc_s, wm, bm, wglu, bglu, wa, wb, wc, wo, lg, lb]
    out_specs = [tok(D_MODEL)]
    out_shape = [jax.ShapeDtypeStruct((t_tok, D_MODEL), F32)]
    if router is not None:
        in_specs += [full((D_MODEL, LANE)), full((D_MODEL, LANE)), full((1, LANE))]
        args += list(router)
        out_specs.append(tok(LANE))
        out_shape.append(jax.ShapeDtypeStruct((t_tok, LANE), F32))
    return pl.pallas_call(
        functools.partial(_merge_kernel, n_ctx_tiles=n_ctx_tiles, with_router=router is not None, split=split),
        grid=(t_tok // tm,),
        in_specs=in_specs,
        out_specs=out_specs,
        out_shape=out_shape,
        compiler_params=_params(("parallel",), 56),
        name="merge",
    )(*args)


def _ffn_kernel(x_ref, mod_ref, w1_ref, w3_ref, w2_ref, lg_ref, lb_ref, o_ref, h_scr, acc_scr):
    f = pl.program_id(1)

    @pl.when(f == 0)
    def _():
        h_scr[...] = (x_ref[...] * (1.0 + mod_ref[4:5, :]) + mod_ref[3:4, :]).astype(BF16)
        acc_scr[...] = jnp.zeros_like(acc_scr)

    h = h_scr[...]
    act = (jax.nn.silu(_dot(h, w1_ref[...])) * _dot(h, w3_ref[...])).astype(BF16)
    acc_scr[...] += _dot(act, w2_ref[...])

    @pl.when(f == pl.num_programs(1) - 1)
    def _():
        o_ref[...] = _layer_norm(ALPHA * x_ref[...] + mod_ref[5:6, :] * acc_scr[...], lg_ref[...], lb_ref[...])


def _dense_ffn(x, mod_l, w1, w3, w2, lg, lb, cfg):
    t_tok = x.shape[0]
    tm, tf = cfg["tm_ffn"], D_FF
    cond = functools.partial(_cond_row, tm=tm, n_ctx_tok=cfg["n_ctx_tok"], lat_len=cfg["lat_len"])
    return pl.pallas_call(
        _ffn_kernel,
        grid=(t_tok // tm, D_FF // tf),
        in_specs=[pl.BlockSpec((tm, D_MODEL), lambda i, f: (i, 0)),
                  pl.BlockSpec((None, 6, D_MODEL), lambda i, f: (cond(i), 0, 0)),
                  pl.BlockSpec((D_MODEL, tf), lambda i, f: (0, f)),
                  pl.BlockSpec((D_MODEL, tf), lambda i, f: (0, f)),
                  pl.BlockSpec((tf, D_MODEL), lambda i, f: (f, 0)),
                  pl.BlockSpec((1, D_MODEL), lambda i, f: (0, 0)),
                  pl.BlockSpec((1, D_MODEL), lambda i, f: (0, 0))],
        out_specs=pl.BlockSpec((tm, D_MODEL), lambda i, f: (i, 0)),
        out_shape=jax.ShapeDtypeStruct((t_tok, D_MODEL), F32),
        scratch_shapes=[pltpu.VMEM((tm, D_MODEL), BF16), pltpu.VMEM((tm, D_MODEL), F32)],
        compiler_params=_params(("parallel", "arbitrary"), 56),
        name="dense_ffn",
    )(x, mod_l, w1, w3, w2, lg, lb)


def _moe_kernel(x_ref, mod_ref, gate_ref, w1_ref, w3_ref, w2_ref, lg_ref, lb_ref, oc_ref, os_ref,
                h_scr, acc_scr, hc_scr, ob_scr, sp_scr, gt_scr, cnt_smem, *, n_ctx_tiles):
    e = pl.program_id(1)
    f = pl.program_id(2)
    last_f = pl.num_programs(2) - 1
    tm = x_ref.shape[0]

    @pl.when((e == 0) & (f == 0))
    def _():
        h_scr[...] = (x_ref[...] * (1.0 + mod_ref[4:5, :]) + mod_ref[3:4, :]).astype(BF16)
        acc_scr[...] = jnp.zeros_like(acc_scr)
        g = gate_ref[...]
        sel = g > 0.0
        ones = jnp.where(sel, 1.0, 0.0)
        before = jnp.where(_iota((tm, tm), 0) > _iota((tm, tm), 1), 1.0, 0.0).astype(BF16)
        pos = _dot(before, ones.astype(BF16))
        spt = jnp.where(sel, pos, -1.0).T
        gt = g.T
        cnt = jnp.sum(ones, axis=0, keepdims=True)
        for ee in range(N_EXPERTS):
            sp_scr[ee] = spt[ee:ee + 1, :]
            gt_scr[ee] = gt[ee:ee + 1, :]
            cnt_smem[ee] = cnt[0, ee].astype(jnp.int32)

    nb = (cnt_smem[e] + (MOE_BLK - 1)) // MOE_BLK
    row = _iota((MOE_BLK, tm), 0)

    def onehot(b):
        return sp_scr[e] == (row + b * MOE_BLK).astype(F32)

    @pl.when(f == 0)
    def _():
        def gather(b, carry):
            p = jnp.where(onehot(b), 1.0, 0.0).astype(BF16)
            hc_scr[b] = _dot(p, h_scr[...]).astype(BF16)
            ob_scr[b] = jnp.zeros((MOE_BLK, D_MODEL), F32)
            return carry
        lax.fori_loop(0, nb, gather, 0)

        @pl.when(nb % 2 == 1)
        def _():
            ob_scr[nb] = jnp.zeros((MOE_BLK, D_MODEL), F32)

    def ffn(hc):
        act = (jax.nn.silu(_dot(hc, w1_ref[...])) * _dot(hc, w3_ref[...])).astype(BF16)
        return _dot(act, w2_ref[...])

    def ffn_pair(p, carry):
        two = pl.ds(2 * p, 2)
        ob_scr[two] += ffn(hc_scr[two].reshape(2 * MOE_BLK, D_MODEL)).reshape(2, MOE_BLK, D_MODEL)
        return carry
    lax.fori_loop(0, nb // 2, ffn_pair, 0)

    @pl.when(nb % 2 == 1)
    def _():
        ob_scr[nb - 1] += ffn(hc_scr[nb - 1])

    @pl.when(f == last_f)
    def _():
        row2 = _iota((2 * MOE_BLK, tm), 0)

        def scatter(p, carry):
            m = sp_scr[e] == (row2 + p * (2 * MOE_BLK)).astype(F32)
            gc = jnp.sum(jnp.where(m, gt_scr[e], 0.0), axis=1, keepdims=True)
            og = (ob_scr[pl.ds(2 * p, 2)].reshape(2 * MOE_BLK, D_MODEL) * gc).astype(BF16)
            acc_scr[...] += _dot_tn(jnp.where(m, 1.0, 0.0).astype(BF16), og)
            return carry
        lax.fori_loop(0, (nb + 1) // 2, scatter, 0)

    last = (e == pl.num_programs(1) - 1) & (f == last_f)
    is_ctx = pl.program_id(0) < n_ctx_tiles
    for o_ref, mine in ((oc_ref, is_ctx), (os_ref, jnp.logical_not(is_ctx))):
        @pl.when(last & mine)
        def _():
            o_ref[...] = _layer_norm(ALPHA * x_ref[...] + mod_ref[5:6, :] * acc_scr[...], lg_ref[...], lb_ref[...])


def _moe_ffn(x, mod_l, gates, w1, w3, w2, layer, lg, lb, cfg):
    t_tok = x.shape[0]
    tm, tf = cfg["tm_moe"], cfg["tf"]
    n_ctx_tiles = cfg["n_ctx_tok"] // tm
    cond = functools.partial(_cond_row, tm=tm, n_ctx_tok=cfg["n_ctx_tok"], lat_len=cfg["lat_len"])
    return pl.pallas_call(
        functools.partial(_moe_kernel, n_ctx_tiles=n_ctx_tiles),
        grid=(t_tok // tm, N_EXPERTS, D_FF // tf),
        in_specs=[pl.BlockSpec((tm, D_MODEL), lambda i, e, f: (i, 0), pipeline_mode=pl.Buffered(1)),
                  pl.BlockSpec((None, 6, D_MODEL), lambda i, e, f: (cond(i), 0, 0)),
                  pl.BlockSpec((tm, LANE), lambda i, e, f: (i, 0), pipeline_mode=pl.Buffered(1)),
                  pl.BlockSpec((None, None, D_MODEL, tf), lambda i, e, f: (layer, e, 0, f)),
                  pl.BlockSpec((None, None, D_MODEL, tf), lambda i, e, f: (layer, e, 0, f)),
                  pl.BlockSpec((None, None, tf, D_MODEL), lambda i, e, f: (layer, e, f, 0)),
                  pl.BlockSpec((1, D_MODEL), lambda i, e, f: (0, 0)),
                  pl.BlockSpec((1, D_MODEL), lambda i, e, f: (0, 0))],
        out_specs=list(_group_specs(tm, D_MODEL, n_ctx_tiles)),
        out_shape=[jax.ShapeDtypeStruct((cfg["n_ctx_tok"], D_MODEL), F32),
                   jax.ShapeDtypeStruct((t_tok - cfg["n_ctx_tok"], D_MODEL), F32)],
        scratch_shapes=[pltpu.VMEM((tm, D_MODEL), BF16), pltpu.VMEM((tm, D_MODEL), F32),
                        pltpu.VMEM((tm // MOE_BLK, MOE_BLK, D_MODEL), BF16),
                        pltpu.VMEM((tm // MOE_BLK, MOE_BLK, D_MODEL), F32),
                        pltpu.VMEM((N_EXPERTS, 1, tm), F32), pltpu.VMEM((N_EXPERTS, 1, tm), F32),
                        pltpu.SMEM((N_EXPERTS,), jnp.int32)],
        compiler_params=_params(("arbitrary", "arbitrary", "arbitrary"), 56),
        name="moe_ffn",
    )(x, mod_l, gates, w1, w3, w2, lg, lb)


def _rotary_tables(n_tok):
    rows = n_tok // GRID_W
    r = jnp.repeat(jnp.arange(rows, dtype=F32), GRID_W)
    col = jnp.tile(jnp.arange(GRID_W, dtype=F32), rows)
    n_freq = HEAD_DIM // 4
    inv = ROPE_BASE ** (-jnp.arange(n_freq, dtype=F32) / n_freq)
    ang = jnp.concatenate([r[:, None] * inv, col[:, None] * inv], -1)
    cos, sin = jnp.cos(ang), jnp.sin(ang)
    return jnp.concatenate([cos, cos], -1), jnp.concatenate([-sin, sin], -1)


def _row2(v):
    return v.reshape(1, -1).astype(F32)


def kernel(x_prompt, x_sample, cache_s5_re, cache_s5_im, cache_ret, cache_ml_c, cache_ml_n, cache_ml_m, c, c_ctx, ada_w, ada_b, w_in, b_in, s5_lam_re, s5_lam_im, s5_log_step, s5_b_re, s5_b_im, s5_c_re, s5_c_im, s5_d, s5_glu_w, s5_glu_b, ret_decay, ret_gn_g, ret_gn_b, ml_gn_g, ml_gn_b, w_a, w_b, w_c, w_o, ln1_g, ln1_b, ln2_g, ln2_b, ffn_w1, ffn_w3, ffn_w2, moe_router, moe_router_b, moe_w1, moe_w3, moe_w2):
    n_ctx_seq, ctx_len, _ = x_prompt.shape
    n_lat_seq, lat_len, _ = x_sample.shape
    n_ctx_tok = n_ctx_seq * ctx_len
    n_lat_tok = n_lat_seq * lat_len
    t_tok = n_ctx_tok + n_lat_tok
    assert n_lat_seq + 1 <= N_COND and n_ctx_seq % 8 == 0 and n_lat_seq % 8 == 0
    assert ctx_len % CHUNK == 0 and lat_len % CHUNK == 0 and n_ctx_tok % lat_len == 0
    cfg = dict(n_ctx_seq=n_ctx_seq, ctx_len=ctx_len, n_lat_seq=n_lat_seq, lat_len=lat_len, n_ctx_tok=n_ctx_tok,
               tm_in=min(2048, lat_len), tm_merge=min(512, lat_len), tm_ffn=min(256, lat_len), tm_moe=min(1024, lat_len), tf=1408,
               s5_rows=min(256, t_tok // SUB // 8))

    x = (x_prompt.reshape(n_ctx_tok, D_MODEL), x_sample.reshape(n_lat_tok, D_MODEL))
    cond = jnp.zeros((N_COND, D_MODEL), F32).at[0].set(c_ctx).at[1:1 + n_lat_seq].set(c)
    mod = _modulation(cond, ada_w, ada_b).reshape(DEPTH, N_COND, 6, D_MODEL)
    cos_t, sin_t = _rotary_tables(lat_len)

    n_main = S5_WIDTH + 8 * WIDTH
    gate_off = n_main
    merge_off = gate_off + 4 * HEADS
    s5_fac = jax.vmap(_s5_factors)(s5_lam_re, s5_lam_im, s5_log_step, s5_b_re, s5_b_im, s5_c_re, s5_c_im)

    st_s5, st_ret, st_c, st_n, st_m = [], [], [], [], []
    zero_ret = jnp.zeros((n_ctx_seq, 2, HEADS, HEAD_DIM, HEAD_DIM), F32)
    zero_vec = jnp.zeros((n_ctx_seq, 2, HEADS, 1, HEAD_DIM), F32)
    for l in range(DEPTH):
        mod_l = mod[l]
        w_main = w_in[l][:, :n_main].astype(BF16)
        b_main = _row2(b_in[l][:n_main])
        w_gate = jnp.zeros((D_MODEL, LANE), F32).at[:, :4 * HEADS].set(w_in[l][:, gate_off:merge_off]).astype(BF16)
        b_gate = jnp.zeros((1, LANE), F32).at[0, :4 * HEADS].set(b_in[l][gate_off:merge_off])
        proj, u4, gates = _inproj(x, mod_l, w_main, b_main, w_gate, b_gate, cfg)

        s5_ain, s5_klag, s5_bout, s5_a = (m[l] for m in s5_fac)
        loc = _s5_state_in(u4, s5_ain, cfg)
        x0 = jnp.stack([cache_s5_re[:, l], cache_s5_im[:, l]], 0)
        x0 = x0.reshape(2, n_lat_seq, 2, S5_Q, S5_HALF).transpose(3, 2, 0, 1, 4).astype(F32)
        xprev, s5_fin = _s5_scan(loc, s5_a, x0, cfg)
        d4 = jnp.broadcast_to(s5_d[l].astype(F32).reshape(S5_Q, S5_P, 1, S5_PB), (S5_Q, S5_P, SUB, S5_PB))
        d4 = d4.reshape(S5_Q, 1, S5_FLAT)
        z = _s5_output(u4, xprev, s5_klag, s5_bout, d4, cfg)
        st_s5.append(s5_fin)

        gg, gb = _row2(ret_gn_g[l]), _row2(ret_gn_b[l])
        dec = ret_decay[l].astype(F32)
        yb_c, ret_fin = _retention(proj, dec, cos_t, sin_t, zero_ret, gg, gb, n_seq=n_ctx_seq, seq_len=ctx_len,
                                   n_sub=8, row0=0, use_rot=False, has_init=False, want_final=True)
        yb_s, = _retention(proj, dec, cos_t, sin_t, cache_ret[:, l].astype(F32), gg, gb, n_seq=n_lat_seq,
                           seq_len=lat_len, n_sub=1, row0=n_ctx_tok, use_rot=True, has_init=True, want_final=False)
        st_ret.append(ret_fin)

        gg = jnp.broadcast_to(ml_gn_g[l].astype(F32)[:, None], (WIDTH, LANE))
        gb = jnp.broadcast_to(ml_gn_b[l].astype(F32)[:, None], (WIDTH, LANE))
        gcol, grow = _gate_prep(gates, cfg)
        yc_c, c_fin, n_fin, m_fin = _mlstm(proj, gcol, grow, zero_ret, zero_vec, zero_vec, gg, gb, n_seq=n_ctx_seq,
                                           seq_len=ctx_len, n_sub=8, row0=0, has_init=False, want_final=True)
        n0 = cache_ml_n[:, l].astype(F32)[:, :, :, None, :]
        m0 = jnp.broadcast_to(cache_ml_m[:, l].astype(F32)[:, :, :, None, None], n0.shape)
        yc_s, = _mlstm(proj, gcol, grow, cache_ml_c[:, l].astype(F32), n0, m0, gg, gb, n_seq=n_lat_seq,
                       seq_len=lat_len, n_sub=1, row0=n_ctx_tok, has_init=True, want_final=False)
        st_c.append(c_fin)
        st_n.append(n_fin[:, :, :, 0, :])
        st_m.append(m_fin[:, :, :, 0, 0])

        j = l // 2
        router = None
        if l % 2 == 1:
            rw = jnp.zeros((D_MODEL, LANE), F32).at[:, :N_EXPERTS].set(moe_router[j])
            rw_hi = rw.astype(BF16)
            router = (rw_hi, (rw - rw_hi.astype(F32)).astype(BF16),
                      jnp.zeros((1, LANE), F32).at[0, :N_EXPERTS].set(moe_router_b[j]))
        merged = _merge(x, mod_l, z, yb_c, yb_s, yc_c, yc_s, w_in[l][:, merge_off:].astype(BF16),
                        _row2(b_in[l][merge_off:]), s5_glu_w[l].astype(BF16), _row2(s5_glu_b[l]), w_a[l].astype(BF16),
                        w_b[l].astype(BF16), w_c[l].astype(BF16), w_o[l].astype(BF16), _row2(ln1_g[l]),
                        _row2(ln1_b[l]), router, cfg)

        if router is None:
            x = _dense_ffn(merged[0], mod_l, ffn_w1[j].astype(BF16), ffn_w3[j].astype(BF16), ffn_w2[j].astype(BF16),
                           _row2(ln2_g[l]), _row2(ln2_b[l]), cfg)
        else:
            x = tuple(_moe_ffn(merged[0], mod_l, merged[1], moe_w1.astype(BF16), moe_w3.astype(BF16),
                               moe_w2.astype(BF16), j, _row2(ln2_g[l]), _row2(ln2_b[l]), cfg))

    y_p = x[0].reshape(n_ctx_seq, ctx_len, D_MODEL)
    y_s = x[1].reshape(n_lat_seq, lat_len, D_MODEL)
    s5 = jnp.stack(st_s5, 0)
    s5 = s5.reshape(DEPTH, S5_Q, 2, 2, n_ctx_seq, S5_QG, S5_STATE).transpose(3, 4, 0, 2, 1, 5, 6)
    s5 = s5.reshape(2, n_ctx_seq, DEPTH, 2, S5_GROUPS, S5_STATE)
    return (y_p, y_s, s5[0], s5[1], jnp.stack(st_ret, 1), jnp.stack(st_c, 1), jnp.stack(st_n, 1),
            jnp.stack(st_m, 1))
```
